```python
import jax, jax.numpy as jnp
from jax import lax
import numpy as np

D_MODEL = 2048
BATCH = 2
SEQ = 4096
DEPTH = 1

MIX_WIDTH = D_MODEL
HGRN_HEADS = 8
HGRN_KDIM = 128
HGRN_VDIM = MIX_WIDTH // 2 // HGRN_HEADS
HGRN_CHUNK = 64
ATTN_HEADS = 8
ATTN_HEAD_DIM = MIX_WIDTH // 2 // ATTN_HEADS
IDX_HEADS = 16
IDX_DIM = 64
DSA_TOPK = 256
DSA_QBLOCK = 64
N_MEM = 256
CROSS_HEADS = 4
CROSS_HEAD_DIM = 128
CROSS_WIDTH = CROSS_HEADS * CROSS_HEAD_DIM
D_FF = 4 * D_MODEL
EPS = 1e-6

HG_K = HGRN_HEADS * HGRN_KDIM
HG_V = HGRN_HEADS * HGRN_VDIM
ATT_W = ATTN_HEADS * ATTN_HEAD_DIM
SPLIT_SIZES = (HG_K, HG_K, HG_V, HG_V, ATT_W, ATT_W, ATT_W, IDX_HEADS * IDX_DIM, IDX_DIM, IDX_HEADS)
SPLIT_POINTS = tuple(int(p) for p in np.cumsum(SPLIT_SIZES)[:-1])
IN_WIDTH = int(sum(SPLIT_SIZES))
OUT_WIDTH = HG_V + ATT_W

kernel_name = "hymba_hgrn2_dsa_hybrid_layer"


def rms_norm(x, gain):
    xf = x.astype(jnp.float32)
    y = xf * lax.rsqrt(jnp.mean(xf * xf, axis=-1, keepdims=True) + EPS)
    return (y * gain.astype(jnp.float32)).astype(x.dtype)


def chunk_gated_recurrence(q, k, v, log_f):
    B, S, H, K = q.shape
    V = v.shape[-1]
    C = HGRN_CHUNK
    n = S // C

    def to_chunks(t):
        return t.reshape(B, n, C, H, t.shape[-1]).transpose(1, 0, 3, 2, 4)

    causal = jnp.tril(jnp.ones((C, C), dtype=bool))[:, :, None]

    def step(state, inp):
        qb, kb, vb, gb = inp
        A = jnp.cumsum(gb, axis=2)
        o_inter = jnp.einsum('bhtk,bhkv->bhtv', qb * jnp.exp(A), state)
        diff = A[:, :, :, None, :] - A[:, :, None, :, :]
        decay = jnp.exp(jnp.where(causal, diff, -jnp.inf))
        scores = jnp.einsum('bhtk,bhtsk,bhsk->bhts', qb, decay, kb)
        o_intra = jnp.einsum('bhts,bhsv->bhtv', scores, vb)
        A_last = A[:, :, -1:, :]
        new_state = state * jnp.exp(A_last[:, :, 0, :])[..., None] + jnp.einsum(
            'bhsk,bhsv->bhkv', kb * jnp.exp(A_last - A), vb)
        return new_state, o_inter + o_intra

    state0 = jnp.zeros((B, H, K, V), jnp.float32)
    _, ys = lax.scan(step, state0, (to_chunks(q), to_chunks(k), to_chunks(v), to_chunks(log_f)))
    return ys.transpose(1, 0, 3, 2, 4).reshape(B, S, H, V)


def hgrn2_group(q, f_pre, i, g, lb, onorm):
    B, S, _ = q.shape
    f32 = jnp.float32
    q = jax.nn.silu(q.reshape(B, S, HGRN_HEADS, HGRN_KDIM).astype(f32)) * (HGRN_KDIM ** -0.5)
    lb = lb.reshape(HGRN_HEADS, HGRN_KDIM).astype(f32)
    f = lb + (1.0 - lb) * jax.nn.sigmoid(f_pre.reshape(B, S, HGRN_HEADS, HGRN_KDIM).astype(f32))
    v = i.reshape(B, S, HGRN_HEADS, HGRN_VDIM).astype(f32)
    o = chunk_gated_recurrence(q, 1.0 - f, v, jnp.log(f))
    o = rms_norm(o, onorm) * jax.nn.silu(g.reshape(B, S, HGRN_HEADS, HGRN_VDIM).astype(f32))
    return o.reshape(B, S, HG_V).astype(i.dtype)


def dsa_group(q, k, v, q_idx, k_idx, w_idx, qnorm, knorm):
    B, S, _ = q.shape
    f32 = jnp.float32
    H, Dh = ATTN_HEADS, ATTN_HEAD_DIM
    q = rms_norm(q.reshape(B, S, H, Dh), qnorm)
    k = rms_norm(k.reshape(B, S, H, Dh), knorm)
    v = v.reshape(B, S, H, Dh)
    q_idx = q_idx.reshape(B, S, IDX_HEADS, IDX_DIM).astype(f32)
    k_idx = k_idx.astype(f32)
    w_idx = w_idx.astype(f32) * (IDX_HEADS ** -0.5 * IDX_DIM ** -0.5)
    top = min(DSA_TOPK, S // 4)
    n_blocks = S // DSA_QBLOCK
    s_pos = jnp.arange(S)
    scale = Dh ** -0.5
    gather = jax.vmap(lambda kv, ix: kv[ix])

    def block(bi):
        start = bi * DSA_QBLOCK
        qb = lax.dynamic_slice_in_dim(q, start, DSA_QBLOCK, axis=1)
        qib = lax.dynamic_slice_in_dim(q_idx, start, DSA_QBLOCK, axis=1)
        wb = lax.dynamic_slice_in_dim(w_idx, start, DSA_QBLOCK, axis=1)
        t_pos = start + jnp.arange(DSA_QBLOCK)
        logits = jnp.einsum('bthd,bsd->bths', qib, k_idx)
        idx_score = jnp.einsum('bth,bths->bts', wb, jax.nn.relu(logits))
        visible = s_pos[None, :] <= t_pos[:, None]
        idx_score = jnp.where(visible[None], idx_score, -jnp.inf)
        _, sel = lax.top_k(idx_score, top)
        valid = sel <= t_pos[None, :, None]
        kg = gather(k, sel)
        vg = gather(v, sel)
        s = jnp.einsum('bthd,btkhd->bthk', qb, kg).astype(f32) * scale
        s = jnp.where(valid[:, :, None, :], s, -jnp.inf)
        p = jax.nn.softmax(s, axis=-1).astype(vg.dtype)
        return jnp.einsum('bthk,btkhd->bthd', p, vg)

    out = lax.map(block, jnp.arange(n_blocks))
    return out.transpose(1, 0, 2, 3, 4).reshape(B, S, ATT_W)


def hybrid_mixer(xn, w_in, lb, hgrn_onorm, attn_qnorm, attn_knorm, w_out):
    proj = xn @ w_in
    (hq, hf, hi, hg, aq, ak, av, iq, ik, iw) = jnp.split(proj, SPLIT_POINTS, axis=-1)
    y_h = hgrn2_group(hq, hf, hi, hg, lb, hgrn_onorm)
    y_a = dsa_group(aq, ak, av, iq, ik, iw, attn_qnorm, attn_knorm)
    return jnp.concatenate([y_h, y_a], axis=-1) @ w_out


def memory_cross_attention(hn, memn, wq, wk, wv, wo, qnorm, knorm):
    B, S, _ = hn.shape
    M = memn.shape[1]
    q = rms_norm((hn @ wq).reshape(B, S, CROSS_HEADS, CROSS_HEAD_DIM), qnorm)
    k = rms_norm((memn @ wk).reshape(B, M, CROSS_HEADS, CROSS_HEAD_DIM), knorm)
    v = (memn @ wv).reshape(B, M, CROSS_HEADS, CROSS_HEAD_DIM)
    s = jnp.einsum('bthd,bmhd->bhtm', q, k).astype(jnp.float32) * (CROSS_HEAD_DIM ** -0.5)
    p = jax.nn.softmax(s, axis=-1).astype(v.dtype)
    o = jnp.einsum('bhtm,bmhd->bthd', p, v).reshape(B, S, CROSS_WIDTH)
    return o @ wo


def sqrelu_mlp(hn, w_up, w_down):
    return jnp.square(jax.nn.relu(hn @ w_up)) @ w_down


def setup_inputs(seed: int = 0) -> dict:
    key = jax.random.key(seed)
    ks = jax.random.split(key, 20)
    f32 = jnp.float32
    nrm = lambda k, shape, fan_in: jax.random.normal(k, shape, f32) * (fan_in ** -0.5)
    gain = lambda k, shape: 1.0 + 0.02 * jax.random.normal(k, shape, f32)
    L = DEPTH
    return {
        "x": jax.random.normal(ks[0], (BATCH, SEQ, D_MODEL), f32),
        "mem": jax.random.normal(ks[1], (BATCH, N_MEM, D_MODEL), f32),
        "norm_mix": gain(ks[2], (L, D_MODEL)),
        "w_in": nrm(ks[3], (L, D_MODEL, IN_WIDTH), D_MODEL),
        "hgrn_lb_logits": 0.1 * jax.random.normal(ks[4], (L + 1, HG_K), f32),
        "hgrn_onorm": gain(ks[5], (L, HGRN_VDIM)),
        "attn_qnorm": gain(ks[6], (L, ATTN_HEAD_DIM)),
        "attn_knorm": gain(ks[7], (L, ATTN_HEAD_DIM)),
        "w_out": nrm(ks[8], (L, OUT_WIDTH, D_MODEL), OUT_WIDTH),
        "norm_cross": gain(ks[9], (L, D_MODEL)),
        "mem_norm": gain(ks[10], (L, D_MODEL)),
        "wq_x": nrm(ks[11], (L, D_MODEL, CROSS_WIDTH), D_MODEL),
        "wk_x": nrm(ks[12], (L, D_MODEL, CROSS_WIDTH), D_MODEL),
        "wv_x": nrm(ks[13], (L, D_MODEL, CROSS_WIDTH), D_MODEL),
        "wo_x": nrm(ks[14], (L, CROSS_WIDTH, D_MODEL), CROSS_WIDTH),
        "xq_norm": gain(ks[15], (L, CROSS_HEAD_DIM)),
        "xk_norm": gain(ks[16], (L, CROSS_HEAD_DIM)),
        "norm_mlp": gain(ks[17], (L, D_MODEL)),
        "w_up": nrm(ks[18], (L, D_MODEL, D_FF), D_MODEL),
        "w_down": nrm(ks[19], (L, D_FF, D_MODEL), D_FF),
    }


def reference(x, mem, norm_mix, w_in, hgrn_lb_logits, hgrn_onorm, attn_qnorm, attn_knorm, w_out,
              norm_cross, mem_norm, wq_x, wk_x, wv_x, wo_x, xq_norm, xk_norm,
              norm_mlp, w_up, w_down):
    lb_all = jnp.cumsum(jax.nn.softmax(hgrn_lb_logits.astype(jnp.float32), axis=0), axis=0)
    h = x
    for l in range(DEPTH):
        h = h + hybrid_mixer(rms_norm(h, norm_mix[l]), w_in[l], lb_all[l], hgrn_onorm[l],
                             attn_qnorm[l], attn_knorm[l], w_out[l])
        h = h + memory_cross_attention(rms_norm(h, norm_cross[l]), rms_norm(mem, mem_norm[l]),
                                       wq_x[l], wk_x[l], wv_x[l], wo_x[l], xq_norm[l], xk_norm[l])
        h = h + sqrelu_mlp(rms_norm(h, norm_mlp[l]), w_up[l], w_down[l])
    return h
```

```python
import functools

import numpy as np
import jax
import jax.numpy as jnp
from jax import lax
from jax.experimental import pallas as pl
from jax.experimental.pallas import tpu as pltpu

F32 = jnp.float32
BF16 = jnp.bfloat16
EPS = 1e-6

LANES = 128
HGRN_HEADS = 8
HGRN_KDIM = 128
HGRN_CHUNK = 64
ATTN_HEADS = 8
ATTN_HEAD_DIM = 128
IDX_HEADS = 16
IDX_DIM = 64
DSA_TOPK = 256
CROSS_HEADS = 4
CROSS_HEAD_DIM = 128
TAIL = 256
VMEM_LIMIT = 56 * 1024 * 1024
NEG_INF = float("-inf")
INT_MIN = -(2 ** 31)


def _params(*sem):
    return pltpu.CompilerParams(dimension_semantics=sem, vmem_limit_bytes=VMEM_LIMIT)


def _rms(x, gain):
    return x * lax.rsqrt(jnp.mean(x * x, axis=-1, keepdims=True) + EPS) * gain


def _sigmoid(x):
    return 1.0 / (1.0 + jnp.exp(-x))


def _dot(a, b):
    return jnp.dot(a, b, preferred_element_type=F32)


def _dot_nt(a, b):
    return lax.dot_general(a, b, (((1,), (1,)), ((), ())), preferred_element_type=F32)


def _dot_tn(a, b):
    return lax.dot_general(a, b, (((0,), (0,)), ((), ())), preferred_element_type=F32)


def _norm_rows_to(x_ref, gain, dst_ref, rows):
    step = 256
    def body(r, c):
        r0 = pl.multiple_of(r * step, step)
        x = x_ref[pl.ds(r0, step), :]
        dst_ref[pl.ds(r0, step), :] = _rms(x, gain).astype(BF16)
        return c
    lax.fori_loop(0, rows // step, body, 0)


def _norm_matmul_kernel(x_ref, g_ref, w_ref, o_ref, xn_ref):
    @pl.when(pl.program_id(1) == 0)
    def _():
        _norm_rows_to(x_ref, g_ref[...], xn_ref, x_ref.shape[0])
    o_ref[...] = _dot(xn_ref[...], w_ref[...]).astype(o_ref.dtype)


def norm_matmul(x, gain, w, tm, tn):
    m, k = x.shape
    n = w.shape[1]
    return pl.pallas_call(
        _norm_matmul_kernel,
        grid=(m // tm, n // tn),
        in_specs=[pl.BlockSpec((tm, k), lambda i, j: (i, 0)),
                  pl.BlockSpec((1, k), lambda i, j: (0, 0)),
                  pl.BlockSpec((k, tn), lambda i, j: (0, j))],
        out_specs=pl.BlockSpec((tm, tn), lambda i, j: (i, j)),
        out_shape=jax.ShapeDtypeStruct((m, n), BF16),
        scratch_shapes=[pltpu.VMEM((tm, k), BF16)],
        compiler_params=_params("parallel", "arbitrary"),
        name="in_proj",
    )(x, gain, w)


def _hgrn_exponent_matrix():
    c = HGRN_CHUNK
    t = np.arange(c)[:, None]
    j = np.arange(c)[None, :]
    mats = [(j <= t), (j > t)]
    h = c // 2
    while h >= 1:
        mid = (t // (2 * h)) * (2 * h) + h
        upper = (t % (2 * h)) >= h
        mats.append(np.where(upper, (j >= mid) & (j <= t), (j > t) & (j < mid)))
        h //= 2
    return np.concatenate(mats, axis=0).astype(np.float32)


def _hgrn_kernel(q_ref, f_ref, i_ref, g_ref, lbl_ref, on_ref, ncat_ref, o_ref, state_ref, *, nchunks):
    c = HGRN_CHUNK
    dk = HGRN_KDIM

    @pl.when(pl.program_id(1) == 0)
    def _():
        state_ref[...] = jnp.zeros_like(state_ref)

    lbl = lbl_ref[...]
    e = jnp.exp(lbl - jnp.max(lbl, axis=0, keepdims=True))
    lb = e[0:1] / jnp.sum(e, axis=0, keepdims=True)
    onorm = on_ref[...]
    ncat = ncat_ref[...]

    row = lax.broadcasted_iota(jnp.int32, (c, c), 0)
    col = lax.broadcasted_iota(jnp.int32, (c, c), 1)
    eye = row == col
    level_masks = []
    lg = 5
    while lg >= 0:
        level_masks.append(((row >> (lg + 1)) == (col >> (lg + 1)))
                           & (((row >> lg) & 1) == 1) & (((col >> lg) & 1) == 0))
        lg -= 1

    def chunk_body(ci, carry):
        r0 = pl.multiple_of(ci * c, c)
        hq = q_ref[0, pl.ds(r0, c), :].astype(F32)
        hf = f_ref[0, pl.ds(r0, c), :].astype(F32)
        qf = hq * _sigmoid(hq) * (dk ** -0.5)
        f = lb + (1.0 - lb) * _sigmoid(hf)
        logf = jnp.log(f)
        kk = 1.0 - f
        g0 = logf.astype(BF16)
        r1 = logf - g0.astype(F32)
        g1 = r1.astype(BF16)
        g2 = (r1 - g1.astype(F32)).astype(BF16)
        ex = _dot(ncat, g0) + _dot(ncat, g1) + _dot(ncat, g2)
        for h in range(HGRN_HEADS):
            sl = slice(h * dk, (h + 1) * dk)
            eh = jnp.exp(ex[:, sl])
            qh = qf[:, sl]
            kh = kk[:, sl]
            vh = i_ref[0, pl.ds(r0, c), sl]
            st = state_ref[h]
            o = _dot_nt((qh * eh[0:c]).astype(BF16), st.astype(BF16))
            sc = jnp.where(eye, jnp.sum(qh * kh, axis=1, keepdims=True), 0.0)
            for l, msk in enumerate(level_masks):
                el = eh[(2 + l) * c:(3 + l) * c]
                p = _dot_nt((qh * el).astype(BF16), (kh * el).astype(BF16))
                sc = sc + jnp.where(msk, p, 0.0)
            o = o + _dot(sc.astype(BF16), vh)
            k_st = (kh * eh[c:2 * c]).astype(BF16)
            state_ref[h] = st * eh[c - 1:c] + _dot_tn(vh, k_st)
            gate = g_ref[0, pl.ds(r0, c), sl].astype(F32)
            y = _rms(o, onorm) * (gate * _sigmoid(gate))
            o_ref[0, pl.ds(r0, c), sl] = y.astype(o_ref.dtype)
        return carry

    lax.fori_loop(0, nchunks, chunk_body, 0)


def hgrn_group(proj, lb_logits, onorm, t_blk):
    b, s, _ = proj.shape
    w = HGRN_HEADS * HGRN_KDIM
    ncat = jnp.asarray(_hgrn_exponent_matrix(), BF16)
    col = lambda cb: pl.BlockSpec((1, t_blk, w), lambda bi, ti, cb=cb: (bi, ti, cb))
    return pl.pallas_call(
        functools.partial(_hgrn_kernel, nchunks=t_blk // HGRN_CHUNK),
        grid=(b, s // t_blk),
        in_specs=[col(0), col(1), col(2), col(3),
                  pl.BlockSpec(lb_logits.shape, lambda bi, ti: (0, 0)),
                  pl.BlockSpec((1, HGRN_KDIM), lambda bi, ti: (0, 0)),
                  pl.BlockSpec(ncat.shape, lambda bi, ti: (0, 0))],
        out_specs=pl.BlockSpec((1, t_blk, w), lambda bi, ti: (bi, ti, 0)),
        out_shape=jax.ShapeDtypeStruct((b, s, w), BF16),
        scratch_shapes=[pltpu.VMEM((HGRN_HEADS, HGRN_KDIM, HGRN_KDIM), F32)],
        compiler_params=_params("parallel", "arbitrary"),
        name="hgrn2",
    )(proj, proj, proj, proj, lb_logits, onorm, ncat)


def _dsa_prep_kernel(k_ref, tail_ref, kg_ref, dup_ref, kn_ref, ikd_ref):
    kg = kg_ref[...]
    for h in range(ATTN_HEADS):
        sl = slice(h * ATTN_HEAD_DIM, (h + 1) * ATTN_HEAD_DIM)
        kn_ref[:, sl] = _rms(k_ref[:, sl].astype(F32), kg).astype(BF16)
    ikd_ref[...] = _dot(tail_ref[...], dup_ref[...]).astype(BF16)


def dsa_prep(proj2d, knorm, tm):
    n = proj2d.shape[0]
    w = ATTN_HEADS * ATTN_HEAD_DIM
    dup = np.zeros((TAIL, LANES), np.float32)
    dup[np.arange(IDX_DIM), np.arange(IDX_DIM)] = 1.0
    dup[np.arange(IDX_DIM), np.arange(IDX_DIM) + IDX_DIM] = 1.0
    return pl.pallas_call(
        _dsa_prep_kernel,
        grid=(n // tm,),
        in_specs=[pl.BlockSpec((tm, w), lambda i: (i, 5)),
                  pl.BlockSpec((tm, TAIL), lambda i: (i, 8 * w // TAIL)),
                  pl.BlockSpec((1, ATTN_HEAD_DIM), lambda i: (0, 0)),
                  pl.BlockSpec((TAIL, LANES), lambda i: (0, 0))],
        out_specs=[pl.BlockSpec((tm, w), lambda i: (i, 0)),
                   pl.BlockSpec((tm, LANES), lambda i: (i, 0))],
        out_shape=[jax.ShapeDtypeStruct((n, w), BF16),
                   jax.ShapeDtypeStruct((n, LANES), BF16)],
        compiler_params=_params("parallel"),
        name="dsa_prep",
    )(proj2d, proj2d, knorm, jnp.asarray(dup, BF16))


def _dsa_kernel(aq_ref, iq_ref, tail_ref, ikd_ref, kn_ref, v_ref, qg_ref, eq_ref, esel_ref, o_ref,
                keys_ref, thr_ref, qn_ref, qw_ref, lo_ref, hi_ref, m_ref, l_ref, acc_ref,
                *, tq, tk, topk):
    i = pl.program_id(1)
    j = pl.program_id(2)
    dh = ATTN_HEAD_DIM
    reps = tk // LANES

    def widen(a):
        return a if reps == 1 else jnp.concatenate([a] * reps, axis=1)

    def causal(jj):
        t_pos = i * tq + lax.broadcasted_iota(jnp.int32, (tq, tk), 0)
        s_pos = jj * tk + lax.broadcasted_iota(jnp.int32, (tq, tk), 1)
        return s_pos <= t_pos

    @pl.when(j == 0)
    def _index_and_select():
        qg = qg_ref[...] * (dh ** -0.5)
        for h in range(ATTN_HEADS):
            sl = slice(h * dh, (h + 1) * dh)
            qn_ref[:, sl] = _rms(aq_ref[0, :, sl].astype(F32), qg).astype(BF16)
        tail = tail_ref[0]
        ww = _dot(tail, eq_ref[...])
        qw = (iq_ref[0].astype(F32) * ww * (IDX_HEADS ** -0.5 * IDX_DIM ** -0.5)).astype(BF16)
        wl = _dot(tail, esel_ref[...])
        lane = lax.broadcasted_iota(jnp.int32, (tq, LANES), 1)
        zero = jnp.zeros((tq, LANES), BF16)
        for p in range(IDX_HEADS // 2):
            pair = qw[:, p * LANES:(p + 1) * LANES]
            qw_ref[2 * p] = jnp.where(lane < IDX_DIM, pair, zero)
            qw_ref[2 * p + 1] = jnp.where(lane >= IDX_DIM, pair, zero)
        for h in range(IDX_HEADS):
            pos = wl[:, h * LANES:(h + 1) * LANES] > 0.0
            lo_ref[h] = jnp.where(pos, 0.0, NEG_INF)
            hi_ref[h] = jnp.where(pos, jnp.inf, 0.0)

        def score_tile(jj, c):
            r0 = pl.multiple_of(jj * tk, tk)
            ik = ikd_ref[0, pl.ds(r0, tk), :]
            comb = jnp.zeros((tq, tk), F32)
            for h in range(IDX_HEADS):
                x = _dot_nt(qw_ref[h], ik)
                comb = comb + jnp.minimum(jnp.maximum(x, widen(lo_ref[h])), widen(hi_ref[h]))
            comb = jnp.where(causal(jj), comb, NEG_INF)
            bits = pltpu.bitcast(comb, jnp.int32)
            keys_ref[jj] = jnp.where(bits < 0, bits ^ 0x7FFFFFFF, bits)
            return c
        lax.fori_loop(0, i + 1, score_tile, 0)

        def count_ge(tc):
            tcw = widen(tc)
            def body(jj, acc):
                ge = jnp.where(keys_ref[jj] >= tcw, 1, 0)
                for r in range(reps):
                    acc = acc + ge[:, r * LANES:(r + 1) * LANES]
                return acc
            acc = lax.fori_loop(0, i + 1, body, jnp.zeros((tq, LANES), jnp.int32))
            return jnp.sum(acc.astype(F32), axis=1, keepdims=True)
        t0 = jnp.where(count_ge(jnp.zeros((tq, LANES), jnp.int32)) >= topk, 0, INT_MIN)
        t0 = jnp.broadcast_to(t0, (tq, LANES)).astype(jnp.int32)
        def bit_step(bi, t):
            tc = t | (jnp.int32(1) << (30 - bi))
            return jnp.where(count_ge(tc) >= topk, tc, t)
        thr_ref[...] = lax.fori_loop(0, 31, bit_step, t0)

        m_ref[...] = jnp.full_like(m_ref, NEG_INF)
        l_ref[...] = jnp.zeros_like(l_ref)
        acc_ref[...] = jnp.zeros_like(acc_ref)

    @pl.when(j <= i)
    def _attend():
        sel = (keys_ref[j] >= widen(thr_ref[...])) & causal(j)
        bias = jnp.where(sel, 0.0, NEG_INF)
        for h in range(ATTN_HEADS):
            sl = slice(h * dh, (h + 1) * dh)
            s = _dot_nt(qn_ref[:, sl], kn_ref[0, :, sl]) + bias
            m_old = m_ref[h]
            m_new = jnp.maximum(m_old, jnp.max(s, axis=1, keepdims=True))
            m_safe = jnp.where(m_new == NEG_INF, 0.0, m_new)
            alpha = jnp.exp(m_old - m_safe)
            p = jnp.exp(s - widen(m_safe))
            l_ref[h] = alpha * l_ref[h] + jnp.sum(p, axis=1, keepdims=True)
            acc_ref[h] = alpha * acc_ref[h] + _dot(p.astype(BF16), v_ref[0, :, sl])
            m_ref[h] = m_new

    @pl.when(j == i)
    def _finish():
        for h in range(ATTN_HEADS):
            sl = slice(h * dh, (h + 1) * dh)
            o_ref[0, :, sl] = (acc_ref[h] / l_ref[h]).astype(o_ref.dtype)


def dsa_group(proj, kn, ikd, qnorm, tq):
    b, s, _ = proj.shape
    tk = tq
    nq = s // tq
    w = ATTN_HEADS * ATTN_HEAD_DIM
    topk = min(DSA_TOPK, s // 4)
    eq = np.zeros((TAIL, IDX_HEADS * IDX_DIM), np.float32)
    esel = np.zeros((TAIL, IDX_HEADS * LANES), np.float32)
    for h in range(IDX_HEADS):
        eq[IDX_DIM + h, h * IDX_DIM:(h + 1) * IDX_DIM] = 1.0
        esel[IDX_DIM + h, h * LANES:(h + 1) * LANES] = 1.0
    kv_map = lambda bi, i, j: (bi, jnp.minimum(j, i), 0)
    return pl.pallas_call(
        functools.partial(_dsa_kernel, tq=tq, tk=tk, topk=topk),
        grid=(b, nq, nq),
        in_specs=[pl.BlockSpec((1, tq, w), lambda bi, i, j: (bi, i, 4)),
                  pl.BlockSpec((1, tq, w), lambda bi, i, j: (bi, i, 7)),
                  pl.BlockSpec((1, tq, TAIL), lambda bi, i, j: (bi, i, 8 * w // TAIL)),
                  pl.BlockSpec((1, s, LANES), lambda bi, i, j: (bi, 0, 0)),
                  pl.BlockSpec((1, tk, w), kv_map),
                  pl.BlockSpec((1, tk, w), lambda bi, i, j: (bi, jnp.minimum(j, i), 6)),
                  pl.BlockSpec((1, ATTN_HEAD_DIM), lambda bi, i, j: (0, 0)),
                  pl.BlockSpec(eq.shape, lambda bi, i, j: (0, 0)),
                  pl.BlockSpec(esel.shape, lambda bi, i, j: (0, 0))],
        out_specs=pl.BlockSpec((1, tq, w), lambda bi, i, j: (bi, i, 0)),
        out_shape=jax.ShapeDtypeStruct((b, s, w), BF16),
        scratch_shapes=[pltpu.VMEM((nq, tq, tk), jnp.int32),
                        pltpu.VMEM((tq, LANES), jnp.int32),
                        pltpu.VMEM((tq, w), BF16),
                        pltpu.VMEM((IDX_HEADS, tq, LANES), BF16),
                        pltpu.VMEM((IDX_HEADS, tq, LANES), F32),
                        pltpu.VMEM((IDX_HEADS, tq, LANES), F32),
                        pltpu.VMEM((ATTN_HEADS, tq, LANES), F32),
                        pltpu.VMEM((ATTN_HEADS, tq, LANES), F32),
                        pltpu.VMEM((ATTN_HEADS, tq, ATTN_HEAD_DIM), F32)],
        compiler_params=_params("parallel", "arbitrary", "arbitrary"),
        name="dsa",
    )(proj, proj, proj, ikd, kn, proj, qnorm, jnp.asarray(eq, BF16), jnp.asarray(esel, BF16))


def _out_proj_kernel(yh_ref, ya_ref, w1_ref, w2_ref, x_ref, o_ref):
    o_ref[...] = x_ref[...] + _dot(yh_ref[...], w1_ref[...]) + _dot(ya_ref[...], w2_ref[...])


def out_proj(yh, ya, w_out, x, tm, tn):
    m, kh = yh.shape
    n = w_out.shape[1]
    return pl.pallas_call(
        _out_proj_kernel,
        grid=(m // tm, n // tn),
        in_specs=[pl.BlockSpec((tm, kh), lambda i, j: (i, 0)),
                  pl.BlockSpec((tm, kh), lambda i, j: (i, 0)),
                  pl.BlockSpec((kh, tn), lambda i, j: (0, j)),
                  pl.BlockSpec((kh, tn), lambda i, j: (1, j)),
                  pl.BlockSpec((tm, tn), lambda i, j: (i, j))],
        out_specs=pl.BlockSpec((tm, tn), lambda i, j: (i, j)),
        out_shape=jax.ShapeDtypeStruct((m, n), F32),
        compiler_params=_params("parallel", "arbitrary"),
        name="out_proj",
    )(yh, ya, w_out, w_out, x)


def _mem_kv_kernel(mem_ref, mg_ref, wk_ref, wv_ref, kg_ref, k_ref, v_ref):
    memn = _rms(mem_ref[0], mg_ref[...]).astype(BF16)
    k = _dot(memn, wk_ref[...])
    kg = kg_ref[...]
    for h in range(CROSS_HEADS):
        sl = slice(h * CROSS_HEAD_DIM, (h + 1) * CROSS_HEAD_DIM)
        k_ref[0, :, sl] = _rms(k[:, sl], kg).astype(BF16)
    v_ref[0] = _dot(memn, wv_ref[...]).astype(BF16)


def mem_kv(mem, mem_norm, wk, wv, xk_norm):
    b, nm, d = mem.shape
    cw = wk.shape[1]
    full = lambda a: pl.BlockSpec(a.shape, lambda bi: (0,) * a.ndim)
    return pl.pallas_call(
        _mem_kv_kernel,
        grid=(b,),
        in_specs=[pl.BlockSpec((1, nm, d), lambda bi: (bi, 0, 0)),
                  full(mem_norm), full(wk), full(wv), full(xk_norm)],
        out_specs=[pl.BlockSpec((1, nm, cw), lambda bi: (bi, 0, 0))] * 2,
        out_shape=[jax.ShapeDtypeStruct((b, nm, cw), BF16)] * 2,
        compiler_params=_params("parallel"),
        name="mem_kv",
    )(mem, mem_norm, wk, wv, xk_norm)


def _cross_kernel(h_ref, ng_ref, wq_ref, qg_ref, k_ref, v_ref, wo_ref, o_ref, hn_ref, oc_ref):
    rows = h_ref.shape[1]
    dh = CROSS_HEAD_DIM
    step = 256
    gain = ng_ref[...]
    def body(r, c):
        r0 = pl.multiple_of(r * step, step)
        hn_ref[pl.ds(r0, step), :] = _rms(h_ref[0, pl.ds(r0, step), :], gain).astype(BF16)
        return c
    lax.fori_loop(0, rows // step, body, 0)
    q = _dot(hn_ref[...], wq_ref[...])
    qg = qg_ref[...] * (dh ** -0.5)
    for h in range(CROSS_HEADS):
        sl = slice(h * dh, (h + 1) * dh)
        qn = _rms(q[:, sl], qg).astype(BF16)
        s = _dot_nt(qn, k_ref[0, :, sl])
        p = jnp.exp(s - jnp.max(s, axis=1, keepdims=True))
        l = jnp.sum(p, axis=1, keepdims=True)
        oc_ref[:, sl] = (_dot(p.astype(BF16), v_ref[0, :, sl]) / l).astype(BF16)
    o_ref[0] = h_ref[0] + _dot(oc_ref[...], wo_ref[...])


def cross_attention(h, norm_cross, wq, xq_norm, kx, vx, wo, tm):
    b, s, d = h.shape
    nm, cw = kx.shape[1:]
    full = lambda a: pl.BlockSpec(a.shape, lambda bi, i: (0,) * a.ndim)
    return pl.pallas_call(
        _cross_kernel,
        grid=(b, s // tm),
        in_specs=[pl.BlockSpec((1, tm, d), lambda bi, i: (bi, i, 0)),
                  full(norm_cross), full(wq), full(xq_norm),
                  pl.BlockSpec((1, nm, cw), lambda bi, i: (bi, 0, 0)),
                  pl.BlockSpec((1, nm, cw), lambda bi, i: (bi, 0, 0)),
                  full(wo)],
        out_specs=pl.BlockSpec((1, tm, d), lambda bi, i: (bi, i, 0)),
        out_shape=jax.ShapeDtypeStruct((b, s, d), F32),
        scratch_shapes=[pltpu.VMEM((tm, d), BF16), pltpu.VMEM((tm, cw), BF16)],
        compiler_params=_params("parallel", "parallel"),
        name="cross_attn",
    )(h, norm_cross, wq, xq_norm, kx, vx, wo)


def _mlp_kernel(h_ref, g_ref, wu_ref, wd_ref, o_ref, hn_ref, acc_ref):
    j = pl.program_id(1)

    @pl.when(j == 0)
    def _():
        _norm_rows_to(h_ref, g_ref[...], hn_ref, h_ref.shape[0])
        acc_ref[...] = jnp.zeros_like(acc_ref)

    u = jnp.maximum(_dot(hn_ref[...], wu_ref[...]), 0.0)
    acc_ref[...] += _dot((u * u).astype(BF16), wd_ref[...])

    @pl.when(j == pl.num_programs(1) - 1)
    def _():
        o_ref[...] = h_ref[...] + acc_ref[...]


def mlp(h, gain, w_up, w_down, tm, tf):
    m, d = h.shape
    f = w_up.shape[1]
    return pl.pallas_call(
        _mlp_kernel,
        grid=(m // tm, f // tf),
        in_specs=[pl.BlockSpec((tm, d), lambda i, j: (i, 0)),
                  pl.BlockSpec((1, d), lambda i, j: (0, 0)),
                  pl.BlockSpec((d, tf), lambda i, j: (0, j)),
                  pl.BlockSpec((tf, d), lambda i, j: (j, 0))],
        out_specs=pl.BlockSpec((tm, d), lambda i, j: (i, 0)),
        out_shape=jax.ShapeDtypeStruct((m, d), F32),
        scratch_shapes=[pltpu.VMEM((tm, d), BF16), pltpu.VMEM((tm, d), F32)],
        compiler_params=_params("parallel", "arbitrary"),
        name="mlp",
    )(h, gain, w_up, w_down)


def _tile(n, pref):
    return pref if n % pref == 0 else n


def kernel(x, mem, norm_mix, w_in, hgrn_lb_logits, hgrn_onorm, attn_qnorm, attn_knorm, w_out,
           norm_cross, mem_norm, wq_x, wk_x, wv_x, wo_x, xq_norm, xk_norm,
           norm_mlp, w_up, w_down):
    b, s, d = x.shape
    n = b * s
    depth = w_in.shape[0]
    assert depth == 1
    l = 0
    in_width = w_in.shape[2]
    main_w = 8 * HGRN_HEADS * HGRN_KDIM
    assert in_width == main_w + IDX_DIM + IDX_HEADS
    pad_w = main_w + TAIL

    w_in_b = jnp.pad(w_in[l].astype(BF16), ((0, 0), (0, pad_w - in_width)))
    x2 = x.reshape(n, d)

    proj = norm_matmul(x2, norm_mix[l:l + 1], w_in_b, _tile(n, 1024), 768)
    proj3 = proj.reshape(b, s, pad_w)

    y_h = hgrn_group(proj3, hgrn_lb_logits, hgrn_onorm[l:l + 1], _tile(s, 256))
    kn, ikd = dsa_prep(proj, attn_knorm[l:l + 1], _tile(n, 512))
    y_a = dsa_group(proj3, kn.reshape(b, s, -1), ikd.reshape(b, s, LANES), attn_qnorm[l:l + 1], _tile(s, 256))

    h1 = out_proj(y_h.reshape(n, -1), y_a.reshape(n, -1), w_out[l].astype(BF16), x2, _tile(n, 1024), 1024)

    kx, vx = mem_kv(mem, mem_norm[l:l + 1], wk_x[l].astype(BF16), wv_x[l].astype(BF16), xk_norm[l:l + 1])
    h2 = cross_attention(h1.reshape(b, s, d), norm_cross[l:l + 1], wq_x[l].astype(BF16), xq_norm[l:l + 1],
                         kx, vx, wo_x[l].astype(BF16), _tile(s, 512))

    h3 = mlp(h2.reshape(n, d), norm_mlp[l:l + 1], w_up[l].astype(BF16), w_down[l].astype(BF16),
             _tile(n, 512), 512)
    return h3.reshape(b, s, d)
```

```python
import functools

import numpy as np
import jax
import jax.numpy as jnp
from jax import lax
from jax.experimental import pallas as pl
from jax.experimental.pallas import tpu as pltpu

F32 = jnp.float32
BF16 = jnp.bfloat16
EPS = 1e-6

LANES = 128
HGRN_HEADS = 8
HGRN_KDIM = 128
HGRN_CHUNK = 64
ATTN_HEADS = 8
ATTN_HEAD_DIM = 128
IDX_HEADS = 16
IDX_DIM = 64
DSA_TOPK = 256
CROSS_HEADS = 4
CROSS_HEAD_DIM = 128
TAIL = 256
VMEM_LIMIT = 56 * 1024 * 1024
NEG_INF = float("-inf")
INT_MIN = -(2 ** 31)
SOFTMAX_FIXED_SHIFT_MAX = 40.0


def _params(*sem, flags=None):
    return pltpu.CompilerParams(dimension_semantics=sem, vmem_limit_bytes=VMEM_LIMIT, flags=flags)


def _rms(x, gain):
    return x * lax.rsqrt(jnp.mean(x * x, axis=-1, keepdims=True) + EPS) * gain


def _sigmoid(x):
    return 1.0 / (1.0 + jnp.exp(-x))


def _dot(a, b):
    return jnp.dot(a, b, preferred_element_type=F32)


def _dot_nt(a, b):
    return lax.dot_general(a, b, (((1,), (1,)), ((), ())), preferred_element_type=F32)


def _dot_tn(a, b):
    return lax.dot_general(a, b, (((0,), (0,)), ((), ())), preferred_element_type=F32)


def _norm_rows_to(x_ref, gain, dst_ref, rows):
    step = 256
    def body(r, c):
        r0 = pl.multiple_of(r * step, step)
        x = x_ref[pl.ds(r0, step), :]
        dst_ref[pl.ds(r0, step), :] = _rms(x, gain).astype(BF16)
        return c
    lax.fori_loop(0, rows // step, body, 0)


def _norm_matmul_kernel(x_ref, g_ref, w_ref, o_ref, xn_ref):
    @pl.when(pl.program_id(1) == 0)
    def _():
        _norm_rows_to(x_ref, g_ref[...], xn_ref, x_ref.shape[0])
    o_ref[...] = _dot(xn_ref[...], w_ref[...]).astype(o_ref.dtype)


def norm_matmul(x, gain, w, tm, tn):
    m, k = x.shape
    n = w.shape[1]
    return pl.pallas_call(
        _norm_matmul_kernel,
        grid=(m // tm, n // tn),
        in_specs=[pl.BlockSpec((tm, k), lambda i, j: (i, 0)),
                  pl.BlockSpec((1, k), lambda i, j: (0, 0)),
                  pl.BlockSpec((k, tn), lambda i, j: (0, j))],
        out_specs=pl.BlockSpec((tm, tn), lambda i, j: (i, j)),
        out_shape=jax.ShapeDtypeStruct((m, n), BF16),
        scratch_shapes=[pltpu.VMEM((tm, k), BF16)],
        compiler_params=_params("parallel", "arbitrary"),
        name="in_proj",
    )(x, gain, w)


def _hgrn_exponent_matrix():
    c = HGRN_CHUNK
    t = np.arange(c)[:, None]
    j = np.arange(c)[None, :]
    mats = [(j <= t), (j > t)]
    h = c // 2
    while h >= 1:
        mid = (t // (2 * h)) * (2 * h) + h
        upper = (t % (2 * h)) >= h
        mats.append(np.where(upper, (j >= mid) & (j <= t), (j > t) & (j < mid)))
        h //= 2
    return np.concatenate(mats, axis=0).astype(np.float32)


def _hgrn_kernel(q_ref, f_ref, i_ref, g_ref, lbl_ref, on_ref, ncat_ref, o_ref, state_ref, *, nchunks):
    c = HGRN_CHUNK
    dk = HGRN_KDIM

    @pl.when(pl.program_id(1) == 0)
    def _():
        state_ref[...] = jnp.zeros_like(state_ref)

    lbl = lbl_ref[...]
    e = jnp.exp(lbl - jnp.max(lbl, axis=0, keepdims=True))
    lb = e[0:1] / jnp.sum(e, axis=0, keepdims=True)
    onorm = on_ref[...]
    ncat = ncat_ref[...]

    row = lax.broadcasted_iota(jnp.int32, (c, c), 0)
    col = lax.broadcasted_iota(jnp.int32, (c, c), 1)
    eye = row == col
    level_masks = []
    lg = 5
    while lg >= 0:
        level_masks.append(((row >> (lg + 1)) == (col >> (lg + 1)))
                           & (((row >> lg) & 1) == 1) & (((col >> lg) & 1) == 0))
        lg -= 1

    def chunk_body(ci, carry):
        r0 = pl.multiple_of(ci * c, c)
        hq = q_ref[0, pl.ds(r0, c), :].astype(F32)
        hf = f_ref[0, pl.ds(r0, c), :].astype(F32)
        qf = hq * _sigmoid(hq) * (dk ** -0.5)
        f = lb + (1.0 - lb) * _sigmoid(hf)
        logf = jnp.log(f)
        kk = 1.0 - f
        g0 = logf.astype(BF16)
        r1 = logf - g0.astype(F32)
        g1 = r1.astype(BF16)
        g2 = (r1 - g1.astype(F32)).astype(BF16)
        ex = _dot(ncat, g0) + _dot(ncat, g1) + _dot(ncat, g2)
        for h in range(HGRN_HEADS):
            sl = slice(h * dk, (h + 1) * dk)
            eh = jnp.exp(ex[:, sl])
            qh = qf[:, sl]
            kh = kk[:, sl]
            vh = i_ref[0, pl.ds(r0, c), sl]
            st = state_ref[h]
            o = _dot_nt((qh * eh[0:c]).astype(BF16), st.astype(BF16))
            sc = jnp.where(eye, jnp.sum(qh * kh, axis=1, keepdims=True), 0.0)
            for l, msk in enumerate(level_masks):
                el = eh[(2 + l) * c:(3 + l) * c]
                p = _dot_nt((qh * el).astype(BF16), (kh * el).astype(BF16))
                sc = sc + jnp.where(msk, p, 0.0)
            o = o + _dot(sc.astype(BF16), vh)
            k_st = (kh * eh[c:2 * c]).astype(BF16)
            state_ref[h] = st * eh[c - 1:c] + _dot_tn(vh, k_st)
            gate = g_ref[0, pl.ds(r0, c), sl].astype(F32)
            y = _rms(o, onorm) * (gate * _sigmoid(gate))
            o_ref[0, pl.ds(r0, c), sl] = y.astype(o_ref.dtype)
        return carry

    lax.fori_loop(0, nchunks, chunk_body, 0)


def hgrn_group(proj, lb_logits, onorm, t_blk):
    b, s, _ = proj.shape
    w = HGRN_HEADS * HGRN_KDIM
    ncat = jnp.asarray(_hgrn_exponent_matrix(), BF16)
    col = lambda cb: pl.BlockSpec((1, t_blk, w), lambda bi, ti, cb=cb: (bi, ti, cb))
    return pl.pallas_call(
        functools.partial(_hgrn_kernel, nchunks=t_blk // HGRN_CHUNK),
        grid=(b, s // t_blk),
        in_specs=[col(0), col(1), col(2), col(3),
                  pl.BlockSpec(lb_logits.shape, lambda bi, ti: (0, 0)),
                  pl.BlockSpec((1, HGRN_KDIM), lambda bi, ti: (0, 0)),
                  pl.BlockSpec(ncat.shape, lambda bi, ti: (0, 0))],
        out_specs=pl.BlockSpec((1, t_blk, w), lambda bi, ti: (bi, ti, 0)),
        out_shape=jax.ShapeDtypeStruct((b, s, w), BF16),
        scratch_shapes=[pltpu.VMEM((HGRN_HEADS, HGRN_KDIM, HGRN_KDIM), F32)],
        compiler_params=_params("parallel", "arbitrary"),
        name="hgrn2",
    )(proj, proj, proj, proj, lb_logits, onorm, ncat)


def _dsa_prep_kernel(k_ref, v_ref, tail_ref, kg_ref, dup_ref, kn_ref, vt_ref, ikd_ref):
    kg = kg_ref[...]
    for h in range(ATTN_HEADS):
        sl = slice(h * ATTN_HEAD_DIM, (h + 1) * ATTN_HEAD_DIM)
        kn_ref[0, :, sl] = _rms(k_ref[0, :, sl].astype(F32), kg).astype(BF16)
        vt_ref[0, sl, :] = v_ref[0, :, sl].astype(F32).T.astype(BF16)
    ikd_ref[0] = _dot(tail_ref[0], dup_ref[...]).astype(BF16)


def dsa_prep(proj, knorm, tm):
    b, s, _ = proj.shape
    w = ATTN_HEADS * ATTN_HEAD_DIM
    dup = np.zeros((TAIL, LANES), np.float32)
    dup[np.arange(IDX_DIM), np.arange(IDX_DIM)] = 1.0
    dup[np.arange(IDX_DIM), np.arange(IDX_DIM) + IDX_DIM] = 1.0
    return pl.pallas_call(
        _dsa_prep_kernel,
        grid=(b, s // tm),
        in_specs=[pl.BlockSpec((1, tm, w), lambda bi, i: (bi, i, 5)),
                  pl.BlockSpec((1, tm, w), lambda bi, i: (bi, i, 6)),
                  pl.BlockSpec((1, tm, TAIL), lambda bi, i: (bi, i, 8 * w // TAIL)),
                  pl.BlockSpec((1, ATTN_HEAD_DIM), lambda bi, i: (0, 0)),
                  pl.BlockSpec((TAIL, LANES), lambda bi, i: (0, 0))],
        out_specs=[pl.BlockSpec((1, tm, w), lambda bi, i: (bi, i, 0)),
                   pl.BlockSpec((1, w, tm), lambda bi, i: (bi, 0, i)),
                   pl.BlockSpec((1, tm, LANES), lambda bi, i: (bi, i, 0))],
        out_shape=[jax.ShapeDtypeStruct((b, s, w), BF16),
                   jax.ShapeDtypeStruct((b, w, s), BF16),
                   jax.ShapeDtypeStruct((b, s, LANES), BF16)],
        compiler_params=_params("parallel", "parallel"),
        name="dsa_prep",
    )(proj, proj, proj, knorm, jnp.asarray(dup, BF16))


def _dsa_kernel(pi_ref, pj_ref, aq_ref, iq_ref, tail_ref, ikd_ref, kn_ref, vt_ref, qg_ref, eq_ref, esel_ref,
                smax_ref, o_ref, keys_ref, thr_ref, qn_ref, qw_ref, lo_ref, hi_ref, m_ref, l_ref, acc_ref, p_ref,
                *, tq, tk, topk):
    step = pl.program_id(1)
    i = pi_ref[step]
    j = pj_ref[step]
    dh = ATTN_HEAD_DIM

    def causal(jj):
        s_pos = jj * tk + lax.broadcasted_iota(jnp.int32, (tk, tq), 0)
        t_pos = i * tq + lax.broadcasted_iota(jnp.int32, (tk, tq), 1)
        return s_pos <= t_pos

    @pl.when(j == 0)
    def _index_and_select():
        qg = qg_ref[...] * (dh ** -0.5)
        for h in range(ATTN_HEADS):
            sl = slice(h * dh, (h + 1) * dh)
            qn_ref[:, sl] = _rms(aq_ref[0, :, sl].astype(F32), qg).astype(BF16)
        tail = tail_ref[0]
        ww = _dot(tail, eq_ref[...])
        qw = (iq_ref[0].astype(F32) * ww * (IDX_HEADS ** -0.5 * IDX_DIM ** -0.5)).astype(BF16)
        lane = lax.broadcasted_iota(jnp.int32, (tq, LANES), 1)
        zero = jnp.zeros((tq, LANES), BF16)
        for p in range(IDX_HEADS // 2):
            pair = qw[:, p * LANES:(p + 1) * LANES]
            qw_ref[2 * p] = jnp.where(lane < IDX_DIM, pair, zero)
            qw_ref[2 * p + 1] = jnp.where(lane >= IDX_DIM, pair, zero)
        w_t = _dot_nt(esel_ref[...], tail)
        lo_ref[...] = jnp.where(w_t > 0.0, 0.0, NEG_INF)
        hi_ref[...] = jnp.where(w_t > 0.0, jnp.inf, 0.0)

        def score_tile(jj, c):
            r0 = pl.multiple_of(jj * tk, tk)
            ik = ikd_ref[0, pl.ds(r0, tk), :]
            comb = jnp.zeros((tk, tq), F32)
            for h in range(IDX_HEADS):
                x = _dot_nt(ik, qw_ref[h])
                comb = comb + jnp.minimum(jnp.maximum(x, lo_ref[h:h + 1, :]), hi_ref[h:h + 1, :])
            comb = jnp.where(causal(jj), comb, NEG_INF)
            bits = pltpu.bitcast(comb, jnp.int32)
            keys_ref[jj] = jnp.where(bits < 0, bits ^ 0x7FFFFFFF, bits)
            return c
        lax.fori_loop(0, i + 1, score_tile, 0)

        def count_ge(tc):
            def body(jj, acc):
                ge = jnp.where(keys_ref[jj] >= tc, 1, 0)
                parts = [ge[r * 8:(r + 1) * 8] for r in range(tk // 8)]
                while len(parts) > 1:
                    parts = [a + b for a, b in zip(parts[0::2], parts[1::2])]
                return acc + parts[0]
            acc = lax.fori_loop(0, i + 1, body, jnp.zeros((8, tq), jnp.int32))
            return jnp.sum(acc.astype(F32), axis=0, keepdims=True)
        def unresolved(carry):
            bi, _, cnt = carry
            return jnp.logical_and(bi < 32, jnp.max(jnp.abs(cnt - topk)) > 0.0)
        def bit_steps(carry):
            bi, t, cnt = carry
            for k in range(4):
                tc = t + (jnp.int32(1) << (31 - bi - k))
                c = count_ge(tc)
                take = c >= topk
                t = jnp.where(take, tc, t)
                cnt = jnp.where(take, c, cnt)
            return bi + 4, t, cnt
        n_all = ((i + 1) * tk).astype(F32)
        _, t_fin, _ = lax.while_loop(unresolved, bit_steps,
                                     (jnp.int32(0), jnp.full((1, tq), INT_MIN, jnp.int32),
                                      jnp.full((1, tq), n_all, F32)))
        thr_ref[...] = t_fin

        m_ref[...] = jnp.full_like(m_ref, NEG_INF)
        l_ref[...] = jnp.zeros_like(l_ref)
        acc_ref[...] = jnp.zeros_like(acc_ref)

    sel = (keys_ref[j] >= thr_ref[...]) & causal(j)
    smax = smax_ref[0]
    fixed_shift = smax <= SOFTMAX_FIXED_SHIFT_MAX

    @pl.when(fixed_shift)
    def _attend_fixed_shift():
        bias = jnp.where(sel, -smax, NEG_INF)
        for h in range(ATTN_HEADS):
            sl = slice(h * dh, (h + 1) * dh)
            p = jnp.exp(_dot_nt(kn_ref[0, :, sl], qn_ref[:, sl]) + bias)
            parts = [p[r * 8:(r + 1) * 8] for r in range(tk // 8)]
            while len(parts) > 1:
                parts = [a + b for a, b in zip(parts[0::2], parts[1::2])]
            l_ref[h] += parts[0]
            p_ref[h] = p.astype(BF16)

    @pl.when(jnp.logical_and(fixed_shift, j <= i))
    def _weighted_values():
        for h in range(ATTN_HEADS):
            sl = slice(h * dh, (h + 1) * dh)
            acc_ref[h] += _dot(vt_ref[0, sl, :], p_ref[h])

    @pl.when(jnp.logical_not(fixed_shift))
    def _attend_running_max():
        bias = jnp.where(sel, 0.0, NEG_INF)
        m_all = m_ref[...]
        m_rows = []
        for h in range(ATTN_HEADS):
            sl = slice(h * dh, (h + 1) * dh)
            s = _dot_nt(kn_ref[0, :, sl], qn_ref[:, sl]) + bias
            m_old = m_all[h:h + 1, :]
            m_new = jnp.maximum(m_old, jnp.max(s, axis=0, keepdims=True))
            m_safe = jnp.where(m_new == NEG_INF, 0.0, m_new)
            alpha = jnp.exp(m_old - m_safe)
            p = jnp.exp(s - m_safe)
            l_ref[h, 0:1, :] = alpha * l_ref[h, 0:1, :] + jnp.sum(p, axis=0, keepdims=True)
            acc_ref[h] = alpha * acc_ref[h] + _dot(vt_ref[0, sl, :], p.astype(BF16))
            m_rows.append(m_new)
        m_ref[...] = jnp.concatenate(m_rows, axis=0)

    @pl.when(j == i)
    def _finish():
        for h in range(ATTN_HEADS):
            sl = slice(h * dh, (h + 1) * dh)
            l = jnp.sum(l_ref[h], axis=0, keepdims=True)
            o_ref[0, :, sl] = (acc_ref[h] / l).T.astype(o_ref.dtype)


def dsa_group(proj, kn, vt, ikd, qnorm, knorm, tq):
    b, s, _ = proj.shape
    tk = tq
    smax = (ATTN_HEAD_DIM ** 0.5 * jnp.max(jnp.abs(qnorm)) * jnp.max(jnp.abs(knorm))).reshape(1).astype(F32)
    nq = s // tq
    w = ATTN_HEADS * ATTN_HEAD_DIM
    topk = min(DSA_TOPK, s // 4)
    eq = np.zeros((TAIL, IDX_HEADS * IDX_DIM), np.float32)
    esel = np.zeros((IDX_HEADS, TAIL), np.float32)
    for h in range(IDX_HEADS):
        eq[IDX_DIM + h, h * IDX_DIM:(h + 1) * IDX_DIM] = 1.0
        esel[h, IDX_DIM + h] = 1.0
    pairs = [(i, j) for i in range(nq) for j in range(i + 1)]
    pi = jnp.asarray([p[0] for p in pairs], jnp.int32)
    pj = jnp.asarray([p[1] for p in pairs], jnp.int32)
    qblk = lambda cb: pl.BlockSpec((1, tq, w), lambda bi, st, pi, pj, cb=cb: (bi, pi[st], cb))
    const = lambda a: pl.BlockSpec(a.shape, lambda bi, st, pi, pj: (0,) * a.ndim)
    grid_spec = pltpu.PrefetchScalarGridSpec(
        num_scalar_prefetch=2,
        grid=(b, len(pairs)),
        in_specs=[qblk(4),
                  qblk(7),
                  pl.BlockSpec((1, tq, TAIL), lambda bi, st, pi, pj: (bi, pi[st], 8 * w // TAIL)),
                  pl.BlockSpec((1, s, LANES), lambda bi, st, pi, pj: (bi, 0, 0)),
                  pl.BlockSpec((1, tk, w), lambda bi, st, pi, pj: (bi, pj[st], 0)),
                  pl.BlockSpec((1, w, tk), lambda bi, st, pi, pj: (bi, 0, pj[st])),
                  const(qnorm), const(eq), const(esel),
                  pl.BlockSpec(memory_space=pltpu.SMEM)],
        out_specs=pl.BlockSpec((1, tq, w), lambda bi, st, pi, pj: (bi, pi[st], 0)),
        scratch_shapes=[pltpu.VMEM((nq, tk, tq), jnp.int32),
                        pltpu.VMEM((1, tq), jnp.int32),
                        pltpu.VMEM((tq, w), BF16),
                        pltpu.VMEM((IDX_HEADS, tq, LANES), BF16),
                        pltpu.VMEM((IDX_HEADS, tq), F32),
                        pltpu.VMEM((IDX_HEADS, tq), F32),
                        pltpu.VMEM((ATTN_HEADS, tq), F32),
                        pltpu.VMEM((ATTN_HEADS, 8, tq), F32),
                        pltpu.VMEM((ATTN_HEADS, ATTN_HEAD_DIM, tq), F32),
                        pltpu.VMEM((ATTN_HEADS, tk, tq), BF16)])
    return pl.pallas_call(
        functools.partial(_dsa_kernel, tq=tq, tk=tk, topk=topk),
        grid_spec=grid_spec,
        out_shape=jax.ShapeDtypeStruct((b, s, w), BF16),
        compiler_params=_params("parallel", "arbitrary"),
        name="dsa",
    )(pi, pj, proj, proj, proj, ikd, kn, vt, qnorm, jnp.asarray(eq, BF16), jnp.asarray(esel, BF16), smax)


def _out_proj_kernel(yh_ref, ya_ref, w1_ref, w2_ref, x_ref, o_ref):
    o_ref[...] = x_ref[...] + _dot(yh_ref[...], w1_ref[...]) + _dot(ya_ref[...], w2_ref[...])


def out_proj(yh, ya, w_out, x, tm, tn):
    m, kh = yh.shape
    n = w_out.shape[1]
    return pl.pallas_call(
        _out_proj_kernel,
        grid=(m // tm, n // tn),
        in_specs=[pl.BlockSpec((tm, kh), lambda i, j: (i, 0)),
                  pl.BlockSpec((tm, kh), lambda i, j: (i, 0)),
                  pl.BlockSpec((kh, tn), lambda i, j: (0, j)),
                  pl.BlockSpec((kh, tn), lambda i, j: (1, j)),
                  pl.BlockSpec((tm, tn), lambda i, j: (i, j))],
        out_specs=pl.BlockSpec((tm, tn), lambda i, j: (i, j)),
        out_shape=jax.ShapeDtypeStruct((m, n), F32),
        compiler_params=_params("parallel", "arbitrary"),
        name="out_proj",
    )(yh, ya, w_out, w_out, x)


def _mem_kv_kernel(mem_ref, mg_ref, wk_ref, wv_ref, kg_ref, k_ref, v_ref):
    memn = _rms(mem_ref[0], mg_ref[...]).astype(BF16)
    k = _dot(memn, wk_ref[...])
    kg = kg_ref[...]
    for h in range(CROSS_HEADS):
        sl = slice(h * CROSS_HEAD_DIM, (h + 1) * CROSS_HEAD_DIM)
        k_ref[0, :, sl] = _rms(k[:, sl], kg).astype(BF16)
    v_ref[0] = _dot(memn, wv_ref[...]).astype(BF16)


def mem_kv(mem, mem_norm, wk, wv, xk_norm):
    b, nm, d = mem.shape
    cw = wk.shape[1]
    full = lambda a: pl.BlockSpec(a.shape, lambda bi: (0,) * a.ndim)
    return pl.pallas_call(
        _mem_kv_kernel,
        grid=(b,),
        in_specs=[pl.BlockSpec((1, nm, d), lambda bi: (bi, 0, 0)),
                  full(mem_norm), full(wk), full(wv), full(xk_norm)],
        out_specs=[pl.BlockSpec((1, nm, cw), lambda bi: (bi, 0, 0))] * 2,
        out_shape=[jax.ShapeDtypeStruct((b, nm, cw), BF16)] * 2,
        compiler_params=_params("parallel"),
        name="mem_kv",
    )(mem, mem_norm, wk, wv, xk_norm)


def _cross_kernel(h_ref, ng_ref, wq_ref, qg_ref, k_ref, v_ref, wo_ref, o_ref, hn_ref, oc_ref):
    rows = h_ref.shape[1]
    dh = CROSS_HEAD_DIM
    step = 256
    gain = ng_ref[...]
    def body(r, c):
        r0 = pl.multiple_of(r * step, step)
        hn_ref[pl.ds(r0, step), :] = _rms(h_ref[0, pl.ds(r0, step), :], gain).astype(BF16)
        return c
    lax.fori_loop(0, rows // step, body, 0)
    q = _dot(hn_ref[...], wq_ref[...])
    qg = qg_ref[...] * (dh ** -0.5)
    for h in range(CROSS_HEADS):
        sl = slice(h * dh, (h + 1) * dh)
        qn = _rms(q[:, sl], qg).astype(BF16)
        s = _dot_nt(qn, k_ref[0, :, sl])
        p = jnp.exp(s - jnp.max(s, axis=1, keepdims=True))
        l = jnp.sum(p, axis=1, keepdims=True)
        oc_ref[:, sl] = (_dot(p.astype(BF16), v_ref[0, :, sl]) / l).astype(BF16)
    o_ref[0] = h_ref[0] + _dot(oc_ref[...], wo_ref[...])


def cross_attention(h, norm_cross, wq, xq_norm, kx, vx, wo, tm):
    b, s, d = h.shape
    nm, cw = kx.shape[1:]
    full = lambda a: pl.BlockSpec(a.shape, lambda bi, i: (0,) * a.ndim)
    return pl.pallas_call(
        _cross_kernel,
        grid=(b, s // tm),
        in_specs=[pl.BlockSpec((1, tm, d), lambda bi, i: (bi, i, 0)),
                  full(norm_cross), full(wq), full(xq_norm),
                  pl.BlockSpec((1, nm, cw), lambda bi, i: (bi, 0, 0)),
                  pl.BlockSpec((1, nm, cw), lambda bi, i: (bi, 0, 0)),
                  full(wo)],
        out_specs=pl.BlockSpec((1, tm, d), lambda bi, i: (bi, i, 0)),
        out_shape=jax.ShapeDtypeStruct((b, s, d), F32),
        scratch_shapes=[pltpu.VMEM((tm, d), BF16), pltpu.VMEM((tm, cw), BF16)],
        compiler_params=_params("parallel", "parallel"),
        name="cross_attn",
    )(h, norm_cross, wq, xq_norm, kx, vx, wo)


def _mlp_kernel(h_ref, g_ref, wu_ref, wd_ref, o_ref, hn_ref, acc_ref):
    j = pl.program_id(1)

    @pl.when(j == 0)
    def _():
        _norm_rows_to(h_ref, g_ref[...], hn_ref, h_ref.shape[0])
        acc_ref[...] = jnp.zeros_like(acc_ref)

    u = jnp.maximum(_dot(hn_ref[...], wu_ref[...]), 0.0)
    acc_ref[...] += _dot((u * u).astype(BF16), wd_ref[...])

    @pl.when(j == pl.num_programs(1) - 1)
    def _():
        o_ref[...] = h_ref[...] + acc_ref[...]


def mlp(h, gain, w_up, w_down, tm, tf):
    m, d = h.shape
    f = w_up.shape[1]
    return pl.pallas_call(
        _mlp_kernel,
        grid=(m // tm, f // tf),
        in_specs=[pl.BlockSpec((tm, d), lambda i, j: (i, 0)),
                  pl.BlockSpec((1, d), lambda i, j: (0, 0)),
                  pl.BlockSpec((d, tf), lambda i, j: (0, j)),
                  pl.BlockSpec((tf, d), lambda i, j: (j, 0))],
        out_specs=pl.BlockSpec((tm, d), lambda i, j: (i, 0)),
        out_shape=jax.ShapeDtypeStruct((m, d), F32),
        scratch_shapes=[pltpu.VMEM((tm, d), BF16), pltpu.VMEM((tm, d), F32)],
        compiler_params=_params("parallel", "arbitrary"),
        name="mlp",
    )(h, gain, w_up, w_down)


def _tile(n, pref):
    return pref if n % pref == 0 else n


def kernel(x, mem, norm_mix, w_in, hgrn_lb_logits, hgrn_onorm, attn_qnorm, attn_knorm, w_out,
           norm_cross, mem_norm, wq_x, wk_x, wv_x, wo_x, xq_norm, xk_norm,
           norm_mlp, w_up, w_down):
    b, s, d = x.shape
    n = b * s
    depth = w_in.shape[0]
    assert depth == 1
    l = 0
    in_width = w_in.shape[2]
    main_w = 8 * HGRN_HEADS * HGRN_KDIM
    assert in_width == main_w + IDX_DIM + IDX_HEADS
    pad_w = main_w + TAIL

    w_in_b = jnp.pad(w_in[l].astype(BF16), ((0, 0), (0, pad_w - in_width)))
    x2 = x.reshape(n, d)

    proj = norm_matmul(x2, norm_mix[l:l + 1], w_in_b, _tile(n, 1024), 768)
    proj3 = proj.reshape(b, s, pad_w)

    y_h = hgrn_group(proj3, hgrn_lb_logits, hgrn_onorm[l:l + 1], _tile(s, 256))
    kn, vt, ikd = dsa_prep(proj3, attn_knorm[l:l + 1], _tile(s, 512))
    y_a = dsa_group(proj3, kn, vt, ikd, attn_qnorm[l:l + 1], attn_knorm[l:l + 1], _tile(s, 256))

    h1 = out_proj(y_h.reshape(n, -1), y_a.reshape(n, -1), w_out[l].astype(BF16), x2, _tile(n, 1024), 1024)

    kx, vx = mem_kv(mem, mem_norm[l:l + 1], wk_x[l].astype(BF16), wv_x[l].astype(BF16), xk_norm[l:l + 1])
    h2 = cross_attention(h1.reshape(b, s, d), norm_cross[l:l + 1], wq_x[l].astype(BF16), xq_norm[l:l + 1],
                         kx, vx, wo_x[l].astype(BF16), _tile(s, 512))

    h3 = mlp(h2.reshape(n, d), norm_mlp[l:l + 1], w_up[l].astype(BF16), w_down[l].astype(BF16),
             _tile(n, 512), 512)
    return h3.reshape(b, s, d)
```

```python
import functools

import numpy as np
import jax
import jax.numpy as jnp
from jax import lax
from jax.experimental import pallas as pl
from jax.experimental.pallas import tpu as pltpu

F32 = jnp.float32
BF16 = jnp.bfloat16
EPS = 1e-6

LANES = 128
HGRN_HEADS = 8
HGRN_KDIM = 128
HGRN_CHUNK = 64
ATTN_HEADS = 8
ATTN_HEAD_DIM = 128
IDX_HEADS = 16
IDX_DIM = 64
DSA_TOPK = 256
CROSS_HEADS = 4
CROSS_HEAD_DIM = 128
TAIL = 256
VMEM_LIMIT = 56 * 1024 * 1024
NEG_INF = float("-inf")
INT_MIN = -(2 ** 31)
SOFTMAX_FIXED_SHIFT_MAX = 40.0


def _params(*sem, flags=None):
    return pltpu.CompilerParams(dimension_semantics=sem, vmem_limit_bytes=VMEM_LIMIT, flags=flags)


def _rms(x, gain):
    return x * lax.rsqrt(jnp.mean(x * x, axis=-1, keepdims=True) + EPS) * gain


def _sigmoid(x):
    return 1.0 / (1.0 + jnp.exp(-x))


def _dot(a, b):
    return jnp.dot(a, b, preferred_element_type=F32)


def _dot_nt(a, b):
    return lax.dot_general(a, b, (((1,), (1,)), ((), ())), preferred_element_type=F32)


def _dot_tn(a, b):
    return lax.dot_general(a, b, (((0,), (0,)), ((), ())), preferred_element_type=F32)


def _norm_rows_to(x_ref, gain, dst_ref, rows):
    step = 256
    def body(r, c):
        r0 = pl.multiple_of(r * step, step)
        x = x_ref[pl.ds(r0, step), :]
        dst_ref[pl.ds(r0, step), :] = _rms(x, gain).astype(BF16)
        return c
    lax.fori_loop(0, rows // step, body, 0)


def _cast_pad_kernel(w_ref, o_ref, *, valid_cols):
    tn = o_ref.shape[1]
    col = pl.program_id(0) * tn + lax.broadcasted_iota(jnp.int32, o_ref.shape, 1)
    o_ref[...] = jnp.where(col < valid_cols, w_ref[...], 0.0).astype(o_ref.dtype)


def cast_pad_cols(w, cols, tn):
    k, valid = w.shape
    return pl.pallas_call(
        functools.partial(_cast_pad_kernel, valid_cols=valid),
        grid=(cols // tn,),
        in_specs=[pl.BlockSpec((k, tn), lambda j: (0, j))],
        out_specs=pl.BlockSpec((k, tn), lambda j: (0, j)),
        out_shape=jax.ShapeDtypeStruct((k, cols), BF16),
        compiler_params=_params("parallel"),
        name="w_in_bf16",
    )(w)


def _norm_matmul_kernel(x_ref, g_ref, w_ref, o_ref, xn_ref):
    @pl.when(pl.program_id(1) == 0)
    def _():
        _norm_rows_to(x_ref, g_ref[...], xn_ref, x_ref.shape[0])
    o_ref[...] = _dot(xn_ref[...], w_ref[...]).astype(o_ref.dtype)


def norm_matmul(x, gain, w, tm, tn):
    m, k = x.shape
    n = w.shape[1]
    return pl.pallas_call(
        _norm_matmul_kernel,
        grid=(m // tm, n // tn),
        in_specs=[pl.BlockSpec((tm, k), lambda i, j: (i, 0)),
                  pl.BlockSpec((1, k), lambda i, j: (0, 0)),
                  pl.BlockSpec((k, tn), lambda i, j: (0, j))],
        out_specs=pl.BlockSpec((tm, tn), lambda i, j: (i, j)),
        out_shape=jax.ShapeDtypeStruct((m, n), BF16),
        scratch_shapes=[pltpu.VMEM((tm, k), BF16)],
        compiler_params=_params("parallel", "arbitrary"),
        name="in_proj",
    )(x, gain, w)


def _hgrn_exponent_matrix():
    c = HGRN_CHUNK
    t = np.arange(c)[:, None]
    j = np.arange(c)[None, :]
    mats = [(j <= t), (j > t)]
    h = c // 2
    while h >= 1:
        mid = (t // (2 * h)) * (2 * h) + h
        upper = (t % (2 * h)) >= h
        mats.append(np.where(upper, (j >= mid) & (j <= t), (j > t) & (j < mid)))
        h //= 2
    return np.concatenate(mats, axis=0).astype(np.float32)


def _hgrn_kernel(q_ref, f_ref, i_ref, g_ref, lbl_ref, on_ref, ncat_ref, o_ref,
                 state_ref, qi_ref, ks_ref, el_ref, qt_ref, kt_ref, sc_ref, scb_ref, *, nchunks):
    c = HGRN_CHUNK
    dk = HGRN_KDIM
    pw = 2 * dk
    npairs = HGRN_HEADS // 2
    nlev = 6

    @pl.when(pl.program_id(1) == 0)
    def _():
        state_ref[...] = jnp.zeros_like(state_ref)
        kt_ref[...] = jnp.zeros_like(kt_ref)

    lbl = lbl_ref[...]
    e = jnp.exp(lbl - jnp.max(lbl, axis=0, keepdims=True))
    lb = e[0:1] / jnp.sum(e, axis=0, keepdims=True)
    onorm = on_ref[...]
    ncat = ncat_ref[...]

    row = lax.broadcasted_iota(jnp.int32, (c, 2 * c), 0)
    lane = lax.broadcasted_iota(jnp.int32, (c, 2 * c), 1)
    col = lane & (c - 1)
    first = lane < c
    eye = row == col
    level_masks = []
    lg = nlev - 1
    while lg >= 0:
        level_masks.append(((row >> (lg + 1)) == (col >> (lg + 1)))
                           & (((row >> lg) & 1) == 1) & (((col >> lg) & 1) == 0))
        lg -= 1
    lane_p = lax.broadcasted_iota(jnp.int32, (c, pw), 1)
    zeros_st = jnp.zeros((dk, dk), BF16)

    def chunk_body(ci, carry):
        r0 = pl.multiple_of(ci * c, c)
        live = ci >= 0

        @pl.when(live)
        def _decay_factors():
            hq = q_ref[0, pl.ds(r0, c), :].astype(F32)
            hf = f_ref[0, pl.ds(r0, c), :].astype(F32)
            qf = hq * _sigmoid(hq) * (dk ** -0.5)
            f = lb + (1.0 - lb) * _sigmoid(hf)
            logf = jnp.log(f)
            kk = 1.0 - f
            g0 = logf.astype(BF16)
            g1 = (logf - g0.astype(F32)).astype(BF16)
            ex = _dot(ncat, g0) + _dot(ncat, g1)
            for p in range(npairs):
                sl = slice(p * pw, (p + 1) * pw)
                ep = jnp.exp(ex[:, sl])
                qp = qf[:, sl]
                kp = kk[:, sl]
                qi_ref[p] = (qp * ep[0:c]).astype(BF16)
                ks_ref[p] = (kp * ep[c:2 * c]).astype(BF16)
                el_ref[p] = ep[c - 1:c]
                for l in range(nlev):
                    el = ep[(2 + l) * c:(3 + l) * c]
                    qt_ref[p, l] = (qp * el).astype(BF16)
                    kl = (kp * el).astype(BF16)
                    kt_ref[p, l, 0:c, 0:dk] = kl[:, 0:dk]
                    kt_ref[p, l, c:2 * c, dk:pw] = kl[:, dk:pw]
                qk = qp * kp
                diag = jnp.where(first, jnp.sum(qk[:, 0:dk], axis=1, keepdims=True),
                                 jnp.sum(qk[:, dk:pw], axis=1, keepdims=True))
                sc_ref[p] = jnp.where(eye, diag, 0.0)

        @pl.when(live)
        def _intra_chunk_scores():
            for p in range(npairs):
                sc = sc_ref[p]
                for l, msk in enumerate(level_masks):
                    sc = sc + jnp.where(msk, _dot_nt(qt_ref[p, l], kt_ref[p, l]), 0.0)
                scb_ref[p] = sc.astype(BF16)

        @pl.when(live)
        def _outputs_and_state():
            for p in range(npairs):
                sl = slice(p * pw, (p + 1) * pw)
                vp = i_ref[0, pl.ds(r0, c), sl]
                st_a = state_ref[2 * p]
                st_b = state_ref[2 * p + 1]
                st_bd = jnp.concatenate(
                    [jnp.concatenate([st_a.astype(BF16), zeros_st], axis=1),
                     jnp.concatenate([zeros_st, st_b.astype(BF16)], axis=1)], axis=0)
                v_bd = jnp.concatenate([jnp.where(lane_p < dk, vp, jnp.zeros_like(vp)),
                                        jnp.where(lane_p >= dk, vp, jnp.zeros_like(vp))], axis=0)
                o = _dot_nt(qi_ref[p], st_bd) + _dot(scb_ref[p], v_bd)
                upd = _dot_tn(vp, ks_ref[p])
                el = el_ref[p]
                state_ref[2 * p] = st_a * el[:, 0:dk] + upd[0:dk, 0:dk]
                state_ref[2 * p + 1] = st_b * el[:, dk:pw] + upd[dk:pw, dk:pw]
                gate = g_ref[0, pl.ds(r0, c), sl].astype(F32)
                y = jnp.concatenate([_rms(o[:, 0:dk], onorm), _rms(o[:, dk:pw], onorm)], axis=1)
                o_ref[0, pl.ds(r0, c), sl] = (y * (gate * _sigmoid(gate))).astype(o_ref.dtype)
        return carry

    lax.fori_loop(0, nchunks, chunk_body, 0)


def hgrn_group(proj, lb_logits, onorm, t_blk):
    b, s, _ = proj.shape
    w = HGRN_HEADS * HGRN_KDIM
    ncat = jnp.asarray(_hgrn_exponent_matrix(), BF16)
    c, pw, npairs, nlev = HGRN_CHUNK, 2 * HGRN_KDIM, HGRN_HEADS // 2, 6
    col = lambda cb: pl.BlockSpec((1, t_blk, w), lambda bi, ti, cb=cb: (bi, ti, cb))
    return pl.pallas_call(
        functools.partial(_hgrn_kernel, nchunks=t_blk // HGRN_CHUNK),
        grid=(b, s // t_blk),
        in_specs=[col(0), col(1), col(2), col(3),
                  pl.BlockSpec(lb_logits.shape, lambda bi, ti: (0, 0)),
                  pl.BlockSpec((1, HGRN_KDIM), lambda bi, ti: (0, 0)),
                  pl.BlockSpec(ncat.shape, lambda bi, ti: (0, 0))],
        out_specs=pl.BlockSpec((1, t_blk, w), lambda bi, ti: (bi, ti, 0)),
        out_shape=jax.ShapeDtypeStruct((b, s, w), BF16),
        scratch_shapes=[pltpu.VMEM((HGRN_HEADS, HGRN_KDIM, HGRN_KDIM), F32),
                        pltpu.VMEM((npairs, c, pw), BF16),
                        pltpu.VMEM((npairs, c, pw), BF16),
                        pltpu.VMEM((npairs, 1, pw), F32),
                        pltpu.VMEM((npairs, nlev, c, pw), BF16),
                        pltpu.VMEM((npairs, nlev, 2 * c, pw), BF16),
                        pltpu.VMEM((npairs, c, 2 * c), F32),
                        pltpu.VMEM((npairs, c, 2 * c), BF16)],
        compiler_params=_params("parallel", "arbitrary"),
        name="hgrn2",
    )(proj, proj, proj, proj, lb_logits, onorm, ncat)


def _dsa_prep_kernel(k_ref, v_ref, tail_ref, kg_ref, dup_ref, kn_ref, vt_ref, ikd_ref):
    kg = kg_ref[...]
    for h in range(ATTN_HEADS):
        sl = slice(h * ATTN_HEAD_DIM, (h + 1) * ATTN_HEAD_DIM)
        kn_ref[0, :, sl] = _rms(k_ref[0, :, sl].astype(F32), kg).astype(BF16)
        vt_ref[0, 0, sl, :] = v_ref[0, :, sl].astype(F32).T.astype(BF16)
    ikd_ref[0] = _dot(tail_ref[0], dup_ref[...]).astype(BF16)


def dsa_prep(proj, knorm, tm):
    b, s, _ = proj.shape
    w = ATTN_HEADS * ATTN_HEAD_DIM
    dup = np.zeros((TAIL, LANES), np.float32)
    dup[np.arange(IDX_DIM), np.arange(IDX_DIM)] = 1.0
    dup[np.arange(IDX_DIM), np.arange(IDX_DIM) + IDX_DIM] = 1.0
    return pl.pallas_call(
        _dsa_prep_kernel,
        grid=(b, s // tm),
        in_specs=[pl.BlockSpec((1, tm, w), lambda bi, i: (bi, i, 5)),
                  pl.BlockSpec((1, tm, w), lambda bi, i: (bi, i, 6)),
                  pl.BlockSpec((1, tm, TAIL), lambda bi, i: (bi, i, 8 * w // TAIL)),
                  pl.BlockSpec((1, ATTN_HEAD_DIM), lambda bi, i: (0, 0)),
                  pl.BlockSpec((TAIL, LANES), lambda bi, i: (0, 0))],
        out_specs=[pl.BlockSpec((1, tm, w), lambda bi, i: (bi, i, 0)),
                   pl.BlockSpec((1, 1, w, tm), lambda bi, i: (bi, i, 0, 0)),
                   pl.BlockSpec((1, tm, LANES), lambda bi, i: (bi, i, 0))],
        out_shape=[jax.ShapeDtypeStruct((b, s, w), BF16),
                   jax.ShapeDtypeStruct((b, s // tm, w, tm), BF16),
                   jax.ShapeDtypeStruct((b, s, LANES), BF16)],
        compiler_params=_params("parallel", "parallel"),
        name="dsa_prep",
    )(proj, proj, proj, knorm, jnp.asarray(dup, BF16))


def _dsa_kernel(pi_ref, pj_ref, aq_ref, iq_ref, tail_ref, ikd_ref, kn_ref, vt_ref, qg_ref, eq_ref, esel_ref,
                smax_ref, o_ref, keys_ref, thr_ref, qn_ref, qw_ref, lo_ref, hi_ref, m_ref, l_ref, acc_ref, p_ref,
                *, tq, tk, topk):
    step = pl.program_id(1)
    i = pi_ref[step]
    j = pj_ref[step]
    dh = ATTN_HEAD_DIM

    def causal(jj):
        s_pos = jj * tk + lax.broadcasted_iota(jnp.int32, (tk, tq), 0)
        t_pos = i * tq + lax.broadcasted_iota(jnp.int32, (tk, tq), 1)
        return s_pos <= t_pos

    @pl.when(j == 0)
    def _index_and_select():
        qg = qg_ref[...] * (dh ** -0.5)
        for h in range(ATTN_HEADS):
            sl = slice(h * dh, (h + 1) * dh)
            qn_ref[:, sl] = _rms(aq_ref[0, :, sl].astype(F32), qg).astype(BF16)
        tail = tail_ref[0]
        ww = _dot(tail, eq_ref[...])
        qw = (iq_ref[0].astype(F32) * ww * (IDX_HEADS ** -0.5 * IDX_DIM ** -0.5)).astype(BF16)
        lane = lax.broadcasted_iota(jnp.int32, (tq, LANES), 1)
        zero = jnp.zeros((tq, LANES), BF16)
        for p in range(IDX_HEADS // 2):
            pair = qw[:, p * LANES:(p + 1) * LANES]
            qw_ref[2 * p] = jnp.where(lane < IDX_DIM, pair, zero)
            qw_ref[2 * p + 1] = jnp.where(lane >= IDX_DIM, pair, zero)
        w_t = _dot_nt(esel_ref[...], tail)
        lo_ref[...] = jnp.where(w_t > 0.0, 0.0, NEG_INF)
        hi_ref[...] = jnp.where(w_t > 0.0, jnp.inf, 0.0)

        def score_tile(jj, c):
            r0 = pl.multiple_of(jj * tk, tk)
            ik = ikd_ref[0, pl.ds(r0, tk), :]
            comb = jnp.zeros((tk, tq), F32)
            for h in range(IDX_HEADS):
                x = _dot_nt(ik, qw_ref[h])
                comb = comb + jnp.minimum(jnp.maximum(x, lo_ref[h:h + 1, :]), hi_ref[h:h + 1, :])
            keys_ref[jj] = jnp.where(causal(jj), comb, NEG_INF)
            return c
        lax.fori_loop(0, i + 1, score_tile, 0)

        def as_float(t):
            return pltpu.bitcast(jnp.where(t < 0, t ^ 0x7FFFFFFF, t), F32)
        def count_ge(tf):
            def body(jj, acc):
                ge = jnp.where(keys_ref[jj] >= tf, 1, 0)
                parts = [ge[r * 8:(r + 1) * 8] for r in range(tk // 8)]
                while len(parts) > 1:
                    parts = [a + b for a, b in zip(parts[0::2], parts[1::2])]
                return acc + parts[0]
            acc = lax.fori_loop(0, i + 1, body, jnp.zeros((8, tq), jnp.int32))
            return jnp.sum(acc.astype(F32), axis=0, keepdims=True)
        def unresolved(carry):
            bi, _, cnt = carry
            return jnp.logical_and(bi < 32, jnp.max(jnp.abs(cnt - topk)) > 0.0)
        def bit_steps(carry):
            bi, t, cnt = carry
            for k in range(4):
                tc = t + (jnp.int32(1) << (31 - bi - k))
                c = count_ge(as_float(tc))
                take = c >= topk
                t = jnp.where(take, tc, t)
                cnt = jnp.where(take, c, cnt)
            return bi + 4, t, cnt
        n_all = ((i + 1) * tk).astype(F32)
        _, t_fin, _ = lax.while_loop(unresolved, bit_steps,
                                     (jnp.int32(0), jnp.full((1, tq), INT_MIN, jnp.int32),
                                      jnp.full((1, tq), n_all, F32)))
        t_float = as_float(t_fin)
        thr_ref[...] = jnp.where(t_float != t_float, NEG_INF, t_float)

        m_ref[...] = jnp.full_like(m_ref, NEG_INF)
        l_ref[...] = jnp.zeros_like(l_ref)
        acc_ref[...] = jnp.zeros_like(acc_ref)

    sel = (keys_ref[j] >= thr_ref[...]) & causal(j)
    smax = smax_ref[0]
    fixed_shift = smax <= SOFTMAX_FIXED_SHIFT_MAX

    @pl.when(fixed_shift)
    def _attend_fixed_shift():
        bias = jnp.where(sel, -smax, NEG_INF)
        for h in range(ATTN_HEADS):
            sl = slice(h * dh, (h + 1) * dh)
            p = jnp.exp(_dot_nt(kn_ref[0, :, sl], qn_ref[:, sl]) + bias)
            parts = [p[r * 8:(r + 1) * 8] for r in range(tk // 8)]
            while len(parts) > 1:
                parts = [a + b for a, b in zip(parts[0::2], parts[1::2])]
            l_ref[h] += parts[0]
            p_ref[h] = p.astype(BF16)

    @pl.when(jnp.logical_and(fixed_shift, j <= i))
    def _weighted_values():
        for h in range(ATTN_HEADS):
            sl = slice(h * dh, (h + 1) * dh)
            acc_ref[h] += _dot(vt_ref[0, 0, sl, :], p_ref[h])

    @pl.when(jnp.logical_not(fixed_shift))
    def _attend_running_max():
        bias = jnp.where(sel, 0.0, NEG_INF)
        m_all = m_ref[...]
        m_rows = []
        for h in range(ATTN_HEADS):
            sl = slice(h * dh, (h + 1) * dh)
            s = _dot_nt(kn_ref[0, :, sl], qn_ref[:, sl]) + bias
            m_old = m_all[h:h + 1, :]
            m_new = jnp.maximum(m_old, jnp.max(s, axis=0, keepdims=True))
            m_safe = jnp.where(m_new == NEG_INF, 0.0, m_new)
            alpha = jnp.exp(m_old - m_safe)
            p = jnp.exp(s - m_safe)
            l_ref[h, 0:1, :] = alpha * l_ref[h, 0:1, :] + jnp.sum(p, axis=0, keepdims=True)
            acc_ref[h] = alpha * acc_ref[h] + _dot(vt_ref[0, 0, sl, :], p.astype(BF16))
            m_rows.append(m_new)
        m_ref[...] = jnp.concatenate(m_rows, axis=0)

    @pl.when(j == i)
    def _finish():
        for h in range(ATTN_HEADS):
            sl = slice(h * dh, (h + 1) * dh)
            l = jnp.sum(l_ref[h], axis=0, keepdims=True)
            o_ref[0, :, sl] = (acc_ref[h] / l).T.astype(o_ref.dtype)


def dsa_group(proj, kn, vt, ikd, qnorm, knorm, tq):
    b, s, _ = proj.shape
    tk = tq
    smax = (ATTN_HEAD_DIM ** 0.5 * jnp.max(jnp.abs(qnorm)) * jnp.max(jnp.abs(knorm))).reshape(1).astype(F32)
    nq = s // tq
    w = ATTN_HEADS * ATTN_HEAD_DIM
    topk = min(DSA_TOPK, s // 4)
    eq = np.zeros((TAIL, IDX_HEADS * IDX_DIM), np.float32)
    esel = np.zeros((IDX_HEADS, TAIL), np.float32)
    for h in range(IDX_HEADS):
        eq[IDX_DIM + h, h * IDX_DIM:(h + 1) * IDX_DIM] = 1.0
        esel[h, IDX_DIM + h] = 1.0
    pairs = [(i, j) for i in range(nq) for j in range(i + 1)]
    pi = jnp.asarray([p[0] for p in pairs], jnp.int32)
    pj = jnp.asarray([p[1] for p in pairs], jnp.int32)
    qblk = lambda cb: pl.BlockSpec((1, tq, w), lambda bi, st, pi, pj, cb=cb: (bi, pi[st], cb))
    const = lambda a: pl.BlockSpec(a.shape, lambda bi, st, pi, pj: (0,) * a.ndim)
    grid_spec = pltpu.PrefetchScalarGridSpec(
        num_scalar_prefetch=2,
        grid=(b, len(pairs)),
        in_specs=[qblk(4),
                  qblk(7),
                  pl.BlockSpec((1, tq, TAIL), lambda bi, st, pi, pj: (bi, pi[st], 8 * w // TAIL)),
                  pl.BlockSpec((1, s, LANES), lambda bi, st, pi, pj: (bi, 0, 0)),
                  pl.BlockSpec((1, tk, w), lambda bi, st, pi, pj: (bi, pj[st], 0)),
                  pl.BlockSpec((1, 1, w, tk), lambda bi, st, pi, pj: (bi, pj[st], 0, 0)),
                  const(qnorm), const(eq), const(esel),
                  pl.BlockSpec(memory_space=pltpu.SMEM)],
        out_specs=pl.BlockSpec((1, tq, w), lambda bi, st, pi, pj: (bi, pi[st], 0)),
        scratch_shapes=[pltpu.VMEM((nq, tk, tq), F32),
                        pltpu.VMEM((1, tq), F32),
                        pltpu.VMEM((tq, w), BF16),
                        pltpu.VMEM((IDX_HEADS, tq, LANES), BF16),
                        pltpu.VMEM((IDX_HEADS, tq), F32),
                        pltpu.VMEM((IDX_HEADS, tq), F32),
                        pltpu.VMEM((ATTN_HEADS, tq), F32),
                        pltpu.VMEM((ATTN_HEADS, 8, tq), F32),
                        pltpu.VMEM((ATTN_HEADS, ATTN_HEAD_DIM, tq), F32),
                        pltpu.VMEM((ATTN_HEADS, tk, tq), BF16)])
    return pl.pallas_call(
        functools.partial(_dsa_kernel, tq=tq, tk=tk, topk=topk),
        grid_spec=grid_spec,
        out_shape=jax.ShapeDtypeStruct((b, s, w), BF16),
        compiler_params=_params("parallel", "arbitrary"),
        name="dsa",
    )(pi, pj, proj, proj, proj, ikd, kn, vt, qnorm, jnp.asarray(eq, BF16), jnp.asarray(esel, BF16), smax)


def _out_proj_kernel(yh_ref, ya_ref, w1_ref, w2_ref, x_ref, o_ref):
    o_ref[...] = x_ref[...] + _dot(yh_ref[...], w1_ref[...]) + _dot(ya_ref[...], w2_ref[...])


def out_proj(yh, ya, w_out, x, tm, tn):
    m, kh = yh.shape
    n = w_out.shape[1]
    return pl.pallas_call(
        _out_proj_kernel,
        grid=(m // tm, n // tn),
        in_specs=[pl.BlockSpec((tm, kh), lambda i, j: (i, 0)),
                  pl.BlockSpec((tm, kh), lambda i, j: (i, 0)),
                  pl.BlockSpec((kh, tn), lambda i, j: (0, j)),
                  pl.BlockSpec((kh, tn), lambda i, j: (1, j)),
                  pl.BlockSpec((tm, tn), lambda i, j: (i, j))],
        out_specs=pl.BlockSpec((tm, tn), lambda i, j: (i, j)),
        out_shape=jax.ShapeDtypeStruct((m, n), F32),
        compiler_params=_params("parallel", "arbitrary"),
        name="out_proj",
    )(yh, ya, w_out, w_out, x)


def _mem_kv_kernel(mem_ref, mg_ref, wk_ref, wv_ref, kg_ref, k_ref, v_ref):
    memn = _rms(mem_ref[0], mg_ref[...]).astype(BF16)
    k = _dot(memn, wk_ref[...])
    kg = kg_ref[...]
    for h in range(CROSS_HEADS):
        sl = slice(h * CROSS_HEAD_DIM, (h + 1) * CROSS_HEAD_DIM)
        k_ref[0, :, sl] = _rms(k[:, sl], kg).astype(BF16)
    v_ref[0] = _dot(memn, wv_ref[...]).astype(BF16)


def mem_kv(mem, mem_norm, wk, wv, xk_norm):
    b, nm, d = mem.shape
    cw = wk.shape[1]
    full = lambda a: pl.BlockSpec(a.shape, lambda bi: (0,) * a.ndim)
    return pl.pallas_call(
        _mem_kv_kernel,
        grid=(b,),
        in_specs=[pl.BlockSpec((1, nm, d), lambda bi: (bi, 0, 0)),
                  full(mem_norm), full(wk), full(wv), full(xk_norm)],
        out_specs=[pl.BlockSpec((1, nm, cw), lambda bi: (bi, 0, 0))] * 2,
        out_shape=[jax.ShapeDtypeStruct((b, nm, cw), BF16)] * 2,
        compiler_params=_params("parallel"),
        name="mem_kv",
    )(mem, mem_norm, wk, wv, xk_norm)


def _cross_kernel(h_ref, ng_ref, wq_ref, qg_ref, k_ref, v_ref, wo_ref, o_ref, hn_ref, oc_ref):
    rows = h_ref.shape[1]
    dh = CROSS_HEAD_DIM
    step = 256
    gain = ng_ref[...]
    def body(r, c):
        r0 = pl.multiple_of(r * step, step)
        hn_ref[pl.ds(r0, step), :] = _rms(h_ref[0, pl.ds(r0, step), :], gain).astype(BF16)
        return c
    lax.fori_loop(0, rows // step, body, 0)
    q = _dot(hn_ref[...], wq_ref[...])
    qg = qg_ref[...] * (dh ** -0.5)
    for h in range(CROSS_HEADS):
        sl = slice(h * dh, (h + 1) * dh)
        qn = _rms(q[:, sl], qg).astype(BF16)
        s = _dot_nt(qn, k_ref[0, :, sl])
        p = jnp.exp(s - jnp.max(s, axis=1, keepdims=True))
        l = jnp.sum(p, axis=1, keepdims=True)
        oc_ref[:, sl] = (_dot(p.astype(BF16), v_ref[0, :, sl]) / l).astype(BF16)
    o_ref[0] = h_ref[0] + _dot(oc_ref[...], wo_ref[...])


def cross_attention(h, norm_cross, wq, xq_norm, kx, vx, wo, tm):
    b, s, d = h.shape
    nm, cw = kx.shape[1:]
    full = lambda a: pl.BlockSpec(a.shape, lambda bi, i: (0,) * a.ndim)
    return pl.pallas_call(
        _cross_kernel,
        grid=(b, s // tm),
        in_specs=[pl.BlockSpec((1, tm, d), lambda bi, i: (bi, i, 0)),
                  full(norm_cross), full(wq), full(xq_norm),
                  pl.BlockSpec((1, nm, cw), lambda bi, i: (bi, 0, 0)),
                  pl.BlockSpec((1, nm, cw), lambda bi, i: (bi, 0, 0)),
                  full(wo)],
        out_specs=pl.BlockSpec((1, tm, d), lambda bi, i: (bi, i, 0)),
        out_shape=jax.ShapeDtypeStruct((b, s, d), F32),
        scratch_shapes=[pltpu.VMEM((tm, d), BF16), pltpu.VMEM((tm, cw), BF16)],
        compiler_params=_params("parallel", "parallel"),
        name="cross_attn",
    )(h, norm_cross, wq, xq_norm, kx, vx, wo)


def _mlp_kernel(h_ref, g_ref, wu_ref, wd_ref, o_ref, hn_ref, acc_ref):
    j = pl.program_id(1)

    @pl.when(j == 0)
    def _():
        _norm_rows_to(h_ref, g_ref[...], hn_ref, h_ref.shape[0])
        acc_ref[...] = jnp.zeros_like(acc_ref)

    u = jnp.maximum(_dot(hn_ref[...], wu_ref[...]), 0.0)
    acc_ref[...] += _dot((u * u).astype(BF16), wd_ref[...])

    @pl.when(j == pl.num_programs(1) - 1)
    def _():
        o_ref[...] = h_ref[...] + acc_ref[...]


def mlp(h, gain, w_up, w_down, tm, tf):
    m, d = h.shape
    f = w_up.shape[1]
    return pl.pallas_call(
        _mlp_kernel,
        grid=(m // tm, f // tf),
        in_specs=[pl.BlockSpec((tm, d), lambda i, j: (i, 0)),
                  pl.BlockSpec((1, d), lambda i, j: (0, 0)),
                  pl.BlockSpec((d, tf), lambda i, j: (0, j)),
                  pl.BlockSpec((tf, d), lambda i, j: (j, 0))],
        out_specs=pl.BlockSpec((tm, d), lambda i, j: (i, 0)),
        out_shape=jax.ShapeDtypeStruct((m, d), F32),
        scratch_shapes=[pltpu.VMEM((tm, d), BF16), pltpu.VMEM((tm, d), F32)],
        compiler_params=_params("parallel", "arbitrary"),
        name="mlp",
    )(h, gain, w_up, w_down)


def _tile(n, pref):
    return pref if n % pref == 0 else n


def kernel(x, mem, norm_mix, w_in, hgrn_lb_logits, hgrn_onorm, attn_qnorm, attn_knorm, w_out,
           norm_cross, mem_norm, wq_x, wk_x, wv_x, wo_x, xq_norm, xk_norm,
           norm_mlp, w_up, w_down):
    b, s, d = x.shape
    n = b * s
    depth = w_in.shape[0]
    assert depth == 1
    l = 0
    in_width = w_in.shape[2]
    main_w = 8 * HGRN_HEADS * HGRN_KDIM
    assert in_width == main_w + IDX_DIM + IDX_HEADS
    pad_w = main_w + TAIL

    w_in_b = cast_pad_cols(w_in[l], pad_w, 768)
    x2 = x.reshape(n, d)

    proj = norm_matmul(x2, norm_mix[l:l + 1], w_in_b, _tile(n, 1024), 768)
    proj3 = proj.reshape(b, s, pad_w)

    y_h = hgrn_group(proj3, hgrn_lb_logits, hgrn_onorm[l:l + 1], _tile(s, 256))
    kn, vt, ikd = dsa_prep(proj3, attn_knorm[l:l + 1], _tile(s, 256))
    y_a = dsa_group(proj3, kn, vt, ikd, attn_qnorm[l:l + 1], attn_knorm[l:l + 1], _tile(s, 256))

    h1 = out_proj(y_h.reshape(n, -1), y_a.reshape(n, -1), w_out[l].astype(BF16), x2, _tile(n, 1024), 1024)

    kx, vx = mem_kv(mem, mem_norm[l:l + 1], wk_x[l].astype(BF16), wv_x[l].astype(BF16), xk_norm[l:l + 1])
    h2 = cross_attention(h1.reshape(b, s, d), norm_cross[l:l + 1], wq_x[l].astype(BF16), xq_norm[l:l + 1],
                         kx, vx, wo_x[l].astype(BF16), _tile(s, 512))

    h3 = mlp(h2.reshape(n, d), norm_mlp[l:l + 1], w_up[l].astype(BF16), w_down[l].astype(BF16),
             _tile(n, 512), 1024)
    return h3.reshape(b, s, d)
```

```python
import functools

import numpy as np
import jax
import jax.numpy as jnp
from jax import lax
from jax.experimental import pallas as pl
from jax.experimental.pallas import tpu as pltpu

F32 = jnp.float32
BF16 = jnp.bfloat16
EPS = 1e-6

LANES = 128
HGRN_HEADS = 8
HGRN_KDIM = 128
HGRN_CHUNK = 64
ATTN_HEADS = 8
ATTN_HEAD_DIM = 128
IDX_HEADS = 16
IDX_DIM = 64
DSA_TOPK = 256
CROSS_HEADS = 4
CROSS_HEAD_DIM = 128
TAIL = 256
VMEM_LIMIT = 56 * 1024 * 1024
NEG_INF = float("-inf")
INT_MIN = -(2 ** 31)
SOFTMAX_FIXED_SHIFT_MAX = 40.0


def _params(*sem, flags=None):
    return pltpu.CompilerParams(dimension_semantics=sem, vmem_limit_bytes=VMEM_LIMIT, flags=flags)


def _rms(x, gain):
    return x * lax.rsqrt(jnp.mean(x * x, axis=-1, keepdims=True) + EPS) * gain


def _sigmoid(x):
    return 1.0 / (1.0 + jnp.exp(-x))


def _dot(a, b):
    return jnp.dot(a, b, preferred_element_type=F32)


def _dot_nt(a, b):
    return lax.dot_general(a, b, (((1,), (1,)), ((), ())), preferred_element_type=F32)


def _dot_tn(a, b):
    return lax.dot_general(a, b, (((0,), (0,)), ((), ())), preferred_element_type=F32)


def _norm_rows_to(x_ref, gain, dst_ref, rows):
    step = 256
    def body(r, c):
        r0 = pl.multiple_of(r * step, step)
        x = x_ref[pl.ds(r0, step), :]
        dst_ref[pl.ds(r0, step), :] = _rms(x, gain).astype(BF16)
        return c
    lax.fori_loop(0, rows // step, body, 0)


def _cast_pad_kernel(w_ref, o_ref, *, valid_cols):
    tn = o_ref.shape[1]
    col = pl.program_id(0) * tn + lax.broadcasted_iota(jnp.int32, o_ref.shape, 1)
    o_ref[...] = jnp.where(col < valid_cols, w_ref[0], 0.0).astype(o_ref.dtype)


def cast_pad_cols(w, layer, cols, tn):
    _, k, valid = w.shape
    return pl.pallas_call(
        functools.partial(_cast_pad_kernel, valid_cols=valid),
        grid=(cols // tn,),
        in_specs=[pl.BlockSpec((1, k, tn), lambda j: (layer, 0, j))],
        out_specs=pl.BlockSpec((k, tn), lambda j: (0, j)),
        out_shape=jax.ShapeDtypeStruct((k, cols), BF16),
        compiler_params=_params("parallel"),
        name="w_in_bf16",
    )(w)


def _norm_matmul_kernel(x_ref, g_ref, w_ref, o_ref, xn_ref):
    @pl.when(pl.program_id(1) == 0)
    def _():
        _norm_rows_to(x_ref, g_ref[...], xn_ref, x_ref.shape[0])
    o_ref[...] = _dot(xn_ref[...], w_ref[...]).astype(o_ref.dtype)


def norm_matmul(x, gain, w, tm, tn):
    m, k = x.shape
    n = w.shape[1]
    return pl.pallas_call(
        _norm_matmul_kernel,
        grid=(m // tm, n // tn),
        in_specs=[pl.BlockSpec((tm, k), lambda i, j: (i, 0)),
                  pl.BlockSpec((1, k), lambda i, j: (0, 0)),
                  pl.BlockSpec((k, tn), lambda i, j: (0, j))],
        out_specs=pl.BlockSpec((tm, tn), lambda i, j: (i, j)),
        out_shape=jax.ShapeDtypeStruct((m, n), BF16),
        scratch_shapes=[pltpu.VMEM((tm, k), BF16)],
        compiler_params=_params("parallel", "arbitrary"),
        name="in_proj",
    )(x, gain, w)


def _hgrn_exponent_matrix():
    c = HGRN_CHUNK
    t = np.arange(c)[:, None]
    j = np.arange(c)[None, :]
    mats = [(j <= t), (j > t)]
    h = c // 2
    while h >= 1:
        mid = (t // (2 * h)) * (2 * h) + h
        upper = (t % (2 * h)) >= h
        mats.append(np.where(upper, (j >= mid) & (j <= t), (j > t) & (j < mid)))
        h //= 2
    return np.concatenate(mats, axis=0).astype(np.float32)


def _hgrn_kernel(q_ref, f_ref, i_ref, g_ref, lbl_ref, on_ref, ncat_ref, o_ref,
                 state_ref, qi_ref, ks_ref, el_ref, qt_ref, kt_ref, sc_ref, scb_ref, *, nchunks):
    c = HGRN_CHUNK
    dk = HGRN_KDIM
    pw = 2 * dk
    npairs = HGRN_HEADS // 2
    nlev = 6

    @pl.when(pl.program_id(1) == 0)
    def _():
        state_ref[...] = jnp.zeros_like(state_ref)
        kt_ref[...] = jnp.zeros_like(kt_ref)

    lbl = lbl_ref[...]
    e = jnp.exp(lbl - jnp.max(lbl, axis=0, keepdims=True))
    lb = e[0:1] / jnp.sum(e, axis=0, keepdims=True)
    onorm = on_ref[...]
    ncat = ncat_ref[...]

    row = lax.broadcasted_iota(jnp.int32, (c, 2 * c), 0)
    lane = lax.broadcasted_iota(jnp.int32, (c, 2 * c), 1)
    col = lane & (c - 1)
    first = lane < c
    eye = row == col
    level_masks = []
    lg = nlev - 1
    while lg >= 0:
        level_masks.append(((row >> (lg + 1)) == (col >> (lg + 1)))
                           & (((row >> lg) & 1) == 1) & (((col >> lg) & 1) == 0))
        lg -= 1
    lane_p = lax.broadcasted_iota(jnp.int32, (c, pw), 1)
    zeros_st = jnp.zeros((dk, dk), BF16)

    def chunk_body(ci, carry):
        r0 = pl.multiple_of(ci * c, c)
        live = ci >= 0

        @pl.when(live)
        def _decay_factors():
            hq = q_ref[0, pl.ds(r0, c), :].astype(F32)
            hf = f_ref[0, pl.ds(r0, c), :].astype(F32)
            qf = hq * _sigmoid(hq) * (dk ** -0.5)
            f = lb + (1.0 - lb) * _sigmoid(hf)
            logf = jnp.log(f)
            kk = 1.0 - f
            g0 = logf.astype(BF16)
            g1 = (logf - g0.astype(F32)).astype(BF16)
            ex = _dot(ncat, g0) + _dot(ncat, g1)
            for p in range(npairs):
                sl = slice(p * pw, (p + 1) * pw)
                ep = jnp.exp(ex[:, sl])
                qp = qf[:, sl]
                kp = kk[:, sl]
                qi_ref[p] = (qp * ep[0:c]).astype(BF16)
                ks_ref[p] = (kp * ep[c:2 * c]).astype(BF16)
                el_ref[p] = ep[c - 1:c]
                for l in range(nlev):
                    el = ep[(2 + l) * c:(3 + l) * c]
                    qt_ref[p, l] = (qp * el).astype(BF16)
                    kl = (kp * el).astype(BF16)
                    kt_ref[p, l, 0:c, 0:dk] = kl[:, 0:dk]
                    kt_ref[p, l, c:2 * c, dk:pw] = kl[:, dk:pw]
                qk = qp * kp
                diag = jnp.where(first, jnp.sum(qk[:, 0:dk], axis=1, keepdims=True),
                                 jnp.sum(qk[:, dk:pw], axis=1, keepdims=True))
                sc_ref[p] = jnp.where(eye, diag, 0.0)

        @pl.when(live)
        def _intra_chunk_scores():
            for p in range(npairs):
                sc = sc_ref[p]
                for l, msk in enumerate(level_masks):
                    sc = sc + jnp.where(msk, _dot_nt(qt_ref[p, l], kt_ref[p, l]), 0.0)
                scb_ref[p] = sc.astype(BF16)

        @pl.when(live)
        def _outputs_and_state():
            for p in range(npairs):
                sl = slice(p * pw, (p + 1) * pw)
                vp = i_ref[0, pl.ds(r0, c), sl]
                st_a = state_ref[2 * p]
                st_b = state_ref[2 * p + 1]
                st_bd = jnp.concatenate(
                    [jnp.concatenate([st_a.astype(BF16), zeros_st], axis=1),
                     jnp.concatenate([zeros_st, st_b.astype(BF16)], axis=1)], axis=0)
                v_bd = jnp.concatenate([jnp.where(lane_p < dk, vp, jnp.zeros_like(vp)),
                                        jnp.where(lane_p >= dk, vp, jnp.zeros_like(vp))], axis=0)
                o = _dot_nt(qi_ref[p], st_bd) + _dot(scb_ref[p], v_bd)
                upd = _dot_tn(vp, ks_ref[p])
                el = el_ref[p]
                state_ref[2 * p] = st_a * el[:, 0:dk] + upd[0:dk, 0:dk]
                state_ref[2 * p + 1] = st_b * el[:, dk:pw] + upd[dk:pw, dk:pw]
                gate = g_ref[0, pl.ds(r0, c), sl].astype(F32)
                y = jnp.concatenate([_rms(o[:, 0:dk], onorm), _rms(o[:, dk:pw], onorm)], axis=1)
                o_ref[0, pl.ds(r0, c), sl] = (y * (gate * _sigmoid(gate))).astype(o_ref.dtype)
        return carry

    lax.fori_loop(0, nchunks, chunk_body, 0)


def hgrn_group(proj, lb_logits, onorm, t_blk):
    b, s, _ = proj.shape
    w = HGRN_HEADS * HGRN_KDIM
    ncat = jnp.asarray(_hgrn_exponent_matrix(), BF16)
    c, pw, npairs, nlev = HGRN_CHUNK, 2 * HGRN_KDIM, HGRN_HEADS // 2, 6
    col = lambda cb: pl.BlockSpec((1, t_blk, w), lambda bi, ti, cb=cb: (bi, ti, cb))
    return pl.pallas_call(
        functools.partial(_hgrn_kernel, nchunks=t_blk // HGRN_CHUNK),
        grid=(b, s // t_blk),
        in_specs=[col(0), col(1), col(2), col(3),
                  pl.BlockSpec(lb_logits.shape, lambda bi, ti: (0, 0)),
                  pl.BlockSpec((1, HGRN_KDIM), lambda bi, ti: (0, 0)),
                  pl.BlockSpec(ncat.shape, lambda bi, ti: (0, 0))],
        out_specs=pl.BlockSpec((1, t_blk, w), lambda bi, ti: (bi, ti, 0)),
        out_shape=jax.ShapeDtypeStruct((b, s, w), BF16),
        scratch_shapes=[pltpu.VMEM((HGRN_HEADS, HGRN_KDIM, HGRN_KDIM), F32),
                        pltpu.VMEM((npairs, c, pw), BF16),
                        pltpu.VMEM((npairs, c, pw), BF16),
                        pltpu.VMEM((npairs, 1, pw), F32),
                        pltpu.VMEM((npairs, nlev, c, pw), BF16),
                        pltpu.VMEM((npairs, nlev, 2 * c, pw), BF16),
                        pltpu.VMEM((npairs, c, 2 * c), F32),
                        pltpu.VMEM((npairs, c, 2 * c), BF16)],
        compiler_params=_params("parallel", "arbitrary"),
        name="hgrn2",
    )(proj, proj, proj, proj, lb_logits, onorm, ncat)


def _dsa_prep_kernel(k_ref, v_ref, tail_ref, kg_ref, dup_ref, kn_ref, vt_ref, ikd_ref):
    kg = kg_ref[...]
    for h in range(ATTN_HEADS):
        sl = slice(h * ATTN_HEAD_DIM, (h + 1) * ATTN_HEAD_DIM)
        kn_ref[0, :, sl] = _rms(k_ref[0, :, sl].astype(F32), kg).astype(BF16)
        vt_ref[0, 0, sl, :] = v_ref[0, :, sl].astype(F32).T.astype(BF16)
    ikd_ref[0] = _dot(tail_ref[0], dup_ref[...]).astype(BF16)


def dsa_prep(proj, knorm, tm):
    b, s, _ = proj.shape
    w = ATTN_HEADS * ATTN_HEAD_DIM
    dup = np.zeros((TAIL, LANES), np.float32)
    dup[np.arange(IDX_DIM), np.arange(IDX_DIM)] = 1.0
    dup[np.arange(IDX_DIM), np.arange(IDX_DIM) + IDX_DIM] = 1.0
    return pl.pallas_call(
        _dsa_prep_kernel,
        grid=(b, s // tm),
        in_specs=[pl.BlockSpec((1, tm, w), lambda bi, i: (bi, i, 5)),
                  pl.BlockSpec((1, tm, w), lambda bi, i: (bi, i, 6)),
                  pl.BlockSpec((1, tm, TAIL), lambda bi, i: (bi, i, 8 * w // TAIL)),
                  pl.BlockSpec((1, ATTN_HEAD_DIM), lambda bi, i: (0, 0)),
                  pl.BlockSpec((TAIL, LANES), lambda bi, i: (0, 0))],
        out_specs=[pl.BlockSpec((1, tm, w), lambda bi, i: (bi, i, 0)),
                   pl.BlockSpec((1, 1, w, tm), lambda bi, i: (bi, i, 0, 0)),
                   pl.BlockSpec((1, tm, LANES), lambda bi, i: (bi, i, 0))],
        out_shape=[jax.ShapeDtypeStruct((b, s, w), BF16),
                   jax.ShapeDtypeStruct((b, s // tm, w, tm), BF16),
                   jax.ShapeDtypeStruct((b, s, LANES), BF16)],
        compiler_params=_params("parallel", "parallel"),
        name="dsa_prep",
    )(proj, proj, proj, knorm, jnp.asarray(dup, BF16))


def _dsa_kernel(pi_ref, pj_ref, aq_ref, iq_ref, tail_ref, ikd_ref, kn_ref, vt_ref, qg_ref, eq_ref, esel_ref,
                smax_ref, o_ref, keys_ref, thr_ref, qn_ref, qw_ref, lo_ref, hi_ref, m_ref, l_ref, acc_ref, p_ref,
                *, tq, tk, topk):
    step = pl.program_id(1)
    i = pi_ref[step]
    j = pj_ref[step]
    dh = ATTN_HEAD_DIM

    def causal(jj):
        s_pos = jj * tk + lax.broadcasted_iota(jnp.int32, (tk, tq), 0)
        t_pos = i * tq + lax.broadcasted_iota(jnp.int32, (tk, tq), 1)
        return s_pos <= t_pos

    @pl.when(j == 0)
    def _index_and_select():
        qg = qg_ref[...] * (dh ** -0.5)
        for h in range(ATTN_HEADS):
            sl = slice(h * dh, (h + 1) * dh)
            qn_ref[:, sl] = _rms(aq_ref[0, :, sl].astype(F32), qg).astype(BF16)
        tail = tail_ref[0]
        ww = _dot(tail, eq_ref[...])
        qw = (iq_ref[0].astype(F32) * ww * (IDX_HEADS ** -0.5 * IDX_DIM ** -0.5)).astype(BF16)
        lane = lax.broadcasted_iota(jnp.int32, (tq, LANES), 1)
        zero = jnp.zeros((tq, LANES), BF16)
        for p in range(IDX_HEADS // 2):
            pair = qw[:, p * LANES:(p + 1) * LANES]
            qw_ref[2 * p] = jnp.where(lane < IDX_DIM, pair, zero)
            qw_ref[2 * p + 1] = jnp.where(lane >= IDX_DIM, pair, zero)
        w_t = _dot_nt(esel_ref[...], tail)
        lo_ref[...] = jnp.where(w_t > 0.0, 0.0, NEG_INF)
        hi_ref[...] = jnp.where(w_t > 0.0, jnp.inf, 0.0)

        def score_tile(jj, c):
            r0 = pl.multiple_of(jj * tk, tk)
            ik = ikd_ref[0, pl.ds(r0, tk), :]
            comb = jnp.zeros((tk, tq), F32)
            for h in range(IDX_HEADS):
                x = _dot_nt(ik, qw_ref[h])
                comb = comb + jnp.minimum(jnp.maximum(x, lo_ref[h:h + 1, :]), hi_ref[h:h + 1, :])
            keys_ref[jj] = jnp.where(causal(jj), comb, NEG_INF)
            return c
        lax.fori_loop(0, i + 1, score_tile, 0)

        def as_float(t):
            return pltpu.bitcast(jnp.where(t < 0, t ^ 0x7FFFFFFF, t), F32)
        def count_ge(tf):
            def body(jj, acc):
                for r0 in range(0, tk, 128):
                    ge = jnp.where(keys_ref[jj, r0:r0 + 128, :] >= tf, 1, 0)
                    parts = [ge[r * 8:(r + 1) * 8] for r in range(16)]
                    while len(parts) > 1:
                        parts = [a + b for a, b in zip(parts[0::2], parts[1::2])]
                    acc = acc + parts[0]
                return acc
            acc = lax.fori_loop(0, i + 1, body, jnp.zeros((8, tq), jnp.int32))
            return jnp.sum(acc.astype(F32), axis=0, keepdims=True)
        def unresolved(carry):
            bi, _, cnt = carry
            return jnp.logical_and(bi < 32, jnp.max(jnp.abs(cnt - topk)) > 0.0)
        def bit_steps(carry):
            bi, t, cnt = carry
            for k in range(4):
                tc = t + (jnp.int32(1) << (31 - bi - k))
                c = count_ge(as_float(tc))
                take = c >= topk
                t = jnp.where(take, tc, t)
                cnt = jnp.where(take, c, cnt)
            return bi + 4, t, cnt
        n_all = ((i + 1) * tk).astype(F32)
        _, t_fin, _ = lax.while_loop(unresolved, bit_steps,
                                     (jnp.int32(0), jnp.full((1, tq), INT_MIN, jnp.int32),
                                      jnp.full((1, tq), n_all, F32)))
        t_float = as_float(t_fin)
        thr_ref[...] = jnp.where(t_float != t_float, NEG_INF, t_float)

        m_ref[...] = jnp.full_like(m_ref, NEG_INF)
        l_ref[...] = jnp.zeros_like(l_ref)
        acc_ref[...] = jnp.zeros_like(acc_ref)

    sel = (keys_ref[j] >= thr_ref[...]) & causal(j)
    smax = smax_ref[0]
    fixed_shift = smax <= SOFTMAX_FIXED_SHIFT_MAX

    @pl.when(fixed_shift)
    def _attend_fixed_shift():
        bias = jnp.where(sel, -smax, NEG_INF)
        for h in range(ATTN_HEADS):
            sl = slice(h * dh, (h + 1) * dh)
            p = jnp.exp(_dot_nt(kn_ref[0, :, sl], qn_ref[:, sl]) + bias)
            parts = [p[r * 8:(r + 1) * 8] for r in range(tk // 8)]
            while len(parts) > 1:
                parts = [a + b for a, b in zip(parts[0::2], parts[1::2])]
            l_ref[h] += parts[0]
            p_ref[h] = p.astype(BF16)

    @pl.when(jnp.logical_and(fixed_shift, j <= i))
    def _weighted_values():
        for h in range(ATTN_HEADS):
            sl = slice(h * dh, (h + 1) * dh)
            acc_ref[h] += _dot(vt_ref[0, 0, sl, :], p_ref[h])

    @pl.when(jnp.logical_not(fixed_shift))
    def _attend_running_max():
        bias = jnp.where(sel, 0.0, NEG_INF)
        m_all = m_ref[...]
        m_rows = []
        for h in range(ATTN_HEADS):
            sl = slice(h * dh, (h + 1) * dh)
            s = _dot_nt(kn_ref[0, :, sl], qn_ref[:, sl]) + bias
            m_old = m_all[h:h + 1, :]
            m_new = jnp.maximum(m_old, jnp.max(s, axis=0, keepdims=True))
            m_safe = jnp.where(m_new == NEG_INF, 0.0, m_new)
            alpha = jnp.exp(m_old - m_safe)
            p = jnp.exp(s - m_safe)
            l_ref[h, 0:1, :] = alpha * l_ref[h, 0:1, :] + jnp.sum(p, axis=0, keepdims=True)
            acc_ref[h] = alpha * acc_ref[h] + _dot(vt_ref[0, 0, sl, :], p.astype(BF16))
            m_rows.append(m_new)
        m_ref[...] = jnp.concatenate(m_rows, axis=0)

    @pl.when(j == i)
    def _finish():
        for h in range(ATTN_HEADS):
            sl = slice(h * dh, (h + 1) * dh)
            l = jnp.sum(l_ref[h], axis=0, keepdims=True)
            o_ref[0, :, sl] = (acc_ref[h] / l).T.astype(o_ref.dtype)


def dsa_group(proj, kn, vt, ikd, qnorm, knorm, tq):
    b, s, _ = proj.shape
    tk = tq
    smax = (ATTN_HEAD_DIM ** 0.5 * jnp.max(jnp.abs(qnorm)) * jnp.max(jnp.abs(knorm))).reshape(1).astype(F32)
    nq = s // tq
    w = ATTN_HEADS * ATTN_HEAD_DIM
    topk = min(DSA_TOPK, s // 4)
    eq = np.zeros((TAIL, IDX_HEADS * IDX_DIM), np.float32)
    esel = np.zeros((IDX_HEADS, TAIL), np.float32)
    for h in range(IDX_HEADS):
        eq[IDX_DIM + h, h * IDX_DIM:(h + 1) * IDX_DIM] = 1.0
        esel[h, IDX_DIM + h] = 1.0
    pairs = [(i, j) for i in range(nq) for j in range(i + 1)]
    pi = jnp.asarray([p[0] for p in pairs], jnp.int32)
    pj = jnp.asarray([p[1] for p in pairs], jnp.int32)
    qblk = lambda cb: pl.BlockSpec((1, tq, w), lambda bi, st, pi, pj, cb=cb: (bi, pi[st], cb))
    const = lambda a: pl.BlockSpec(a.shape, lambda bi, st, pi, pj: (0,) * a.ndim)
    grid_spec = pltpu.PrefetchScalarGridSpec(
        num_scalar_prefetch=2,
        grid=(b, len(pairs)),
        in_specs=[qblk(4),
                  qblk(7),
                  pl.BlockSpec((1, tq, TAIL), lambda bi, st, pi, pj: (bi, pi[st], 8 * w // TAIL)),
                  pl.BlockSpec((1, s, LANES), lambda bi, st, pi, pj: (bi, 0, 0)),
                  pl.BlockSpec((1, tk, w), lambda bi, st, pi, pj: (bi, pj[st], 0)),
                  pl.BlockSpec((1, 1, w, tk), lambda bi, st, pi, pj: (bi, pj[st], 0, 0)),
                  const(qnorm), const(eq), const(esel),
                  pl.BlockSpec(memory_space=pltpu.SMEM)],
        out_specs=pl.BlockSpec((1, tq, w), lambda bi, st, pi, pj: (bi, pi[st], 0)),
        scratch_shapes=[pltpu.VMEM((nq, tk, tq), F32),
                        pltpu.VMEM((1, tq), F32),
                        pltpu.VMEM((tq, w), BF16),
                        pltpu.VMEM((IDX_HEADS, tq, LANES), BF16),
                        pltpu.VMEM((IDX_HEADS, tq), F32),
                        pltpu.VMEM((IDX_HEADS, tq), F32),
                        pltpu.VMEM((ATTN_HEADS, tq), F32),
                        pltpu.VMEM((ATTN_HEADS, 8, tq), F32),
                        pltpu.VMEM((ATTN_HEADS, ATTN_HEAD_DIM, tq), F32),
                        pltpu.VMEM((ATTN_HEADS, tk, tq), BF16)])
    return pl.pallas_call(
        functools.partial(_dsa_kernel, tq=tq, tk=tk, topk=topk),
        grid_spec=grid_spec,
        out_shape=jax.ShapeDtypeStruct((b, s, w), BF16),
        compiler_params=_params("parallel", "arbitrary"),
        name="dsa",
    )(pi, pj, proj, proj, proj, ikd, kn, vt, qnorm, jnp.asarray(eq, BF16), jnp.asarray(esel, BF16), smax)


def _out_proj_kernel(yh_ref, ya_ref, w1_ref, w2_ref, x_ref, o_ref):
    o_ref[...] = x_ref[...] + _dot(yh_ref[...], w1_ref[...]) + _dot(ya_ref[...], w2_ref[...])


def out_proj(yh, ya, w_out, x, tm, tn):
    m, kh = yh.shape
    n = w_out.shape[1]
    return pl.pallas_call(
        _out_proj_kernel,
        grid=(m // tm, n // tn),
        in_specs=[pl.BlockSpec((tm, kh), lambda i, j: (i, 0)),
                  pl.BlockSpec((tm, kh), lambda i, j: (i, 0)),
                  pl.BlockSpec((kh, tn), lambda i, j: (0, j)),
                  pl.BlockSpec((kh, tn), lambda i, j: (1, j)),
                  pl.BlockSpec((tm, tn), lambda i, j: (i, j))],
        out_specs=pl.BlockSpec((tm, tn), lambda i, j: (i, j)),
        out_shape=jax.ShapeDtypeStruct((m, n), F32),
        compiler_params=_params("parallel", "arbitrary"),
        name="out_proj",
    )(yh, ya, w_out, w_out, x)


def _mem_kv_kernel(mem_ref, mg_ref, wk_ref, wv_ref, kg_ref, k_ref, v_ref):
    memn = _rms(mem_ref[0], mg_ref[...]).astype(BF16)
    k = _dot(memn, wk_ref[...])
    kg = kg_ref[...]
    for h in range(CROSS_HEADS):
        sl = slice(h * CROSS_HEAD_DIM, (h + 1) * CROSS_HEAD_DIM)
        k_ref[0, :, sl] = _rms(k[:, sl], kg).astype(BF16)
    v_ref[0] = _dot(memn, wv_ref[...]).astype(BF16)


def mem_kv(mem, mem_norm, wk, wv, xk_norm):
    b, nm, d = mem.shape
    cw = wk.shape[1]
    full = lambda a: pl.BlockSpec(a.shape, lambda bi: (0,) * a.ndim)
    return pl.pallas_call(
        _mem_kv_kernel,
        grid=(b,),
        in_specs=[pl.BlockSpec((1, nm, d), lambda bi: (bi, 0, 0)),
                  full(mem_norm), full(wk), full(wv), full(xk_norm)],
        out_specs=[pl.BlockSpec((1, nm, cw), lambda bi: (bi, 0, 0))] * 2,
        out_shape=[jax.ShapeDtypeStruct((b, nm, cw), BF16)] * 2,
        compiler_params=_params("parallel"),
        name="mem_kv",
    )(mem, mem_norm, wk, wv, xk_norm)


def _cross_kernel(h_ref, ng_ref, wq_ref, qg_ref, k_ref, v_ref, wo_ref, o_ref, hn_ref, oc_ref):
    rows = h_ref.shape[1]
    dh = CROSS_HEAD_DIM
    step = 256
    gain = ng_ref[...]
    def body(r, c):
        r0 = pl.multiple_of(r * step, step)
        hn_ref[pl.ds(r0, step), :] = _rms(h_ref[0, pl.ds(r0, step), :], gain).astype(BF16)
        return c
    lax.fori_loop(0, rows // step, body, 0)
    q = _dot(hn_ref[...], wq_ref[...])
    qg = qg_ref[...] * (dh ** -0.5)
    for h in range(CROSS_HEADS):
        sl = slice(h * dh, (h + 1) * dh)
        qn = _rms(q[:, sl], qg).astype(BF16)
        s = _dot_nt(qn, k_ref[0, :, sl])
        p = jnp.exp(s - jnp.max(s, axis=1, keepdims=True))
        l = jnp.sum(p, axis=1, keepdims=True)
        oc_ref[:, sl] = (_dot(p.astype(BF16), v_ref[0, :, sl]) / l).astype(BF16)
    o_ref[0] = h_ref[0] + _dot(oc_ref[...], wo_ref[...])


def cross_attention(h, norm_cross, wq, xq_norm, kx, vx, wo, tm):
    b, s, d = h.shape
    nm, cw = kx.shape[1:]
    full = lambda a: pl.BlockSpec(a.shape, lambda bi, i: (0,) * a.ndim)
    return pl.pallas_call(
        _cross_kernel,
        grid=(b, s // tm),
        in_specs=[pl.BlockSpec((1, tm, d), lambda bi, i: (bi, i, 0)),
                  full(norm_cross), full(wq), full(xq_norm),
                  pl.BlockSpec((1, nm, cw), lambda bi, i: (bi, 0, 0)),
                  pl.BlockSpec((1, nm, cw), lambda bi, i: (bi, 0, 0)),
                  full(wo)],
        out_specs=pl.BlockSpec((1, tm, d), lambda bi, i: (bi, i, 0)),
        out_shape=jax.ShapeDtypeStruct((b, s, d), F32),
        scratch_shapes=[pltpu.VMEM((tm, d), BF16), pltpu.VMEM((tm, cw), BF16)],
        compiler_params=_params("parallel", "parallel"),
        name="cross_attn",
    )(h, norm_cross, wq, xq_norm, kx, vx, wo)


def _mlp_kernel(h_ref, g_ref, wu_ref, wd_ref, o_ref, hn_ref):
    @pl.when(pl.program_id(1) == 0)
    def _():
        _norm_rows_to(h_ref, g_ref[...], hn_ref, h_ref.shape[0])
        o_ref[...] = h_ref[...]

    u = jnp.maximum(_dot(hn_ref[...], wu_ref[0].astype(BF16)), 0.0)
    o_ref[...] += _dot((u * u).astype(BF16), wd_ref[0].astype(BF16))


def mlp(h, gain, w_up, w_down, layer, tm, tf):
    m, d = h.shape
    f = w_up.shape[2]
    return pl.pallas_call(
        _mlp_kernel,
        grid=(m // tm, f // tf),
        in_specs=[pl.BlockSpec((tm, d), lambda i, j: (i, 0)),
                  pl.BlockSpec((1, d), lambda i, j: (0, 0)),
                  pl.BlockSpec((1, d, tf), lambda i, j: (layer, 0, j)),
                  pl.BlockSpec((1, tf, d), lambda i, j: (layer, j, 0))],
        out_specs=pl.BlockSpec((tm, d), lambda i, j: (i, 0), pipeline_mode=pl.Buffered(1)),
        out_shape=jax.ShapeDtypeStruct((m, d), F32),
        scratch_shapes=[pltpu.VMEM((tm, d), BF16)],
        compiler_params=_params("parallel", "arbitrary"),
        name="mlp",
    )(h, gain, w_up, w_down)


def _tile(n, pref):
    return pref if n % pref == 0 else n


def _tiles(n, s):
    return dict(
        proj_rows=_tile(n, 1024), proj_cols=768,
        hgrn_rows=_tile(s, 256),
        dsa_block=_tile(s, 512),
        out_rows=_tile(n, 1024), out_cols=1024,
        cross_rows=_tile(s, 512),
        mlp_rows=_tile(n, 1024), mlp_ff=512)


def kernel(x, mem, norm_mix, w_in, hgrn_lb_logits, hgrn_onorm, attn_qnorm, attn_knorm, w_out,
           norm_cross, mem_norm, wq_x, wk_x, wv_x, wo_x, xq_norm, xk_norm,
           norm_mlp, w_up, w_down):
    b, s, d = x.shape
    n = b * s
    depth = w_in.shape[0]
    assert depth == 1
    l = 0
    in_width = w_in.shape[2]
    main_w = 8 * HGRN_HEADS * HGRN_KDIM
    assert in_width == main_w + IDX_DIM + IDX_HEADS
    pad_w = main_w + TAIL

    t = _tiles(n, s)
    w_in_b = cast_pad_cols(w_in, l, pad_w, t["proj_cols"])
    x2 = x.reshape(n, d)

    proj = norm_matmul(x2, norm_mix[l:l + 1], w_in_b, t["proj_rows"], t["proj_cols"])
    proj3 = proj.reshape(b, s, pad_w)

    y_h = hgrn_group(proj3, hgrn_lb_logits, hgrn_onorm[l:l + 1], t["hgrn_rows"])
    kn, vt, ikd = dsa_prep(proj3, attn_knorm[l:l + 1], t["dsa_block"])
    y_a = dsa_group(proj3, kn, vt, ikd, attn_qnorm[l:l + 1], attn_knorm[l:l + 1], t["dsa_block"])

    h1 = out_proj(y_h.reshape(n, -1), y_a.reshape(n, -1), w_out[l].astype(BF16), x2,
                  t["out_rows"], t["out_cols"])

    kx, vx = mem_kv(mem, mem_norm[l:l + 1], wk_x[l].astype(BF16), wv_x[l].astype(BF16), xk_norm[l:l + 1])
    h2 = cross_attention(h1.reshape(b, s, d), norm_cross[l:l + 1], wq_x[l].astype(BF16), xq_norm[l:l + 1],
                         kx, vx, wo_x[l].astype(BF16), t["cross_rows"])

    h3 = mlp(h2.reshape(n, d), norm_mlp[l:l + 1], w_up, w_down, l, t["mlp_rows"], t["mlp_ff"])
    return h3.reshape(b, s, d)
```

```python
import functools

import numpy as np
import jax
import jax.numpy as jnp
from jax import lax
from jax.experimental import pallas as pl
from jax.experimental.pallas import tpu as pltpu

F32 = jnp.float32
BF16 = jnp.bfloat16
EPS = 1e-6

LANES = 128
HGRN_HEADS = 8
HGRN_KDIM = 128
HGRN_CHUNK = 64
ATTN_HEADS = 8
ATTN_HEAD_DIM = 128
IDX_HEADS = 16
IDX_DIM = 64
DSA_TOPK = 256
CROSS_HEADS = 4
CROSS_HEAD_DIM = 128
TAIL = 256
VMEM_LIMIT = 56 * 1024 * 1024
NEG_INF = float("-inf")
INT_MIN = -(2 ** 31)
SOFTMAX_FIXED_SHIFT_MAX = 40.0


def _params(*sem, flags=None):
    return pltpu.CompilerParams(dimension_semantics=sem, vmem_limit_bytes=VMEM_LIMIT, flags=flags)


def _rms(x, gain):
    return x * lax.rsqrt(jnp.mean(x * x, axis=-1, keepdims=True) + EPS) * gain


def _sigmoid(x):
    return 1.0 / (1.0 + jnp.exp(-x))


def _dot(a, b):
    return jnp.dot(a, b, preferred_element_type=F32)


def _dot_nt(a, b):
    return lax.dot_general(a, b, (((1,), (1,)), ((), ())), preferred_element_type=F32)


def _dot_tn(a, b):
    return lax.dot_general(a, b, (((0,), (0,)), ((), ())), preferred_element_type=F32)


def _norm_rows_to(x_ref, gain, dst_ref, rows):
    step = 256
    def body(r, c):
        r0 = pl.multiple_of(r * step, step)
        x = x_ref[pl.ds(r0, step), :]
        dst_ref[pl.ds(r0, step), :] = _rms(x, gain).astype(BF16)
        return c
    lax.fori_loop(0, rows // step, body, 0)


def _norm_matmul_kernel(x_ref, g_ref, w_ref, o_ref, xn_ref, *, valid_cols):
    j = pl.program_id(1)
    tn = o_ref.shape[1]

    @pl.when(j == 0)
    def _():
        _norm_rows_to(x_ref, g_ref[...], xn_ref, x_ref.shape[0])

    res = _dot(xn_ref[...], w_ref[...])

    @pl.when((j + 1) * tn <= valid_cols)
    def _():
        o_ref[...] = res.astype(o_ref.dtype)

    @pl.when((j + 1) * tn > valid_cols)
    def _():
        col = j * tn + lax.broadcasted_iota(jnp.int32, res.shape, 1)
        o_ref[...] = jnp.where(col < valid_cols, res, 0.0).astype(o_ref.dtype)


def norm_matmul(x, gain, w, n, tm, tn):
    m, k = x.shape
    return pl.pallas_call(
        functools.partial(_norm_matmul_kernel, valid_cols=w.shape[1]),
        grid=(m // tm, n // tn),
        in_specs=[pl.BlockSpec((tm, k), lambda i, j: (i, 0)),
                  pl.BlockSpec((1, k), lambda i, j: (0, 0)),
                  pl.BlockSpec((k, tn), lambda i, j: (0, j))],
        out_specs=pl.BlockSpec((tm, tn), lambda i, j: (i, j)),
        out_shape=jax.ShapeDtypeStruct((m, n), BF16),
        scratch_shapes=[pltpu.VMEM((tm, k), BF16)],
        compiler_params=_params("parallel", "arbitrary"),
        name="in_proj",
    )(x, gain, w)


def _hgrn_kernel(q_ref, f_ref, i_ref, g_ref, lbl_ref, on_ref, tril_ref, o_ref,
                 state_ref, a_ref, qi_ref, ks_ref, el_ref, qt_ref, kt_ref, sc_ref, scb_ref, *, nchunks):
    c = HGRN_CHUNK
    dk = HGRN_KDIM
    pw = 2 * dk
    npairs = HGRN_HEADS // 2
    nlev = 6

    @pl.when(pl.program_id(1) == 0)
    def _():
        state_ref[...] = jnp.zeros_like(state_ref)
        kt_ref[...] = jnp.zeros_like(kt_ref)

    lbl = lbl_ref[...]
    e = jnp.exp(lbl - jnp.max(lbl, axis=0, keepdims=True))
    lb = e[0:1] / jnp.sum(e, axis=0, keepdims=True)
    onorm = on_ref[...]
    tril = tril_ref[...]

    row = lax.broadcasted_iota(jnp.int32, (c, 2 * c), 0)
    lane = lax.broadcasted_iota(jnp.int32, (c, 2 * c), 1)
    col = lane & (c - 1)
    first = lane < c
    eye = row == col
    level_masks = []
    lg = nlev - 1
    while lg >= 0:
        level_masks.append(((row >> (lg + 1)) == (col >> (lg + 1)))
                           & (((row >> lg) & 1) == 1) & (((col >> lg) & 1) == 0))
        lg -= 1
    lane_p = lax.broadcasted_iota(jnp.int32, (c, pw), 1)
    odd_row = (lax.broadcasted_iota(jnp.int32, (c, pw), 0) & 1) == 1
    sub8 = lax.broadcasted_iota(jnp.int32, (8, pw), 0)
    zeros_st = jnp.zeros((dk, dk), BF16)

    def chunk_body(ci, carry):
        r0 = pl.multiple_of(ci * c, c)
        live = ci >= 0

        @pl.when(live)
        def _decay_factors():
            hq = q_ref[0, pl.ds(r0, c), :].astype(F32)
            hf = f_ref[0, pl.ds(r0, c), :].astype(F32)
            qf = hq * _sigmoid(hq) * (dk ** -0.5)
            f = lb + (1.0 - lb) * _sigmoid(hf)
            logf = jnp.log(f)
            kk = 1.0 - f
            g0 = logf.astype(BF16)
            g1 = (logf - g0.astype(F32)).astype(BF16)
            a_ref[...] = _dot(tril, g0) + _dot(tril, g1)
            for p in range(npairs):
                sl = slice(p * pw, (p + 1) * pw)
                a = a_ref[:, sl]
                row = lambda r, n: jnp.broadcast_to(a_ref[r:r + 1, sl], (n, pw))
                e0 = jnp.exp(a)
                qp = qf[:, sl]
                kp = kk[:, sl]
                qi_ref[p] = (qp * e0).astype(BF16)
                ks_ref[p] = (kp * jnp.exp(row(c - 1, c) - a)).astype(BF16)
                el_ref[p] = e0[c - 1:c]
                for l in range(nlev):
                    h = c >> (l + 1)
                    if h >= 4:
                        ref = jnp.concatenate([row(m * 2 * h + h - 1, 2 * h) for m in range(c // (2 * h))], axis=0)
                    elif h == 2:
                        ref = jnp.concatenate([jnp.where(sub8 < 4, row(8 * m + 1, 8), row(8 * m + 5, 8))
                                               for m in range(c // 8)], axis=0)
                    else:
                        ref = jnp.where(odd_row, pltpu.roll(a, 1, 0), a)
                    el = jnp.exp(-jnp.abs(a - ref))
                    qt_ref[p, l] = (qp * el).astype(BF16)
                    kl = (kp * el).astype(BF16)
                    kt_ref[p, l, 0:c, 0:dk] = kl[:, 0:dk]
                    kt_ref[p, l, c:2 * c, dk:pw] = kl[:, dk:pw]
                qk = qp * kp
                diag = jnp.where(first, jnp.sum(qk[:, 0:dk], axis=1, keepdims=True),
                                 jnp.sum(qk[:, dk:pw], axis=1, keepdims=True))
                sc_ref[p] = jnp.where(eye, diag, 0.0)

        @pl.when(live)
        def _intra_chunk_scores():
            for p in range(npairs):
                sc = sc_ref[p]
                for l, msk in enumerate(level_masks):
                    sc = sc + jnp.where(msk, _dot_nt(qt_ref[p, l], kt_ref[p, l]), 0.0)
                scb_ref[p] = sc.astype(BF16)

        @pl.when(live)
        def _outputs_and_state():
            for p in range(npairs):
                sl = slice(p * pw, (p + 1) * pw)
                vp = i_ref[0, pl.ds(r0, c), sl]
                st_a = state_ref[2 * p]
                st_b = state_ref[2 * p + 1]
                st_bd = jnp.concatenate(
                    [jnp.concatenate([st_a.astype(BF16), zeros_st], axis=1),
                     jnp.concatenate([zeros_st, st_b.astype(BF16)], axis=1)], axis=0)
                v_bd = jnp.concatenate([jnp.where(lane_p < dk, vp, jnp.zeros_like(vp)),
                                        jnp.where(lane_p >= dk, vp, jnp.zeros_like(vp))], axis=0)
                o = _dot_nt(qi_ref[p], st_bd) + _dot(scb_ref[p], v_bd)
                upd = _dot_tn(vp, ks_ref[p])
                el = el_ref[p]
                state_ref[2 * p] = st_a * el[:, 0:dk] + upd[0:dk, 0:dk]
                state_ref[2 * p + 1] = st_b * el[:, dk:pw] + upd[dk:pw, dk:pw]
                gate = g_ref[0, pl.ds(r0, c), sl].astype(F32)
                y = jnp.concatenate([_rms(o[:, 0:dk], onorm), _rms(o[:, dk:pw], onorm)], axis=1)
                o_ref[0, pl.ds(r0, c), sl] = (y * (gate * _sigmoid(gate))).astype(o_ref.dtype)
        return carry

    lax.fori_loop(0, nchunks, chunk_body, 0)


def hgrn_group(proj, lb_logits, onorm, t_blk):
    b, s, _ = proj.shape
    w = HGRN_HEADS * HGRN_KDIM
    tril = jnp.asarray(np.tril(np.ones((HGRN_CHUNK, HGRN_CHUNK), np.float32)), BF16)
    c, pw, npairs, nlev = HGRN_CHUNK, 2 * HGRN_KDIM, HGRN_HEADS // 2, 6
    col = lambda cb: pl.BlockSpec((1, t_blk, w), lambda bi, ti, cb=cb: (bi, ti, cb))
    return pl.pallas_call(
        functools.partial(_hgrn_kernel, nchunks=t_blk // HGRN_CHUNK),
        grid=(b, s // t_blk),
        in_specs=[col(0), col(1), col(2), col(3),
                  pl.BlockSpec(lb_logits.shape, lambda bi, ti: (0, 0)),
                  pl.BlockSpec((1, HGRN_KDIM), lambda bi, ti: (0, 0)),
                  pl.BlockSpec(tril.shape, lambda bi, ti: (0, 0))],
        out_specs=pl.BlockSpec((1, t_blk, w), lambda bi, ti: (bi, ti, 0)),
        out_shape=jax.ShapeDtypeStruct((b, s, w), BF16),
        scratch_shapes=[pltpu.VMEM((HGRN_HEADS, HGRN_KDIM, HGRN_KDIM), F32),
                        pltpu.VMEM((c, w), F32),
                        pltpu.VMEM((npairs, c, pw), BF16),
                        pltpu.VMEM((npairs, c, pw), BF16),
                        pltpu.VMEM((npairs, 1, pw), F32),
                        pltpu.VMEM((npairs, nlev, c, pw), BF16),
                        pltpu.VMEM((npairs, nlev, 2 * c, pw), BF16),
                        pltpu.VMEM((npairs, c, 2 * c), F32),
                        pltpu.VMEM((npairs, c, 2 * c), BF16)],
        compiler_params=_params("parallel", "arbitrary"),
        name="hgrn2",
    )(proj, proj, proj, proj, lb_logits, onorm, tril)


def _dsa_prep_kernel(k_ref, v_ref, tail_ref, kg_ref, dup_ref, kn_ref, vt_ref, ikd_ref):
    kg = kg_ref[...]
    for h in range(ATTN_HEADS):
        sl = slice(h * ATTN_HEAD_DIM, (h + 1) * ATTN_HEAD_DIM)
        kn_ref[0, :, sl] = _rms(k_ref[0, :, sl].astype(F32), kg).astype(BF16)
        vt_ref[0, 0, sl, :] = v_ref[0, :, sl].astype(F32).T.astype(BF16)
    ikd_ref[0] = _dot(tail_ref[0], dup_ref[...]).astype(BF16)


def dsa_prep(proj, knorm, tm):
    b, s, _ = proj.shape
    w = ATTN_HEADS * ATTN_HEAD_DIM
    dup = np.zeros((TAIL, LANES), np.float32)
    dup[np.arange(IDX_DIM), np.arange(IDX_DIM)] = 1.0
    dup[np.arange(IDX_DIM), np.arange(IDX_DIM) + IDX_DIM] = 1.0
    return pl.pallas_call(
        _dsa_prep_kernel,
        grid=(b, s // tm),
        in_specs=[pl.BlockSpec((1, tm, w), lambda bi, i: (bi, i, 5)),
                  pl.BlockSpec((1, tm, w), lambda bi, i: (bi, i, 6)),
                  pl.BlockSpec((1, tm, TAIL), lambda bi, i: (bi, i, 8 * w // TAIL)),
                  pl.BlockSpec((1, ATTN_HEAD_DIM), lambda bi, i: (0, 0)),
                  pl.BlockSpec((TAIL, LANES), lambda bi, i: (0, 0))],
        out_specs=[pl.BlockSpec((1, tm, w), lambda bi, i: (bi, i, 0)),
                   pl.BlockSpec((1, 1, w, tm), lambda bi, i: (bi, i, 0, 0)),
                   pl.BlockSpec((1, tm, LANES), lambda bi, i: (bi, i, 0))],
        out_shape=[jax.ShapeDtypeStruct((b, s, w), BF16),
                   jax.ShapeDtypeStruct((b, s // tm, w, tm), BF16),
                   jax.ShapeDtypeStruct((b, s, LANES), BF16)],
        compiler_params=_params("parallel", "parallel"),
        name="dsa_prep",
    )(proj, proj, proj, knorm, jnp.asarray(dup, BF16))


def _dsa_kernel(pi_ref, pj_ref, aq_ref, iq_ref, tail_ref, ikd_ref, kn_ref, vt_ref, qg_ref, eq_ref, esel_ref,
                smax_ref, o_ref, keys_ref, thr_ref, qn_ref, qw_ref, lo_ref, hi_ref, m_ref, l_ref, acc_ref, p_ref,
                *, tq, tk, topk):
    step = pl.program_id(1)
    i = pi_ref[step]
    j = pj_ref[step]
    dh = ATTN_HEAD_DIM

    def causal(jj):
        s_pos = jj * tk + lax.broadcasted_iota(jnp.int32, (tk, tq), 0)
        t_pos = i * tq + lax.broadcasted_iota(jnp.int32, (tk, tq), 1)
        return s_pos <= t_pos

    @pl.when(j == 0)
    def _index_and_select():
        qg = qg_ref[...] * (dh ** -0.5)
        for h in range(ATTN_HEADS):
            sl = slice(h * dh, (h + 1) * dh)
            qn_ref[:, sl] = _rms(aq_ref[0, :, sl].astype(F32), qg).astype(BF16)
        tail = tail_ref[0]
        ww = _dot(tail, eq_ref[...])
        qw = (iq_ref[0].astype(F32) * ww * (IDX_HEADS ** -0.5 * IDX_DIM ** -0.5)).astype(BF16)
        lane = lax.broadcasted_iota(jnp.int32, (tq, LANES), 1)
        zero = jnp.zeros((tq, LANES), BF16)
        for p in range(IDX_HEADS // 2):
            pair = qw[:, p * LANES:(p + 1) * LANES]
            qw_ref[2 * p] = jnp.where(lane < IDX_DIM, pair, zero)
            qw_ref[2 * p + 1] = jnp.where(lane >= IDX_DIM, pair, zero)
        w_t = _dot_nt(esel_ref[...], tail)
        lo_ref[...] = jnp.where(w_t > 0.0, 0.0, NEG_INF)
        hi_ref[...] = jnp.where(w_t > 0.0, jnp.inf, 0.0)

        def score_tile(jj, c):
            r0 = pl.multiple_of(jj * tk, tk)
            ik = ikd_ref[0, pl.ds(r0, tk), :]
            comb = jnp.zeros((tk, tq), F32)
            for h in range(IDX_HEADS):
                x = _dot_nt(ik, qw_ref[h])
                comb = comb + jnp.minimum(jnp.maximum(x, lo_ref[h:h + 1, :]), hi_ref[h:h + 1, :])
            keys_ref[jj] = jnp.where(causal(jj), comb, NEG_INF)
            return c
        lax.fori_loop(0, i + 1, score_tile, 0)

        def as_float(t):
            return pltpu.bitcast(jnp.where(t < 0, t ^ 0x7FFFFFFF, t), F32)
        def count_ge(tf):
            def body(jj, acc):
                for r0 in range(0, tk, 128):
                    ge = jnp.where(keys_ref[jj, r0:r0 + 128, :] >= tf, 1, 0)
                    parts = [ge[r * 8:(r + 1) * 8] for r in range(16)]
                    while len(parts) > 1:
                        parts = [a + b for a, b in zip(parts[0::2], parts[1::2])]
                    acc = acc + parts[0]
                return acc
            acc = lax.fori_loop(0, i + 1, body, jnp.zeros((8, tq), jnp.int32))
            return jnp.sum(acc.astype(F32), axis=0, keepdims=True)
        def unresolved(carry):
            bi, _, cnt = carry
            return jnp.logical_and(bi < 32, jnp.max(jnp.abs(cnt - topk)) > 0.0)
        def bit_steps(carry):
            bi, t, cnt = carry
            for k in range(4):
                tc = t + (jnp.int32(1) << (31 - bi - k))
                c = count_ge(as_float(tc))
                take = c >= topk
                t = jnp.where(take, tc, t)
                cnt = jnp.where(take, c, cnt)
            return bi + 4, t, cnt
        n_all = ((i + 1) * tk).astype(F32)
        _, t_fin, _ = lax.while_loop(unresolved, bit_steps,
                                     (jnp.int32(0), jnp.full((1, tq), INT_MIN, jnp.int32),
                                      jnp.full((1, tq), n_all, F32)))
        t_float = as_float(t_fin)
        thr_ref[...] = jnp.where(t_float != t_float, NEG_INF, t_float)

        m_ref[...] = jnp.full_like(m_ref, NEG_INF)
        l_ref[...] = jnp.zeros_like(l_ref)
        acc_ref[...] = jnp.zeros_like(acc_ref)

    sel = (keys_ref[j] >= thr_ref[...]) & causal(j)
    smax = smax_ref[0]
    fixed_shift = smax <= SOFTMAX_FIXED_SHIFT_MAX

    @pl.when(fixed_shift)
    def _attend_fixed_shift():
        bias = jnp.where(sel, -smax, NEG_INF)
        for h in range(ATTN_HEADS):
            sl = slice(h * dh, (h + 1) * dh)
            p = jnp.exp(_dot_nt(kn_ref[0, :, sl], qn_ref[:, sl]) + bias)
            parts = [p[r * 8:(r + 1) * 8] for r in range(tk // 8)]
            while len(parts) > 1:
                parts = [a + b for a, b in zip(parts[0::2], parts[1::2])]
            l_ref[h] += parts[0]
            p_ref[h] = p.astype(BF16)

    @pl.when(jnp.logical_and(fixed_shift, j <= i))
    def _weighted_values():
        for h in range(ATTN_HEADS):
            sl = slice(h * dh, (h + 1) * dh)
            acc_ref[h] += _dot(vt_ref[0, 0, sl, :], p_ref[h])

    @pl.when(jnp.logical_not(fixed_shift))
    def _attend_running_max():
        bias = jnp.where(sel, 0.0, NEG_INF)
        m_all = m_ref[...]
        m_rows = []
        for h in range(ATTN_HEADS):
            sl = slice(h * dh, (h + 1) * dh)
            s = _dot_nt(kn_ref[0, :, sl], qn_ref[:, sl]) + bias
            m_old = m_all[h:h + 1, :]
            m_new = jnp.maximum(m_old, jnp.max(s, axis=0, keepdims=True))
            m_safe = jnp.where(m_new == NEG_INF, 0.0, m_new)
            alpha = jnp.exp(m_old - m_safe)
            p = jnp.exp(s - m_safe)
            l_ref[h, 0:1, :] = alpha * l_ref[h, 0:1, :] + jnp.sum(p, axis=0, keepdims=True)
            acc_ref[h] = alpha * acc_ref[h] + _dot(vt_ref[0, 0, sl, :], p.astype(BF16))
            m_rows.append(m_new)
        m_ref[...] = jnp.concatenate(m_rows, axis=0)

    @pl.when(j == i)
    def _finish():
        for h in range(ATTN_HEADS):
            sl = slice(h * dh, (h + 1) * dh)
            l = jnp.sum(l_ref[h], axis=0, keepdims=True)
            o_ref[0, :, sl] = (acc_ref[h] / l).T.astype(o_ref.dtype)


def dsa_group(proj, kn, vt, ikd, qnorm, knorm, tq):
    b, s, _ = proj.shape
    tk = tq
    smax = (ATTN_HEAD_DIM ** 0.5 * jnp.max(jnp.abs(qnorm)) * jnp.max(jnp.abs(knorm))).reshape(1).astype(F32)
    nq = s // tq
    w = ATTN_HEADS * ATTN_HEAD_DIM
    topk = min(DSA_TOPK, s // 4)
    eq = np.zeros((TAIL, IDX_HEADS * IDX_DIM), np.float32)
    esel = np.zeros((IDX_HEADS, TAIL), np.float32)
    for h in range(IDX_HEADS):
        eq[IDX_DIM + h, h * IDX_DIM:(h + 1) * IDX_DIM] = 1.0
        esel[h, IDX_DIM + h] = 1.0
    pairs = [(i, j) for i in range(nq) for j in range(i + 1)]
    pi = jnp.asarray([p[0] for p in pairs], jnp.int32)
    pj = jnp.asarray([p[1] for p in pairs], jnp.int32)
    qblk = lambda cb: pl.BlockSpec((1, tq, w), lambda bi, st, pi, pj, cb=cb: (bi, pi[st], cb))
    const = lambda a: pl.BlockSpec(a.shape, lambda bi, st, pi, pj: (0,) * a.ndim)
    grid_spec = pltpu.PrefetchScalarGridSpec(
        num_scalar_prefetch=2,
        grid=(b, len(pairs)),
        in_specs=[qblk(4),
                  qblk(7),
                  pl.BlockSpec((1, tq, TAIL), lambda bi, st, pi, pj: (bi, pi[st], 8 * w // TAIL)),
                  pl.BlockSpec((1, s, LANES), lambda bi, st, pi, pj: (bi, 0, 0)),
                  pl.BlockSpec((1, tk, w), lambda bi, st, pi, pj: (bi, pj[st], 0)),
                  pl.BlockSpec((1, 1, w, tk), lambda bi, st, pi, pj: (bi, pj[st], 0, 0)),
                  const(qnorm), const(eq), const(esel),
                  pl.BlockSpec(memory_space=pltpu.SMEM)],
        out_specs=pl.BlockSpec((1, tq, w), lambda bi, st, pi, pj: (bi, pi[st], 0)),
        scratch_shapes=[pltpu.VMEM((nq, tk, tq), F32),
                        pltpu.VMEM((1, tq), F32),
                        pltpu.VMEM((tq, w), BF16),
                        pltpu.VMEM((IDX_HEADS, tq, LANES), BF16),
                        pltpu.VMEM((IDX_HEADS, tq), F32),
                        pltpu.VMEM((IDX_HEADS, tq), F32),
                        pltpu.VMEM((ATTN_HEADS, tq), F32),
                        pltpu.VMEM((ATTN_HEADS, 8, tq), F32),
                        pltpu.VMEM((ATTN_HEADS, ATTN_HEAD_DIM, tq), F32),
                        pltpu.VMEM((ATTN_HEADS, tk, tq), BF16)])
    return pl.pallas_call(
        functools.partial(_dsa_kernel, tq=tq, tk=tk, topk=topk),
        grid_spec=grid_spec,
        out_shape=jax.ShapeDtypeStruct((b, s, w), BF16),
        compiler_params=_params("parallel", "arbitrary"),
        name="dsa",
    )(pi, pj, proj, proj, proj, ikd, kn, vt, qnorm, jnp.asarray(eq, BF16), jnp.asarray(esel, BF16), smax)


def _out_proj_kernel(yh_ref, ya_ref, w1_ref, w2_ref, x_ref, o_ref):
    o_ref[...] = x_ref[...] + _dot(yh_ref[...], w1_ref[...]) + _dot(ya_ref[...], w2_ref[...])


def out_proj(yh, ya, w_out, x, tm, tn):
    m, kh = yh.shape
    n = w_out.shape[1]
    return pl.pallas_call(
        _out_proj_kernel,
        grid=(m // tm, n // tn),
        in_specs=[pl.BlockSpec((tm, kh), lambda i, j: (i, 0)),
                  pl.BlockSpec((tm, kh), lambda i, j: (i, 0)),
                  pl.BlockSpec((kh, tn), lambda i, j: (0, j)),
                  pl.BlockSpec((kh, tn), lambda i, j: (1, j)),
                  pl.BlockSpec((tm, tn), lambda i, j: (i, j))],
        out_specs=pl.BlockSpec((tm, tn), lambda i, j: (i, j)),
        out_shape=jax.ShapeDtypeStruct((m, n), F32),
        compiler_params=_params("parallel", "arbitrary"),
        name="out_proj",
    )(yh, ya, w_out, w_out, x)


def _mem_kv_kernel(mem_ref, mg_ref, wk_ref, wv_ref, kg_ref, k_ref, v_ref):
    memn = _rms(mem_ref[0], mg_ref[...]).astype(BF16)
    k = _dot(memn, wk_ref[...])
    kg = kg_ref[...]
    for h in range(CROSS_HEADS):
        sl = slice(h * CROSS_HEAD_DIM, (h + 1) * CROSS_HEAD_DIM)
        k_ref[0, :, sl] = _rms(k[:, sl], kg).astype(BF16)
    v_ref[0] = _dot(memn, wv_ref[...]).astype(BF16)


def mem_kv(mem, mem_norm, wk, wv, xk_norm):
    b, nm, d = mem.shape
    cw = wk.shape[1]
    full = lambda a: pl.BlockSpec(a.shape, lambda bi: (0,) * a.ndim)
    return pl.pallas_call(
        _mem_kv_kernel,
        grid=(b,),
        in_specs=[pl.BlockSpec((1, nm, d), lambda bi: (bi, 0, 0)),
                  full(mem_norm), full(wk), full(wv), full(xk_norm)],
        out_specs=[pl.BlockSpec((1, nm, cw), lambda bi: (bi, 0, 0))] * 2,
        out_shape=[jax.ShapeDtypeStruct((b, nm, cw), BF16)] * 2,
        compiler_params=_params("parallel"),
        name="mem_kv",
    )(mem, mem_norm, wk, wv, xk_norm)


def _cross_kernel(h_ref, ng_ref, wq_ref, qg_ref, k_ref, v_ref, wo_ref, o_ref, hn_ref, oc_ref):
    rows = h_ref.shape[1]
    dh = CROSS_HEAD_DIM
    step = 256
    gain = ng_ref[...]
    def body(r, c):
        r0 = pl.multiple_of(r * step, step)
        hn_ref[pl.ds(r0, step), :] = _rms(h_ref[0, pl.ds(r0, step), :], gain).astype(BF16)
        return c
    lax.fori_loop(0, rows // step, body, 0)
    q = _dot(hn_ref[...], wq_ref[...])
    qg = qg_ref[...] * (dh ** -0.5)
    for h in range(CROSS_HEADS):
        sl = slice(h * dh, (h + 1) * dh)
        qn = _rms(q[:, sl], qg).astype(BF16)
        s = _dot_nt(qn, k_ref[0, :, sl])
        p = jnp.exp(s - jnp.max(s, axis=1, keepdims=True))
        l = jnp.sum(p, axis=1, keepdims=True)
        oc_ref[:, sl] = (_dot(p.astype(BF16), v_ref[0, :, sl]) / l).astype(BF16)
    o_ref[0] = h_ref[0] + _dot(oc_ref[...], wo_ref[...])


def cross_attention(h, norm_cross, wq, xq_norm, kx, vx, wo, tm):
    b, s, d = h.shape
    nm, cw = kx.shape[1:]
    full = lambda a: pl.BlockSpec(a.shape, lambda bi, i: (0,) * a.ndim)
    return pl.pallas_call(
        _cross_kernel,
        grid=(b, s // tm),
        in_specs=[pl.BlockSpec((1, tm, d), lambda bi, i: (bi, i, 0)),
                  full(norm_cross), full(wq), full(xq_norm),
                  pl.BlockSpec((1, nm, cw), lambda bi, i: (bi, 0, 0)),
                  pl.BlockSpec((1, nm, cw), lambda bi, i: (bi, 0, 0)),
                  full(wo)],
        out_specs=pl.BlockSpec((1, tm, d), lambda bi, i: (bi, i, 0)),
        out_shape=jax.ShapeDtypeStruct((b, s, d), F32),
        scratch_shapes=[pltpu.VMEM((tm, d), BF16), pltpu.VMEM((tm, cw), BF16)],
        compiler_params=_params("parallel", "parallel"),
        name="cross_attn",
    )(h, norm_cross, wq, xq_norm, kx, vx, wo)


def _mlp_kernel(h_ref, g_ref, wu_ref, wd_ref, o_ref, hn_ref):
    @pl.when(pl.program_id(1) == 0)
    def _():
        _norm_rows_to(h_ref, g_ref[...], hn_ref, h_ref.shape[0])
        o_ref[...] = h_ref[...]

    u = jnp.maximum(_dot(hn_ref[...], wu_ref[0].astype(BF16)), 0.0)
    o_ref[...] += _dot((u * u).astype(BF16), wd_ref[0].astype(BF16))


def mlp(h, gain, w_up, w_down, layer, tm, tf):
    m, d = h.shape
    f = w_up.shape[2]
    return pl.pallas_call(
        _mlp_kernel,
        grid=(m // tm, f // tf),
        in_specs=[pl.BlockSpec((tm, d), lambda i, j: (i, 0)),
                  pl.BlockSpec((1, d), lambda i, j: (0, 0)),
                  pl.BlockSpec((1, d, tf), lambda i, j: (layer, 0, j)),
                  pl.BlockSpec((1, tf, d), lambda i, j: (layer, j, 0))],
        out_specs=pl.BlockSpec((tm, d), lambda i, j: (i, 0), pipeline_mode=pl.Buffered(1)),
        out_shape=jax.ShapeDtypeStruct((m, d), F32),
        scratch_shapes=[pltpu.VMEM((tm, d), BF16)],
        compiler_params=_params("parallel", "arbitrary"),
        name="mlp",
    )(h, gain, w_up, w_down)


def _tile(n, pref):
    return pref if n % pref == 0 else n


def _tiles(n, s):
    return dict(
        proj_rows=_tile(n, 1024), proj_cols=768,
        hgrn_rows=_tile(s, 256),
        dsa_block=_tile(s, 512),
        out_rows=_tile(n, 1024), out_cols=1024,
        cross_rows=_tile(s, 512),
        mlp_rows=_tile(n, 1024), mlp_ff=512)


def kernel(x, mem, norm_mix, w_in, hgrn_lb_logits, hgrn_onorm, attn_qnorm, attn_knorm, w_out,
           norm_cross, mem_norm, wq_x, wk_x, wv_x, wo_x, xq_norm, xk_norm,
           norm_mlp, w_up, w_down):
    b, s, d = x.shape
    n = b * s
    depth = w_in.shape[0]
    assert depth == 1
    l = 0
    in_width = w_in.shape[2]
    main_w = 8 * HGRN_HEADS * HGRN_KDIM
    assert in_width == main_w + IDX_DIM + IDX_HEADS
    pad_w = main_w + TAIL

    t = _tiles(n, s)
    x2 = x.reshape(n, d)

    proj = norm_matmul(x2, norm_mix[l:l + 1], w_in[l].astype(BF16), pad_w,
                       t["proj_rows"], t["proj_cols"])
    proj3 = proj.reshape(b, s, pad_w)

    y_h = hgrn_group(proj3, hgrn_lb_logits, hgrn_onorm[l:l + 1], t["hgrn_rows"])
    kn, vt, ikd = dsa_prep(proj3, attn_knorm[l:l + 1], t["dsa_block"])
    y_a = dsa_group(proj3, kn, vt, ikd, attn_qnorm[l:l + 1], attn_knorm[l:l + 1], t["dsa_block"])

    h1 = out_proj(y_h.reshape(n, -1), y_a.reshape(n, -1), w_out[l].astype(BF16), x2,
                  t["out_rows"], t["out_cols"])

    kx, vx = mem_kv(mem, mem_norm[l:l + 1], wk_x[l].astype(BF16), wv_x[l].astype(BF16), xk_norm[l:l + 1])
    h2 = cross_attention(h1.reshape(b, s, d), norm_cross[l:l + 1], wq_x[l].astype(BF16), xq_norm[l:l + 1],
                         kx, vx, wo_x[l].astype(BF16), t["cross_rows"])

    h3 = mlp(h2.reshape(n, d), norm_mlp[l:l + 1], w_up, w_down, l, t["mlp_rows"], t["mlp_ff"])
    return h3.reshape(b, s, d)
```

```python
import functools

import numpy as np
import jax
import jax.numpy as jnp
from jax import lax
from jax.experimental import pallas as pl
from jax.experimental.pallas import tpu as pltpu

F32 = jnp.float32
BF16 = jnp.bfloat16
EPS = 1e-6

LANES = 128
HGRN_HEADS = 8
HGRN_KDIM = 128
HGRN_CHUNK = 64
ATTN_HEADS = 8
ATTN_HEAD_DIM = 128
IDX_HEADS = 16
IDX_DIM = 64
DSA_TOPK = 256
CROSS_HEADS = 4
CROSS_HEAD_DIM = 128
TAIL = 256
VMEM_LIMIT = 56 * 1024 * 1024
NEG_INF = float("-inf")
INT_MIN = -(2 ** 31)
SOFTMAX_FIXED_SHIFT_MAX = 40.0


def _params(*sem, flags=None):
    return pltpu.CompilerParams(dimension_semantics=sem, vmem_limit_bytes=VMEM_LIMIT, flags=flags)


def _rms(x, gain):
    return x * lax.rsqrt(jnp.mean(x * x, axis=-1, keepdims=True) + EPS) * gain


def _sigmoid(x):
    return 1.0 / (1.0 + jnp.exp(-x))


def _dot(a, b):
    return jnp.dot(a, b, preferred_element_type=F32)


def _dot_nt(a, b):
    return lax.dot_general(a, b, (((1,), (1,)), ((), ())), preferred_element_type=F32)


def _dot_tn(a, b):
    return lax.dot_general(a, b, (((0,), (0,)), ((), ())), preferred_element_type=F32)


def _norm_rows_to(x_ref, gain, dst_ref, rows):
    step = 256
    def body(r, c):
        r0 = pl.multiple_of(r * step, step)
        x = x_ref[pl.ds(r0, step), :]
        dst_ref[pl.ds(r0, step), :] = _rms(x, gain).astype(BF16)
        return c
    lax.fori_loop(0, rows // step, body, 0)


def _norm_matmul_kernel(x_ref, g_ref, wt_ref, o_ref, xn_ref, *, valid_cols):
    j = pl.program_id(1)
    tn = o_ref.shape[1]

    @pl.when(j == 0)
    def _():
        _norm_rows_to(x_ref, g_ref[...], xn_ref, x_ref.shape[0])

    res = _dot_nt(xn_ref[...], wt_ref[...])

    @pl.when((j + 1) * tn <= valid_cols)
    def _():
        o_ref[...] = res.astype(o_ref.dtype)

    @pl.when((j + 1) * tn > valid_cols)
    def _():
        col = j * tn + lax.broadcasted_iota(jnp.int32, res.shape, 1)
        o_ref[...] = jnp.where(col < valid_cols, res, 0.0).astype(o_ref.dtype)


def norm_matmul(x, gain, wt, n, tm, tn):
    m, k = x.shape
    return pl.pallas_call(
        functools.partial(_norm_matmul_kernel, valid_cols=wt.shape[0]),
        grid=(m // tm, n // tn),
        in_specs=[pl.BlockSpec((tm, k), lambda i, j: (i, 0)),
                  pl.BlockSpec((1, k), lambda i, j: (0, 0)),
                  pl.BlockSpec((tn, k), lambda i, j: (j, 0))],
        out_specs=pl.BlockSpec((tm, tn), lambda i, j: (i, j)),
        out_shape=jax.ShapeDtypeStruct((m, n), BF16),
        scratch_shapes=[pltpu.VMEM((tm, k), BF16)],
        compiler_params=_params("parallel", "arbitrary"),
        name="in_proj",
    )(x, gain, wt)


def _hgrn_kernel(q_ref, f_ref, i_ref, g_ref, lbl_ref, on_ref, tril_ref, o_ref,
                 state_ref, a_ref, qi_ref, ks_ref, el_ref, qt_ref, kt_ref, sc_ref, scb_ref, *, nchunks):
    c = HGRN_CHUNK
    dk = HGRN_KDIM
    pw = 2 * dk
    npairs = HGRN_HEADS // 2
    nlev = 6

    @pl.when(pl.program_id(1) == 0)
    def _():
        state_ref[...] = jnp.zeros_like(state_ref)
        kt_ref[...] = jnp.zeros_like(kt_ref)

    lbl = lbl_ref[...]
    e = jnp.exp(lbl - jnp.max(lbl, axis=0, keepdims=True))
    lb = e[0:1] / jnp.sum(e, axis=0, keepdims=True)
    onorm = on_ref[...]
    tril = tril_ref[...]

    row = lax.broadcasted_iota(jnp.int32, (c, 2 * c), 0)
    lane = lax.broadcasted_iota(jnp.int32, (c, 2 * c), 1)
    col = lane & (c - 1)
    first = lane < c
    eye = row == col
    level_masks = []
    lg = nlev - 1
    while lg >= 0:
        level_masks.append(((row >> (lg + 1)) == (col >> (lg + 1)))
                           & (((row >> lg) & 1) == 1) & (((col >> lg) & 1) == 0))
        lg -= 1
    lane_p = lax.broadcasted_iota(jnp.int32, (c, pw), 1)
    odd_row = (lax.broadcasted_iota(jnp.int32, (c, pw), 0) & 1) == 1
    sub8 = lax.broadcasted_iota(jnp.int32, (8, pw), 0)
    zeros_st = jnp.zeros((dk, dk), BF16)

    def chunk_body(ci, carry):
        r0 = pl.multiple_of(ci * c, c)
        live = ci >= 0

        @pl.when(live)
        def _decay_factors():
            hq = q_ref[0, pl.ds(r0, c), :].astype(F32)
            hf = f_ref[0, pl.ds(r0, c), :].astype(F32)
            qf = hq * _sigmoid(hq) * (dk ** -0.5)
            f = lb + (1.0 - lb) * _sigmoid(hf)
            logf = jnp.log(f)
            kk = 1.0 - f
            g0 = logf.astype(BF16)
            g1 = (logf - g0.astype(F32)).astype(BF16)
            a_ref[...] = _dot(tril, g0) + _dot(tril, g1)
            for p in range(npairs):
                sl = slice(p * pw, (p + 1) * pw)
                a = a_ref[:, sl]
                row = lambda r, n: jnp.broadcast_to(a_ref[r:r + 1, sl], (n, pw))
                e0 = jnp.exp(a)
                qp = qf[:, sl]
                kp = kk[:, sl]
                qi_ref[p] = (qp * e0).astype(BF16)
                ks_ref[p] = (kp * jnp.exp(row(c - 1, c) - a)).astype(BF16)
                el_ref[p] = e0[c - 1:c]
                for l in range(nlev):
                    h = c >> (l + 1)
                    if h >= 4:
                        ref = jnp.concatenate([row(m * 2 * h + h - 1, 2 * h) for m in range(c // (2 * h))], axis=0)
                    elif h == 2:
                        ref = jnp.concatenate([jnp.where(sub8 < 4, row(8 * m + 1, 8), row(8 * m + 5, 8))
                                               for m in range(c // 8)], axis=0)
                    else:
                        ref = jnp.where(odd_row, pltpu.roll(a, 1, 0), a)
                    el = jnp.exp(-jnp.abs(a - ref))
                    qt_ref[p, l] = (qp * el).astype(BF16)
                    kl = (kp * el).astype(BF16)
                    kt_ref[p, l, 0:c, 0:dk] = kl[:, 0:dk]
                    kt_ref[p, l, c:2 * c, dk:pw] = kl[:, dk:pw]
                qk = qp * kp
                diag = jnp.where(first, jnp.sum(qk[:, 0:dk], axis=1, keepdims=True),
                                 jnp.sum(qk[:, dk:pw], axis=1, keepdims=True))
                sc_ref[p] = jnp.where(eye, diag, 0.0)

        @pl.when(live)
        def _intra_chunk_scores():
            for p in range(npairs):
                sc = sc_ref[p]
                for l, msk in enumerate(level_masks):
                    sc = sc + jnp.where(msk, _dot_nt(qt_ref[p, l], kt_ref[p, l]), 0.0)
                scb_ref[p] = sc.astype(BF16)

        @pl.when(live)
        def _outputs_and_state():
            for p in range(npairs):
                sl = slice(p * pw, (p + 1) * pw)
                vp = i_ref[0, pl.ds(r0, c), sl]
                st_a = state_ref[2 * p]
                st_b = state_ref[2 * p + 1]
                st_bd = jnp.concatenate(
                    [jnp.concatenate([st_a.astype(BF16), zeros_st], axis=1),
                     jnp.concatenate([zeros_st, st_b.astype(BF16)], axis=1)], axis=0)
                v_bd = jnp.concatenate([jnp.where(lane_p < dk, vp, jnp.zeros_like(vp)),
                                        jnp.where(lane_p >= dk, vp, jnp.zeros_like(vp))], axis=0)
                o = _dot_nt(qi_ref[p], st_bd) + _dot(scb_ref[p], v_bd)
                upd = _dot_tn(vp, ks_ref[p])
                el = el_ref[p]
                state_ref[2 * p] = st_a * el[:, 0:dk] + upd[0:dk, 0:dk]
                state_ref[2 * p + 1] = st_b * el[:, dk:pw] + upd[dk:pw, dk:pw]
                gate = g_ref[0, pl.ds(r0, c), sl].astype(F32)
                y = jnp.concatenate([_rms(o[:, 0:dk], onorm), _rms(o[:, dk:pw], onorm)], axis=1)
                o_ref[0, pl.ds(r0, c), sl] = (y * (gate * _sigmoid(gate))).astype(o_ref.dtype)
        return carry

    lax.fori_loop(0, nchunks, chunk_body, 0)


def hgrn_group(proj, lb_logits, onorm, t_blk):
    b, s, _ = proj.shape
    w = HGRN_HEADS * HGRN_KDIM
    tril = jnp.asarray(np.tril(np.ones((HGRN_CHUNK, HGRN_CHUNK), np.float32)), BF16)
    c, pw, npairs, nlev = HGRN_CHUNK, 2 * HGRN_KDIM, HGRN_HEADS // 2, 6
    col = lambda cb: pl.BlockSpec((1, t_blk, w), lambda bi, ti, cb=cb: (bi, ti, cb))
    return pl.pallas_call(
        functools.partial(_hgrn_kernel, nchunks=t_blk // HGRN_CHUNK),
        grid=(b, s // t_blk),
        in_specs=[col(0), col(1), col(2), col(3),
                  pl.BlockSpec(lb_logits.shape, lambda bi, ti: (0, 0)),
                  pl.BlockSpec((1, HGRN_KDIM), lambda bi, ti: (0, 0)),
                  pl.BlockSpec(tril.shape, lambda bi, ti: (0, 0))],
        out_specs=pl.BlockSpec((1, t_blk, w), lambda bi, ti: (bi, ti, 0)),
        out_shape=jax.ShapeDtypeStruct((b, s, w), BF16),
        scratch_shapes=[pltpu.VMEM((HGRN_HEADS, HGRN_KDIM, HGRN_KDIM), F32),
                        pltpu.VMEM((c, w), F32),
                        pltpu.VMEM((npairs, c, pw), BF16),
                        pltpu.VMEM((npairs, c, pw), BF16),
                        pltpu.VMEM((npairs, 1, pw), F32),
                        pltpu.VMEM((npairs, nlev, c, pw), BF16),
                        pltpu.VMEM((npairs, nlev, 2 * c, pw), BF16),
                        pltpu.VMEM((npairs, c, 2 * c), F32),
                        pltpu.VMEM((npairs, c, 2 * c), BF16)],
        compiler_params=_params("parallel", "arbitrary"),
        name="hgrn2",
    )(proj, proj, proj, proj, lb_logits, onorm, tril)


def _dsa_prep_kernel(k_ref, v_ref, tail_ref, kg_ref, dup_ref, kn_ref, vt_ref, ikd_ref):
    kg = kg_ref[...]
    for h in range(ATTN_HEADS):
        sl = slice(h * ATTN_HEAD_DIM, (h + 1) * ATTN_HEAD_DIM)
        kn_ref[0, :, sl] = _rms(k_ref[0, :, sl].astype(F32), kg).astype(BF16)
        vt_ref[0, 0, sl, :] = v_ref[0, :, sl].astype(F32).T.astype(BF16)
    ikd_ref[0] = _dot(tail_ref[0], dup_ref[...]).astype(BF16)


def dsa_prep(proj, knorm, tm):
    b, s, _ = proj.shape
    w = ATTN_HEADS * ATTN_HEAD_DIM
    dup = np.zeros((TAIL, LANES), np.float32)
    dup[np.arange(IDX_DIM), np.arange(IDX_DIM)] = 1.0
    dup[np.arange(IDX_DIM), np.arange(IDX_DIM) + IDX_DIM] = 1.0
    return pl.pallas_call(
        _dsa_prep_kernel,
        grid=(b, s // tm),
        in_specs=[pl.BlockSpec((1, tm, w), lambda bi, i: (bi, i, 5)),
                  pl.BlockSpec((1, tm, w), lambda bi, i: (bi, i, 6)),
                  pl.BlockSpec((1, tm, TAIL), lambda bi, i: (bi, i, 8 * w // TAIL)),
                  pl.BlockSpec((1, ATTN_HEAD_DIM), lambda bi, i: (0, 0)),
                  pl.BlockSpec((TAIL, LANES), lambda bi, i: (0, 0))],
        out_specs=[pl.BlockSpec((1, tm, w), lambda bi, i: (bi, i, 0)),
                   pl.BlockSpec((1, 1, w, tm), lambda bi, i: (bi, i, 0, 0)),
                   pl.BlockSpec((1, tm, LANES), lambda bi, i: (bi, i, 0))],
        out_shape=[jax.ShapeDtypeStruct((b, s, w), BF16),
                   jax.ShapeDtypeStruct((b, s // tm, w, tm), BF16),
                   jax.ShapeDtypeStruct((b, s, LANES), BF16)],
        compiler_params=_params("parallel", "parallel"),
        name="dsa_prep",
    )(proj, proj, proj, knorm, jnp.asarray(dup, BF16))


def _dsa_kernel(pi_ref, pj_ref, aq_ref, iq_ref, tail_ref, ikd_ref, kn_ref, vt_ref, qg_ref, eq_ref, esel_ref,
                smax_ref, o_ref, keys_ref, thr_ref, qn_ref, qw_ref, lo_ref, hi_ref, m_ref, l_ref, acc_ref, p_ref,
                *, tq, tk, topk):
    step = pl.program_id(1)
    i = pi_ref[step]
    j = pj_ref[step]
    dh = ATTN_HEAD_DIM

    def causal(jj):
        s_pos = jj * tk + lax.broadcasted_iota(jnp.int32, (tk, tq), 0)
        t_pos = i * tq + lax.broadcasted_iota(jnp.int32, (tk, tq), 1)
        return s_pos <= t_pos

    @pl.when(j == 0)
    def _index_and_select():
        qg = qg_ref[...] * (dh ** -0.5)
        for h in range(ATTN_HEADS):
            sl = slice(h * dh, (h + 1) * dh)
            qn_ref[:, sl] = _rms(aq_ref[0, :, sl].astype(F32), qg).astype(BF16)
        tail = tail_ref[0]
        ww = _dot(tail, eq_ref[...])
        qw = (iq_ref[0].astype(F32) * ww * (IDX_HEADS ** -0.5 * IDX_DIM ** -0.5)).astype(BF16)
        lane = lax.broadcasted_iota(jnp.int32, (tq, LANES), 1)
        zero = jnp.zeros((tq, LANES), BF16)
        for p in range(IDX_HEADS // 2):
            pair = qw[:, p * LANES:(p + 1) * LANES]
            qw_ref[2 * p] = jnp.where(lane < IDX_DIM, pair, zero)
            qw_ref[2 * p + 1] = jnp.where(lane >= IDX_DIM, pair, zero)
        w_t = _dot_nt(esel_ref[...], tail)
        lo_ref[...] = jnp.where(w_t > 0.0, 0.0, NEG_INF)
        hi_ref[...] = jnp.where(w_t > 0.0, jnp.inf, 0.0)

        def score_tile(jj, c):
            r0 = pl.multiple_of(jj * tk, tk)
            ik = ikd_ref[0, pl.ds(r0, tk), :]
            comb = jnp.zeros((tk, tq), F32)
            for h in range(IDX_HEADS):
                x = _dot_nt(ik, qw_ref[h])
                comb = comb + jnp.minimum(jnp.maximum(x, lo_ref[h:h + 1, :]), hi_ref[h:h + 1, :])
            keys_ref[jj] = jnp.where(causal(jj), comb, NEG_INF)
            return c
        lax.fori_loop(0, i + 1, score_tile, 0)

        def as_float(t):
            return pltpu.bitcast(jnp.where(t < 0, t ^ 0x7FFFFFFF, t), F32)
        def count_ge(tf):
            def body(jj, acc):
                for r0 in range(0, tk, 128):
                    ge = jnp.where(keys_ref[jj, r0:r0 + 128, :] >= tf, 1, 0)
                    parts = [ge[r * 8:(r + 1) * 8] for r in range(16)]
                    while len(parts) > 1:
                        parts = [a + b for a, b in zip(parts[0::2], parts[1::2])]
                    acc = acc + parts[0]
                return acc
            acc = lax.fori_loop(0, i + 1, body, jnp.zeros((8, tq), jnp.int32))
            return jnp.sum(acc.astype(F32), axis=0, keepdims=True)
        def unresolved(carry):
            bi, _, cnt = carry
            return jnp.logical_and(bi < 32, jnp.max(jnp.abs(cnt - topk)) > 0.0)
        def bit_steps(carry):
            bi, t, cnt = carry
            for k in range(4):
                tc = t + (jnp.int32(1) << (31 - bi - k))
                c = count_ge(as_float(tc))
                take = c >= topk
                t = jnp.where(take, tc, t)
                cnt = jnp.where(take, c, cnt)
            return bi + 4, t, cnt
        n_all = ((i + 1) * tk).astype(F32)
        _, t_fin, _ = lax.while_loop(unresolved, bit_steps,
                                     (jnp.int32(0), jnp.full((1, tq), INT_MIN, jnp.int32),
                                      jnp.full((1, tq), n_all, F32)))
        t_float = as_float(t_fin)
        thr_ref[...] = jnp.where(t_float != t_float, NEG_INF, t_float)

        m_ref[...] = jnp.full_like(m_ref, NEG_INF)
        l_ref[...] = jnp.zeros_like(l_ref)
        acc_ref[...] = jnp.zeros_like(acc_ref)

    sel = (keys_ref[j] >= thr_ref[...]) & causal(j)
    smax = smax_ref[0]
    fixed_shift = smax <= SOFTMAX_FIXED_SHIFT_MAX

    @pl.when(fixed_shift)
    def _attend_fixed_shift():
        bias = jnp.where(sel, -smax, NEG_INF)
        for h in range(ATTN_HEADS):
            sl = slice(h * dh, (h + 1) * dh)
            p = jnp.exp(_dot_nt(kn_ref[0, :, sl], qn_ref[:, sl]) + bias)
            parts = [p[r * 8:(r + 1) * 8] for r in range(tk // 8)]
            while len(parts) > 1:
                parts = [a + b for a, b in zip(parts[0::2], parts[1::2])]
            l_ref[h] += parts[0]
            p_ref[h] = p.astype(BF16)

    @pl.when(jnp.logical_and(fixed_shift, j <= i))
    def _weighted_values():
        for h in range(ATTN_HEADS):
            sl = slice(h * dh, (h + 1) * dh)
            acc_ref[h] += _dot(vt_ref[0, 0, sl, :], p_ref[h])

    @pl.when(jnp.logical_not(fixed_shift))
    def _attend_running_max():
        bias = jnp.where(sel, 0.0, NEG_INF)
        m_all = m_ref[...]
        m_rows = []
        for h in range(ATTN_HEADS):
            sl = slice(h * dh, (h + 1) * dh)
            s = _dot_nt(kn_ref[0, :, sl], qn_ref[:, sl]) + bias
            m_old = m_all[h:h + 1, :]
            m_new = jnp.maximum(m_old, jnp.max(s, axis=0, keepdims=True))
            m_safe = jnp.where(m_new == NEG_INF, 0.0, m_new)
            alpha = jnp.exp(m_old - m_safe)
            p = jnp.exp(s - m_safe)
            l_ref[h, 0:1, :] = alpha * l_ref[h, 0:1, :] + jnp.sum(p, axis=0, keepdims=True)
            acc_ref[h] = alpha * acc_ref[h] + _dot(vt_ref[0, 0, sl, :], p.astype(BF16))
            m_rows.append(m_new)
        m_ref[...] = jnp.concatenate(m_rows, axis=0)

    @pl.when(j == i)
    def _finish():
        for h in range(ATTN_HEADS):
            sl = slice(h * dh, (h + 1) * dh)
            l = jnp.sum(l_ref[h], axis=0, keepdims=True)
            o_ref[0, :, sl] = (acc_ref[h] / l).T.astype(o_ref.dtype)


def dsa_group(proj, kn, vt, ikd, qnorm, knorm, tq):
    b, s, _ = proj.shape
    tk = tq
    smax = (ATTN_HEAD_DIM ** 0.5 * jnp.max(jnp.abs(qnorm)) * jnp.max(jnp.abs(knorm))).reshape(1).astype(F32)
    nq = s // tq
    w = ATTN_HEADS * ATTN_HEAD_DIM
    topk = min(DSA_TOPK, s // 4)
    eq = np.zeros((TAIL, IDX_HEADS * IDX_DIM), np.float32)
    esel = np.zeros((IDX_HEADS, TAIL), np.float32)
    for h in range(IDX_HEADS):
        eq[IDX_DIM + h, h * IDX_DIM:(h + 1) * IDX_DIM] = 1.0
        esel[h, IDX_DIM + h] = 1.0
    pairs = [(i, j) for i in range(nq) for j in range(i + 1)]
    pi = jnp.asarray([p[0] for p in pairs], jnp.int32)
    pj = jnp.asarray([p[1] for p in pairs], jnp.int32)
    qblk = lambda cb: pl.BlockSpec((1, tq, w), lambda bi, st, pi, pj, cb=cb: (bi, pi[st], cb))
    const = lambda a: pl.BlockSpec(a.shape, lambda bi, st, pi, pj: (0,) * a.ndim)
    grid_spec = pltpu.PrefetchScalarGridSpec(
        num_scalar_prefetch=2,
        grid=(b, len(pairs)),
        in_specs=[qblk(4),
                  qblk(7),
                  pl.BlockSpec((1, tq, TAIL), lambda bi, st, pi, pj: (bi, pi[st], 8 * w // TAIL)),
                  pl.BlockSpec((1, s, LANES), lambda bi, st, pi, pj: (bi, 0, 0)),
                  pl.BlockSpec((1, tk, w), lambda bi, st, pi, pj: (bi, pj[st], 0)),
                  pl.BlockSpec((1, 1, w, tk), lambda bi, st, pi, pj: (bi, pj[st], 0, 0)),
                  const(qnorm), const(eq), const(esel),
                  pl.BlockSpec(memory_space=pltpu.SMEM)],
        out_specs=pl.BlockSpec((1, tq, w), lambda bi, st, pi, pj: (bi, pi[st], 0)),
        scratch_shapes=[pltpu.VMEM((nq, tk, tq), F32),
                        pltpu.VMEM((1, tq), F32),
                        pltpu.VMEM((tq, w), BF16),
                        pltpu.VMEM((IDX_HEADS, tq, LANES), BF16),
                        pltpu.VMEM((IDX_HEADS, tq), F32),
                        pltpu.VMEM((IDX_HEADS, tq), F32),
                        pltpu.VMEM((ATTN_HEADS, tq), F32),
                        pltpu.VMEM((ATTN_HEADS, 8, tq), F32),
                        pltpu.VMEM((ATTN_HEADS, ATTN_HEAD_DIM, tq), F32),
                        pltpu.VMEM((ATTN_HEADS, tk, tq), BF16)])
    return pl.pallas_call(
        functools.partial(_dsa_kernel, tq=tq, tk=tk, topk=topk),
        grid_spec=grid_spec,
        out_shape=jax.ShapeDtypeStruct((b, s, w), BF16),
        compiler_params=_params("parallel", "arbitrary"),
        name="dsa",
    )(pi, pj, proj, proj, proj, ikd, kn, vt, qnorm, jnp.asarray(eq, BF16), jnp.asarray(esel, BF16), smax)


def _out_proj_kernel(yh_ref, ya_ref, w1_ref, w2_ref, x_ref, o_ref):
    o_ref[...] = x_ref[...] + _dot(yh_ref[...], w1_ref[...]) + _dot(ya_ref[...], w2_ref[...])


def out_proj(yh, ya, w_out, x, tm, tn):
    m, kh = yh.shape
    n = w_out.shape[1]
    return pl.pallas_call(
        _out_proj_kernel,
        grid=(m // tm, n // tn),
        in_specs=[pl.BlockSpec((tm, kh), lambda i, j: (i, 0)),
                  pl.BlockSpec((tm, kh), lambda i, j: (i, 0)),
                  pl.BlockSpec((kh, tn), lambda i, j: (0, j)),
                  pl.BlockSpec((kh, tn), lambda i, j: (1, j)),
                  pl.BlockSpec((tm, tn), lambda i, j: (i, j))],
        out_specs=pl.BlockSpec((tm, tn), lambda i, j: (i, j)),
        out_shape=jax.ShapeDtypeStruct((m, n), F32),
        compiler_params=_params("parallel", "arbitrary"),
        name="out_proj",
    )(yh, ya, w_out, w_out, x)


def _mem_kv_kernel(mem_ref, mg_ref, wk_ref, wv_ref, kg_ref, k_ref, v_ref):
    memn = _rms(mem_ref[0], mg_ref[...]).astype(BF16)
    k = _dot(memn, wk_ref[...])
    kg = kg_ref[...]
    for h in range(CROSS_HEADS):
        sl = slice(h * CROSS_HEAD_DIM, (h + 1) * CROSS_HEAD_DIM)
        k_ref[0, :, sl] = _rms(k[:, sl], kg).astype(BF16)
    v_ref[0] = _dot(memn, wv_ref[...]).astype(BF16)


def mem_kv(mem, mem_norm, wk, wv, xk_norm):
    b, nm, d = mem.shape
    cw = wk.shape[1]
    full = lambda a: pl.BlockSpec(a.shape, lambda bi: (0,) * a.ndim)
    return pl.pallas_call(
        _mem_kv_kernel,
        grid=(b,),
        in_specs=[pl.BlockSpec((1, nm, d), lambda bi: (bi, 0, 0)),
                  full(mem_norm), full(wk), full(wv), full(xk_norm)],
        out_specs=[pl.BlockSpec((1, nm, cw), lambda bi: (bi, 0, 0))] * 2,
        out_shape=[jax.ShapeDtypeStruct((b, nm, cw), BF16)] * 2,
        compiler_params=_params("parallel"),
        name="mem_kv",
    )(mem, mem_norm, wk, wv, xk_norm)


def _cross_kernel(h_ref, ng_ref, wq_ref, qg_ref, k_ref, v_ref, wo_ref, o_ref, hn_ref, oc_ref):
    rows = h_ref.shape[1]
    dh = CROSS_HEAD_DIM
    step = 256
    gain = ng_ref[...]
    def body(r, c):
        r0 = pl.multiple_of(r * step, step)
        hn_ref[pl.ds(r0, step), :] = _rms(h_ref[0, pl.ds(r0, step), :], gain).astype(BF16)
        return c
    lax.fori_loop(0, rows // step, body, 0)
    q = _dot(hn_ref[...], wq_ref[...])
    qg = qg_ref[...] * (dh ** -0.5)
    for h in range(CROSS_HEADS):
        sl = slice(h * dh, (h + 1) * dh)
        qn = _rms(q[:, sl], qg).astype(BF16)
        s = _dot_nt(qn, k_ref[0, :, sl])
        p = jnp.exp(s - jnp.max(s, axis=1, keepdims=True))
        l = jnp.sum(p, axis=1, keepdims=True)
        oc_ref[:, sl] = (_dot(p.astype(BF16), v_ref[0, :, sl]) / l).astype(BF16)
    o_ref[0] = h_ref[0] + _dot(oc_ref[...], wo_ref[...])


def cross_attention(h, norm_cross, wq, xq_norm, kx, vx, wo, tm):
    b, s, d = h.shape
    nm, cw = kx.shape[1:]
    full = lambda a: pl.BlockSpec(a.shape, lambda bi, i: (0,) * a.ndim)
    return pl.pallas_call(
        _cross_kernel,
        grid=(b, s // tm),
        in_specs=[pl.BlockSpec((1, tm, d), lambda bi, i: (bi, i, 0)),
                  full(norm_cross), full(wq), full(xq_norm),
                  pl.BlockSpec((1, nm, cw), lambda bi, i: (bi, 0, 0)),
                  pl.BlockSpec((1, nm, cw), lambda bi, i: (bi, 0, 0)),
                  full(wo)],
        out_specs=pl.BlockSpec((1, tm, d), lambda bi, i: (bi, i, 0)),
        out_shape=jax.ShapeDtypeStruct((b, s, d), F32),
        scratch_shapes=[pltpu.VMEM((tm, d), BF16), pltpu.VMEM((tm, cw), BF16)],
        compiler_params=_params("parallel", "parallel"),
        name="cross_attn",
    )(h, norm_cross, wq, xq_norm, kx, vx, wo)


def _mlp_kernel(h_ref, g_ref, wu_ref, wd_ref, o_ref, hn_ref):
    @pl.when(pl.program_id(1) == 0)
    def _():
        _norm_rows_to(h_ref, g_ref[...], hn_ref, h_ref.shape[0])
        o_ref[...] = h_ref[...]

    u = jnp.maximum(_dot(hn_ref[...], wu_ref[0].astype(BF16)), 0.0)
    o_ref[...] += _dot((u * u).astype(BF16), wd_ref[0].astype(BF16))


def mlp(h, gain, w_up, w_down, layer, tm, tf):
    m, d = h.shape
    f = w_up.shape[2]
    return pl.pallas_call(
        _mlp_kernel,
        grid=(m // tm, f // tf),
        in_specs=[pl.BlockSpec((tm, d), lambda i, j: (i, 0)),
                  pl.BlockSpec((1, d), lambda i, j: (0, 0)),
                  pl.BlockSpec((1, d, tf), lambda i, j: (layer, 0, j)),
                  pl.BlockSpec((1, tf, d), lambda i, j: (layer, j, 0))],
        out_specs=pl.BlockSpec((tm, d), lambda i, j: (i, 0), pipeline_mode=pl.Buffered(1)),
        out_shape=jax.ShapeDtypeStruct((m, d), F32),
        scratch_shapes=[pltpu.VMEM((tm, d), BF16)],
        compiler_params=_params("parallel", "arbitrary"),
        name="mlp",
    )(h, gain, w_up, w_down)


def _tile(n, pref):
    return pref if n % pref == 0 else n


def _tiles(n, s):
    return dict(
        proj_rows=_tile(n, 1024), proj_cols=768,
        hgrn_rows=_tile(s, 512),
        dsa_block=_tile(s, 512),
        out_rows=_tile(n, 1024), out_cols=1024,
        cross_rows=_tile(s, 512),
        mlp_rows=_tile(n, 1024), mlp_ff=512)


def kernel(x, mem, norm_mix, w_in, hgrn_lb_logits, hgrn_onorm, attn_qnorm, attn_knorm, w_out,
           norm_cross, mem_norm, wq_x, wk_x, wv_x, wo_x, xq_norm, xk_norm,
           norm_mlp, w_up, w_down):
    b, s, d = x.shape
    n = b * s
    depth = w_in.shape[0]
    assert depth == 1
    l = 0
    in_width = w_in.shape[2]
    main_w = 8 * HGRN_HEADS * HGRN_KDIM
    assert in_width == main_w + IDX_DIM + IDX_HEADS
    pad_w = main_w + TAIL

    t = _tiles(n, s)
    x2 = x.reshape(n, d)

    proj = norm_matmul(x2, norm_mix[l:l + 1], w_in[l].T.astype(BF16), pad_w,
                       t["proj_rows"], t["proj_cols"])
    proj3 = proj.reshape(b, s, pad_w)

    y_h = hgrn_group(proj3, hgrn_lb_logits, hgrn_onorm[l:l + 1], t["hgrn_rows"])
    kn, vt, ikd = dsa_prep(proj3, attn_knorm[l:l + 1], t["dsa_block"])
    y_a = dsa_group(proj3, kn, vt, ikd, attn_qnorm[l:l + 1], attn_knorm[l:l + 1], t["dsa_block"])

    h1 = out_proj(y_h.reshape(n, -1), y_a.reshape(n, -1), w_out[l].astype(BF16), x2,
                  t["out_rows"], t["out_cols"])

    kx, vx = mem_kv(mem, mem_norm[l:l + 1], wk_x[l].astype(BF16), wv_x[l].astype(BF16), xk_norm[l:l + 1])
    h2 = cross_attention(h1.reshape(b, s, d), norm_cross[l:l + 1], wq_x[l].astype(BF16), xq_norm[l:l + 1],
                         kx, vx, wo_x[l].astype(BF16), t["cross_rows"])

    h3 = mlp(h2.reshape(n, d), norm_mlp[l:l + 1], w_up, w_down, l, t["mlp_rows"], t["mlp_ff"])
    return h3.reshape(b, s, d)
```

```python
import functools

import numpy as np
import jax
import jax.numpy as jnp
from jax import lax
from jax.experimental import pallas as pl
from jax.experimental.pallas import tpu as pltpu

F32 = jnp.float32
BF16 = jnp.bfloat16
EPS = 1e-6

LANES = 128
HGRN_HEADS = 8
HGRN_KDIM = 128
HGRN_CHUNK = 64
ATTN_HEADS = 8
ATTN_HEAD_DIM = 128
IDX_HEADS = 16
IDX_DIM = 64
DSA_TOPK = 256
CROSS_HEADS = 4
CROSS_HEAD_DIM = 128
TAIL = 256
VMEM_LIMIT = 56 * 1024 * 1024
NEG_INF = float("-inf")
INT_MIN = -(2 ** 31)
SOFTMAX_FIXED_SHIFT_MAX = 40.0


def _params(*sem, flags=None):
    return pltpu.CompilerParams(dimension_semantics=sem, vmem_limit_bytes=VMEM_LIMIT, flags=flags)


def _rms(x, gain):
    return x * lax.rsqrt(jnp.mean(x * x, axis=-1, keepdims=True) + EPS) * gain


def _sigmoid(x):
    return 1.0 / (1.0 + jnp.exp(-x))


def _dot(a, b):
    return jnp.dot(a, b, preferred_element_type=F32)


def _dot_nt(a, b):
    return lax.dot_general(a, b, (((1,), (1,)), ((), ())), preferred_element_type=F32)


def _dot_tn(a, b):
    return lax.dot_general(a, b, (((0,), (0,)), ((), ())), preferred_element_type=F32)


def _norm_rows_to(x_ref, gain, dst_ref, rows):
    step = 256
    def body(r, c):
        r0 = pl.multiple_of(r * step, step)
        x = x_ref[pl.ds(r0, step), :]
        dst_ref[pl.ds(r0, step), :] = _rms(x, gain).astype(BF16)
        return c
    lax.fori_loop(0, rows // step, body, 0)


def _norm_matmul_kernel(x_ref, g_ref, wt_ref, o_ref, xn_ref, *, valid_cols):
    j = pl.program_id(1)
    tn = o_ref.shape[1]

    @pl.when(j == 0)
    def _():
        _norm_rows_to(x_ref, g_ref[...], xn_ref, x_ref.shape[0])

    res = _dot_nt(xn_ref[...], wt_ref[...].astype(BF16))

    @pl.when((j + 1) * tn <= valid_cols)
    def _():
        o_ref[...] = res.astype(o_ref.dtype)

    @pl.when((j + 1) * tn > valid_cols)
    def _():
        col = j * tn + lax.broadcasted_iota(jnp.int32, res.shape, 1)
        o_ref[...] = jnp.where(col < valid_cols, res, 0.0).astype(o_ref.dtype)


def norm_matmul(x, gain, wt, n, tm, tn):
    m, k = x.shape
    return pl.pallas_call(
        functools.partial(_norm_matmul_kernel, valid_cols=wt.shape[0]),
        grid=(m // tm, n // tn),
        in_specs=[pl.BlockSpec((tm, k), lambda i, j: (i, 0)),
                  pl.BlockSpec((1, k), lambda i, j: (0, 0)),
                  pl.BlockSpec((tn, k), lambda i, j: (j, 0))],
        out_specs=pl.BlockSpec((tm, tn), lambda i, j: (i, j)),
        out_shape=jax.ShapeDtypeStruct((m, n), BF16),
        scratch_shapes=[pltpu.VMEM((tm, k), BF16)],
        compiler_params=_params("parallel", "arbitrary"),
        name="in_proj",
    )(x, gain, wt)


def _hgrn_kernel(q_ref, f_ref, i_ref, g_ref, lbl_ref, on_ref, tril_ref, o_ref,
                 state_ref, a_ref, qi_ref, ks_ref, el_ref, qt_ref, kt_ref, sc_ref, scb_ref, *, nchunks):
    c = HGRN_CHUNK
    dk = HGRN_KDIM
    pw = 2 * dk
    npairs = HGRN_HEADS // 2
    nlev = 6

    @pl.when(pl.program_id(1) == 0)
    def _():
        state_ref[...] = jnp.zeros_like(state_ref)
        kt_ref[...] = jnp.zeros_like(kt_ref)

    lbl = lbl_ref[...]
    e = jnp.exp(lbl - jnp.max(lbl, axis=0, keepdims=True))
    lb = e[0:1] / jnp.sum(e, axis=0, keepdims=True)
    onorm = on_ref[...]
    tril = tril_ref[...]

    row = lax.broadcasted_iota(jnp.int32, (c, 2 * c), 0)
    lane = lax.broadcasted_iota(jnp.int32, (c, 2 * c), 1)
    col = lane & (c - 1)
    first = lane < c
    eye = row == col
    level_masks = []
    lg = nlev - 1
    while lg >= 0:
        level_masks.append(((row >> (lg + 1)) == (col >> (lg + 1)))
                           & (((row >> lg) & 1) == 1) & (((col >> lg) & 1) == 0))
        lg -= 1
    lane_p = lax.broadcasted_iota(jnp.int32, (c, pw), 1)
    odd_row = (lax.broadcasted_iota(jnp.int32, (c, pw), 0) & 1) == 1
    sub8 = lax.broadcasted_iota(jnp.int32, (8, pw), 0)
    zeros_st = jnp.zeros((dk, dk), BF16)

    def chunk_body(ci, carry):
        r0 = pl.multiple_of(ci * c, c)
        live = ci >= 0

        @pl.when(live)
        def _decay_factors():
            hq = q_ref[0, pl.ds(r0, c), :].astype(F32)
            hf = f_ref[0, pl.ds(r0, c), :].astype(F32)
            qf = hq * _sigmoid(hq) * (dk ** -0.5)
            f = lb + (1.0 - lb) * _sigmoid(hf)
            logf = jnp.log(f)
            kk = 1.0 - f
            g0 = logf.astype(BF16)
            g1 = (logf - g0.astype(F32)).astype(BF16)
            a_ref[...] = _dot(tril, g0) + _dot(tril, g1)
            for p in range(npairs):
                sl = slice(p * pw, (p + 1) * pw)
                a = a_ref[:, sl]
                row = lambda r, n: jnp.broadcast_to(a_ref[r:r + 1, sl], (n, pw))
                e0 = jnp.exp(a)
                qp = qf[:, sl]
                kp = kk[:, sl]
                qi_ref[p] = (qp * e0).astype(BF16)
                ks_ref[p] = (kp * jnp.exp(row(c - 1, c) - a)).astype(BF16)
                el_ref[p] = e0[c - 1:c]
                for l in range(nlev):
                    h = c >> (l + 1)
                    if h >= 4:
                        ref = jnp.concatenate([row(m * 2 * h + h - 1, 2 * h) for m in range(c // (2 * h))], axis=0)
                    elif h == 2:
                        ref = jnp.concatenate([jnp.where(sub8 < 4, row(8 * m + 1, 8), row(8 * m + 5, 8))
                                               for m in range(c // 8)], axis=0)
                    else:
                        ref = jnp.where(odd_row, pltpu.roll(a, 1, 0), a)
                    el = jnp.exp(-jnp.abs(a - ref))
                    qt_ref[p, l] = (qp * el).astype(BF16)
                    kl = (kp * el).astype(BF16)
                    kt_ref[p, l, 0:c, 0:dk] = kl[:, 0:dk]
                    kt_ref[p, l, c:2 * c, dk:pw] = kl[:, dk:pw]
                qk = qp * kp
                diag = jnp.where(first, jnp.sum(qk[:, 0:dk], axis=1, keepdims=True),
                                 jnp.sum(qk[:, dk:pw], axis=1, keepdims=True))
                sc_ref[p] = jnp.where(eye, diag, 0.0)

        @pl.when(live)
        def _intra_chunk_scores():
            for p in range(npairs):
                sc = sc_ref[p]
                for l, msk in enumerate(level_masks):
                    sc = sc + jnp.where(msk, _dot_nt(qt_ref[p, l], kt_ref[p, l]), 0.0)
                scb_ref[p] = sc.astype(BF16)

        @pl.when(live)
        def _outputs_and_state():
            for p in range(npairs):
                sl = slice(p * pw, (p + 1) * pw)
                vp = i_ref[0, pl.ds(r0, c), sl]
                st_a = state_ref[2 * p]
                st_b = state_ref[2 * p + 1]
                st_bd = jnp.concatenate(
                    [jnp.concatenate([st_a.astype(BF16), zeros_st], axis=1),
                     jnp.concatenate([zeros_st, st_b.astype(BF16)], axis=1)], axis=0)
                v_bd = jnp.concatenate([jnp.where(lane_p < dk, vp, jnp.zeros_like(vp)),
                                        jnp.where(lane_p >= dk, vp, jnp.zeros_like(vp))], axis=0)
                o = _dot_nt(qi_ref[p], st_bd) + _dot(scb_ref[p], v_bd)
                upd = _dot_tn(vp, ks_ref[p])
                el = el_ref[p]
                state_ref[2 * p] = st_a * el[:, 0:dk] + upd[0:dk, 0:dk]
                state_ref[2 * p + 1] = st_b * el[:, dk:pw] + upd[dk:pw, dk:pw]
                gate = g_ref[0, pl.ds(r0, c), sl].astype(F32)
                y = jnp.concatenate([_rms(o[:, 0:dk], onorm), _rms(o[:, dk:pw], onorm)], axis=1)
                o_ref[0, pl.ds(r0, c), sl] = (y * (gate * _sigmoid(gate))).astype(o_ref.dtype)
        return carry

    lax.fori_loop(0, nchunks, chunk_body, 0)


def hgrn_group(proj, lb_logits, onorm, t_blk):
    b, s, _ = proj.shape
    w = HGRN_HEADS * HGRN_KDIM
    tril = jnp.asarray(np.tril(np.ones((HGRN_CHUNK, HGRN_CHUNK), np.float32)), BF16)
    c, pw, npairs, nlev = HGRN_CHUNK, 2 * HGRN_KDIM, HGRN_HEADS // 2, 6
    col = lambda cb: pl.BlockSpec((1, t_blk, w), lambda bi, ti, cb=cb: (bi, ti, cb))
    return pl.pallas_call(
        functools.partial(_hgrn_kernel, nchunks=t_blk // HGRN_CHUNK),
        grid=(b, s // t_blk),
        in_specs=[col(0), col(1), col(2), col(3),
                  pl.BlockSpec(lb_logits.shape, lambda bi, ti: (0, 0)),
                  pl.BlockSpec((1, HGRN_KDIM), lambda bi, ti: (0, 0)),
                  pl.BlockSpec(tril.shape, lambda bi, ti: (0, 0))],
        out_specs=pl.BlockSpec((1, t_blk, w), lambda bi, ti: (bi, ti, 0)),
        out_shape=jax.ShapeDtypeStruct((b, s, w), BF16),
        scratch_shapes=[pltpu.VMEM((HGRN_HEADS, HGRN_KDIM, HGRN_KDIM), F32),
                        pltpu.VMEM((c, w), F32),
                        pltpu.VMEM((npairs, c, pw), BF16),
                        pltpu.VMEM((npairs, c, pw), BF16),
                        pltpu.VMEM((npairs, 1, pw), F32),
                        pltpu.VMEM((npairs, nlev, c, pw), BF16),
                        pltpu.VMEM((npairs, nlev, 2 * c, pw), BF16),
                        pltpu.VMEM((npairs, c, 2 * c), F32),
                        pltpu.VMEM((npairs, c, 2 * c), BF16)],
        compiler_params=_params("parallel", "arbitrary"),
        name="hgrn2",
    )(proj, proj, proj, proj, lb_logits, onorm, tril)


def _dsa_prep_kernel(k_ref, v_ref, tail_ref, kg_ref, dup_ref, kn_ref, vt_ref, ikd_ref):
    kg = kg_ref[...]
    for h in range(ATTN_HEADS):
        sl = slice(h * ATTN_HEAD_DIM, (h + 1) * ATTN_HEAD_DIM)
        kn_ref[0, :, sl] = _rms(k_ref[0, :, sl].astype(F32), kg).astype(BF16)
        vt_ref[0, 0, sl, :] = v_ref[0, :, sl].astype(F32).T.astype(BF16)
    ikd_ref[0] = _dot(tail_ref[0], dup_ref[...]).astype(BF16)


def dsa_prep(proj, knorm, tm):
    b, s, _ = proj.shape
    w = ATTN_HEADS * ATTN_HEAD_DIM
    dup = np.zeros((TAIL, LANES), np.float32)
    dup[np.arange(IDX_DIM), np.arange(IDX_DIM)] = 1.0
    dup[np.arange(IDX_DIM), np.arange(IDX_DIM) + IDX_DIM] = 1.0
    return pl.pallas_call(
        _dsa_prep_kernel,
        grid=(b, s // tm),
        in_specs=[pl.BlockSpec((1, tm, w), lambda bi, i: (bi, i, 5)),
                  pl.BlockSpec((1, tm, w), lambda bi, i: (bi, i, 6)),
                  pl.BlockSpec((1, tm, TAIL), lambda bi, i: (bi, i, 8 * w // TAIL)),
                  pl.BlockSpec((1, ATTN_HEAD_DIM), lambda bi, i: (0, 0)),
                  pl.BlockSpec((TAIL, LANES), lambda bi, i: (0, 0))],
        out_specs=[pl.BlockSpec((1, tm, w), lambda bi, i: (bi, i, 0)),
                   pl.BlockSpec((1, 1, w, tm), lambda bi, i: (bi, i, 0, 0)),
                   pl.BlockSpec((1, tm, LANES), lambda bi, i: (bi, i, 0))],
        out_shape=[jax.ShapeDtypeStruct((b, s, w), BF16),
                   jax.ShapeDtypeStruct((b, s // tm, w, tm), BF16),
                   jax.ShapeDtypeStruct((b, s, LANES), BF16)],
        compiler_params=_params("parallel", "parallel"),
        name="dsa_prep",
    )(proj, proj, proj, knorm, jnp.asarray(dup, BF16))


def _dsa_kernel(pi_ref, pj_ref, aq_ref, iq_ref, tail_ref, ikd_ref, kn_ref, vt_ref, qg_ref, eq_ref, esel_ref,
                smax_ref, o_ref, keys_ref, thr_ref, qn_ref, qw_ref, lo_ref, hi_ref, m_ref, l_ref, acc_ref, p_ref,
                *, tq, tk, topk):
    step = pl.program_id(1)
    i = pi_ref[step]
    j = pj_ref[step]
    dh = ATTN_HEAD_DIM

    def causal(jj):
        s_pos = jj * tk + lax.broadcasted_iota(jnp.int32, (tk, tq), 0)
        t_pos = i * tq + lax.broadcasted_iota(jnp.int32, (tk, tq), 1)
        return s_pos <= t_pos

    @pl.when(j == 0)
    def _index_and_select():
        qg = qg_ref[...] * (dh ** -0.5)
        for h in range(ATTN_HEADS):
            sl = slice(h * dh, (h + 1) * dh)
            qn_ref[:, sl] = _rms(aq_ref[0, :, sl].astype(F32), qg).astype(BF16)
        tail = tail_ref[0]
        ww = _dot(tail, eq_ref[...])
        qw = (iq_ref[0].astype(F32) * ww * (IDX_HEADS ** -0.5 * IDX_DIM ** -0.5)).astype(BF16)
        lane = lax.broadcasted_iota(jnp.int32, (tq, LANES), 1)
        zero = jnp.zeros((tq, LANES), BF16)
        for p in range(IDX_HEADS // 2):
            pair = qw[:, p * LANES:(p + 1) * LANES]
            qw_ref[2 * p] = jnp.where(lane < IDX_DIM, pair, zero)
            qw_ref[2 * p + 1] = jnp.where(lane >= IDX_DIM, pair, zero)
        w_t = _dot_nt(esel_ref[...], tail)
        lo_ref[...] = jnp.where(w_t > 0.0, 0.0, NEG_INF)
        hi_ref[...] = jnp.where(w_t > 0.0, jnp.inf, 0.0)

        def score_tile(jj, c):
            r0 = pl.multiple_of(jj * tk, tk)
            ik = ikd_ref[0, pl.ds(r0, tk), :]
            comb = jnp.zeros((tk, tq), F32)
            for h in range(IDX_HEADS):
                x = _dot_nt(ik, qw_ref[h])
                comb = comb + jnp.minimum(jnp.maximum(x, lo_ref[h:h + 1, :]), hi_ref[h:h + 1, :])
            keys_ref[jj] = jnp.where(causal(jj), comb, NEG_INF)
            return c
        lax.fori_loop(0, i + 1, score_tile, 0)

        def as_float(t):
            return pltpu.bitcast(jnp.where(t < 0, t ^ 0x7FFFFFFF, t), F32)
        def count_ge(tf):
            def body(jj, acc):
                accs = [acc, jnp.zeros_like(acc), jnp.zeros_like(acc), jnp.zeros_like(acc)]
                for r0 in range(0, tk, 8):
                    a = accs[(r0 // 8) % 4]
                    accs[(r0 // 8) % 4] = jnp.where(keys_ref[jj, r0:r0 + 8, :] >= tf, a + 1, a)
                return (accs[0] + accs[1]) + (accs[2] + accs[3])
            acc = lax.fori_loop(0, i + 1, body, jnp.zeros((8, tq), jnp.int32))
            return jnp.sum(acc.astype(F32), axis=0, keepdims=True)
        def unresolved(carry):
            bi, _, cnt = carry
            return jnp.logical_and(bi < 32, jnp.max(jnp.abs(cnt - topk)) > 0.0)
        def bit_steps(carry):
            bi, t, cnt = carry
            for k in range(4):
                tc = t + (jnp.int32(1) << (31 - bi - k))
                c = count_ge(as_float(tc))
                take = c >= topk
                t = jnp.where(take, tc, t)
                cnt = jnp.where(take, c, cnt)
            return bi + 4, t, cnt
        n_all = ((i + 1) * tk).astype(F32)
        _, t_fin, _ = lax.while_loop(unresolved, bit_steps,
                                     (jnp.int32(0), jnp.full((1, tq), INT_MIN, jnp.int32),
                                      jnp.full((1, tq), n_all, F32)))
        t_float = as_float(t_fin)
        thr_ref[...] = jnp.where(t_float != t_float, NEG_INF, t_float)

        m_ref[...] = jnp.full_like(m_ref, NEG_INF)
        l_ref[...] = jnp.zeros_like(l_ref)
        acc_ref[...] = jnp.zeros_like(acc_ref)

    sel = (keys_ref[j] >= thr_ref[...]) & causal(j)
    smax = smax_ref[0]
    fixed_shift = smax <= SOFTMAX_FIXED_SHIFT_MAX

    @pl.when(fixed_shift)
    def _attend_fixed_shift():
        bias = jnp.where(sel, -smax, NEG_INF)
        for h in range(ATTN_HEADS):
            sl = slice(h * dh, (h + 1) * dh)
            p = jnp.exp(_dot_nt(kn_ref[0, :, sl], qn_ref[:, sl]) + bias)
            parts = [p[r * 8:(r + 1) * 8] for r in range(tk // 8)]
            while len(parts) > 1:
                parts = [a + b for a, b in zip(parts[0::2], parts[1::2])]
            l_ref[h] += parts[0]
            p_ref[h] = p.astype(BF16)

    @pl.when(jnp.logical_and(fixed_shift, j <= i))
    def _weighted_values():
        for h in range(ATTN_HEADS):
            sl = slice(h * dh, (h + 1) * dh)
            acc_ref[h] += _dot(vt_ref[0, 0, sl, :], p_ref[h])

    @pl.when(jnp.logical_not(fixed_shift))
    def _attend_running_max():
        bias = jnp.where(sel, 0.0, NEG_INF)
        m_all = m_ref[...]
        m_rows = []
        for h in range(ATTN_HEADS):
            sl = slice(h * dh, (h + 1) * dh)
            s = _dot_nt(kn_ref[0, :, sl], qn_ref[:, sl]) + bias
            m_old = m_all[h:h + 1, :]
            m_new = jnp.maximum(m_old, jnp.max(s, axis=0, keepdims=True))
            m_safe = jnp.where(m_new == NEG_INF, 0.0, m_new)
            alpha = jnp.exp(m_old - m_safe)
            p = jnp.exp(s - m_safe)
            l_ref[h, 0:1, :] = alpha * l_ref[h, 0:1, :] + jnp.sum(p, axis=0, keepdims=True)
            acc_ref[h] = alpha * acc_ref[h] + _dot(vt_ref[0, 0, sl, :], p.astype(BF16))
            m_rows.append(m_new)
        m_ref[...] = jnp.concatenate(m_rows, axis=0)

    @pl.when(j == i)
    def _finish():
        for h in range(ATTN_HEADS):
            sl = slice(h * dh, (h + 1) * dh)
            l = jnp.sum(l_ref[h], axis=0, keepdims=True)
            o_ref[0, :, sl] = (acc_ref[h] / l).T.astype(o_ref.dtype)


def dsa_group(proj, kn, vt, ikd, qnorm, knorm, tq):
    b, s, _ = proj.shape
    tk = tq
    smax = (ATTN_HEAD_DIM ** 0.5 * jnp.max(jnp.abs(qnorm)) * jnp.max(jnp.abs(knorm))).reshape(1).astype(F32)
    nq = s // tq
    w = ATTN_HEADS * ATTN_HEAD_DIM
    topk = min(DSA_TOPK, s // 4)
    eq = np.zeros((TAIL, IDX_HEADS * IDX_DIM), np.float32)
    esel = np.zeros((IDX_HEADS, TAIL), np.float32)
    for h in range(IDX_HEADS):
        eq[IDX_DIM + h, h * IDX_DIM:(h + 1) * IDX_DIM] = 1.0
        esel[h, IDX_DIM + h] = 1.0
    pairs = [(i, j) for i in range(nq) for j in range(i + 1)]
    pi = jnp.asarray([p[0] for p in pairs], jnp.int32)
    pj = jnp.asarray([p[1] for p in pairs], jnp.int32)
    qblk = lambda cb: pl.BlockSpec((1, tq, w), lambda bi, st, pi, pj, cb=cb: (bi, pi[st], cb))
    const = lambda a: pl.BlockSpec(a.shape, lambda bi, st, pi, pj: (0,) * a.ndim)
    grid_spec = pltpu.PrefetchScalarGridSpec(
        num_scalar_prefetch=2,
        grid=(b, len(pairs)),
        in_specs=[qblk(4),
                  qblk(7),
                  pl.BlockSpec((1, tq, TAIL), lambda bi, st, pi, pj: (bi, pi[st], 8 * w // TAIL)),
                  pl.BlockSpec((1, s, LANES), lambda bi, st, pi, pj: (bi, 0, 0)),
                  pl.BlockSpec((1, tk, w), lambda bi, st, pi, pj: (bi, pj[st], 0)),
                  pl.BlockSpec((1, 1, w, tk), lambda bi, st, pi, pj: (bi, pj[st], 0, 0)),
                  const(qnorm), const(eq), const(esel),
                  pl.BlockSpec(memory_space=pltpu.SMEM)],
        out_specs=pl.BlockSpec((1, tq, w), lambda bi, st, pi, pj: (bi, pi[st], 0)),
        scratch_shapes=[pltpu.VMEM((nq, tk, tq), F32),
                        pltpu.VMEM((1, tq), F32),
                        pltpu.VMEM((tq, w), BF16),
                        pltpu.VMEM((IDX_HEADS, tq, LANES), BF16),
                        pltpu.VMEM((IDX_HEADS, tq), F32),
                        pltpu.VMEM((IDX_HEADS, tq), F32),
                        pltpu.VMEM((ATTN_HEADS, tq), F32),
                        pltpu.VMEM((ATTN_HEADS, 8, tq), F32),
                        pltpu.VMEM((ATTN_HEADS, ATTN_HEAD_DIM, tq), F32),
                        pltpu.VMEM((ATTN_HEADS, tk, tq), BF16)])
    return pl.pallas_call(
        functools.partial(_dsa_kernel, tq=tq, tk=tk, topk=topk),
        grid_spec=grid_spec,
        out_shape=jax.ShapeDtypeStruct((b, s, w), BF16),
        compiler_params=_params("parallel", "arbitrary"),
        name="dsa",
    )(pi, pj, proj, proj, proj, ikd, kn, vt, qnorm, jnp.asarray(eq, BF16), jnp.asarray(esel, BF16), smax)


def _out_proj_kernel(yh_ref, ya_ref, w1_ref, w2_ref, x_ref, o_ref):
    o_ref[...] = x_ref[...] + _dot(yh_ref[...], w1_ref[...]) + _dot(ya_ref[...], w2_ref[...])


def out_proj(yh, ya, w_out, x, tm, tn):
    m, kh = yh.shape
    n = w_out.shape[1]
    return pl.pallas_call(
        _out_proj_kernel,
        grid=(m // tm, n // tn),
        in_specs=[pl.BlockSpec((tm, kh), lambda i, j: (i, 0)),
                  pl.BlockSpec((tm, kh), lambda i, j: (i, 0)),
                  pl.BlockSpec((kh, tn), lambda i, j: (0, j)),
                  pl.BlockSpec((kh, tn), lambda i, j: (1, j)),
                  pl.BlockSpec((tm, tn), lambda i, j: (i, j))],
        out_specs=pl.BlockSpec((tm, tn), lambda i, j: (i, j)),
        out_shape=jax.ShapeDtypeStruct((m, n), F32),
        compiler_params=_params("parallel", "arbitrary"),
        name="out_proj",
    )(yh, ya, w_out, w_out, x)


def _mem_kv_kernel(mem_ref, mg_ref, wk_ref, wv_ref, kg_ref, k_ref, v_ref):
    memn = _rms(mem_ref[0], mg_ref[...]).astype(BF16)
    k = _dot(memn, wk_ref[...])
    kg = kg_ref[...]
    for h in range(CROSS_HEADS):
        sl = slice(h * CROSS_HEAD_DIM, (h + 1) * CROSS_HEAD_DIM)
        k_ref[0, :, sl] = _rms(k[:, sl], kg).astype(BF16)
    v_ref[0] = _dot(memn, wv_ref[...]).astype(BF16)


def mem_kv(mem, mem_norm, wk, wv, xk_norm):
    b, nm, d = mem.shape
    cw = wk.shape[1]
    full = lambda a: pl.BlockSpec(a.shape, lambda bi: (0,) * a.ndim)
    return pl.pallas_call(
        _mem_kv_kernel,
        grid=(b,),
        in_specs=[pl.BlockSpec((1, nm, d), lambda bi: (bi, 0, 0)),
                  full(mem_norm), full(wk), full(wv), full(xk_norm)],
        out_specs=[pl.BlockSpec((1, nm, cw), lambda bi: (bi, 0, 0))] * 2,
        out_shape=[jax.ShapeDtypeStruct((b, nm, cw), BF16)] * 2,
        compiler_params=_params("parallel"),
        name="mem_kv",
    )(mem, mem_norm, wk, wv, xk_norm)


def _cross_kernel(h_ref, ng_ref, wq_ref, qg_ref, k_ref, v_ref, wo_ref, o_ref, hn_ref, oc_ref):
    rows = h_ref.shape[1]
    dh = CROSS_HEAD_DIM
    step = 256
    gain = ng_ref[...]
    def body(r, c):
        r0 = pl.multiple_of(r * step, step)
        hn_ref[pl.ds(r0, step), :] = _rms(h_ref[0, pl.ds(r0, step), :], gain).astype(BF16)
        return c
    lax.fori_loop(0, rows // step, body, 0)
    q = _dot(hn_ref[...], wq_ref[...])
    qg = qg_ref[...] * (dh ** -0.5)
    for h in range(CROSS_HEADS):
        sl = slice(h * dh, (h + 1) * dh)
        qn = _rms(q[:, sl], qg).astype(BF16)
        s = _dot_nt(qn, k_ref[0, :, sl])
        p = jnp.exp(s - jnp.max(s, axis=1, keepdims=True))
        l = jnp.sum(p, axis=1, keepdims=True)
        oc_ref[:, sl] = (_dot(p.astype(BF16), v_ref[0, :, sl]) / l).astype(BF16)
    o_ref[0] = h_ref[0] + _dot(oc_ref[...], wo_ref[...])


def cross_attention(h, norm_cross, wq, xq_norm, kx, vx, wo, tm):
    b, s, d = h.shape
    nm, cw = kx.shape[1:]
    full = lambda a: pl.BlockSpec(a.shape, lambda bi, i: (0,) * a.ndim)
    return pl.pallas_call(
        _cross_kernel,
        grid=(b, s // tm),
        in_specs=[pl.BlockSpec((1, tm, d), lambda bi, i: (bi, i, 0)),
                  full(norm_cross), full(wq), full(xq_norm),
                  pl.BlockSpec((1, nm, cw), lambda bi, i: (bi, 0, 0)),
                  pl.BlockSpec((1, nm, cw), lambda bi, i: (bi, 0, 0)),
                  full(wo)],
        out_specs=pl.BlockSpec((1, tm, d), lambda bi, i: (bi, i, 0)),
        out_shape=jax.ShapeDtypeStruct((b, s, d), F32),
        scratch_shapes=[pltpu.VMEM((tm, d), BF16), pltpu.VMEM((tm, cw), BF16)],
        compiler_params=_params("parallel", "parallel"),
        name="cross_attn",
    )(h, norm_cross, wq, xq_norm, kx, vx, wo)


def _mlp_kernel(h_ref, g_ref, wu_ref, wd_ref, o_ref, hn_ref):
    @pl.when(pl.program_id(1) == 0)
    def _():
        _norm_rows_to(h_ref, g_ref[...], hn_ref, h_ref.shape[0])
        o_ref[...] = h_ref[...]

    u = jnp.maximum(_dot(hn_ref[...], wu_ref[0].astype(BF16)), 0.0)
    o_ref[...] += _dot((u * u).astype(BF16), wd_ref[0].astype(BF16))


def mlp(h, gain, w_up, w_down, layer, tm, tf):
    m, d = h.shape
    f = w_up.shape[2]
    return pl.pallas_call(
        _mlp_kernel,
        grid=(m // tm, f // tf),
        in_specs=[pl.BlockSpec((tm, d), lambda i, j: (i, 0)),
                  pl.BlockSpec((1, d), lambda i, j: (0, 0)),
                  pl.BlockSpec((1, d, tf), lambda i, j: (layer, 0, j)),
                  pl.BlockSpec((1, tf, d), lambda i, j: (layer, j, 0))],
        out_specs=pl.BlockSpec((tm, d), lambda i, j: (i, 0), pipeline_mode=pl.Buffered(1)),
        out_shape=jax.ShapeDtypeStruct((m, d), F32),
        scratch_shapes=[pltpu.VMEM((tm, d), BF16)],
        compiler_params=_params("parallel", "arbitrary"),
        name="mlp",
    )(h, gain, w_up, w_down)


def _tile(n, pref):
    return pref if n % pref == 0 else n


def _tiles(n, s):
    return dict(
        proj_rows=_tile(n, 1024), proj_cols=768,
        hgrn_rows=_tile(s, 512),
        dsa_block=_tile(s, 512),
        out_rows=_tile(n, 1024), out_cols=1024,
        cross_rows=_tile(s, 512),
        mlp_rows=_tile(n, 1024), mlp_ff=512)


def kernel(x, mem, norm_mix, w_in, hgrn_lb_logits, hgrn_onorm, attn_qnorm, attn_knorm, w_out,
           norm_cross, mem_norm, wq_x, wk_x, wv_x, wo_x, xq_norm, xk_norm,
           norm_mlp, w_up, w_down):
    b, s, d = x.shape
    n = b * s
    depth = w_in.shape[0]
    assert depth == 1
    l = 0
    in_width = w_in.shape[2]
    main_w = 8 * HGRN_HEADS * HGRN_KDIM
    assert in_width == main_w + IDX_DIM + IDX_HEADS
    pad_w = main_w + TAIL

    t = _tiles(n, s)
    x2 = x.reshape(n, d)

    proj = norm_matmul(x2, norm_mix[l:l + 1], w_in[l].T, pad_w,
                       t["proj_rows"], t["proj_cols"])
    proj3 = proj.reshape(b, s, pad_w)

    y_h = hgrn_group(proj3, hgrn_lb_logits, hgrn_onorm[l:l + 1], t["hgrn_rows"])
    kn, vt, ikd = dsa_prep(proj3, attn_knorm[l:l + 1], t["dsa_block"])
    y_a = dsa_group(proj3, kn, vt, ikd, attn_qnorm[l:l + 1], attn_knorm[l:l + 1], t["dsa_block"])

    h1 = out_proj(y_h.reshape(n, -1), y_a.reshape(n, -1), w_out[l].astype(BF16), x2,
                  t["out_rows"], t["out_cols"])

    kx, vx = mem_kv(mem, mem_norm[l:l + 1], wk_x[l].astype(BF16), wv_x[l].astype(BF16), xk_norm[l:l + 1])
    h2 = cross_attention(h1.reshape(b, s, d), norm_cross[l:l + 1], wq_x[l].astype(BF16), xq_norm[l:l + 1],
                         kx, vx, wo_x[l].astype(BF16), t["cross_rows"])

    h3 = mlp(h2.reshape(n, d), norm_mlp[l:l + 1], w_up, w_down, l, t["mlp_rows"], t["mlp_ff"])
    return h3.reshape(b, s, d)
```

```python
import functools

import numpy as np
import jax
import jax.numpy as jnp
from jax import lax
from jax.experimental import pallas as pl
from jax.experimental.pallas import tpu as pltpu

F32 = jnp.float32
BF16 = jnp.bfloat16
EPS = 1e-6

LANES = 128
HGRN_HEADS = 8
HGRN_KDIM = 128
HGRN_CHUNK = 64
ATTN_HEADS = 8
ATTN_HEAD_DIM = 128
IDX_HEADS = 16
IDX_DIM = 64
DSA_TOPK = 256
CROSS_HEADS = 4
CROSS_HEAD_DIM = 128
TAIL = 256
VMEM_LIMIT = 56 * 1024 * 1024
NEG_INF = float("-inf")
INT_MIN = -(2 ** 31)
LOG2E = 1.4426950408889634
SOFTMAX_FIXED_SHIFT_MAX = 40.0


def _params(*sem, flags=None):
    return pltpu.CompilerParams(dimension_semantics=sem, vmem_limit_bytes=VMEM_LIMIT, flags=flags)


def _rms(x, gain):
    return x * lax.rsqrt(jnp.mean(x * x, axis=-1, keepdims=True) + EPS) * gain


def _sigmoid(x):
    return 1.0 / (1.0 + jnp.exp(-x))


def _dot(a, b):
    return jnp.dot(a, b, preferred_element_type=F32)


def _dot_nt(a, b):
    return lax.dot_general(a, b, (((1,), (1,)), ((), ())), preferred_element_type=F32)


def _dot_tn(a, b):
    return lax.dot_general(a, b, (((0,), (0,)), ((), ())), preferred_element_type=F32)


def _norm_rows_to(x_ref, gain, dst_ref, rows):
    step = 256
    def body(r, c):
        r0 = pl.multiple_of(r * step, step)
        x = x_ref[pl.ds(r0, step), :]
        dst_ref[pl.ds(r0, step), :] = _rms(x, gain).astype(BF16)
        return c
    lax.fori_loop(0, rows // step, body, 0)


def _norm_matmul_kernel(x_ref, g_ref, wt_ref, o_ref, xn_ref, *, valid_cols):
    j = pl.program_id(1)
    tn = o_ref.shape[1]

    @pl.when(j == 0)
    def _():
        _norm_rows_to(x_ref, g_ref[...], xn_ref, x_ref.shape[0])

    res = _dot_nt(xn_ref[...], wt_ref[...].astype(BF16))

    @pl.when((j + 1) * tn <= valid_cols)
    def _():
        o_ref[...] = res.astype(o_ref.dtype)

    @pl.when((j + 1) * tn > valid_cols)
    def _():
        col = j * tn + lax.broadcasted_iota(jnp.int32, res.shape, 1)
        o_ref[...] = jnp.where(col < valid_cols, res, 0.0).astype(o_ref.dtype)


def norm_matmul(x, gain, wt, n, tm, tn):
    m, k = x.shape
    return pl.pallas_call(
        functools.partial(_norm_matmul_kernel, valid_cols=wt.shape[0]),
        grid=(m // tm, n // tn),
        in_specs=[pl.BlockSpec((tm, k), lambda i, j: (i, 0)),
                  pl.BlockSpec((1, k), lambda i, j: (0, 0)),
                  pl.BlockSpec((tn, k), lambda i, j: (j, 0))],
        out_specs=pl.BlockSpec((tm, tn), lambda i, j: (i, j)),
        out_shape=jax.ShapeDtypeStruct((m, n), BF16),
        scratch_shapes=[pltpu.VMEM((tm, k), BF16)],
        compiler_params=_params("parallel", "arbitrary"),
        name="in_proj",
    )(x, gain, wt)


def _hgrn_kernel(q_ref, f_ref, i_ref, g_ref, lbl_ref, on_ref, tril_ref, o_ref,
                 state_ref, a_ref, qi_ref, ks_ref, el_ref, qt_ref, kt_ref, sc_ref, scb_ref, *, nchunks):
    c = HGRN_CHUNK
    dk = HGRN_KDIM
    pw = 2 * dk
    npairs = HGRN_HEADS // 2
    nlev = 6

    @pl.when(pl.program_id(1) == 0)
    def _():
        state_ref[...] = jnp.zeros_like(state_ref)
        kt_ref[...] = jnp.zeros_like(kt_ref)

    lbl = lbl_ref[...]
    e = jnp.exp(lbl - jnp.max(lbl, axis=0, keepdims=True))
    lb = e[0:1] / jnp.sum(e, axis=0, keepdims=True)
    onorm = on_ref[...]
    tril = tril_ref[...]

    row = lax.broadcasted_iota(jnp.int32, (c, 2 * c), 0)
    lane = lax.broadcasted_iota(jnp.int32, (c, 2 * c), 1)
    col = lane & (c - 1)
    first = lane < c
    eye = row == col
    level_masks = []
    lg = nlev - 1
    while lg >= 0:
        level_masks.append(((row >> (lg + 1)) == (col >> (lg + 1)))
                           & (((row >> lg) & 1) == 1) & (((col >> lg) & 1) == 0))
        lg -= 1
    lane_p = lax.broadcasted_iota(jnp.int32, (c, pw), 1)
    odd_row = (lax.broadcasted_iota(jnp.int32, (c, pw), 0) & 1) == 1
    sub8 = lax.broadcasted_iota(jnp.int32, (8, pw), 0)
    zeros_st = jnp.zeros((dk, dk), BF16)

    def chunk_body(ci, carry):
        r0 = pl.multiple_of(ci * c, c)
        live = ci >= 0

        @pl.when(live)
        def _decay_factors():
            hq = q_ref[0, pl.ds(r0, c), :].astype(F32)
            hf = f_ref[0, pl.ds(r0, c), :].astype(F32)
            qf = hq * _sigmoid(hq) * (dk ** -0.5)
            f = lb + (1.0 - lb) * _sigmoid(hf)
            logf = jnp.log(f) * LOG2E
            kk = 1.0 - f
            g0 = logf.astype(BF16)
            g1 = (logf - g0.astype(F32)).astype(BF16)
            a_ref[...] = _dot(tril, g0) + _dot(tril, g1)
            for p in range(npairs):
                sl = slice(p * pw, (p + 1) * pw)
                a = a_ref[:, sl]
                row = lambda r, n: jnp.broadcast_to(a_ref[r:r + 1, sl], (n, pw))
                e0 = jnp.exp2(a)
                qp = qf[:, sl]
                kp = kk[:, sl]
                qi_ref[p] = (qp * e0).astype(BF16)
                ks_ref[p] = (kp * jnp.exp2(row(c - 1, c) - a)).astype(BF16)
                el_ref[p] = e0[c - 1:c]
                for l in range(nlev):
                    h = c >> (l + 1)
                    if h >= 4:
                        ref = jnp.concatenate([row(m * 2 * h + h - 1, 2 * h) for m in range(c // (2 * h))], axis=0)
                    elif h == 2:
                        ref = jnp.concatenate([jnp.where(sub8 < 4, row(8 * m + 1, 8), row(8 * m + 5, 8))
                                               for m in range(c // 8)], axis=0)
                    else:
                        ref = jnp.where(odd_row, pltpu.roll(a, 1, 0), a)
                    el = jnp.exp2(-jnp.abs(a - ref))
                    qt_ref[p, l] = (qp * el).astype(BF16)
                    kl = (kp * el).astype(BF16)
                    kt_ref[p, l, 0:c, 0:dk] = kl[:, 0:dk]
                    kt_ref[p, l, c:2 * c, dk:pw] = kl[:, dk:pw]
                qk = qp * kp
                diag = jnp.where(first, jnp.sum(qk[:, 0:dk], axis=1, keepdims=True),
                                 jnp.sum(qk[:, dk:pw], axis=1, keepdims=True))
                sc_ref[p] = jnp.where(eye, diag, 0.0)

        @pl.when(live)
        def _intra_chunk_scores():
            for p in range(npairs):
                sc = sc_ref[p]
                for l, msk in enumerate(level_masks):
                    sc = sc + jnp.where(msk, _dot_nt(qt_ref[p, l], kt_ref[p, l]), 0.0)
                scb_ref[p] = sc.astype(BF16)

        @pl.when(live)
        def _outputs_and_state():
            for p in range(npairs):
                sl = slice(p * pw, (p + 1) * pw)
                vp = i_ref[0, pl.ds(r0, c), sl]
                st_a = state_ref[2 * p]
                st_b = state_ref[2 * p + 1]
                st_bd = jnp.concatenate(
                    [jnp.concatenate([st_a.astype(BF16), zeros_st], axis=1),
                     jnp.concatenate([zeros_st, st_b.astype(BF16)], axis=1)], axis=0)
                v_bd = jnp.concatenate([jnp.where(lane_p < dk, vp, jnp.zeros_like(vp)),
                                        jnp.where(lane_p >= dk, vp, jnp.zeros_like(vp))], axis=0)
                o = _dot_nt(qi_ref[p], st_bd) + _dot(scb_ref[p], v_bd)
                upd = _dot_tn(vp, ks_ref[p])
                el = el_ref[p]
                state_ref[2 * p] = st_a * el[:, 0:dk] + upd[0:dk, 0:dk]
                state_ref[2 * p + 1] = st_b * el[:, dk:pw] + upd[dk:pw, dk:pw]
                gate = g_ref[0, pl.ds(r0, c), sl].astype(F32)
                y = jnp.concatenate([_rms(o[:, 0:dk], onorm), _rms(o[:, dk:pw], onorm)], axis=1)
                o_ref[0, pl.ds(r0, c), sl] = (y * (gate * _sigmoid(gate))).astype(o_ref.dtype)
        return carry

    lax.fori_loop(0, nchunks, chunk_body, 0)


def hgrn_group(proj, lb_logits, onorm, t_blk):
    b, s, _ = proj.shape
    w = HGRN_HEADS * HGRN_KDIM
    tril = jnp.asarray(np.tril(np.ones((HGRN_CHUNK, HGRN_CHUNK), np.float32)), BF16)
    c, pw, npairs, nlev = HGRN_CHUNK, 2 * HGRN_KDIM, HGRN_HEADS // 2, 6
    col = lambda cb: pl.BlockSpec((1, t_blk, w), lambda bi, ti, cb=cb: (bi, ti, cb))
    return pl.pallas_call(
        functools.partial(_hgrn_kernel, nchunks=t_blk // HGRN_CHUNK),
        grid=(b, s // t_blk),
        in_specs=[col(0), col(1), col(2), col(3),
                  pl.BlockSpec(lb_logits.shape, lambda bi, ti: (0, 0)),
                  pl.BlockSpec((1, HGRN_KDIM), lambda bi, ti: (0, 0)),
                  pl.BlockSpec(tril.shape, lambda bi, ti: (0, 0))],
        out_specs=pl.BlockSpec((1, t_blk, w), lambda bi, ti: (bi, ti, 0)),
        out_shape=jax.ShapeDtypeStruct((b, s, w), BF16),
        scratch_shapes=[pltpu.VMEM((HGRN_HEADS, HGRN_KDIM, HGRN_KDIM), F32),
                        pltpu.VMEM((c, w), F32),
                        pltpu.VMEM((npairs, c, pw), BF16),
                        pltpu.VMEM((npairs, c, pw), BF16),
                        pltpu.VMEM((npairs, 1, pw), F32),
                        pltpu.VMEM((npairs, nlev, c, pw), BF16),
                        pltpu.VMEM((npairs, nlev, 2 * c, pw), BF16),
                        pltpu.VMEM((npairs, c, 2 * c), F32),
                        pltpu.VMEM((npairs, c, 2 * c), BF16)],
        compiler_params=_params("parallel", "arbitrary"),
        name="hgrn2",
    )(proj, proj, proj, proj, lb_logits, onorm, tril)


def _dsa_prep_kernel(k_ref, v_ref, tail_ref, kg_ref, dup_ref, kn_ref, vt_ref, ikd_ref):
    kg = kg_ref[...]
    for h in range(ATTN_HEADS):
        sl = slice(h * ATTN_HEAD_DIM, (h + 1) * ATTN_HEAD_DIM)
        kn_ref[0, :, sl] = _rms(k_ref[0, :, sl].astype(F32), kg).astype(BF16)
        vt_ref[0, 0, sl, :] = v_ref[0, :, sl].astype(F32).T.astype(BF16)
    ikd_ref[0] = _dot(tail_ref[0], dup_ref[...]).astype(BF16)


def dsa_prep(proj, knorm, tm):
    b, s, _ = proj.shape
    w = ATTN_HEADS * ATTN_HEAD_DIM
    dup = np.zeros((TAIL, LANES), np.float32)
    dup[np.arange(IDX_DIM), np.arange(IDX_DIM)] = 1.0
    dup[np.arange(IDX_DIM), np.arange(IDX_DIM) + IDX_DIM] = 1.0
    return pl.pallas_call(
        _dsa_prep_kernel,
        grid=(b, s // tm),
        in_specs=[pl.BlockSpec((1, tm, w), lambda bi, i: (bi, i, 5)),
                  pl.BlockSpec((1, tm, w), lambda bi, i: (bi, i, 6)),
                  pl.BlockSpec((1, tm, TAIL), lambda bi, i: (bi, i, 8 * w // TAIL)),
                  pl.BlockSpec((1, ATTN_HEAD_DIM), lambda bi, i: (0, 0)),
                  pl.BlockSpec((TAIL, LANES), lambda bi, i: (0, 0))],
        out_specs=[pl.BlockSpec((1, tm, w), lambda bi, i: (bi, i, 0)),
                   pl.BlockSpec((1, 1, w, tm), lambda bi, i: (bi, i, 0, 0)),
                   pl.BlockSpec((1, tm, LANES), lambda bi, i: (bi, i, 0))],
        out_shape=[jax.ShapeDtypeStruct((b, s, w), BF16),
                   jax.ShapeDtypeStruct((b, s // tm, w, tm), BF16),
                   jax.ShapeDtypeStruct((b, s, LANES), BF16)],
        compiler_params=_params("parallel", "parallel"),
        name="dsa_prep",
    )(proj, proj, proj, knorm, jnp.asarray(dup, BF16))


def _dsa_kernel(pi_ref, pj_ref, aq_ref, iq_ref, tail_ref, ikd_ref, kn_ref, vt_ref, qg_ref, eq_ref, esel_ref,
                smax_ref, o_ref, keys_ref, thr_ref, qn_ref, qw_ref, lo_ref, hi_ref, m_ref, l_ref, acc_ref, p_ref,
                *, tq, tk, topk):
    step = pl.program_id(1)
    i = pi_ref[step]
    j = pj_ref[step]
    dh = ATTN_HEAD_DIM

    def causal(jj):
        s_pos = jj * tk + lax.broadcasted_iota(jnp.int32, (tk, tq), 0)
        t_pos = i * tq + lax.broadcasted_iota(jnp.int32, (tk, tq), 1)
        return s_pos <= t_pos

    @pl.when(j == 0)
    def _index_and_select():
        qg = qg_ref[...] * (dh ** -0.5 * LOG2E)
        for h in range(ATTN_HEADS):
            sl = slice(h * dh, (h + 1) * dh)
            qn_ref[:, sl] = _rms(aq_ref[0, :, sl].astype(F32), qg).astype(BF16)
        tail = tail_ref[0]
        ww = _dot(tail, eq_ref[...])
        qw = (iq_ref[0].astype(F32) * ww * (IDX_HEADS ** -0.5 * IDX_DIM ** -0.5)).astype(BF16)
        lane = lax.broadcasted_iota(jnp.int32, (tq, LANES), 1)
        zero = jnp.zeros((tq, LANES), BF16)
        for p in range(IDX_HEADS // 2):
            pair = qw[:, p * LANES:(p + 1) * LANES]
            qw_ref[2 * p] = jnp.where(lane < IDX_DIM, pair, zero)
            qw_ref[2 * p + 1] = jnp.where(lane >= IDX_DIM, pair, zero)
        w_t = _dot_nt(esel_ref[...], tail)
        lo_ref[...] = jnp.where(w_t > 0.0, 0.0, NEG_INF)
        hi_ref[...] = jnp.where(w_t > 0.0, jnp.inf, 0.0)

        def score_tile(jj, c):
            r0 = pl.multiple_of(jj * tk, tk)
            ik = ikd_ref[0, pl.ds(r0, tk), :]
            comb = jnp.zeros((tk, tq), F32)
            for h in range(IDX_HEADS):
                x = _dot_nt(ik, qw_ref[h])
                comb = comb + jnp.minimum(jnp.maximum(x, lo_ref[h:h + 1, :]), hi_ref[h:h + 1, :])
            keys_ref[jj] = jnp.where(causal(jj), comb, NEG_INF)
            return c
        lax.fori_loop(0, i + 1, score_tile, 0)

        def as_float(t):
            return pltpu.bitcast(jnp.where(t < 0, t ^ 0x7FFFFFFF, t), F32)
        def count_ge(tf):
            def body(jj, acc):
                accs = [acc, jnp.zeros_like(acc), jnp.zeros_like(acc), jnp.zeros_like(acc)]
                for r0 in range(0, tk, 8):
                    a = accs[(r0 // 8) % 4]
                    accs[(r0 // 8) % 4] = jnp.where(keys_ref[jj, r0:r0 + 8, :] >= tf, a + 1, a)
                return (accs[0] + accs[1]) + (accs[2] + accs[3])
            acc = lax.fori_loop(0, i + 1, body, jnp.zeros((8, tq), jnp.int32))
            return jnp.sum(acc.astype(F32), axis=0, keepdims=True)
        def unresolved(carry):
            bi, _, cnt = carry
            return jnp.logical_and(bi < 32, jnp.max(jnp.abs(cnt - topk)) > 0.0)
        def bit_steps(carry):
            bi, t, cnt = carry
            for k in range(4):
                tc = t + (jnp.int32(1) << (31 - bi - k))
                c = count_ge(as_float(tc))
                take = c >= topk
                t = jnp.where(take, tc, t)
                cnt = jnp.where(take, c, cnt)
            return bi + 4, t, cnt
        n_all = ((i + 1) * tk).astype(F32)
        _, t_fin, _ = lax.while_loop(unresolved, bit_steps,
                                     (jnp.int32(0), jnp.full((1, tq), INT_MIN, jnp.int32),
                                      jnp.full((1, tq), n_all, F32)))
        t_float = as_float(t_fin)
        thr_ref[...] = jnp.where(t_float != t_float, NEG_INF, t_float)

        m_ref[...] = jnp.full_like(m_ref, NEG_INF)
        l_ref[...] = jnp.zeros_like(l_ref)
        acc_ref[...] = jnp.zeros_like(acc_ref)

    sel = (keys_ref[j] >= thr_ref[...]) & causal(j)
    smax = smax_ref[0]
    fixed_shift = smax <= SOFTMAX_FIXED_SHIFT_MAX

    @pl.when(fixed_shift)
    def _attend_fixed_shift():
        bias = jnp.where(sel, -smax * LOG2E, NEG_INF)
        for h in range(ATTN_HEADS):
            sl = slice(h * dh, (h + 1) * dh)
            p = jnp.exp2(_dot_nt(kn_ref[0, :, sl], qn_ref[:, sl]) + bias)
            parts = [p[r * 8:(r + 1) * 8] for r in range(tk // 8)]
            while len(parts) > 1:
                parts = [a + b for a, b in zip(parts[0::2], parts[1::2])]
            l_ref[h] += parts[0]
            p_ref[h] = p.astype(BF16)

    @pl.when(jnp.logical_and(fixed_shift, j <= i))
    def _weighted_values():
        for h in range(ATTN_HEADS):
            sl = slice(h * dh, (h + 1) * dh)
            acc_ref[h] += _dot(vt_ref[0, 0, sl, :], p_ref[h])

    @pl.when(jnp.logical_not(fixed_shift))
    def _attend_running_max():
        bias = jnp.where(sel, 0.0, NEG_INF)
        m_all = m_ref[...]
        m_rows = []
        for h in range(ATTN_HEADS):
            sl = slice(h * dh, (h + 1) * dh)
            s = _dot_nt(kn_ref[0, :, sl], qn_ref[:, sl]) + bias
            m_old = m_all[h:h + 1, :]
            m_new = jnp.maximum(m_old, jnp.max(s, axis=0, keepdims=True))
            m_safe = jnp.where(m_new == NEG_INF, 0.0, m_new)
            alpha = jnp.exp2(m_old - m_safe)
            p = jnp.exp2(s - m_safe)
            l_ref[h, 0:1, :] = alpha * l_ref[h, 0:1, :] + jnp.sum(p, axis=0, keepdims=True)
            acc_ref[h] = alpha * acc_ref[h] + _dot(vt_ref[0, 0, sl, :], p.astype(BF16))
            m_rows.append(m_new)
        m_ref[...] = jnp.concatenate(m_rows, axis=0)

    @pl.when(j == i)
    def _finish():
        for h in range(ATTN_HEADS):
            sl = slice(h * dh, (h + 1) * dh)
            l = jnp.sum(l_ref[h], axis=0, keepdims=True)
            o_ref[0, :, sl] = (acc_ref[h] / l).T.astype(o_ref.dtype)


def dsa_group(proj, kn, vt, ikd, qnorm, knorm, tq):
    b, s, _ = proj.shape
    tk = tq
    smax = (ATTN_HEAD_DIM ** 0.5 * jnp.max(jnp.abs(qnorm)) * jnp.max(jnp.abs(knorm))).reshape(1).astype(F32)
    nq = s // tq
    w = ATTN_HEADS * ATTN_HEAD_DIM
    topk = min(DSA_TOPK, s // 4)
    eq = np.zeros((TAIL, IDX_HEADS * IDX_DIM), np.float32)
    esel = np.zeros((IDX_HEADS, TAIL), np.float32)
    for h in range(IDX_HEADS):
        eq[IDX_DIM + h, h * IDX_DIM:(h + 1) * IDX_DIM] = 1.0
        esel[h, IDX_DIM + h] = 1.0
    pairs = [(i, j) for i in range(nq) for j in range(i + 1)]
    pi = jnp.asarray([p[0] for p in pairs], jnp.int32)
    pj = jnp.asarray([p[1] for p in pairs], jnp.int32)
    qblk = lambda cb: pl.BlockSpec((1, tq, w), lambda bi, st, pi, pj, cb=cb: (bi, pi[st], cb))
    const = lambda a: pl.BlockSpec(a.shape, lambda bi, st, pi, pj: (0,) * a.ndim)
    grid_spec = pltpu.PrefetchScalarGridSpec(
        num_scalar_prefetch=2,
        grid=(b, len(pairs)),
        in_specs=[qblk(4),
                  qblk(7),
                  pl.BlockSpec((1, tq, TAIL), lambda bi, st, pi, pj: (bi, pi[st], 8 * w // TAIL)),
                  pl.BlockSpec((1, s, LANES), lambda bi, st, pi, pj: (bi, 0, 0)),
                  pl.BlockSpec((1, tk, w), lambda bi, st, pi, pj: (bi, pj[st], 0)),
                  pl.BlockSpec((1, 1, w, tk), lambda bi, st, pi, pj: (bi, pj[st], 0, 0)),
                  const(qnorm), const(eq), const(esel),
                  pl.BlockSpec(memory_space=pltpu.SMEM)],
        out_specs=pl.BlockSpec((1, tq, w), lambda bi, st, pi, pj: (bi, pi[st], 0)),
        scratch_shapes=[pltpu.VMEM((nq, tk, tq), F32),
                        pltpu.VMEM((1, tq), F32),
                        pltpu.VMEM((tq, w), BF16),
                        pltpu.VMEM((IDX_HEADS, tq, LANES), BF16),
                        pltpu.VMEM((IDX_HEADS, tq), F32),
                        pltpu.VMEM((IDX_HEADS, tq), F32),
                        pltpu.VMEM((ATTN_HEADS, tq), F32),
                        pltpu.VMEM((ATTN_HEADS, 8, tq), F32),
                        pltpu.VMEM((ATTN_HEADS, ATTN_HEAD_DIM, tq), F32),
                        pltpu.VMEM((ATTN_HEADS, tk, tq), BF16)])
    return pl.pallas_call(
        functools.partial(_dsa_kernel, tq=tq, tk=tk, topk=topk),
        grid_spec=grid_spec,
        out_shape=jax.ShapeDtypeStruct((b, s, w), BF16),
        compiler_params=_params("parallel", "arbitrary"),
        name="dsa",
    )(pi, pj, proj, proj, proj, ikd, kn, vt, qnorm, jnp.asarray(eq, BF16), jnp.asarray(esel, BF16), smax)


def _out_proj_kernel(yh_ref, ya_ref, w1_ref, w2_ref, x_ref, o_ref):
    o_ref[...] = x_ref[...] + _dot(yh_ref[...], w1_ref[...]) + _dot(ya_ref[...], w2_ref[...])


def out_proj(yh, ya, w_out, x, tm, tn):
    m, kh = yh.shape
    n = w_out.shape[1]
    return pl.pallas_call(
        _out_proj_kernel,
        grid=(m // tm, n // tn),
        in_specs=[pl.BlockSpec((tm, kh), lambda i, j: (i, 0)),
                  pl.BlockSpec((tm, kh), lambda i, j: (i, 0)),
                  pl.BlockSpec((kh, tn), lambda i, j: (0, j)),
                  pl.BlockSpec((kh, tn), lambda i, j: (1, j)),
                  pl.BlockSpec((tm, tn), lambda i, j: (i, j))],
        out_specs=pl.BlockSpec((tm, tn), lambda i, j: (i, j)),
        out_shape=jax.ShapeDtypeStruct((m, n), F32),
        compiler_params=_params("parallel", "arbitrary"),
        name="out_proj",
    )(yh, ya, w_out, w_out, x)


def _mem_kv_kernel(mem_ref, mg_ref, wk_ref, wv_ref, kg_ref, k_ref, v_ref):
    memn = _rms(mem_ref[0], mg_ref[...]).astype(BF16)
    k = _dot(memn, wk_ref[...])
    kg = kg_ref[...]
    for h in range(CROSS_HEADS):
        sl = slice(h * CROSS_HEAD_DIM, (h + 1) * CROSS_HEAD_DIM)
        k_ref[0, :, sl] = _rms(k[:, sl], kg).astype(BF16)
    v_ref[0] = _dot(memn, wv_ref[...]).astype(BF16)


def mem_kv(mem, mem_norm, wk, wv, xk_norm):
    b, nm, d = mem.shape
    cw = wk.shape[1]
    full = lambda a: pl.BlockSpec(a.shape, lambda bi: (0,) * a.ndim)
    return pl.pallas_call(
        _mem_kv_kernel,
        grid=(b,),
        in_specs=[pl.BlockSpec((1, nm, d), lambda bi: (bi, 0, 0)),
                  full(mem_norm), full(wk), full(wv), full(xk_norm)],
        out_specs=[pl.BlockSpec((1, nm, cw), lambda bi: (bi, 0, 0))] * 2,
        out_shape=[jax.ShapeDtypeStruct((b, nm, cw), BF16)] * 2,
        compiler_params=_params("parallel"),
        name="mem_kv",
    )(mem, mem_norm, wk, wv, xk_norm)


def _cross_kernel(h_ref, ng_ref, wq_ref, qg_ref, k_ref, v_ref, wo_ref, o_ref, hn_ref, oc_ref):
    rows = h_ref.shape[1]
    dh = CROSS_HEAD_DIM
    step = 256
    gain = ng_ref[...]
    def body(r, c):
        r0 = pl.multiple_of(r * step, step)
        hn_ref[pl.ds(r0, step), :] = _rms(h_ref[0, pl.ds(r0, step), :], gain).astype(BF16)
        return c
    lax.fori_loop(0, rows // step, body, 0)
    q = _dot(hn_ref[...], wq_ref[...])
    qg = qg_ref[...] * (dh ** -0.5)
    for h in range(CROSS_HEADS):
        sl = slice(h * dh, (h + 1) * dh)
        qn = _rms(q[:, sl], qg).astype(BF16)
        s = _dot_nt(qn, k_ref[0, :, sl])
        p = jnp.exp(s - jnp.max(s, axis=1, keepdims=True))
        l = jnp.sum(p, axis=1, keepdims=True)
        oc_ref[:, sl] = (_dot(p.astype(BF16), v_ref[0, :, sl]) / l).astype(BF16)
    o_ref[0] = h_ref[0] + _dot(oc_ref[...], wo_ref[...])


def cross_attention(h, norm_cross, wq, xq_norm, kx, vx, wo, tm):
    b, s, d = h.shape
    nm, cw = kx.shape[1:]
    full = lambda a: pl.BlockSpec(a.shape, lambda bi, i: (0,) * a.ndim)
    return pl.pallas_call(
        _cross_kernel,
        grid=(b, s // tm),
        in_specs=[pl.BlockSpec((1, tm, d), lambda bi, i: (bi, i, 0)),
                  full(norm_cross), full(wq), full(xq_norm),
                  pl.BlockSpec((1, nm, cw), lambda bi, i: (bi, 0, 0)),
                  pl.BlockSpec((1, nm, cw), lambda bi, i: (bi, 0, 0)),
                  full(wo)],
        out_specs=pl.BlockSpec((1, tm, d), lambda bi, i: (bi, i, 0)),
        out_shape=jax.ShapeDtypeStruct((b, s, d), F32),
        scratch_shapes=[pltpu.VMEM((tm, d), BF16), pltpu.VMEM((tm, cw), BF16)],
        compiler_params=_params("parallel", "parallel"),
        name="cross_attn",
    )(h, norm_cross, wq, xq_norm, kx, vx, wo)


def _mlp_kernel(h_ref, g_ref, wu_ref, wd_ref, o_ref, hn_ref):
    @pl.when(pl.program_id(1) == 0)
    def _():
        _norm_rows_to(h_ref, g_ref[...], hn_ref, h_ref.shape[0])
        o_ref[...] = h_ref[...]

    u = jnp.maximum(_dot(hn_ref[...], wu_ref[0].astype(BF16)), 0.0)
    o_ref[...] += _dot((u * u).astype(BF16), wd_ref[0].astype(BF16))


def mlp(h, gain, w_up, w_down, layer, tm, tf):
    m, d = h.shape
    f = w_up.shape[2]
    return pl.pallas_call(
        _mlp_kernel,
        grid=(m // tm, f // tf),
        in_specs=[pl.BlockSpec((tm, d), lambda i, j: (i, 0)),
                  pl.BlockSpec((1, d), lambda i, j: (0, 0)),
                  pl.BlockSpec((1, d, tf), lambda i, j: (layer, 0, j)),
                  pl.BlockSpec((1, tf, d), lambda i, j: (layer, j, 0))],
        out_specs=pl.BlockSpec((tm, d), lambda i, j: (i, 0), pipeline_mode=pl.Buffered(1)),
        out_shape=jax.ShapeDtypeStruct((m, d), F32),
        scratch_shapes=[pltpu.VMEM((tm, d), BF16)],
        compiler_params=_params("parallel", "arbitrary"),
        name="mlp",
    )(h, gain, w_up, w_down)


def _tile(n, pref):
    return pref if n % pref == 0 else n


def _tiles(n, s):
    return dict(
        proj_rows=_tile(n, 1024), proj_cols=768,
        hgrn_rows=_tile(s, 512),
        dsa_block=_tile(s, 512),
        out_rows=_tile(n, 1024), out_cols=1024,
        cross_rows=_tile(s, 512),
        mlp_rows=_tile(n, 1024), mlp_ff=512)


def kernel(x, mem, norm_mix, w_in, hgrn_lb_logits, hgrn_onorm, attn_qnorm, attn_knorm, w_out,
           norm_cross, mem_norm, wq_x, wk_x, wv_x, wo_x, xq_norm, xk_norm,
           norm_mlp, w_up, w_down):
    b, s, d = x.shape
    n = b * s
    depth = w_in.shape[0]
    assert depth == 1
    l = 0
    in_width = w_in.shape[2]
    main_w = 8 * HGRN_HEADS * HGRN_KDIM
    assert in_width == main_w + IDX_DIM + IDX_HEADS
    pad_w = main_w + TAIL

    t = _tiles(n, s)
    x2 = x.reshape(n, d)

    proj = norm_matmul(x2, norm_mix[l:l + 1], w_in[l].T, pad_w,
                       t["proj_rows"], t["proj_cols"])
    proj3 = proj.reshape(b, s, pad_w)

    y_h = hgrn_group(proj3, hgrn_lb_logits, hgrn_onorm[l:l + 1], t["hgrn_rows"])
    kn, vt, ikd = dsa_prep(proj3, attn_knorm[l:l + 1], t["dsa_block"])
    y_a = dsa_group(proj3, kn, vt, ikd, attn_qnorm[l:l + 1], attn_knorm[l:l + 1], t["dsa_block"])

    h1 = out_proj(y_h.reshape(n, -1), y_a.reshape(n, -1), w_out[l].astype(BF16), x2,
                  t["out_rows"], t["out_cols"])

    kx, vx = mem_kv(mem, mem_norm[l:l + 1], wk_x[l].astype(BF16), wv_x[l].astype(BF16), xk_norm[l:l + 1])
    h2 = cross_attention(h1.reshape(b, s, d), norm_cross[l:l + 1], wq_x[l].astype(BF16), xq_norm[l:l + 1],
                         kx, vx, wo_x[l].astype(BF16), t["cross_rows"])

    h3 = mlp(h2.reshape(n, d), norm_mlp[l:l + 1], w_up, w_down, l, t["mlp_rows"], t["mlp_ff"])
    return h3.reshape(b, s, d)
```

```python
import functools

import numpy as np
import jax
import jax.numpy as jnp
from jax import lax
from jax.experimental import pallas as pl
from jax.experimental.pallas import tpu as pltpu

F32 = jnp.float32
BF16 = jnp.bfloat16
EPS = 1e-6

LANES = 128
HGRN_HEADS = 8
HGRN_KDIM = 128
HGRN_CHUNK = 64
ATTN_HEADS = 8
ATTN_HEAD_DIM = 128
IDX_HEADS = 16
IDX_DIM = 64
DSA_TOPK = 256
CROSS_HEADS = 4
CROSS_HEAD_DIM = 128
TAIL = 256
VMEM_LIMIT = 56 * 1024 * 1024
NEG_INF = float("-inf")
INT_MIN = -(2 ** 31)
LOG2E = 1.4426950408889634
SOFTMAX_FIXED_SHIFT_MAX = 40.0


def _params(*sem, flags=None):
    return pltpu.CompilerParams(dimension_semantics=sem, vmem_limit_bytes=VMEM_LIMIT, flags=flags)


def _rms(x, gain):
    return x * lax.rsqrt(jnp.mean(x * x, axis=-1, keepdims=True) + EPS) * gain


def _sigmoid(x):
    return 1.0 / (1.0 + jnp.exp(-x))


def _dot(a, b):
    return jnp.dot(a, b, preferred_element_type=F32)


def _dot_nt(a, b):
    return lax.dot_general(a, b, (((1,), (1,)), ((), ())), preferred_element_type=F32)


def _dot_tn(a, b):
    return lax.dot_general(a, b, (((0,), (0,)), ((), ())), preferred_element_type=F32)


def _norm_rows_to(x_ref, gain, dst_ref, rows):
    step = 256
    def body(r, c):
        r0 = pl.multiple_of(r * step, step)
        x = x_ref[pl.ds(r0, step), :]
        dst_ref[pl.ds(r0, step), :] = _rms(x, gain).astype(BF16)
        return c
    lax.fori_loop(0, rows // step, body, 0)


def _norm_matmul_kernel(x_ref, g_ref, wt_ref, o_ref, xn_ref, *, valid_cols):
    j = pl.program_id(1)
    tn = o_ref.shape[1]

    @pl.when(j == 0)
    def _():
        _norm_rows_to(x_ref, g_ref[...], xn_ref, x_ref.shape[0])

    res = _dot_nt(xn_ref[...], wt_ref[...].astype(BF16))

    @pl.when((j + 1) * tn <= valid_cols)
    def _():
        o_ref[...] = res.astype(o_ref.dtype)

    @pl.when((j + 1) * tn > valid_cols)
    def _():
        col = j * tn + lax.broadcasted_iota(jnp.int32, res.shape, 1)
        o_ref[...] = jnp.where(col < valid_cols, res, 0.0).astype(o_ref.dtype)


def norm_matmul(x, gain, wt, n, tm, tn):
    m, k = x.shape
    return pl.pallas_call(
        functools.partial(_norm_matmul_kernel, valid_cols=wt.shape[0]),
        grid=(m // tm, n // tn),
        in_specs=[pl.BlockSpec((tm, k), lambda i, j: (i, 0)),
                  pl.BlockSpec((1, k), lambda i, j: (0, 0)),
                  pl.BlockSpec((tn, k), lambda i, j: (j, 0))],
        out_specs=pl.BlockSpec((tm, tn), lambda i, j: (i, j)),
        out_shape=jax.ShapeDtypeStruct((m, n), BF16),
        scratch_shapes=[pltpu.VMEM((tm, k), BF16)],
        compiler_params=_params("parallel", "arbitrary"),
        name="in_proj",
    )(x, gain, wt)


def _hgrn_kernel(q_ref, f_ref, i_ref, g_ref, lbl_ref, on_ref, tril_ref, o_ref,
                 state_ref, a_ref, qi_ref, ks_ref, el_ref, qt_ref, kt_ref, sc_ref, scb_ref, *, nchunks):
    c = HGRN_CHUNK
    dk = HGRN_KDIM
    pw = 2 * dk
    npairs = HGRN_HEADS // 2
    nlev = 6

    @pl.when(pl.program_id(1) == 0)
    def _():
        state_ref[...] = jnp.zeros_like(state_ref)
        kt_ref[...] = jnp.zeros_like(kt_ref)

    lbl = lbl_ref[...]
    e = jnp.exp(lbl - jnp.max(lbl, axis=0, keepdims=True))
    lb = e[0:1] / jnp.sum(e, axis=0, keepdims=True)
    onorm = on_ref[...]
    tril = tril_ref[...]

    row = lax.broadcasted_iota(jnp.int32, (c, 2 * c), 0)
    lane = lax.broadcasted_iota(jnp.int32, (c, 2 * c), 1)
    col = lane & (c - 1)
    first = lane < c
    eye = row == col
    level_masks = []
    lg = nlev - 1
    while lg >= 0:
        level_masks.append(((row >> (lg + 1)) == (col >> (lg + 1)))
                           & (((row >> lg) & 1) == 1) & (((col >> lg) & 1) == 0))
        lg -= 1
    lane_p = lax.broadcasted_iota(jnp.int32, (c, pw), 1)
    odd_row = (lax.broadcasted_iota(jnp.int32, (c, pw), 0) & 1) == 1
    sub8 = lax.broadcasted_iota(jnp.int32, (8, pw), 0)
    zeros_st = jnp.zeros((dk, dk), BF16)

    def chunk_body(ci, carry):
        r0 = pl.multiple_of(ci * c, c)
        live = ci >= 0

        @pl.when(live)
        def _decay_factors():
            hq = q_ref[0, pl.ds(r0, c), :].astype(F32)
            hf = f_ref[0, pl.ds(r0, c), :].astype(F32)
            qf = hq * _sigmoid(hq) * (dk ** -0.5)
            f = lb + (1.0 - lb) * _sigmoid(hf)
            logf = jnp.log(f) * LOG2E
            kk = 1.0 - f
            g0 = logf.astype(BF16)
            g1 = (logf - g0.astype(F32)).astype(BF16)
            a_ref[...] = _dot(tril, g0) + _dot(tril, g1)
            for p in range(npairs):
                sl = slice(p * pw, (p + 1) * pw)
                a = a_ref[:, sl]
                row = lambda r, n: jnp.broadcast_to(a_ref[r:r + 1, sl], (n, pw))
                e0 = jnp.exp2(a)
                qp = qf[:, sl]
                kp = kk[:, sl]
                qi_ref[p] = (qp * e0).astype(BF16)
                ks_ref[p] = (kp * jnp.exp2(row(c - 1, c) - a)).astype(BF16)
                el_ref[p] = e0[c - 1:c]
                for l in range(nlev):
                    h = c >> (l + 1)
                    if h >= 4:
                        ref = jnp.concatenate([row(m * 2 * h + h - 1, 2 * h) for m in range(c // (2 * h))], axis=0)
                    elif h == 2:
                        ref = jnp.concatenate([jnp.where(sub8 < 4, row(8 * m + 1, 8), row(8 * m + 5, 8))
                                               for m in range(c // 8)], axis=0)
                    else:
                        ref = jnp.where(odd_row, pltpu.roll(a, 1, 0), a)
                    el = jnp.exp2(-jnp.abs(a - ref))
                    qt_ref[p, l] = (qp * el).astype(BF16)
                    kl = (kp * el).astype(BF16)
                    kt_ref[p, l, 0:c, 0:dk] = kl[:, 0:dk]
                    kt_ref[p, l, c:2 * c, dk:pw] = kl[:, dk:pw]
                qk = qp * kp
                diag = jnp.where(first, jnp.sum(qk[:, 0:dk], axis=1, keepdims=True),
                                 jnp.sum(qk[:, dk:pw], axis=1, keepdims=True))
                sc_ref[p] = jnp.where(eye, diag, 0.0)

        @pl.when(live)
        def _intra_chunk_scores():
            for p in range(npairs):
                sc = sc_ref[p]
                for l, msk in enumerate(level_masks):
                    sc = sc + jnp.where(msk, _dot_nt(qt_ref[p, l], kt_ref[p, l]), 0.0)
                scb_ref[p] = sc.astype(BF16)

        @pl.when(live)
        def _outputs_and_state():
            for p in range(npairs):
                sl = slice(p * pw, (p + 1) * pw)
                vp = i_ref[0, pl.ds(r0, c), sl]
                st_a = state_ref[2 * p]
                st_b = state_ref[2 * p + 1]
                st_bd = jnp.concatenate(
                    [jnp.concatenate([st_a.astype(BF16), zeros_st], axis=1),
                     jnp.concatenate([zeros_st, st_b.astype(BF16)], axis=1)], axis=0)
                v_bd = jnp.concatenate([jnp.where(lane_p < dk, vp, jnp.zeros_like(vp)),
                                        jnp.where(lane_p >= dk, vp, jnp.zeros_like(vp))], axis=0)
                o = _dot_nt(qi_ref[p], st_bd) + _dot(scb_ref[p], v_bd)
                upd = _dot_tn(vp, ks_ref[p])
                el = el_ref[p]
                state_ref[2 * p] = st_a * el[:, 0:dk] + upd[0:dk, 0:dk]
                state_ref[2 * p + 1] = st_b * el[:, dk:pw] + upd[dk:pw, dk:pw]
                gate = g_ref[0, pl.ds(r0, c), sl].astype(F32)
                y = jnp.concatenate([_rms(o[:, 0:dk], onorm), _rms(o[:, dk:pw], onorm)], axis=1)
                o_ref[0, pl.ds(r0, c), sl] = (y * (gate * _sigmoid(gate))).astype(o_ref.dtype)
        return carry

    lax.fori_loop(0, nchunks, chunk_body, 0)


def hgrn_group(proj, lb_logits, onorm, t_blk):
    b, s, _ = proj.shape
    w = HGRN_HEADS * HGRN_KDIM
    tril = jnp.asarray(np.tril(np.ones((HGRN_CHUNK, HGRN_CHUNK), np.float32)), BF16)
    c, pw, npairs, nlev = HGRN_CHUNK, 2 * HGRN_KDIM, HGRN_HEADS // 2, 6
    col = lambda cb: pl.BlockSpec((1, t_blk, w), lambda bi, ti, cb=cb: (bi, ti, cb))
    return pl.pallas_call(
        functools.partial(_hgrn_kernel, nchunks=t_blk // HGRN_CHUNK),
        grid=(b, s // t_blk),
        in_specs=[col(0), col(1), col(2), col(3),
                  pl.BlockSpec(lb_logits.shape, lambda bi, ti: (0, 0)),
                  pl.BlockSpec((1, HGRN_KDIM), lambda bi, ti: (0, 0)),
                  pl.BlockSpec(tril.shape, lambda bi, ti: (0, 0))],
        out_specs=pl.BlockSpec((1, t_blk, w), lambda bi, ti: (bi, ti, 0)),
        out_shape=jax.ShapeDtypeStruct((b, s, w), BF16),
        scratch_shapes=[pltpu.VMEM((HGRN_HEADS, HGRN_KDIM, HGRN_KDIM), F32),
                        pltpu.VMEM((c, w), F32),
                        pltpu.VMEM((npairs, c, pw), BF16),
                        pltpu.VMEM((npairs, c, pw), BF16),
                        pltpu.VMEM((npairs, 1, pw), F32),
                        pltpu.VMEM((npairs, nlev, c, pw), BF16),
                        pltpu.VMEM((npairs, nlev, 2 * c, pw), BF16),
                        pltpu.VMEM((npairs, c, 2 * c), F32),
                        pltpu.VMEM((npairs, c, 2 * c), BF16)],
        compiler_params=_params("parallel", "arbitrary"),
        name="hgrn2",
    )(proj, proj, proj, proj, lb_logits, onorm, tril)


def _dsa_prep_kernel(k_ref, v_ref, tail_ref, kg_ref, dup_ref, kn_ref, vt_ref, ikd_ref):
    kg = kg_ref[...]
    for h in range(ATTN_HEADS):
        sl = slice(h * ATTN_HEAD_DIM, (h + 1) * ATTN_HEAD_DIM)
        kn_ref[0, :, sl] = _rms(k_ref[0, :, sl].astype(F32), kg).astype(BF16)
        vt_ref[0, 0, sl, :] = v_ref[0, :, sl].astype(F32).T.astype(BF16)
    ikd_ref[0] = _dot(tail_ref[0], dup_ref[...]).astype(BF16)


def dsa_prep(proj, knorm, tm):
    b, s, _ = proj.shape
    w = ATTN_HEADS * ATTN_HEAD_DIM
    dup = np.zeros((TAIL, LANES), np.float32)
    dup[np.arange(IDX_DIM), np.arange(IDX_DIM)] = 1.0
    dup[np.arange(IDX_DIM), np.arange(IDX_DIM) + IDX_DIM] = 1.0
    return pl.pallas_call(
        _dsa_prep_kernel,
        grid=(b, s // tm),
        in_specs=[pl.BlockSpec((1, tm, w), lambda bi, i: (bi, i, 5)),
                  pl.BlockSpec((1, tm, w), lambda bi, i: (bi, i, 6)),
                  pl.BlockSpec((1, tm, TAIL), lambda bi, i: (bi, i, 8 * w // TAIL)),
                  pl.BlockSpec((1, ATTN_HEAD_DIM), lambda bi, i: (0, 0)),
                  pl.BlockSpec((TAIL, LANES), lambda bi, i: (0, 0))],
        out_specs=[pl.BlockSpec((1, tm, w), lambda bi, i: (bi, i, 0)),
                   pl.BlockSpec((1, 1, w, tm), lambda bi, i: (bi, i, 0, 0)),
                   pl.BlockSpec((1, tm, LANES), lambda bi, i: (bi, i, 0))],
        out_shape=[jax.ShapeDtypeStruct((b, s, w), BF16),
                   jax.ShapeDtypeStruct((b, s // tm, w, tm), BF16),
                   jax.ShapeDtypeStruct((b, s, LANES), BF16)],
        compiler_params=_params("parallel", "parallel"),
        name="dsa_prep",
    )(proj, proj, proj, knorm, jnp.asarray(dup, BF16))


def _dsa_kernel(pi_ref, pj_ref, aq_ref, iq_ref, tail_ref, ikd_ref, kn_ref, vt_ref, qg_ref, eq_ref, esel_ref,
                smax_ref, o_ref, keys_ref, thr_ref, qn_ref, qw_ref, lo_ref, hi_ref, m_ref, l_ref, acc_ref, p_ref,
                *, tq, tk, topk):
    step = pl.program_id(1)
    i = pi_ref[step]
    j = pj_ref[step]
    dh = ATTN_HEAD_DIM

    def causal(jj):
        s_pos = jj * tk + lax.broadcasted_iota(jnp.int32, (tk, tq), 0)
        t_pos = i * tq + lax.broadcasted_iota(jnp.int32, (tk, tq), 1)
        return s_pos <= t_pos

    @pl.when(j == 0)
    def _index_and_select():
        qg = qg_ref[...] * (dh ** -0.5 * LOG2E)
        for h in range(ATTN_HEADS):
            sl = slice(h * dh, (h + 1) * dh)
            qn_ref[:, sl] = _rms(aq_ref[0, :, sl].astype(F32), qg).astype(BF16)
        tail = tail_ref[0]
        ww = _dot(tail, eq_ref[...])
        qw = (iq_ref[0].astype(F32) * ww * (IDX_HEADS ** -0.5 * IDX_DIM ** -0.5)).astype(BF16)
        lane = lax.broadcasted_iota(jnp.int32, (tq, LANES), 1)
        zero = jnp.zeros((tq, LANES), BF16)
        for p in range(IDX_HEADS // 2):
            pair = qw[:, p * LANES:(p + 1) * LANES]
            qw_ref[2 * p] = jnp.where(lane < IDX_DIM, pair, zero)
            qw_ref[2 * p + 1] = jnp.where(lane >= IDX_DIM, pair, zero)
        w_t = _dot_nt(esel_ref[...], tail)
        lo_ref[...] = jnp.where(w_t > 0.0, 0.0, NEG_INF)
        hi_ref[...] = jnp.where(w_t > 0.0, jnp.inf, 0.0)

        def score_tile(jj, c):
            r0 = pl.multiple_of(jj * tk, tk)
            ik = ikd_ref[0, pl.ds(r0, tk), :]
            comb = jnp.zeros((tk, tq), F32)
            for h in range(IDX_HEADS):
                x = _dot_nt(ik, qw_ref[h])
                comb = comb + jnp.minimum(jnp.maximum(x, lo_ref[h:h + 1, :]), hi_ref[h:h + 1, :])
            keys_ref[jj] = jnp.where(causal(jj), comb, NEG_INF)
            return c
        lax.fori_loop(0, i + 1, score_tile, 0)

        def as_float(t):
            return pltpu.bitcast(jnp.where(t < 0, t ^ 0x7FFFFFFF, t), F32)
        def count_ge(tf):
            def body(jj, acc):
                accs = [acc, jnp.zeros_like(acc), jnp.zeros_like(acc), jnp.zeros_like(acc)]
                for r0 in range(0, tk, 8):
                    a = accs[(r0 // 8) % 4]
                    accs[(r0 // 8) % 4] = jnp.where(keys_ref[jj, r0:r0 + 8, :] >= tf, a + 1, a)
                return (accs[0] + accs[1]) + (accs[2] + accs[3])
            acc = lax.fori_loop(0, i + 1, body, jnp.zeros((8, tq), jnp.int32))
            return jnp.sum(acc.astype(F32), axis=0, keepdims=True)
        def unresolved(carry):
            bi, _, cnt = carry
            return jnp.logical_and(bi < 32, jnp.max(jnp.abs(cnt - topk)) > 0.0)
        def bit_steps(carry):
            bi, t, cnt = carry
            for k in range(4):
                tc = t + (jnp.int32(1) << (31 - bi - k))
                c = count_ge(as_float(tc))
                take = c >= topk
                t = jnp.where(take, tc, t)
                cnt = jnp.where(take, c, cnt)
            return bi + 4, t, cnt
        n_all = ((i + 1) * tk).astype(F32)
        _, t_fin, _ = lax.while_loop(unresolved, bit_steps,
                                     (jnp.int32(0), jnp.full((1, tq), INT_MIN, jnp.int32),
                                      jnp.full((1, tq), n_all, F32)))
        t_float = as_float(t_fin)
        thr_ref[...] = jnp.where(t_float != t_float, NEG_INF, t_float)

        m_ref[...] = jnp.full_like(m_ref, NEG_INF)
        l_ref[...] = jnp.zeros_like(l_ref)
        acc_ref[...] = jnp.zeros_like(acc_ref)

    sel = (keys_ref[j] >= thr_ref[...]) & causal(j)
    smax = smax_ref[0]
    fixed_shift = smax <= SOFTMAX_FIXED_SHIFT_MAX

    @pl.when(fixed_shift)
    def _attend_fixed_shift():
        bias = jnp.where(sel, -smax * LOG2E, NEG_INF)
        for h in range(ATTN_HEADS):
            sl = slice(h * dh, (h + 1) * dh)
            p = jnp.exp2(_dot_nt(kn_ref[0, :, sl], qn_ref[:, sl]) + bias)
            parts = [p[r * 8:(r + 1) * 8] for r in range(tk // 8)]
            while len(parts) > 1:
                parts = [a + b for a, b in zip(parts[0::2], parts[1::2])]
            l_ref[h] += parts[0]
            p_ref[h] = p.astype(BF16)

    @pl.when(jnp.logical_and(fixed_shift, j <= i))
    def _weighted_values():
        for h in range(ATTN_HEADS):
            sl = slice(h * dh, (h + 1) * dh)
            acc_ref[h] += _dot(vt_ref[0, 0, sl, :], p_ref[h])

    @pl.when(jnp.logical_not(fixed_shift))
    def _attend_running_max():
        bias = jnp.where(sel, 0.0, NEG_INF)
        m_all = m_ref[...]
        m_rows = []
        for h in range(ATTN_HEADS):
            sl = slice(h * dh, (h + 1) * dh)
            s = _dot_nt(kn_ref[0, :, sl], qn_ref[:, sl]) + bias
            m_old = m_all[h:h + 1, :]
            m_new = jnp.maximum(m_old, jnp.max(s, axis=0, keepdims=True))
            m_safe = jnp.where(m_new == NEG_INF, 0.0, m_new)
            alpha = jnp.exp2(m_old - m_safe)
            p = jnp.exp2(s - m_safe)
            l_ref[h, 0:1, :] = alpha * l_ref[h, 0:1, :] + jnp.sum(p, axis=0, keepdims=True)
            acc_ref[h] = alpha * acc_ref[h] + _dot(vt_ref[0, 0, sl, :], p.astype(BF16))
            m_rows.append(m_new)
        m_ref[...] = jnp.concatenate(m_rows, axis=0)

    @pl.when(j == i)
    def _finish():
        for h in range(ATTN_HEADS):
            sl = slice(h * dh, (h + 1) * dh)
            l = jnp.sum(l_ref[h], axis=0, keepdims=True)
            o_ref[0, :, sl] = (acc_ref[h] / l).T.astype(o_ref.dtype)


def dsa_group(proj, kn, vt, ikd, qnorm, knorm, tq):
    b, s, _ = proj.shape
    tk = tq
    smax = (ATTN_HEAD_DIM ** 0.5 * jnp.max(jnp.abs(qnorm)) * jnp.max(jnp.abs(knorm))).reshape(1).astype(F32)
    nq = s // tq
    w = ATTN_HEADS * ATTN_HEAD_DIM
    topk = min(DSA_TOPK, s // 4)
    eq = np.zeros((TAIL, IDX_HEADS * IDX_DIM), np.float32)
    esel = np.zeros((IDX_HEADS, TAIL), np.float32)
    for h in range(IDX_HEADS):
        eq[IDX_DIM + h, h * IDX_DIM:(h + 1) * IDX_DIM] = 1.0
        esel[h, IDX_DIM + h] = 1.0
    pairs = [(i, j) for i in range(nq) for j in range(i + 1)]
    pi = jnp.asarray([p[0] for p in pairs], jnp.int32)
    pj = jnp.asarray([p[1] for p in pairs], jnp.int32)
    qblk = lambda cb: pl.BlockSpec((1, tq, w), lambda bi, st, pi, pj, cb=cb: (bi, pi[st], cb))
    const = lambda a: pl.BlockSpec(a.shape, lambda bi, st, pi, pj: (0,) * a.ndim)
    grid_spec = pltpu.PrefetchScalarGridSpec(
        num_scalar_prefetch=2,
        grid=(b, len(pairs)),
        in_specs=[qblk(4),
                  qblk(7),
                  pl.BlockSpec((1, tq, TAIL), lambda bi, st, pi, pj: (bi, pi[st], 8 * w // TAIL)),
                  pl.BlockSpec((1, s, LANES), lambda bi, st, pi, pj: (bi, 0, 0)),
                  pl.BlockSpec((1, tk, w), lambda bi, st, pi, pj: (bi, pj[st], 0)),
                  pl.BlockSpec((1, 1, w, tk), lambda bi, st, pi, pj: (bi, pj[st], 0, 0)),
                  const(qnorm), const(eq), const(esel),
                  pl.BlockSpec(memory_space=pltpu.SMEM)],
        out_specs=pl.BlockSpec((1, tq, w), lambda bi, st, pi, pj: (bi, pi[st], 0)),
        scratch_shapes=[pltpu.VMEM((nq, tk, tq), F32),
                        pltpu.VMEM((1, tq), F32),
                        pltpu.VMEM((tq, w), BF16),
                        pltpu.VMEM((IDX_HEADS, tq, LANES), BF16),
                        pltpu.VMEM((IDX_HEADS, tq), F32),
                        pltpu.VMEM((IDX_HEADS, tq), F32),
                        pltpu.VMEM((ATTN_HEADS, tq), F32),
                        pltpu.VMEM((ATTN_HEADS, 8, tq), F32),
                        pltpu.VMEM((ATTN_HEADS, ATTN_HEAD_DIM, tq), F32),
                        pltpu.VMEM((ATTN_HEADS, tk, tq), BF16)])
    return pl.pallas_call(
        functools.partial(_dsa_kernel, tq=tq, tk=tk, topk=topk),
        grid_spec=grid_spec,
        out_shape=jax.ShapeDtypeStruct((b, s, w), BF16),
        compiler_params=_params("parallel", "arbitrary"),
        name="dsa",
    )(pi, pj, proj, proj, proj, ikd, kn, vt, qnorm, jnp.asarray(eq, BF16), jnp.asarray(esel, BF16), smax)


def _cast_rows_to(src_ref, dst_ref, step=256):
    def body(r, c):
        r0 = pl.multiple_of(r * step, step)
        dst_ref[pl.ds(r0, step), :] = src_ref[0, pl.ds(r0, step), :].astype(BF16)
        return c
    lax.fori_loop(0, dst_ref.shape[0] // step, body, 0)


def _out_proj_kernel(yh_ref, ya_ref, w_ref, x_ref, o_ref, wb_ref):
    @pl.when(pl.program_id(0) == 0)
    def _():
        _cast_rows_to(w_ref, wb_ref)
    kh = yh_ref.shape[1]
    o_ref[...] = x_ref[...] + _dot(yh_ref[...], wb_ref[0:kh, :]) + _dot(ya_ref[...], wb_ref[kh:2 * kh, :])


def out_proj(yh, ya, w_out, layer, x, tm):
    m, kh = yh.shape
    _, k, n = w_out.shape
    return pl.pallas_call(
        _out_proj_kernel,
        grid=(m // tm,),
        in_specs=[pl.BlockSpec((tm, kh), lambda i: (i, 0)),
                  pl.BlockSpec((tm, kh), lambda i: (i, 0)),
                  pl.BlockSpec((1, k, n), lambda i: (layer, 0, 0), pipeline_mode=pl.Buffered(1)),
                  pl.BlockSpec((tm, n), lambda i: (i, 0))],
        out_specs=pl.BlockSpec((tm, n), lambda i: (i, 0)),
        out_shape=jax.ShapeDtypeStruct((m, n), F32),
        scratch_shapes=[pltpu.VMEM((k, n), BF16)],
        compiler_params=_params("arbitrary"),
        name="out_proj",
    )(yh, ya, w_out, x)


def _mem_kv_kernel(mem_ref, mg_ref, wk_ref, wv_ref, kg_ref, k_ref, v_ref):
    memn = _rms(mem_ref[0], mg_ref[...]).astype(BF16)
    k = _dot(memn, wk_ref[0].astype(BF16))
    kg = kg_ref[...]
    for h in range(CROSS_HEADS):
        sl = slice(h * CROSS_HEAD_DIM, (h + 1) * CROSS_HEAD_DIM)
        k_ref[0, :, sl] = _rms(k[:, sl], kg).astype(BF16)
    v_ref[0] = _dot(memn, wv_ref[0].astype(BF16)).astype(BF16)


def mem_kv(mem, mem_norm, wk, wv, layer, xk_norm):
    b, nm, d = mem.shape
    cw = wk.shape[2]
    full = lambda a: pl.BlockSpec(a.shape, lambda bi: (0,) * a.ndim)
    wspec = pl.BlockSpec((1, d, cw), lambda bi: (layer, 0, 0))
    return pl.pallas_call(
        _mem_kv_kernel,
        grid=(b,),
        in_specs=[pl.BlockSpec((1, nm, d), lambda bi: (bi, 0, 0)),
                  full(mem_norm), wspec, wspec, full(xk_norm)],
        out_specs=[pl.BlockSpec((1, nm, cw), lambda bi: (bi, 0, 0))] * 2,
        out_shape=[jax.ShapeDtypeStruct((b, nm, cw), BF16)] * 2,
        compiler_params=_params("parallel"),
        name="mem_kv",
    )(mem, mem_norm, wk, wv, xk_norm)


def _cross_kernel(h_ref, ng_ref, wq_ref, qg_ref, k_ref, v_ref, wo_ref, o_ref, hn_ref, oc_ref, wqb_ref, wob_ref):
    @pl.when((pl.program_id(0) == 0) & (pl.program_id(1) == 0))
    def _():
        _cast_rows_to(wq_ref, wqb_ref)
        _cast_rows_to(wo_ref, wob_ref)
    rows = h_ref.shape[1]
    dh = CROSS_HEAD_DIM
    step = 256
    gain = ng_ref[...]
    def body(r, c):
        r0 = pl.multiple_of(r * step, step)
        hn_ref[pl.ds(r0, step), :] = _rms(h_ref[0, pl.ds(r0, step), :], gain).astype(BF16)
        return c
    lax.fori_loop(0, rows // step, body, 0)
    q = _dot(hn_ref[...], wqb_ref[...])
    qg = qg_ref[...] * (dh ** -0.5)
    for h in range(CROSS_HEADS):
        sl = slice(h * dh, (h + 1) * dh)
        qn = _rms(q[:, sl], qg).astype(BF16)
        s = _dot_nt(qn, k_ref[0, :, sl])
        p = jnp.exp(s - jnp.max(s, axis=1, keepdims=True))
        l = jnp.sum(p, axis=1, keepdims=True)
        oc_ref[:, sl] = (_dot(p.astype(BF16), v_ref[0, :, sl]) / l).astype(BF16)
    o_ref[0] = h_ref[0] + _dot(oc_ref[...], wob_ref[...])


def cross_attention(h, norm_cross, wq, xq_norm, kx, vx, wo, layer, tm):
    b, s, d = h.shape
    nm, cw = kx.shape[1:]
    full = lambda a: pl.BlockSpec(a.shape, lambda bi, i: (0,) * a.ndim)
    resident = lambda a: pl.BlockSpec((1,) + a.shape[1:], lambda bi, i: (layer, 0, 0),
                                      pipeline_mode=pl.Buffered(1))
    return pl.pallas_call(
        _cross_kernel,
        grid=(b, s // tm),
        in_specs=[pl.BlockSpec((1, tm, d), lambda bi, i: (bi, i, 0)),
                  full(norm_cross), resident(wq), full(xq_norm),
                  pl.BlockSpec((1, nm, cw), lambda bi, i: (bi, 0, 0)),
                  pl.BlockSpec((1, nm, cw), lambda bi, i: (bi, 0, 0)),
                  resident(wo)],
        out_specs=pl.BlockSpec((1, tm, d), lambda bi, i: (bi, i, 0)),
        out_shape=jax.ShapeDtypeStruct((b, s, d), F32),
        scratch_shapes=[pltpu.VMEM((tm, d), BF16), pltpu.VMEM((tm, cw), BF16),
                        pltpu.VMEM(wq.shape[1:], BF16), pltpu.VMEM(wo.shape[1:], BF16)],
        compiler_params=_params("arbitrary", "arbitrary"),
        name="cross_attn",
    )(h, norm_cross, wq, xq_norm, kx, vx, wo)


def _mlp_kernel(h_ref, g_ref, wu_ref, wd_ref, o_ref, hn_ref):
    @pl.when(pl.program_id(1) == 0)
    def _():
        _norm_rows_to(h_ref, g_ref[...], hn_ref, h_ref.shape[0])
        o_ref[...] = h_ref[...]

    u = jnp.maximum(_dot(hn_ref[...], wu_ref[0].astype(BF16)), 0.0)
    o_ref[...] += _dot((u * u).astype(BF16), wd_ref[0].astype(BF16))


def mlp(h, gain, w_up, w_down, layer, tm, tf):
    m, d = h.shape
    f = w_up.shape[2]
    return pl.pallas_call(
        _mlp_kernel,
        grid=(m // tm, f // tf),
        in_specs=[pl.BlockSpec((tm, d), lambda i, j: (i, 0)),
                  pl.BlockSpec((1, d), lambda i, j: (0, 0)),
                  pl.BlockSpec((1, d, tf), lambda i, j: (layer, 0, j)),
                  pl.BlockSpec((1, tf, d), lambda i, j: (layer, j, 0))],
        out_specs=pl.BlockSpec((tm, d), lambda i, j: (i, 0), pipeline_mode=pl.Buffered(1)),
        out_shape=jax.ShapeDtypeStruct((m, d), F32),
        scratch_shapes=[pltpu.VMEM((tm, d), BF16)],
        compiler_params=_params("parallel", "arbitrary"),
        name="mlp",
    )(h, gain, w_up, w_down)


def _tile(n, pref):
    return pref if n % pref == 0 else n


def _tiles(n, s):
    return dict(
        proj_rows=_tile(n, 1024), proj_cols=768,
        hgrn_rows=_tile(s, 512),
        dsa_block=_tile(s, 512),
        out_rows=_tile(n, 512),
        cross_rows=_tile(s, 512),
        mlp_rows=_tile(n, 1024), mlp_ff=512)


def kernel(x, mem, norm_mix, w_in, hgrn_lb_logits, hgrn_onorm, attn_qnorm, attn_knorm, w_out,
           norm_cross, mem_norm, wq_x, wk_x, wv_x, wo_x, xq_norm, xk_norm,
           norm_mlp, w_up, w_down):
    b, s, d = x.shape
    n = b * s
    depth = w_in.shape[0]
    assert depth == 1
    l = 0
    in_width = w_in.shape[2]
    main_w = 8 * HGRN_HEADS * HGRN_KDIM
    assert in_width == main_w + IDX_DIM + IDX_HEADS
    pad_w = main_w + TAIL

    t = _tiles(n, s)
    x2 = x.reshape(n, d)

    proj = norm_matmul(x2, norm_mix[l:l + 1], w_in[l].T, pad_w,
                       t["proj_rows"], t["proj_cols"])
    proj3 = proj.reshape(b, s, pad_w)

    y_h = hgrn_group(proj3, hgrn_lb_logits, hgrn_onorm[l:l + 1], t["hgrn_rows"])
    kn, vt, ikd = dsa_prep(proj3, attn_knorm[l:l + 1], t["dsa_block"])
    y_a = dsa_group(proj3, kn, vt, ikd, attn_qnorm[l:l + 1], attn_knorm[l:l + 1], t["dsa_block"])

    h1 = out_proj(y_h.reshape(n, -1), y_a.reshape(n, -1), w_out, l, x2, t["out_rows"])

    kx, vx = mem_kv(mem, mem_norm[l:l + 1], wk_x, wv_x, l, xk_norm[l:l + 1])
    h2 = cross_attention(h1.reshape(b, s, d), norm_cross[l:l + 1], wq_x, xq_norm[l:l + 1],
                         kx, vx, wo_x, l, t["cross_rows"])

    h3 = mlp(h2.reshape(n, d), norm_mlp[l:l + 1], w_up, w_down, l, t["mlp_rows"], t["mlp_ff"])
    return h3.reshape(b, s, d)
```

```python
import functools

import numpy as np
import jax
import jax.numpy as jnp
from jax import lax
from jax.experimental import pallas as pl
from jax.experimental.pallas import tpu as pltpu

F32 = jnp.float32
BF16 = jnp.bfloat16
EPS = 1e-6

LANES = 128
HGRN_HEADS = 8
HGRN_KDIM = 128
HGRN_CHUNK = 64
ATTN_HEADS = 8
ATTN_HEAD_DIM = 128
IDX_HEADS = 16
IDX_DIM = 64
DSA_TOPK = 256
CROSS_HEADS = 4
CROSS_HEAD_DIM = 128
TAIL = 256
VMEM_LIMIT = 56 * 1024 * 1024
NEG_INF = float("-inf")
INT_MIN = -(2 ** 31)
LOG2E = 1.4426950408889634
SOFTMAX_FIXED_SHIFT_MAX = 40.0


def _params(*sem, flags=None):
    return pltpu.CompilerParams(dimension_semantics=sem, vmem_limit_bytes=VMEM_LIMIT, flags=flags)


def _rms(x, gain):
    return x * lax.rsqrt(jnp.mean(x * x, axis=-1, keepdims=True) + EPS) * gain


def _sigmoid(x):
    return 1.0 / (1.0 + jnp.exp(-x))


def _dot(a, b):
    return jnp.dot(a, b, preferred_element_type=F32)


def _dot_nt(a, b):
    return lax.dot_general(a, b, (((1,), (1,)), ((), ())), preferred_element_type=F32)


def _dot_tn(a, b):
    return lax.dot_general(a, b, (((0,), (0,)), ((), ())), preferred_element_type=F32)


def _norm_rows_to(x_ref, gain, dst_ref, rows):
    step = 256
    def body(r, c):
        r0 = pl.multiple_of(r * step, step)
        x = x_ref[pl.ds(r0, step), :]
        dst_ref[pl.ds(r0, step), :] = _rms(x, gain).astype(BF16)
        return c
    lax.fori_loop(0, rows // step, body, 0)


def _norm_matmul_kernel(x_ref, g_ref, wt_ref, o_ref, xn_ref, *, valid_cols):
    j = pl.program_id(1)
    tn = o_ref.shape[1]

    @pl.when(j == 0)
    def _():
        _norm_rows_to(x_ref, g_ref[...], xn_ref, x_ref.shape[0])

    res = _dot_nt(xn_ref[...], wt_ref[...].astype(BF16))

    @pl.when((j + 1) * tn <= valid_cols)
    def _():
        o_ref[...] = res.astype(o_ref.dtype)

    @pl.when((j + 1) * tn > valid_cols)
    def _():
        col = j * tn + lax.broadcasted_iota(jnp.int32, res.shape, 1)
        o_ref[...] = jnp.where(col < valid_cols, res, 0.0).astype(o_ref.dtype)


def norm_matmul(x, gain, wt, n, tm, tn):
    m, k = x.shape
    return pl.pallas_call(
        functools.partial(_norm_matmul_kernel, valid_cols=wt.shape[0]),
        grid=(m // tm, n // tn),
        in_specs=[pl.BlockSpec((tm, k), lambda i, j: (i, 0)),
                  pl.BlockSpec((1, k), lambda i, j: (0, 0)),
                  pl.BlockSpec((tn, k), lambda i, j: (j, 0))],
        out_specs=pl.BlockSpec((tm, tn), lambda i, j: (i, j)),
        out_shape=jax.ShapeDtypeStruct((m, n), BF16),
        scratch_shapes=[pltpu.VMEM((tm, k), BF16)],
        compiler_params=_params("parallel", "arbitrary"),
        name="in_proj",
    )(x, gain, wt)


def _hgrn_kernel(q_ref, f_ref, i_ref, g_ref, lbl_ref, on_ref, tril_ref, o_ref,
                 state_ref, a_ref, qi_ref, ks_ref, el_ref, qt_ref, kt_ref, sc_ref, scb_ref, *, nchunks):
    c = HGRN_CHUNK
    dk = HGRN_KDIM
    pw = 2 * dk
    npairs = HGRN_HEADS // 2
    nlev = 6

    @pl.when(pl.program_id(1) == 0)
    def _():
        state_ref[...] = jnp.zeros_like(state_ref)
        kt_ref[...] = jnp.zeros_like(kt_ref)

    lbl = lbl_ref[...]
    e = jnp.exp(lbl - jnp.max(lbl, axis=0, keepdims=True))
    lb = e[0:1] / jnp.sum(e, axis=0, keepdims=True)
    onorm = on_ref[...]
    tril = tril_ref[...]

    row = lax.broadcasted_iota(jnp.int32, (c, 2 * c), 0)
    lane = lax.broadcasted_iota(jnp.int32, (c, 2 * c), 1)
    col = lane & (c - 1)
    first = lane < c
    eye = row == col
    level_masks = []
    lg = nlev - 1
    while lg >= 0:
        level_masks.append(((row >> (lg + 1)) == (col >> (lg + 1)))
                           & (((row >> lg) & 1) == 1) & (((col >> lg) & 1) == 0))
        lg -= 1
    lane_p = lax.broadcasted_iota(jnp.int32, (c, pw), 1)
    odd_row = (lax.broadcasted_iota(jnp.int32, (c, pw), 0) & 1) == 1
    sub8 = lax.broadcasted_iota(jnp.int32, (8, pw), 0)
    zeros_st = jnp.zeros((dk, dk), BF16)

    def chunk_body(ci, carry):
        r0 = pl.multiple_of(ci * c, c)
        live = ci >= 0

        @pl.when(live)
        def _decay_factors():
            hq = q_ref[0, pl.ds(r0, c), :].astype(F32)
            hf = f_ref[0, pl.ds(r0, c), :].astype(F32)
            qf = hq * _sigmoid(hq) * (dk ** -0.5)
            f = lb + (1.0 - lb) * _sigmoid(hf)
            logf = jnp.log(f) * LOG2E
            kk = 1.0 - f
            g0 = logf.astype(BF16)
            g1 = (logf - g0.astype(F32)).astype(BF16)
            a_ref[...] = _dot(tril, g0) + _dot(tril, g1)
            for p in range(npairs):
                sl = slice(p * pw, (p + 1) * pw)
                a = a_ref[:, sl]
                row = lambda r, n: jnp.broadcast_to(a_ref[r:r + 1, sl], (n, pw))
                e0 = jnp.exp2(a)
                qp = qf[:, sl]
                kp = kk[:, sl]
                qi_ref[p] = (qp * e0).astype(BF16)
                ks_ref[p] = (kp * jnp.exp2(row(c - 1, c) - a)).astype(BF16)
                el_ref[p] = e0[c - 1:c]
                for l in range(nlev):
                    h = c >> (l + 1)
                    if h >= 4:
                        ref = jnp.concatenate([row(m * 2 * h + h - 1, 2 * h) for m in range(c // (2 * h))], axis=0)
                    elif h == 2:
                        ref = jnp.concatenate([jnp.where(sub8 < 4, row(8 * m + 1, 8), row(8 * m + 5, 8))
                                               for m in range(c // 8)], axis=0)
                    else:
                        ref = jnp.where(odd_row, pltpu.roll(a, 1, 0), a)
                    el = jnp.exp2(-jnp.abs(a - ref))
                    qt_ref[p, l] = (qp * el).astype(BF16)
                    kl = (kp * el).astype(BF16)
                    kt_ref[p, l, 0:c, 0:dk] = kl[:, 0:dk]
                    kt_ref[p, l, c:2 * c, dk:pw] = kl[:, dk:pw]
                qk = qp * kp
                diag = jnp.where(first, jnp.sum(qk[:, 0:dk], axis=1, keepdims=True),
                                 jnp.sum(qk[:, dk:pw], axis=1, keepdims=True))
                sc_ref[p] = jnp.where(eye, diag, 0.0)

        @pl.when(live)
        def _intra_chunk_scores():
            for p in range(npairs):
                sc = sc_ref[p]
                for l, msk in enumerate(level_masks):
                    sc = sc + jnp.where(msk, _dot_nt(qt_ref[p, l], kt_ref[p, l]), 0.0)
                scb_ref[p] = sc.astype(BF16)

        @pl.when(live)
        def _outputs_and_state():
            for p in range(npairs):
                sl = slice(p * pw, (p + 1) * pw)
                vp = i_ref[0, pl.ds(r0, c), sl]
                st_a = state_ref[2 * p]
                st_b = state_ref[2 * p + 1]
                st_bd = jnp.concatenate(
                    [jnp.concatenate([st_a.astype(BF16), zeros_st], axis=1),
                     jnp.concatenate([zeros_st, st_b.astype(BF16)], axis=1)], axis=0)
                v_bd = jnp.concatenate([jnp.where(lane_p < dk, vp, jnp.zeros_like(vp)),
                                        jnp.where(lane_p >= dk, vp, jnp.zeros_like(vp))], axis=0)
                o = _dot_nt(qi_ref[p], st_bd) + _dot(scb_ref[p], v_bd)
                upd = _dot_tn(vp, ks_ref[p])
                el = el_ref[p]
                state_ref[2 * p] = st_a * el[:, 0:dk] + upd[0:dk, 0:dk]
                state_ref[2 * p + 1] = st_b * el[:, dk:pw] + upd[dk:pw, dk:pw]
                gate = g_ref[0, pl.ds(r0, c), sl].astype(F32)
                y = jnp.concatenate([_rms(o[:, 0:dk], onorm), _rms(o[:, dk:pw], onorm)], axis=1)
                o_ref[0, pl.ds(r0, c), sl] = (y * (gate * _sigmoid(gate))).astype(o_ref.dtype)
        return carry

    lax.fori_loop(0, nchunks, chunk_body, 0)


def hgrn_group(proj, lb_logits, onorm, t_blk):
    b, s, _ = proj.shape
    w = HGRN_HEADS * HGRN_KDIM
    tril = jnp.asarray(np.tril(np.ones((HGRN_CHUNK, HGRN_CHUNK), np.float32)), BF16)
    c, pw, npairs, nlev = HGRN_CHUNK, 2 * HGRN_KDIM, HGRN_HEADS // 2, 6
    col = lambda cb: pl.BlockSpec((1, t_blk, w), lambda bi, ti, cb=cb: (bi, ti, cb))
    return pl.pallas_call(
        functools.partial(_hgrn_kernel, nchunks=t_blk // HGRN_CHUNK),
        grid=(b, s // t_blk),
        in_specs=[col(0), col(1), col(2), col(3),
                  pl.BlockSpec(lb_logits.shape, lambda bi, ti: (0, 0)),
                  pl.BlockSpec((1, HGRN_KDIM), lambda bi, ti: (0, 0)),
                  pl.BlockSpec(tril.shape, lambda bi, ti: (0, 0))],
        out_specs=pl.BlockSpec((1, t_blk, w), lambda bi, ti: (bi, ti, 0)),
        out_shape=jax.ShapeDtypeStruct((b, s, w), BF16),
        scratch_shapes=[pltpu.VMEM((HGRN_HEADS, HGRN_KDIM, HGRN_KDIM), F32),
                        pltpu.VMEM((c, w), F32),
                        pltpu.VMEM((npairs, c, pw), BF16),
                        pltpu.VMEM((npairs, c, pw), BF16),
                        pltpu.VMEM((npairs, 1, pw), F32),
                        pltpu.VMEM((npairs, nlev, c, pw), BF16),
                        pltpu.VMEM((npairs, nlev, 2 * c, pw), BF16),
                        pltpu.VMEM((npairs, c, 2 * c), F32),
                        pltpu.VMEM((npairs, c, 2 * c), BF16)],
        compiler_params=_params("parallel", "arbitrary"),
        name="hgrn2",
    )(proj, proj, proj, proj, lb_logits, onorm, tril)


def _dsa_prep_kernel(k_ref, tail_ref, kg_ref, dup_ref, kn_ref, ikd_ref):
    kg = kg_ref[...]
    for h in range(ATTN_HEADS):
        sl = slice(h * ATTN_HEAD_DIM, (h + 1) * ATTN_HEAD_DIM)
        kn_ref[0, :, sl] = _rms(k_ref[0, :, sl].astype(F32), kg).astype(BF16)
    ikd_ref[0] = _dot(tail_ref[0], dup_ref[...]).astype(BF16)


def dsa_prep(proj, knorm, tm):
    b, s, _ = proj.shape
    w = ATTN_HEADS * ATTN_HEAD_DIM
    dup = np.zeros((TAIL, LANES), np.float32)
    dup[np.arange(IDX_DIM), np.arange(IDX_DIM)] = 1.0
    dup[np.arange(IDX_DIM), np.arange(IDX_DIM) + IDX_DIM] = 1.0
    return pl.pallas_call(
        _dsa_prep_kernel,
        grid=(b, s // tm),
        in_specs=[pl.BlockSpec((1, tm, w), lambda bi, i: (bi, i, 5)),
                  pl.BlockSpec((1, tm, TAIL), lambda bi, i: (bi, i, 8 * w // TAIL)),
                  pl.BlockSpec((1, ATTN_HEAD_DIM), lambda bi, i: (0, 0)),
                  pl.BlockSpec((TAIL, LANES), lambda bi, i: (0, 0))],
        out_specs=[pl.BlockSpec((1, tm, w), lambda bi, i: (bi, i, 0)),
                   pl.BlockSpec((1, tm, LANES), lambda bi, i: (bi, i, 0))],
        out_shape=[jax.ShapeDtypeStruct((b, s, w), BF16),
                   jax.ShapeDtypeStruct((b, s, LANES), BF16)],
        compiler_params=_params("parallel", "parallel"),
        name="dsa_prep",
    )(proj, proj, knorm, jnp.asarray(dup, BF16))


def _dsa_kernel(pi_ref, pj_ref, aq_ref, iq_ref, tail_ref, ikd_ref, kn_ref, v_ref, qg_ref, eq_ref, esel_ref,
                smax_ref, o_ref, keys_ref, thr_ref, qn_ref, qw_ref, lo_ref, hi_ref, m_ref, l_ref, acc_ref, p_ref,
                *, tq, tk, topk):
    step = pl.program_id(1)
    i = pi_ref[step]
    j = pj_ref[step]
    dh = ATTN_HEAD_DIM

    def causal(jj):
        s_pos = jj * tk + lax.broadcasted_iota(jnp.int32, (tk, tq), 0)
        t_pos = i * tq + lax.broadcasted_iota(jnp.int32, (tk, tq), 1)
        return s_pos <= t_pos

    @pl.when(j == 0)
    def _index_and_select():
        qg = qg_ref[...] * (dh ** -0.5 * LOG2E)
        for h in range(ATTN_HEADS):
            sl = slice(h * dh, (h + 1) * dh)
            qn_ref[:, sl] = _rms(aq_ref[0, :, sl].astype(F32), qg).astype(BF16)
        tail = tail_ref[0]
        ww = _dot(tail, eq_ref[...])
        qw = (iq_ref[0].astype(F32) * ww * (IDX_HEADS ** -0.5 * IDX_DIM ** -0.5)).astype(BF16)
        lane = lax.broadcasted_iota(jnp.int32, (tq, LANES), 1)
        zero = jnp.zeros((tq, LANES), BF16)
        for p in range(IDX_HEADS // 2):
            pair = qw[:, p * LANES:(p + 1) * LANES]
            qw_ref[2 * p] = jnp.where(lane < IDX_DIM, pair, zero)
            qw_ref[2 * p + 1] = jnp.where(lane >= IDX_DIM, pair, zero)
        w_t = _dot_nt(esel_ref[...], tail)
        lo_ref[...] = jnp.where(w_t > 0.0, 0.0, NEG_INF)
        hi_ref[...] = jnp.where(w_t > 0.0, jnp.inf, 0.0)

        def score_tile(jj, c):
            r0 = pl.multiple_of(jj * tk, tk)
            ik = ikd_ref[0, pl.ds(r0, tk), :]
            comb = jnp.zeros((tk, tq), F32)
            for h in range(IDX_HEADS):
                x = _dot_nt(ik, qw_ref[h])
                comb = comb + jnp.minimum(jnp.maximum(x, lo_ref[h:h + 1, :]), hi_ref[h:h + 1, :])
            keys_ref[jj] = jnp.where(causal(jj), comb, NEG_INF)
            return c
        lax.fori_loop(0, i + 1, score_tile, 0)

        def as_float(t):
            return pltpu.bitcast(jnp.where(t < 0, t ^ 0x7FFFFFFF, t), F32)
        def count_ge(tf):
            def body(jj, acc):
                accs = [acc, jnp.zeros_like(acc), jnp.zeros_like(acc), jnp.zeros_like(acc)]
                for r0 in range(0, tk, 8):
                    a = accs[(r0 // 8) % 4]
                    accs[(r0 // 8) % 4] = jnp.where(keys_ref[jj, r0:r0 + 8, :] >= tf, a + 1, a)
                return (accs[0] + accs[1]) + (accs[2] + accs[3])
            acc = lax.fori_loop(0, i + 1, body, jnp.zeros((8, tq), jnp.int32))
            return jnp.sum(acc.astype(F32), axis=0, keepdims=True)
        def unresolved(carry):
            bi, _, cnt = carry
            return jnp.logical_and(bi < 32, jnp.max(jnp.abs(cnt - topk)) > 0.0)
        def bit_steps(carry):
            bi, t, cnt = carry
            for k in range(4):
                tc = t + (jnp.int32(1) << (31 - bi - k))
                c = count_ge(as_float(tc))
                take = c >= topk
                t = jnp.where(take, tc, t)
                cnt = jnp.where(take, c, cnt)
            return bi + 4, t, cnt
        n_all = ((i + 1) * tk).astype(F32)
        _, t_fin, _ = lax.while_loop(unresolved, bit_steps,
                                     (jnp.int32(0), jnp.full((1, tq), INT_MIN, jnp.int32),
                                      jnp.full((1, tq), n_all, F32)))
        t_float = as_float(t_fin)
        thr_ref[...] = jnp.where(t_float != t_float, NEG_INF, t_float)

        m_ref[...] = jnp.full_like(m_ref, NEG_INF)
        l_ref[...] = jnp.zeros_like(l_ref)
        acc_ref[...] = jnp.zeros_like(acc_ref)

    sel = (keys_ref[j] >= thr_ref[...]) & causal(j)
    smax = smax_ref[0]
    fixed_shift = smax <= SOFTMAX_FIXED_SHIFT_MAX

    @pl.when(fixed_shift)
    def _attend_fixed_shift():
        bias = jnp.where(sel, -smax * LOG2E, NEG_INF)
        for h in range(ATTN_HEADS):
            sl = slice(h * dh, (h + 1) * dh)
            p = jnp.exp2(_dot_nt(kn_ref[0, :, sl], qn_ref[:, sl]) + bias)
            parts = [p[r * 8:(r + 1) * 8] for r in range(tk // 8)]
            while len(parts) > 1:
                parts = [a + b for a, b in zip(parts[0::2], parts[1::2])]
            l_ref[h] += parts[0]
            p_ref[h] = p.astype(BF16)

    @pl.when(jnp.logical_and(fixed_shift, j <= i))
    def _weighted_values():
        for h in range(ATTN_HEADS):
            sl = slice(h * dh, (h + 1) * dh)
            acc_ref[h] += _dot_tn(v_ref[0, :, sl], p_ref[h])

    @pl.when(jnp.logical_not(fixed_shift))
    def _attend_running_max():
        bias = jnp.where(sel, 0.0, NEG_INF)
        m_all = m_ref[...]
        m_rows = []
        for h in range(ATTN_HEADS):
            sl = slice(h * dh, (h + 1) * dh)
            s = _dot_nt(kn_ref[0, :, sl], qn_ref[:, sl]) + bias
            m_old = m_all[h:h + 1, :]
            m_new = jnp.maximum(m_old, jnp.max(s, axis=0, keepdims=True))
            m_safe = jnp.where(m_new == NEG_INF, 0.0, m_new)
            alpha = jnp.exp2(m_old - m_safe)
            p = jnp.exp2(s - m_safe)
            l_ref[h, 0:1, :] = alpha * l_ref[h, 0:1, :] + jnp.sum(p, axis=0, keepdims=True)
            acc_ref[h] = alpha * acc_ref[h] + _dot_tn(v_ref[0, :, sl], p.astype(BF16))
            m_rows.append(m_new)
        m_ref[...] = jnp.concatenate(m_rows, axis=0)

    @pl.when(j == i)
    def _finish():
        for h in range(ATTN_HEADS):
            sl = slice(h * dh, (h + 1) * dh)
            l = jnp.sum(l_ref[h], axis=0, keepdims=True)
            o_ref[0, :, sl] = (acc_ref[h] / l).T.astype(o_ref.dtype)


def dsa_group(proj, kn, ikd, qnorm, knorm, tq):
    b, s, _ = proj.shape
    tk = tq
    smax = (ATTN_HEAD_DIM ** 0.5 * jnp.max(jnp.abs(qnorm)) * jnp.max(jnp.abs(knorm))).reshape(1).astype(F32)
    nq = s // tq
    w = ATTN_HEADS * ATTN_HEAD_DIM
    topk = min(DSA_TOPK, s // 4)
    eq = np.zeros((TAIL, IDX_HEADS * IDX_DIM), np.float32)
    esel = np.zeros((IDX_HEADS, TAIL), np.float32)
    for h in range(IDX_HEADS):
        eq[IDX_DIM + h, h * IDX_DIM:(h + 1) * IDX_DIM] = 1.0
        esel[h, IDX_DIM + h] = 1.0
    pairs = [(i, j) for i in range(nq) for j in range(i + 1)]
    pi = jnp.asarray([p[0] for p in pairs], jnp.int32)
    pj = jnp.asarray([p[1] for p in pairs], jnp.int32)
    qblk = lambda cb: pl.BlockSpec((1, tq, w), lambda bi, st, pi, pj, cb=cb: (bi, pi[st], cb))
    const = lambda a: pl.BlockSpec(a.shape, lambda bi, st, pi, pj: (0,) * a.ndim)
    grid_spec = pltpu.PrefetchScalarGridSpec(
        num_scalar_prefetch=2,
        grid=(b, len(pairs)),
        in_specs=[qblk(4),
                  qblk(7),
                  pl.BlockSpec((1, tq, TAIL), lambda bi, st, pi, pj: (bi, pi[st], 8 * w // TAIL)),
                  pl.BlockSpec((1, s, LANES), lambda bi, st, pi, pj: (bi, 0, 0)),
                  pl.BlockSpec((1, tk, w), lambda bi, st, pi, pj: (bi, pj[st], 0)),
                  pl.BlockSpec((1, tk, w), lambda bi, st, pi, pj: (bi, pj[st], 6)),
                  const(qnorm), const(eq), const(esel),
                  pl.BlockSpec(memory_space=pltpu.SMEM)],
        out_specs=pl.BlockSpec((1, tq, w), lambda bi, st, pi, pj: (bi, pi[st], 0)),
        scratch_shapes=[pltpu.VMEM((nq, tk, tq), F32),
                        pltpu.VMEM((1, tq), F32),
                        pltpu.VMEM((tq, w), BF16),
                        pltpu.VMEM((IDX_HEADS, tq, LANES), BF16),
                        pltpu.VMEM((IDX_HEADS, tq), F32),
                        pltpu.VMEM((IDX_HEADS, tq), F32),
                        pltpu.VMEM((ATTN_HEADS, tq), F32),
                        pltpu.VMEM((ATTN_HEADS, 8, tq), F32),
                        pltpu.VMEM((ATTN_HEADS, ATTN_HEAD_DIM, tq), F32),
                        pltpu.VMEM((ATTN_HEADS, tk, tq), BF16)])
    return pl.pallas_call(
        functools.partial(_dsa_kernel, tq=tq, tk=tk, topk=topk),
        grid_spec=grid_spec,
        out_shape=jax.ShapeDtypeStruct((b, s, w), BF16),
        compiler_params=_params("parallel", "arbitrary"),
        name="dsa",
    )(pi, pj, proj, proj, proj, ikd, kn, proj, qnorm, jnp.asarray(eq, BF16), jnp.asarray(esel, BF16), smax)


def _cast_rows_to(src_ref, dst_ref, step=256):
    def body(r, c):
        r0 = pl.multiple_of(r * step, step)
        dst_ref[pl.ds(r0, step), :] = src_ref[0, pl.ds(r0, step), :].astype(BF16)
        return c
    lax.fori_loop(0, dst_ref.shape[0] // step, body, 0)


def _out_proj_kernel(yh_ref, ya_ref, w_ref, x_ref, o_ref, wb_ref):
    @pl.when(pl.program_id(0) == 0)
    def _():
        _cast_rows_to(w_ref, wb_ref)
    kh = yh_ref.shape[1]
    o_ref[...] = x_ref[...] + _dot(yh_ref[...], wb_ref[0:kh, :]) + _dot(ya_ref[...], wb_ref[kh:2 * kh, :])


def out_proj(yh, ya, w_out, layer, x, tm):
    m, kh = yh.shape
    _, k, n = w_out.shape
    return pl.pallas_call(
        _out_proj_kernel,
        grid=(m // tm,),
        in_specs=[pl.BlockSpec((tm, kh), lambda i: (i, 0)),
                  pl.BlockSpec((tm, kh), lambda i: (i, 0)),
                  pl.BlockSpec((1, k, n), lambda i: (layer, 0, 0), pipeline_mode=pl.Buffered(1)),
                  pl.BlockSpec((tm, n), lambda i: (i, 0))],
        out_specs=pl.BlockSpec((tm, n), lambda i: (i, 0)),
        out_shape=jax.ShapeDtypeStruct((m, n), F32),
        scratch_shapes=[pltpu.VMEM((k, n), BF16)],
        compiler_params=_params("arbitrary"),
        name="out_proj",
    )(yh, ya, w_out, x)


def _mem_kv_kernel(mem_ref, mg_ref, wk_ref, wv_ref, kg_ref, k_ref, v_ref):
    memn = _rms(mem_ref[0], mg_ref[...]).astype(BF16)
    k = _dot(memn, wk_ref[0].astype(BF16))
    kg = kg_ref[...]
    for h in range(CROSS_HEADS):
        sl = slice(h * CROSS_HEAD_DIM, (h + 1) * CROSS_HEAD_DIM)
        k_ref[0, :, sl] = _rms(k[:, sl], kg).astype(BF16)
    v_ref[0] = _dot(memn, wv_ref[0].astype(BF16)).astype(BF16)


def mem_kv(mem, mem_norm, wk, wv, layer, xk_norm):
    b, nm, d = mem.shape
    cw = wk.shape[2]
    full = lambda a: pl.BlockSpec(a.shape, lambda bi: (0,) * a.ndim)
    wspec = pl.BlockSpec((1, d, cw), lambda bi: (layer, 0, 0))
    return pl.pallas_call(
        _mem_kv_kernel,
        grid=(b,),
        in_specs=[pl.BlockSpec((1, nm, d), lambda bi: (bi, 0, 0)),
                  full(mem_norm), wspec, wspec, full(xk_norm)],
        out_specs=[pl.BlockSpec((1, nm, cw), lambda bi: (bi, 0, 0))] * 2,
        out_shape=[jax.ShapeDtypeStruct((b, nm, cw), BF16)] * 2,
        compiler_params=_params("parallel"),
        name="mem_kv",
    )(mem, mem_norm, wk, wv, xk_norm)


def _cross_kernel(h_ref, ng_ref, wq_ref, qg_ref, k_ref, v_ref, wo_ref, o_ref, hn_ref, oc_ref, wqb_ref, wob_ref):
    @pl.when((pl.program_id(0) == 0) & (pl.program_id(1) == 0))
    def _():
        _cast_rows_to(wq_ref, wqb_ref)
        _cast_rows_to(wo_ref, wob_ref)
    rows = h_ref.shape[1]
    dh = CROSS_HEAD_DIM
    step = 256
    gain = ng_ref[...]
    def body(r, c):
        r0 = pl.multiple_of(r * step, step)
        hn_ref[pl.ds(r0, step), :] = _rms(h_ref[0, pl.ds(r0, step), :], gain).astype(BF16)
        return c
    lax.fori_loop(0, rows // step, body, 0)
    q = _dot(hn_ref[...], wqb_ref[...])
    qg = qg_ref[...] * (dh ** -0.5)
    for h in range(CROSS_HEADS):
        sl = slice(h * dh, (h + 1) * dh)
        qn = _rms(q[:, sl], qg).astype(BF16)
        s = _dot_nt(qn, k_ref[0, :, sl])
        p = jnp.exp(s - jnp.max(s, axis=1, keepdims=True))
        l = jnp.sum(p, axis=1, keepdims=True)
        oc_ref[:, sl] = (_dot(p.astype(BF16), v_ref[0, :, sl]) / l).astype(BF16)
    o_ref[0] = h_ref[0] + _dot(oc_ref[...], wob_ref[...])


def cross_attention(h, norm_cross, wq, xq_norm, kx, vx, wo, layer, tm):
    b, s, d = h.shape
    nm, cw = kx.shape[1:]
    full = lambda a: pl.BlockSpec(a.shape, lambda bi, i: (0,) * a.ndim)
    resident = lambda a: pl.BlockSpec((1,) + a.shape[1:], lambda bi, i: (layer, 0, 0),
                                      pipeline_mode=pl.Buffered(1))
    return pl.pallas_call(
        _cross_kernel,
        grid=(b, s // tm),
        in_specs=[pl.BlockSpec((1, tm, d), lambda bi, i: (bi, i, 0)),
                  full(norm_cross), resident(wq), full(xq_norm),
                  pl.BlockSpec((1, nm, cw), lambda bi, i: (bi, 0, 0)),
                  pl.BlockSpec((1, nm, cw), lambda bi, i: (bi, 0, 0)),
                  resident(wo)],
        out_specs=pl.BlockSpec((1, tm, d), lambda bi, i: (bi, i, 0)),
        out_shape=jax.ShapeDtypeStruct((b, s, d), F32),
        scratch_shapes=[pltpu.VMEM((tm, d), BF16), pltpu.VMEM((tm, cw), BF16),
                        pltpu.VMEM(wq.shape[1:], BF16), pltpu.VMEM(wo.shape[1:], BF16)],
        compiler_params=_params("arbitrary", "arbitrary"),
        name="cross_attn",
    )(h, norm_cross, wq, xq_norm, kx, vx, wo)


def _mlp_kernel(h_ref, g_ref, wu_ref, wd_ref, o_ref, hn_ref):
    @pl.when(pl.program_id(1) == 0)
    def _():
        _norm_rows_to(h_ref, g_ref[...], hn_ref, h_ref.shape[0])
        o_ref[...] = h_ref[...]

    u = jnp.maximum(_dot(hn_ref[...], wu_ref[0].astype(BF16)), 0.0)
    o_ref[...] += _dot((u * u).astype(BF16), wd_ref[0].astype(BF16))


def mlp(h, gain, w_up, w_down, layer, tm, tf):
    m, d = h.shape
    f = w_up.shape[2]
    return pl.pallas_call(
        _mlp_kernel,
        grid=(m // tm, f // tf),
        in_specs=[pl.BlockSpec((tm, d), lambda i, j: (i, 0)),
                  pl.BlockSpec((1, d), lambda i, j: (0, 0)),
                  pl.BlockSpec((1, d, tf), lambda i, j: (layer, 0, j)),
                  pl.BlockSpec((1, tf, d), lambda i, j: (layer, j, 0))],
        out_specs=pl.BlockSpec((tm, d), lambda i, j: (i, 0), pipeline_mode=pl.Buffered(1)),
        out_shape=jax.ShapeDtypeStruct((m, d), F32),
        scratch_shapes=[pltpu.VMEM((tm, d), BF16)],
        compiler_params=_params("parallel", "arbitrary"),
        name="mlp",
    )(h, gain, w_up, w_down)


def _tile(n, pref):
    return pref if n % pref == 0 else n


def _tiles(n, s):
    return dict(
        proj_rows=_tile(n, 1024), proj_cols=768,
        hgrn_rows=_tile(s, 512),
        dsa_block=_tile(s, 512),
        out_rows=_tile(n, 512),
        cross_rows=_tile(s, 512),
        mlp_rows=_tile(n, 1024), mlp_ff=512)


def kernel(x, mem, norm_mix, w_in, hgrn_lb_logits, hgrn_onorm, attn_qnorm, attn_knorm, w_out,
           norm_cross, mem_norm, wq_x, wk_x, wv_x, wo_x, xq_norm, xk_norm,
           norm_mlp, w_up, w_down):
    b, s, d = x.shape
    n = b * s
    depth = w_in.shape[0]
    assert depth == 1
    l = 0
    in_width = w_in.shape[2]
    main_w = 8 * HGRN_HEADS * HGRN_KDIM
    assert in_width == main_w + IDX_DIM + IDX_HEADS
    pad_w = main_w + TAIL

    t = _tiles(n, s)
    x2 = x.reshape(n, d)

    proj = norm_matmul(x2, norm_mix[l:l + 1], w_in[l].T, pad_w,
                       t["proj_rows"], t["proj_cols"])
    proj3 = proj.reshape(b, s, pad_w)

    y_h = hgrn_group(proj3, hgrn_lb_logits, hgrn_onorm[l:l + 1], t["hgrn_rows"])
    kn, ikd = dsa_prep(proj3, attn_knorm[l:l + 1], t["dsa_block"])
    y_a = dsa_group(proj3, kn, ikd, attn_qnorm[l:l + 1], attn_knorm[l:l + 1], t["dsa_block"])

    h1 = out_proj(y_h.reshape(n, -1), y_a.reshape(n, -1), w_out, l, x2, t["out_rows"])

    kx, vx = mem_kv(mem, mem_norm[l:l + 1], wk_x, wv_x, l, xk_norm[l:l + 1])
    h2 = cross_attention(h1.reshape(b, s, d), norm_cross[l:l + 1], wq_x, xq_norm[l:l + 1],
                         kx, vx, wo_x, l, t["cross_rows"])

    h3 = mlp(h2.reshape(n, d), norm_mlp[l:l + 1], w_up, w_down, l, t["mlp_rows"], t["mlp_ff"])
    return h3.reshape(b, s, d)
```

```python
import functools

import numpy as np
import jax
import jax.numpy as jnp
from jax import lax
from jax.experimental import pallas as pl
from jax.experimental.pallas import tpu as pltpu

F32 = jnp.float32
BF16 = jnp.bfloat16
EPS = 1e-6

LANES = 128
HGRN_HEADS = 8
HGRN_KDIM = 128
HGRN_CHUNK = 64
ATTN_HEADS = 8
ATTN_HEAD_DIM = 128
IDX_HEADS = 16
IDX_DIM = 64
DSA_TOPK = 256
CROSS_HEADS = 4
CROSS_HEAD_DIM = 128
TAIL = 256
VMEM_LIMIT = 56 * 1024 * 1024
NEG_INF = float("-inf")
INT_MIN = -(2 ** 31)
LOG2E = 1.4426950408889634
SOFTMAX_FIXED_SHIFT_MAX = 40.0


def _params(*sem, flags=None):
    return pltpu.CompilerParams(dimension_semantics=sem, vmem_limit_bytes=VMEM_LIMIT, flags=flags)


def _rms(x, gain):
    return x * lax.rsqrt(jnp.mean(x * x, axis=-1, keepdims=True) + EPS) * gain


def _sigmoid(x):
    return 1.0 / (1.0 + jnp.exp(-x))


def _dot(a, b):
    return jnp.dot(a, b, preferred_element_type=F32)


def _dot_nt(a, b):
    return lax.dot_general(a, b, (((1,), (1,)), ((), ())), preferred_element_type=F32)


def _dot_tn(a, b):
    return lax.dot_general(a, b, (((0,), (0,)), ((), ())), preferred_element_type=F32)


def _norm_rows_to(x_ref, gain, dst_ref, rows):
    step = 256
    def body(r, c):
        r0 = pl.multiple_of(r * step, step)
        x = x_ref[pl.ds(r0, step), :]
        dst_ref[pl.ds(r0, step), :] = _rms(x, gain).astype(BF16)
        return c
    lax.fori_loop(0, rows // step, body, 0)


def _norm_matmul_kernel(x_ref, g_ref, wt_ref, o_ref, xn_ref, *, valid_cols):
    j = pl.program_id(1)
    tn = o_ref.shape[1]

    @pl.when(j == 0)
    def _():
        _norm_rows_to(x_ref, g_ref[...], xn_ref, x_ref.shape[0])

    res = _dot_nt(xn_ref[...], wt_ref[...].astype(BF16))

    @pl.when((j + 1) * tn <= valid_cols)
    def _():
        o_ref[...] = res.astype(o_ref.dtype)

    @pl.when((j + 1) * tn > valid_cols)
    def _():
        col = j * tn + lax.broadcasted_iota(jnp.int32, res.shape, 1)
        o_ref[...] = jnp.where(col < valid_cols, res, 0.0).astype(o_ref.dtype)


def norm_matmul(x, gain, wt, n, tm, tn):
    m, k = x.shape
    return pl.pallas_call(
        functools.partial(_norm_matmul_kernel, valid_cols=wt.shape[0]),
        grid=(m // tm, n // tn),
        in_specs=[pl.BlockSpec((tm, k), lambda i, j: (i, 0)),
                  pl.BlockSpec((1, k), lambda i, j: (0, 0)),
                  pl.BlockSpec((tn, k), lambda i, j: (j, 0))],
        out_specs=pl.BlockSpec((tm, tn), lambda i, j: (i, j)),
        out_shape=jax.ShapeDtypeStruct((m, n), BF16),
        scratch_shapes=[pltpu.VMEM((tm, k), BF16)],
        compiler_params=_params("parallel", "arbitrary"),
        name="in_proj",
    )(x, gain, wt)


def _hgrn_kernel(q_ref, f_ref, i_ref, g_ref, lbl_ref, on_ref, tril_ref, o_ref,
                 state_ref, a_ref, qi_ref, ks_ref, el_ref, qt_ref, kt_ref, sc_ref, scb_ref, *, nchunks):
    c = HGRN_CHUNK
    dk = HGRN_KDIM
    pw = 2 * dk
    npairs = HGRN_HEADS // 2
    nlev = 6

    @pl.when(pl.program_id(1) == 0)
    def _():
        state_ref[...] = jnp.zeros_like(state_ref)
        kt_ref[...] = jnp.zeros_like(kt_ref)

    lbl = lbl_ref[...]
    e = jnp.exp(lbl - jnp.max(lbl, axis=0, keepdims=True))
    lb = e[0:1] / jnp.sum(e, axis=0, keepdims=True)
    onorm = on_ref[...]
    tril = tril_ref[...]

    row = lax.broadcasted_iota(jnp.int32, (c, 2 * c), 0)
    lane = lax.broadcasted_iota(jnp.int32, (c, 2 * c), 1)
    col = lane & (c - 1)
    first = lane < c
    eye = row == col
    level_masks = []
    lg = nlev - 1
    while lg >= 0:
        level_masks.append(((row >> (lg + 1)) == (col >> (lg + 1)))
                           & (((row >> lg) & 1) == 1) & (((col >> lg) & 1) == 0))
        lg -= 1
    lane_p = lax.broadcasted_iota(jnp.int32, (c, pw), 1)
    odd_row = (lax.broadcasted_iota(jnp.int32, (c, pw), 0) & 1) == 1
    sub8 = lax.broadcasted_iota(jnp.int32, (8, pw), 0)
    zeros_st = jnp.zeros((dk, dk), BF16)

    def chunk_body(ci, carry):
        r0 = pl.multiple_of(ci * c, c)
        live = ci >= 0

        @pl.when(live)
        def _decay_factors():
            hq = q_ref[0, pl.ds(r0, c), :].astype(F32)
            hf = f_ref[0, pl.ds(r0, c), :].astype(F32)
            qf = hq * _sigmoid(hq) * (dk ** -0.5)
            f = lb + (1.0 - lb) * _sigmoid(hf)
            logf = jnp.log(f) * LOG2E
            kk = 1.0 - f
            g0 = logf.astype(BF16)
            g1 = (logf - g0.astype(F32)).astype(BF16)
            a_ref[...] = _dot(tril, g0) + _dot(tril, g1)
            for p in range(npairs):
                sl = slice(p * pw, (p + 1) * pw)
                a = a_ref[:, sl]
                row = lambda r, n: jnp.broadcast_to(a_ref[r:r + 1, sl], (n, pw))
                e0 = jnp.exp2(a)
                qp = qf[:, sl]
                kp = kk[:, sl]
                qi_ref[p] = (qp * e0).astype(BF16)
                ks_ref[p] = (kp * jnp.exp2(row(c - 1, c) - a)).astype(BF16)
                el_ref[p] = e0[c - 1:c]
                for l in range(nlev):
                    h = c >> (l + 1)
                    if h >= 4:
                        ref = jnp.concatenate([row(m * 2 * h + h - 1, 2 * h) for m in range(c // (2 * h))], axis=0)
                    elif h == 2:
                        ref = jnp.concatenate([jnp.where(sub8 < 4, row(8 * m + 1, 8), row(8 * m + 5, 8))
                                               for m in range(c // 8)], axis=0)
                    else:
                        ref = jnp.where(odd_row, pltpu.roll(a, 1, 0), a)
                    el = jnp.exp2(-jnp.abs(a - ref))
                    qt_ref[p, l] = (qp * el).astype(BF16)
                    kl = (kp * el).astype(BF16)
                    kt_ref[p, l, 0:c, 0:dk] = kl[:, 0:dk]
                    kt_ref[p, l, c:2 * c, dk:pw] = kl[:, dk:pw]
                qk = qp * kp
                diag = jnp.where(first, jnp.sum(qk[:, 0:dk], axis=1, keepdims=True),
                                 jnp.sum(qk[:, dk:pw], axis=1, keepdims=True))
                sc_ref[p] = jnp.where(eye, diag, 0.0)

        @pl.when(live)
        def _intra_chunk_scores():
            for p in range(npairs):
                sc = sc_ref[p]
                for l, msk in enumerate(level_masks):
                    sc = sc + jnp.where(msk, _dot_nt(qt_ref[p, l], kt_ref[p, l]), 0.0)
                scb_ref[p] = sc.astype(BF16)

        @pl.when(live)
        def _outputs_and_state():
            for p in range(npairs):
                sl = slice(p * pw, (p + 1) * pw)
                vp = i_ref[0, pl.ds(r0, c), sl]
                st_a = state_ref[2 * p]
                st_b = state_ref[2 * p + 1]
                st_bd = jnp.concatenate(
                    [jnp.concatenate([st_a.astype(BF16), zeros_st], axis=1),
                     jnp.concatenate([zeros_st, st_b.astype(BF16)], axis=1)], axis=0)
                v_bd = jnp.concatenate([jnp.where(lane_p < dk, vp, jnp.zeros_like(vp)),
                                        jnp.where(lane_p >= dk, vp, jnp.zeros_like(vp))], axis=0)
                o = _dot_nt(qi_ref[p], st_bd) + _dot(scb_ref[p], v_bd)
                upd = _dot_tn(vp, ks_ref[p])
                el = el_ref[p]
                state_ref[2 * p] = st_a * el[:, 0:dk] + upd[0:dk, 0:dk]
                state_ref[2 * p + 1] = st_b * el[:, dk:pw] + upd[dk:pw, dk:pw]
                gate = g_ref[0, pl.ds(r0, c), sl].astype(F32)
                y = jnp.concatenate([_rms(o[:, 0:dk], onorm), _rms(o[:, dk:pw], onorm)], axis=1)
                o_ref[0, pl.ds(r0, c), sl] = (y * (gate * _sigmoid(gate))).astype(o_ref.dtype)
        return carry

    lax.fori_loop(0, nchunks, chunk_body, 0)


def hgrn_group(proj, lb_logits, onorm, t_blk):
    b, s, _ = proj.shape
    w = HGRN_HEADS * HGRN_KDIM
    tril = jnp.asarray(np.tril(np.ones((HGRN_CHUNK, HGRN_CHUNK), np.float32)), BF16)
    c, pw, npairs, nlev = HGRN_CHUNK, 2 * HGRN_KDIM, HGRN_HEADS // 2, 6
    col = lambda cb: pl.BlockSpec((1, t_blk, w), lambda bi, ti, cb=cb: (bi, ti, cb))
    return pl.pallas_call(
        functools.partial(_hgrn_kernel, nchunks=t_blk // HGRN_CHUNK),
        grid=(b, s // t_blk),
        in_specs=[col(0), col(1), col(2), col(3),
                  pl.BlockSpec(lb_logits.shape, lambda bi, ti: (0, 0)),
                  pl.BlockSpec((1, HGRN_KDIM), lambda bi, ti: (0, 0)),
                  pl.BlockSpec(tril.shape, lambda bi, ti: (0, 0))],
        out_specs=pl.BlockSpec((1, t_blk, w), lambda bi, ti: (bi, ti, 0)),
        out_shape=jax.ShapeDtypeStruct((b, s, w), BF16),
        scratch_shapes=[pltpu.VMEM((HGRN_HEADS, HGRN_KDIM, HGRN_KDIM), F32),
                        pltpu.VMEM((c, w), F32),
                        pltpu.VMEM((npairs, c, pw), BF16),
                        pltpu.VMEM((npairs, c, pw), BF16),
                        pltpu.VMEM((npairs, 1, pw), F32),
                        pltpu.VMEM((npairs, nlev, c, pw), BF16),
                        pltpu.VMEM((npairs, nlev, 2 * c, pw), BF16),
                        pltpu.VMEM((npairs, c, 2 * c), F32),
                        pltpu.VMEM((npairs, c, 2 * c), BF16)],
        compiler_params=_params("parallel", "arbitrary"),
        name="hgrn2",
    )(proj, proj, proj, proj, lb_logits, onorm, tril)


def _dsa_prep_kernel(k_ref, tail_ref, kg_ref, dup_ref, kn_ref, ikd_ref):
    kg = kg_ref[...]
    for h in range(ATTN_HEADS):
        sl = slice(h * ATTN_HEAD_DIM, (h + 1) * ATTN_HEAD_DIM)
        kn_ref[0, :, sl] = _rms(k_ref[0, :, sl].astype(F32), kg).astype(BF16)
    ikd_ref[0] = _dot(tail_ref[0], dup_ref[...]).astype(BF16)


def dsa_prep(proj, knorm, tm):
    b, s, _ = proj.shape
    w = ATTN_HEADS * ATTN_HEAD_DIM
    dup = np.zeros((TAIL, LANES), np.float32)
    dup[np.arange(IDX_DIM), np.arange(IDX_DIM)] = 1.0
    dup[np.arange(IDX_DIM), np.arange(IDX_DIM) + IDX_DIM] = 1.0
    return pl.pallas_call(
        _dsa_prep_kernel,
        grid=(b, s // tm),
        in_specs=[pl.BlockSpec((1, tm, w), lambda bi, i: (bi, i, 5)),
                  pl.BlockSpec((1, tm, TAIL), lambda bi, i: (bi, i, 8 * w // TAIL)),
                  pl.BlockSpec((1, ATTN_HEAD_DIM), lambda bi, i: (0, 0)),
                  pl.BlockSpec((TAIL, LANES), lambda bi, i: (0, 0))],
        out_specs=[pl.BlockSpec((1, tm, w), lambda bi, i: (bi, i, 0)),
                   pl.BlockSpec((1, tm, LANES), lambda bi, i: (bi, i, 0))],
        out_shape=[jax.ShapeDtypeStruct((b, s, w), BF16),
                   jax.ShapeDtypeStruct((b, s, LANES), BF16)],
        compiler_params=_params("parallel", "parallel"),
        name="dsa_prep",
    )(proj, proj, knorm, jnp.asarray(dup, BF16))


def _dsa_kernel(pi_ref, pj_ref, aq_ref, iq_ref, tail_ref, ikd_ref, kn_ref, v_ref, qg_ref, eq_ref, esel_ref,
                smax_ref, o_ref, keys_ref, thr_ref, qn_ref, qw_ref, lo_ref, hi_ref, m_ref, l_ref, acc_ref, p_ref,
                *, tq, tk, topk):
    step = pl.program_id(1)
    i = pi_ref[step]
    j = pj_ref[step]
    dh = ATTN_HEAD_DIM

    def causal(jj):
        s_pos = jj * tk + lax.broadcasted_iota(jnp.int32, (tk, tq), 0)
        t_pos = i * tq + lax.broadcasted_iota(jnp.int32, (tk, tq), 1)
        return s_pos <= t_pos

    @pl.when(j == 0)
    def _index_and_select():
        qg = qg_ref[...] * (dh ** -0.5 * LOG2E)
        for h in range(ATTN_HEADS):
            sl = slice(h * dh, (h + 1) * dh)
            qn_ref[:, sl] = _rms(aq_ref[0, :, sl].astype(F32), qg).astype(BF16)
        tail = tail_ref[0]
        ww = _dot(tail, eq_ref[...])
        qw = (iq_ref[0].astype(F32) * ww * (IDX_HEADS ** -0.5 * IDX_DIM ** -0.5)).astype(BF16)
        lane = lax.broadcasted_iota(jnp.int32, (tq, LANES), 1)
        zero = jnp.zeros((tq, LANES), BF16)
        for p in range(IDX_HEADS // 2):
            pair = qw[:, p * LANES:(p + 1) * LANES]
            qw_ref[2 * p] = jnp.where(lane < IDX_DIM, pair, zero)
            qw_ref[2 * p + 1] = jnp.where(lane >= IDX_DIM, pair, zero)
        w_t = _dot_nt(esel_ref[...], tail)
        lo_ref[...] = jnp.where(w_t > 0.0, 0.0, NEG_INF)
        hi_ref[...] = jnp.where(w_t > 0.0, jnp.inf, 0.0)

        def score_tile(jj, c):
            r0 = pl.multiple_of(jj * tk, tk)
            ik = ikd_ref[0, pl.ds(r0, tk), :]
            comb = jnp.zeros((tk, tq), F32)
            for h in range(IDX_HEADS):
                x = _dot_nt(ik, qw_ref[h])
                comb = comb + jnp.minimum(jnp.maximum(x, lo_ref[h:h + 1, :]), hi_ref[h:h + 1, :])
            keys_ref[jj] = jnp.where(causal(jj), comb, NEG_INF)
            return c
        lax.fori_loop(0, i + 1, score_tile, 0)

        def as_float(t):
            return pltpu.bitcast(jnp.where(t < 0, t ^ 0x7FFFFFFF, t), F32)
        def count_ge(tf):
            def body(jj, acc):
                accs = [acc, jnp.zeros_like(acc), jnp.zeros_like(acc), jnp.zeros_like(acc)]
                for r0 in range(0, tk, 8):
                    a = accs[(r0 // 8) % 4]
                    accs[(r0 // 8) % 4] = jnp.where(keys_ref[jj, r0:r0 + 8, :] >= tf, a + 1, a)
                return (accs[0] + accs[1]) + (accs[2] + accs[3])
            acc = lax.fori_loop(0, i + 1, body, jnp.zeros((8, tq), jnp.int32))
            return jnp.sum(acc.astype(F32), axis=0, keepdims=True)
        def unresolved(carry):
            bi, _, cnt = carry
            return jnp.logical_and(bi < 32, jnp.max(jnp.abs(cnt - topk)) > 0.0)
        def bit_steps(carry):
            bi, t, cnt = carry
            nbits = jnp.where(bi == 0, 24, 4)
            def one_bit(k, tc_cnt):
                t, cnt = tc_cnt
                tc = t + (jnp.int32(1) << (31 - bi - k))
                c = count_ge(as_float(tc))
                take = c >= topk
                return jnp.where(take, tc, t), jnp.where(take, c, cnt)
            t, cnt = lax.fori_loop(0, nbits, one_bit, (t, cnt))
            return bi + nbits, t, cnt
        n_all = ((i + 1) * tk).astype(F32)
        _, t_fin, _ = lax.while_loop(unresolved, bit_steps,
                                     (jnp.int32(0), jnp.full((1, tq), INT_MIN, jnp.int32),
                                      jnp.full((1, tq), n_all, F32)))
        t_float = as_float(t_fin)
        thr_ref[...] = jnp.where(t_float != t_float, NEG_INF, t_float)

        m_ref[...] = jnp.full_like(m_ref, NEG_INF)
        l_ref[...] = jnp.zeros_like(l_ref)
        acc_ref[...] = jnp.zeros_like(acc_ref)

    sel = (keys_ref[j] >= thr_ref[...]) & causal(j)
    smax = smax_ref[0]
    fixed_shift = smax <= SOFTMAX_FIXED_SHIFT_MAX

    @pl.when(fixed_shift)
    def _attend_fixed_shift():
        bias = jnp.where(sel, -smax * LOG2E, NEG_INF)
        for h in range(ATTN_HEADS):
            sl = slice(h * dh, (h + 1) * dh)
            p = jnp.exp2(_dot_nt(kn_ref[0, :, sl], qn_ref[:, sl]) + bias)
            parts = [p[r * 8:(r + 1) * 8] for r in range(tk // 8)]
            while len(parts) > 1:
                parts = [a + b for a, b in zip(parts[0::2], parts[1::2])]
            l_ref[h] += parts[0]
            p_ref[h] = p.astype(BF16)

    @pl.when(jnp.logical_and(fixed_shift, j <= i))
    def _weighted_values():
        for h in range(ATTN_HEADS):
            sl = slice(h * dh, (h + 1) * dh)
            acc_ref[h] += _dot_tn(v_ref[0, :, sl], p_ref[h])

    @pl.when(jnp.logical_not(fixed_shift))
    def _attend_running_max():
        bias = jnp.where(sel, 0.0, NEG_INF)
        m_all = m_ref[...]
        m_rows = []
        for h in range(ATTN_HEADS):
            sl = slice(h * dh, (h + 1) * dh)
            s = _dot_nt(kn_ref[0, :, sl], qn_ref[:, sl]) + bias
            m_old = m_all[h:h + 1, :]
            m_new = jnp.maximum(m_old, jnp.max(s, axis=0, keepdims=True))
            m_safe = jnp.where(m_new == NEG_INF, 0.0, m_new)
            alpha = jnp.exp2(m_old - m_safe)
            p = jnp.exp2(s - m_safe)
            l_ref[h, 0:1, :] = alpha * l_ref[h, 0:1, :] + jnp.sum(p, axis=0, keepdims=True)
            acc_ref[h] = alpha * acc_ref[h] + _dot_tn(v_ref[0, :, sl], p.astype(BF16))
            m_rows.append(m_new)
        m_ref[...] = jnp.concatenate(m_rows, axis=0)

    @pl.when(j == i)
    def _finish():
        for h in range(ATTN_HEADS):
            sl = slice(h * dh, (h + 1) * dh)
            l = jnp.sum(l_ref[h], axis=0, keepdims=True)
            o_ref[0, :, sl] = (acc_ref[h] / l).T.astype(o_ref.dtype)


def dsa_group(proj, kn, ikd, qnorm, knorm, tq):
    b, s, _ = proj.shape
    tk = tq
    smax = (ATTN_HEAD_DIM ** 0.5 * jnp.max(jnp.abs(qnorm)) * jnp.max(jnp.abs(knorm))).reshape(1).astype(F32)
    nq = s // tq
    w = ATTN_HEADS * ATTN_HEAD_DIM
    topk = min(DSA_TOPK, s // 4)
    eq = np.zeros((TAIL, IDX_HEADS * IDX_DIM), np.float32)
    esel = np.zeros((IDX_HEADS, TAIL), np.float32)
    for h in range(IDX_HEADS):
        eq[IDX_DIM + h, h * IDX_DIM:(h + 1) * IDX_DIM] = 1.0
        esel[h, IDX_DIM + h] = 1.0
    pairs = [(i, j) for i in range(nq) for j in range(i + 1)]
    pi = jnp.asarray([p[0] for p in pairs], jnp.int32)
    pj = jnp.asarray([p[1] for p in pairs], jnp.int32)
    qblk = lambda cb: pl.BlockSpec((1, tq, w), lambda bi, st, pi, pj, cb=cb: (bi, pi[st], cb))
    const = lambda a: pl.BlockSpec(a.shape, lambda bi, st, pi, pj: (0,) * a.ndim)
    grid_spec = pltpu.PrefetchScalarGridSpec(
        num_scalar_prefetch=2,
        grid=(b, len(pairs)),
        in_specs=[qblk(4),
                  qblk(7),
                  pl.BlockSpec((1, tq, TAIL), lambda bi, st, pi, pj: (bi, pi[st], 8 * w // TAIL)),
                  pl.BlockSpec((1, s, LANES), lambda bi, st, pi, pj: (bi, 0, 0)),
                  pl.BlockSpec((1, tk, w), lambda bi, st, pi, pj: (bi, pj[st], 0)),
                  pl.BlockSpec((1, tk, w), lambda bi, st, pi, pj: (bi, pj[st], 6)),
                  const(qnorm), const(eq), const(esel),
                  pl.BlockSpec(memory_space=pltpu.SMEM)],
        out_specs=pl.BlockSpec((1, tq, w), lambda bi, st, pi, pj: (bi, pi[st], 0)),
        scratch_shapes=[pltpu.VMEM((nq, tk, tq), F32),
                        pltpu.VMEM((1, tq), F32),
                        pltpu.VMEM((tq, w), BF16),
                        pltpu.VMEM((IDX_HEADS, tq, LANES), BF16),
                        pltpu.VMEM((IDX_HEADS, tq), F32),
                        pltpu.VMEM((IDX_HEADS, tq), F32),
                        pltpu.VMEM((ATTN_HEADS, tq), F32),
                        pltpu.VMEM((ATTN_HEADS, 8, tq), F32),
                        pltpu.VMEM((ATTN_HEADS, ATTN_HEAD_DIM, tq), F32),
                        pltpu.VMEM((ATTN_HEADS, tk, tq), BF16)])
    return pl.pallas_call(
        functools.partial(_dsa_kernel, tq=tq, tk=tk, topk=topk),
        grid_spec=grid_spec,
        out_shape=jax.ShapeDtypeStruct((b, s, w), BF16),
        compiler_params=_params("parallel", "arbitrary"),
        name="dsa",
    )(pi, pj, proj, proj, proj, ikd, kn, proj, qnorm, jnp.asarray(eq, BF16), jnp.asarray(esel, BF16), smax)


def _cast_rows_to(src_ref, dst_ref, step=256):
    def body(r, c):
        r0 = pl.multiple_of(r * step, step)
        dst_ref[pl.ds(r0, step), :] = src_ref[0, pl.ds(r0, step), :].astype(BF16)
        return c
    lax.fori_loop(0, dst_ref.shape[0] // step, body, 0)


def _out_proj_kernel(yh_ref, ya_ref, w_ref, x_ref, o_ref, wb_ref):
    @pl.when(pl.program_id(0) == 0)
    def _():
        _cast_rows_to(w_ref, wb_ref)
    kh = yh_ref.shape[1]
    o_ref[...] = x_ref[...] + _dot(yh_ref[...], wb_ref[0:kh, :]) + _dot(ya_ref[...], wb_ref[kh:2 * kh, :])


def out_proj(yh, ya, w_out, layer, x, tm):
    m, kh = yh.shape
    _, k, n = w_out.shape
    return pl.pallas_call(
        _out_proj_kernel,
        grid=(m // tm,),
        in_specs=[pl.BlockSpec((tm, kh), lambda i: (i, 0)),
                  pl.BlockSpec((tm, kh), lambda i: (i, 0)),
                  pl.BlockSpec((1, k, n), lambda i: (layer, 0, 0), pipeline_mode=pl.Buffered(1)),
                  pl.BlockSpec((tm, n), lambda i: (i, 0))],
        out_specs=pl.BlockSpec((tm, n), lambda i: (i, 0)),
        out_shape=jax.ShapeDtypeStruct((m, n), F32),
        scratch_shapes=[pltpu.VMEM((k, n), BF16)],
        compiler_params=_params("arbitrary"),
        name="out_proj",
    )(yh, ya, w_out, x)


def _mem_kv_kernel(mem_ref, mg_ref, wk_ref, wv_ref, kg_ref, k_ref, v_ref):
    memn = _rms(mem_ref[0], mg_ref[...]).astype(BF16)
    k = _dot(memn, wk_ref[0].astype(BF16))
    kg = kg_ref[...]
    for h in range(CROSS_HEADS):
        sl = slice(h * CROSS_HEAD_DIM, (h + 1) * CROSS_HEAD_DIM)
        k_ref[0, :, sl] = _rms(k[:, sl], kg).astype(BF16)
    v_ref[0] = _dot(memn, wv_ref[0].astype(BF16)).astype(BF16)


def mem_kv(mem, mem_norm, wk, wv, layer, xk_norm):
    b, nm, d = mem.shape
    cw = wk.shape[2]
    full = lambda a: pl.BlockSpec(a.shape, lambda bi: (0,) * a.ndim)
    wspec = pl.BlockSpec((1, d, cw), lambda bi: (layer, 0, 0))
    return pl.pallas_call(
        _mem_kv_kernel,
        grid=(b,),
        in_specs=[pl.BlockSpec((1, nm, d), lambda bi: (bi, 0, 0)),
                  full(mem_norm), wspec, wspec, full(xk_norm)],
        out_specs=[pl.BlockSpec((1, nm, cw), lambda bi: (bi, 0, 0))] * 2,
        out_shape=[jax.ShapeDtypeStruct((b, nm, cw), BF16)] * 2,
        compiler_params=_params("parallel"),
        name="mem_kv",
    )(mem, mem_norm, wk, wv, xk_norm)


def _cross_kernel(h_ref, ng_ref, wq_ref, qg_ref, k_ref, v_ref, wo_ref, o_ref, hn_ref, oc_ref, wqb_ref, wob_ref):
    @pl.when((pl.program_id(0) == 0) & (pl.program_id(1) == 0))
    def _():
        _cast_rows_to(wq_ref, wqb_ref)
        _cast_rows_to(wo_ref, wob_ref)
    rows = h_ref.shape[1]
    dh = CROSS_HEAD_DIM
    step = 256
    gain = ng_ref[...]
    def body(r, c):
        r0 = pl.multiple_of(r * step, step)
        hn_ref[pl.ds(r0, step), :] = _rms(h_ref[0, pl.ds(r0, step), :], gain).astype(BF16)
        return c
    lax.fori_loop(0, rows // step, body, 0)
    q = _dot(hn_ref[...], wqb_ref[...])
    qg = qg_ref[...] * (dh ** -0.5)
    for h in range(CROSS_HEADS):
        sl = slice(h * dh, (h + 1) * dh)
        qn = _rms(q[:, sl], qg).astype(BF16)
        s = _dot_nt(qn, k_ref[0, :, sl])
        p = jnp.exp(s - jnp.max(s, axis=1, keepdims=True))
        l = jnp.sum(p, axis=1, keepdims=True)
        oc_ref[:, sl] = (_dot(p.astype(BF16), v_ref[0, :, sl]) / l).astype(BF16)
    o_ref[0] = h_ref[0] + _dot(oc_ref[...], wob_ref[...])


def cross_attention(h, norm_cross, wq, xq_norm, kx, vx, wo, layer, tm):
    b, s, d = h.shape
    nm, cw = kx.shape[1:]
    full = lambda a: pl.BlockSpec(a.shape, lambda bi, i: (0,) * a.ndim)
    resident = lambda a: pl.BlockSpec((1,) + a.shape[1:], lambda bi, i: (layer, 0, 0),
                                      pipeline_mode=pl.Buffered(1))
    return pl.pallas_call(
        _cross_kernel,
        grid=(b, s // tm),
        in_specs=[pl.BlockSpec((1, tm, d), lambda bi, i: (bi, i, 0)),
                  full(norm_cross), resident(wq), full(xq_norm),
                  pl.BlockSpec((1, nm, cw), lambda bi, i: (bi, 0, 0)),
                  pl.BlockSpec((1, nm, cw), lambda bi, i: (bi, 0, 0)),
                  resident(wo)],
        out_specs=pl.BlockSpec((1, tm, d), lambda bi, i: (bi, i, 0)),
        out_shape=jax.ShapeDtypeStruct((b, s, d), F32),
        scratch_shapes=[pltpu.VMEM((tm, d), BF16), pltpu.VMEM((tm, cw), BF16),
                        pltpu.VMEM(wq.shape[1:], BF16), pltpu.VMEM(wo.shape[1:], BF16)],
        compiler_params=_params("arbitrary", "arbitrary"),
        name="cross_attn",
    )(h, norm_cross, wq, xq_norm, kx, vx, wo)


def _mlp_kernel(h_ref, g_ref, wu_ref, wd_ref, o_ref, hn_ref):
    @pl.when(pl.program_id(1) == 0)
    def _():
        _norm_rows_to(h_ref, g_ref[...], hn_ref, h_ref.shape[0])
        o_ref[...] = h_ref[...]

    u = jnp.maximum(_dot(hn_ref[...], wu_ref[0].astype(BF16)), 0.0)
    o_ref[...] += _dot((u * u).astype(BF16), wd_ref[0].astype(BF16))


def mlp(h, gain, w_up, w_down, layer, tm, tf):
    m, d = h.shape
    f = w_up.shape[2]
    return pl.pallas_call(
        _mlp_kernel,
        grid=(m // tm, f // tf),
        in_specs=[pl.BlockSpec((tm, d), lambda i, j: (i, 0)),
                  pl.BlockSpec((1, d), lambda i, j: (0, 0)),
                  pl.BlockSpec((1, d, tf), lambda i, j: (layer, 0, j)),
                  pl.BlockSpec((1, tf, d), lambda i, j: (layer, j, 0))],
        out_specs=pl.BlockSpec((tm, d), lambda i, j: (i, 0), pipeline_mode=pl.Buffered(1)),
        out_shape=jax.ShapeDtypeStruct((m, d), F32),
        scratch_shapes=[pltpu.VMEM((tm, d), BF16)],
        compiler_params=_params("parallel", "arbitrary"),
        name="mlp",
    )(h, gain, w_up, w_down)


def _tile(n, pref):
    return pref if n % pref == 0 else n


def _tiles(n, s):
    return dict(
        proj_rows=_tile(n, 1024), proj_cols=768,
        hgrn_rows=_tile(s, 1024),
        dsa_block=_tile(s, 512),
        out_rows=_tile(n, 512),
        cross_rows=_tile(s, 1024),
        mlp_rows=_tile(n, 1024), mlp_ff=512)


def kernel(x, mem, norm_mix, w_in, hgrn_lb_logits, hgrn_onorm, attn_qnorm, attn_knorm, w_out,
           norm_cross, mem_norm, wq_x, wk_x, wv_x, wo_x, xq_norm, xk_norm,
           norm_mlp, w_up, w_down):
    b, s, d = x.shape
    n = b * s
    depth = w_in.shape[0]
    assert depth == 1
    l = 0
    in_width = w_in.shape[2]
    main_w = 8 * HGRN_HEADS * HGRN_KDIM
    assert in_width == main_w + IDX_DIM + IDX_HEADS
    pad_w = main_w + TAIL

    t = _tiles(n, s)
    x2 = x.reshape(n, d)

    proj = norm_matmul(x2, norm_mix[l:l + 1], w_in[l].T, pad_w,
                       t["proj_rows"], t["proj_cols"])
    proj3 = proj.reshape(b, s, pad_w)

    y_h = hgrn_group(proj3, hgrn_lb_logits, hgrn_onorm[l:l + 1], t["hgrn_rows"])
    kn, ikd = dsa_prep(proj3, attn_knorm[l:l + 1], t["dsa_block"])
    y_a = dsa_group(proj3, kn, ikd, attn_qnorm[l:l + 1], attn_knorm[l:l + 1], t["dsa_block"])

    h1 = out_proj(y_h.reshape(n, -1), y_a.reshape(n, -1), w_out, l, x2, t["out_rows"])

    kx, vx = mem_kv(mem, mem_norm[l:l + 1], wk_x, wv_x, l, xk_norm[l:l + 1])
    h2 = cross_attention(h1.reshape(b, s, d), norm_cross[l:l + 1], wq_x, xq_norm[l:l + 1],
                         kx, vx, wo_x, l, t["cross_rows"])

    h3 = mlp(h2.reshape(n, d), norm_mlp[l:l + 1], w_up, w_down, l, t["mlp_rows"], t["mlp_ff"])
    return h3.reshape(b, s, d)
```

```python
import functools

import numpy as np
import jax
import jax.numpy as jnp
from jax import lax
from jax.experimental import pallas as pl
from jax.experimental.pallas import tpu as pltpu

F32 = jnp.float32
BF16 = jnp.bfloat16
EPS = 1e-6

LANES = 128
HGRN_HEADS = 8
HGRN_KDIM = 128
HGRN_CHUNK = 64
ATTN_HEADS = 8
ATTN_HEAD_DIM = 128
IDX_HEADS = 16
IDX_DIM = 64
DSA_TOPK = 256
CROSS_HEADS = 4
CROSS_HEAD_DIM = 128
TAIL = 256
VMEM_LIMIT = 56 * 1024 * 1024
NEG_INF = float("-inf")
INT_MIN = -(2 ** 31)
LOG2E = 1.4426950408889634
SOFTMAX_FIXED_SHIFT_MAX = 40.0


def _params(*sem, flags=None):
    return pltpu.CompilerParams(dimension_semantics=sem, vmem_limit_bytes=VMEM_LIMIT, flags=flags)


def _rms(x, gain):
    return x * lax.rsqrt(jnp.mean(x * x, axis=-1, keepdims=True) + EPS) * gain


def _sigmoid(x):
    return 1.0 / (1.0 + jnp.exp(-x))


def _dot(a, b):
    return jnp.dot(a, b, preferred_element_type=F32)


def _dot_nt(a, b):
    return lax.dot_general(a, b, (((1,), (1,)), ((), ())), preferred_element_type=F32)


def _dot_tn(a, b):
    return lax.dot_general(a, b, (((0,), (0,)), ((), ())), preferred_element_type=F32)


def _norm_rows_to(x_ref, gain, dst_ref, rows):
    step = 256
    def body(r, c):
        r0 = pl.multiple_of(r * step, step)
        x = x_ref[pl.ds(r0, step), :]
        dst_ref[pl.ds(r0, step), :] = _rms(x, gain).astype(BF16)
        return c
    lax.fori_loop(0, rows // step, body, 0)


def _norm_matmul_kernel(x_ref, g_ref, wt_ref, o_ref, xn_ref, *, valid_cols):
    j = pl.program_id(1)
    tn = o_ref.shape[1]

    @pl.when(j == 0)
    def _():
        _norm_rows_to(x_ref, g_ref[...], xn_ref, x_ref.shape[0])

    res = _dot_nt(xn_ref[...], wt_ref[...].astype(BF16))

    @pl.when((j + 1) * tn <= valid_cols)
    def _():
        o_ref[...] = res.astype(o_ref.dtype)

    @pl.when((j + 1) * tn > valid_cols)
    def _():
        col = j * tn + lax.broadcasted_iota(jnp.int32, res.shape, 1)
        o_ref[...] = jnp.where(col < valid_cols, res, 0.0).astype(o_ref.dtype)


def norm_matmul(x, gain, wt, n, tm, tn):
    m, k = x.shape
    return pl.pallas_call(
        functools.partial(_norm_matmul_kernel, valid_cols=wt.shape[0]),
        grid=(m // tm, n // tn),
        in_specs=[pl.BlockSpec((tm, k), lambda i, j: (i, 0)),
                  pl.BlockSpec((1, k), lambda i, j: (0, 0)),
                  pl.BlockSpec((tn, k), lambda i, j: (j, 0))],
        out_specs=pl.BlockSpec((tm, tn), lambda i, j: (i, j)),
        out_shape=jax.ShapeDtypeStruct((m, n), BF16),
        scratch_shapes=[pltpu.VMEM((tm, k), BF16)],
        compiler_params=_params("parallel", "arbitrary"),
        name="in_proj",
    )(x, gain, wt)


def _hgrn_kernel(q_ref, f_ref, i_ref, g_ref, lbl_ref, on_ref, tril_ref, o_ref,
                 state_ref, a_ref, qi_ref, ks_ref, el_ref, qt_ref, kt_ref, sc_ref, scb_ref, *, nchunks):
    c = HGRN_CHUNK
    dk = HGRN_KDIM
    pw = 2 * dk
    npairs = HGRN_HEADS // 2
    nlev = 6

    @pl.when(pl.program_id(1) == 0)
    def _():
        state_ref[...] = jnp.zeros_like(state_ref)
        kt_ref[...] = jnp.zeros_like(kt_ref)

    lbl = lbl_ref[...]
    e = jnp.exp(lbl - jnp.max(lbl, axis=0, keepdims=True))
    lb = e[0:1] / jnp.sum(e, axis=0, keepdims=True)
    onorm = on_ref[...]
    tril = tril_ref[...]

    row = lax.broadcasted_iota(jnp.int32, (c, 2 * c), 0)
    lane = lax.broadcasted_iota(jnp.int32, (c, 2 * c), 1)
    col = lane & (c - 1)
    first = lane < c
    eye = row == col
    level_masks = []
    lg = nlev - 1
    while lg >= 0:
        level_masks.append(((row >> (lg + 1)) == (col >> (lg + 1)))
                           & (((row >> lg) & 1) == 1) & (((col >> lg) & 1) == 0))
        lg -= 1
    lane_p = lax.broadcasted_iota(jnp.int32, (c, pw), 1)
    odd_row = (lax.broadcasted_iota(jnp.int32, (c, pw), 0) & 1) == 1
    sub8 = lax.broadcasted_iota(jnp.int32, (8, pw), 0)
    zeros_st = jnp.zeros((dk, dk), BF16)

    def chunk_body(ci, carry):
        r0 = pl.multiple_of(ci * c, c)

        def _decay_factors():
            hq = q_ref[0, pl.ds(r0, c), :].astype(F32)
            hf = f_ref[0, pl.ds(r0, c), :].astype(F32)
            qf = hq * _sigmoid(hq) * (dk ** -0.5)
            f = lb + (1.0 - lb) * _sigmoid(hf)
            logf = jnp.log(f) * LOG2E
            kk = 1.0 - f
            g0 = logf.astype(BF16)
            g1 = (logf - g0.astype(F32)).astype(BF16)
            a_ref[...] = _dot(tril, g0) + _dot(tril, g1)
            for p in range(npairs):
                sl = slice(p * pw, (p + 1) * pw)
                a = a_ref[:, sl]
                row = lambda r, n: jnp.broadcast_to(a_ref[r:r + 1, sl], (n, pw))
                e0 = jnp.exp2(a)
                qp = qf[:, sl]
                kp = kk[:, sl]
                qi_ref[p] = (qp * e0).astype(BF16)
                ks_ref[p] = (kp * jnp.exp2(row(c - 1, c) - a)).astype(BF16)
                el_ref[p] = e0[c - 1:c]
                for l in range(nlev):
                    h = c >> (l + 1)
                    if h >= 4:
                        ref = jnp.concatenate([row(m * 2 * h + h - 1, 2 * h) for m in range(c // (2 * h))], axis=0)
                    elif h == 2:
                        ref = jnp.concatenate([jnp.where(sub8 < 4, row(8 * m + 1, 8), row(8 * m + 5, 8))
                                               for m in range(c // 8)], axis=0)
                    else:
                        ref = jnp.where(odd_row, pltpu.roll(a, 1, 0), a)
                    el = jnp.exp2(-jnp.abs(a - ref))
                    qt_ref[p, l] = (qp * el).astype(BF16)
                    kl = (kp * el).astype(BF16)
                    kt_ref[p, l, 0:c, 0:dk] = kl[:, 0:dk]
                    kt_ref[p, l, c:2 * c, dk:pw] = kl[:, dk:pw]
                qk = qp * kp
                diag = jnp.where(first, jnp.sum(qk[:, 0:dk], axis=1, keepdims=True),
                                 jnp.sum(qk[:, dk:pw], axis=1, keepdims=True))
                sc_ref[p] = jnp.where(eye, diag, 0.0)

        def _intra_chunk_scores():
            for p in range(npairs):
                sc = sc_ref[p]
                for l, msk in enumerate(level_masks):
                    sc = sc + jnp.where(msk, _dot_nt(qt_ref[p, l], kt_ref[p, l]), 0.0)
                scb_ref[p] = sc.astype(BF16)

        def _outputs_and_state():
            for p in range(npairs):
                sl = slice(p * pw, (p + 1) * pw)
                vp = i_ref[0, pl.ds(r0, c), sl]
                st_a = state_ref[2 * p]
                st_b = state_ref[2 * p + 1]
                st_bd = jnp.concatenate(
                    [jnp.concatenate([st_a.astype(BF16), zeros_st], axis=1),
                     jnp.concatenate([zeros_st, st_b.astype(BF16)], axis=1)], axis=0)
                v_bd = jnp.concatenate([jnp.where(lane_p < dk, vp, jnp.zeros_like(vp)),
                                        jnp.where(lane_p >= dk, vp, jnp.zeros_like(vp))], axis=0)
                o = _dot_nt(qi_ref[p], st_bd) + _dot(scb_ref[p], v_bd)
                upd = _dot_tn(vp, ks_ref[p])
                el = el_ref[p]
                state_ref[2 * p] = st_a * el[:, 0:dk] + upd[0:dk, 0:dk]
                state_ref[2 * p + 1] = st_b * el[:, dk:pw] + upd[dk:pw, dk:pw]
                gate = g_ref[0, pl.ds(r0, c), sl].astype(F32)
                y = jnp.concatenate([_rms(o[:, 0:dk], onorm), _rms(o[:, dk:pw], onorm)], axis=1)
                o_ref[0, pl.ds(r0, c), sl] = (y * (gate * _sigmoid(gate))).astype(o_ref.dtype)

        _decay_factors()
        _intra_chunk_scores()
        _outputs_and_state()
        return carry

    lax.fori_loop(0, nchunks, chunk_body, 0, unroll=4)


def hgrn_group(proj, lb_logits, onorm, t_blk):
    b, s, _ = proj.shape
    w = HGRN_HEADS * HGRN_KDIM
    tril = jnp.asarray(np.tril(np.ones((HGRN_CHUNK, HGRN_CHUNK), np.float32)), BF16)
    c, pw, npairs, nlev = HGRN_CHUNK, 2 * HGRN_KDIM, HGRN_HEADS // 2, 6
    col = lambda cb: pl.BlockSpec((1, t_blk, w), lambda bi, ti, cb=cb: (bi, ti, cb))
    return pl.pallas_call(
        functools.partial(_hgrn_kernel, nchunks=t_blk // HGRN_CHUNK),
        grid=(b, s // t_blk),
        in_specs=[col(0), col(1), col(2), col(3),
                  pl.BlockSpec(lb_logits.shape, lambda bi, ti: (0, 0)),
                  pl.BlockSpec((1, HGRN_KDIM), lambda bi, ti: (0, 0)),
                  pl.BlockSpec(tril.shape, lambda bi, ti: (0, 0))],
        out_specs=pl.BlockSpec((1, t_blk, w), lambda bi, ti: (bi, ti, 0)),
        out_shape=jax.ShapeDtypeStruct((b, s, w), BF16),
        scratch_shapes=[pltpu.VMEM((HGRN_HEADS, HGRN_KDIM, HGRN_KDIM), F32),
                        pltpu.VMEM((c, w), F32),
                        pltpu.VMEM((npairs, c, pw), BF16),
                        pltpu.VMEM((npairs, c, pw), BF16),
                        pltpu.VMEM((npairs, 1, pw), F32),
                        pltpu.VMEM((npairs, nlev, c, pw), BF16),
                        pltpu.VMEM((npairs, nlev, 2 * c, pw), BF16),
                        pltpu.VMEM((npairs, c, 2 * c), F32),
                        pltpu.VMEM((npairs, c, 2 * c), BF16)],
        compiler_params=_params("parallel", "arbitrary"),
        name="hgrn2",
    )(proj, proj, proj, proj, lb_logits, onorm, tril)


def _dsa_prep_kernel(k_ref, tail_ref, kg_ref, dup_ref, kn_ref, ikd_ref):
    kg = kg_ref[...]
    for h in range(ATTN_HEADS):
        sl = slice(h * ATTN_HEAD_DIM, (h + 1) * ATTN_HEAD_DIM)
        kn_ref[0, :, sl] = _rms(k_ref[0, :, sl].astype(F32), kg).astype(BF16)
    ikd_ref[0] = _dot(tail_ref[0], dup_ref[...]).astype(BF16)


def dsa_prep(proj, knorm, tm):
    b, s, _ = proj.shape
    w = ATTN_HEADS * ATTN_HEAD_DIM
    dup = np.zeros((TAIL, LANES), np.float32)
    dup[np.arange(IDX_DIM), np.arange(IDX_DIM)] = 1.0
    dup[np.arange(IDX_DIM), np.arange(IDX_DIM) + IDX_DIM] = 1.0
    return pl.pallas_call(
        _dsa_prep_kernel,
        grid=(b, s // tm),
        in_specs=[pl.BlockSpec((1, tm, w), lambda bi, i: (bi, i, 5)),
                  pl.BlockSpec((1, tm, TAIL), lambda bi, i: (bi, i, 8 * w // TAIL)),
                  pl.BlockSpec((1, ATTN_HEAD_DIM), lambda bi, i: (0, 0)),
                  pl.BlockSpec((TAIL, LANES), lambda bi, i: (0, 0))],
        out_specs=[pl.BlockSpec((1, tm, w), lambda bi, i: (bi, i, 0)),
                   pl.BlockSpec((1, tm, LANES), lambda bi, i: (bi, i, 0))],
        out_shape=[jax.ShapeDtypeStruct((b, s, w), BF16),
                   jax.ShapeDtypeStruct((b, s, LANES), BF16)],
        compiler_params=_params("parallel", "parallel"),
        name="dsa_prep",
    )(proj, proj, knorm, jnp.asarray(dup, BF16))


def _dsa_kernel(pi_ref, pj_ref, aq_ref, iq_ref, tail_ref, ikd_ref, kn_ref, v_ref, qg_ref, eq_ref, esel_ref,
                smax_ref, o_ref, keys_ref, thr_ref, qn_ref, qw_ref, lo_ref, hi_ref, m_ref, l_ref, acc_ref, p_ref,
                *, tq, tk, topk):
    step = pl.program_id(1)
    i = pi_ref[step]
    j = pj_ref[step]
    dh = ATTN_HEAD_DIM

    def causal(jj):
        s_pos = jj * tk + lax.broadcasted_iota(jnp.int32, (tk, tq), 0)
        t_pos = i * tq + lax.broadcasted_iota(jnp.int32, (tk, tq), 1)
        return s_pos <= t_pos

    @pl.when(j == 0)
    def _index_and_select():
        qg = qg_ref[...] * (dh ** -0.5 * LOG2E)
        for h in range(ATTN_HEADS):
            sl = slice(h * dh, (h + 1) * dh)
            qn_ref[:, sl] = _rms(aq_ref[0, :, sl].astype(F32), qg).astype(BF16)
        tail = tail_ref[0]
        ww = _dot(tail, eq_ref[...])
        qw = (iq_ref[0].astype(F32) * ww * (IDX_HEADS ** -0.5 * IDX_DIM ** -0.5)).astype(BF16)
        lane = lax.broadcasted_iota(jnp.int32, (tq, LANES), 1)
        zero = jnp.zeros((tq, LANES), BF16)
        for p in range(IDX_HEADS // 2):
            pair = qw[:, p * LANES:(p + 1) * LANES]
            qw_ref[2 * p] = jnp.where(lane < IDX_DIM, pair, zero)
            qw_ref[2 * p + 1] = jnp.where(lane >= IDX_DIM, pair, zero)
        w_t = _dot_nt(esel_ref[...], tail)
        lo_ref[...] = jnp.where(w_t > 0.0, 0.0, NEG_INF)
        hi_ref[...] = jnp.where(w_t > 0.0, jnp.inf, 0.0)

        def score_tile(jj, c):
            r0 = pl.multiple_of(jj * tk, tk)
            ik = ikd_ref[0, pl.ds(r0, tk), :]
            comb = jnp.zeros((tk, tq), F32)
            for h in range(IDX_HEADS):
                x = _dot_nt(ik, qw_ref[h])
                comb = comb + jnp.minimum(jnp.maximum(x, lo_ref[h:h + 1, :]), hi_ref[h:h + 1, :])
            keys_ref[jj] = jnp.where(causal(jj), comb, NEG_INF)
            return c
        lax.fori_loop(0, i + 1, score_tile, 0)

        def as_float(t):
            return pltpu.bitcast(jnp.where(t < 0, t ^ 0x7FFFFFFF, t), F32)
        def count_ge(tf):
            def body(jj, acc):
                accs = [acc, jnp.zeros_like(acc), jnp.zeros_like(acc), jnp.zeros_like(acc)]
                for r0 in range(0, tk, 8):
                    a = accs[(r0 // 8) % 4]
                    accs[(r0 // 8) % 4] = jnp.where(keys_ref[jj, r0:r0 + 8, :] >= tf, a + 1, a)
                return (accs[0] + accs[1]) + (accs[2] + accs[3])
            acc = lax.fori_loop(0, i + 1, body, jnp.zeros((8, tq), jnp.int32))
            return jnp.sum(acc.astype(F32), axis=0, keepdims=True)
        def unresolved(carry):
            bi, _, cnt = carry
            return jnp.logical_and(bi < 32, jnp.max(jnp.abs(cnt - topk)) > 0.0)
        def bit_steps(carry):
            bi, t, cnt = carry
            nbits = jnp.where(bi == 0, 24, 4)
            def one_bit(k, tc_cnt):
                t, cnt = tc_cnt
                tc = t + (jnp.int32(1) << (31 - bi - k))
                c = count_ge(as_float(tc))
                take = c >= topk
                return jnp.where(take, tc, t), jnp.where(take, c, cnt)
            t, cnt = lax.fori_loop(0, nbits, one_bit, (t, cnt))
            return bi + nbits, t, cnt
        n_all = ((i + 1) * tk).astype(F32)
        _, t_fin, _ = lax.while_loop(unresolved, bit_steps,
                                     (jnp.int32(0), jnp.full((1, tq), INT_MIN, jnp.int32),
                                      jnp.full((1, tq), n_all, F32)))
        t_float = as_float(t_fin)
        thr_ref[...] = jnp.where(t_float != t_float, NEG_INF, t_float)

        m_ref[...] = jnp.full_like(m_ref, NEG_INF)
        l_ref[...] = jnp.zeros_like(l_ref)
        acc_ref[...] = jnp.zeros_like(acc_ref)

    sel = (keys_ref[j] >= thr_ref[...]) & causal(j)
    smax = smax_ref[0]
    fixed_shift = smax <= SOFTMAX_FIXED_SHIFT_MAX

    @pl.when(fixed_shift)
    def _attend_fixed_shift():
        bias = jnp.where(sel, -smax * LOG2E, NEG_INF)
        for h in range(ATTN_HEADS):
            sl = slice(h * dh, (h + 1) * dh)
            p = jnp.exp2(_dot_nt(kn_ref[0, :, sl], qn_ref[:, sl]) + bias)
            parts = [p[r * 8:(r + 1) * 8] for r in range(tk // 8)]
            while len(parts) > 1:
                parts = [a + b for a, b in zip(parts[0::2], parts[1::2])]
            l_ref[h] += parts[0]
            p_ref[h] = p.astype(BF16)

    @pl.when(jnp.logical_and(fixed_shift, j <= i))
    def _weighted_values():
        for h in range(ATTN_HEADS):
            sl = slice(h * dh, (h + 1) * dh)
            acc_ref[h] += _dot_tn(v_ref[0, :, sl], p_ref[h])

    @pl.when(jnp.logical_not(fixed_shift))
    def _attend_running_max():
        bias = jnp.where(sel, 0.0, NEG_INF)
        m_all = m_ref[...]
        m_rows = []
        for h in range(ATTN_HEADS):
            sl = slice(h * dh, (h + 1) * dh)
            s = _dot_nt(kn_ref[0, :, sl], qn_ref[:, sl]) + bias
            m_old = m_all[h:h + 1, :]
            m_new = jnp.maximum(m_old, jnp.max(s, axis=0, keepdims=True))
            m_safe = jnp.where(m_new == NEG_INF, 0.0, m_new)
            alpha = jnp.exp2(m_old - m_safe)
            p = jnp.exp2(s - m_safe)
            l_ref[h, 0:1, :] = alpha * l_ref[h, 0:1, :] + jnp.sum(p, axis=0, keepdims=True)
            acc_ref[h] = alpha * acc_ref[h] + _dot_tn(v_ref[0, :, sl], p.astype(BF16))
            m_rows.append(m_new)
        m_ref[...] = jnp.concatenate(m_rows, axis=0)

    @pl.when(j == i)
    def _finish():
        for h in range(ATTN_HEADS):
            sl = slice(h * dh, (h + 1) * dh)
            l = jnp.sum(l_ref[h], axis=0, keepdims=True)
            o_ref[0, :, sl] = (acc_ref[h] / l).T.astype(o_ref.dtype)


def dsa_group(proj, kn, ikd, qnorm, knorm, tq):
    b, s, _ = proj.shape
    tk = tq
    smax = (ATTN_HEAD_DIM ** 0.5 * jnp.max(jnp.abs(qnorm)) * jnp.max(jnp.abs(knorm))).reshape(1).astype(F32)
    nq = s // tq
    w = ATTN_HEADS * ATTN_HEAD_DIM
    topk = min(DSA_TOPK, s // 4)
    eq = np.zeros((TAIL, IDX_HEADS * IDX_DIM), np.float32)
    esel = np.zeros((IDX_HEADS, TAIL), np.float32)
    for h in range(IDX_HEADS):
        eq[IDX_DIM + h, h * IDX_DIM:(h + 1) * IDX_DIM] = 1.0
        esel[h, IDX_DIM + h] = 1.0
    pairs = [(i, j) for i in range(nq) for j in range(i + 1)]
    pi = jnp.asarray([p[0] for p in pairs], jnp.int32)
    pj = jnp.asarray([p[1] for p in pairs], jnp.int32)
    qblk = lambda cb: pl.BlockSpec((1, tq, w), lambda bi, st, pi, pj, cb=cb: (bi, pi[st], cb))
    const = lambda a: pl.BlockSpec(a.shape, lambda bi, st, pi, pj: (0,) * a.ndim)
    grid_spec = pltpu.PrefetchScalarGridSpec(
        num_scalar_prefetch=2,
        grid=(b, len(pairs)),
        in_specs=[qblk(4),
                  qblk(7),
                  pl.BlockSpec((1, tq, TAIL), lambda bi, st, pi, pj: (bi, pi[st], 8 * w // TAIL)),
                  pl.BlockSpec((1, s, LANES), lambda bi, st, pi, pj: (bi, 0, 0)),
                  pl.BlockSpec((1, tk, w), lambda bi, st, pi, pj: (bi, pj[st], 0)),
                  pl.BlockSpec((1, tk, w), lambda bi, st, pi, pj: (bi, pj[st], 6)),
                  const(qnorm), const(eq), const(esel),
                  pl.BlockSpec(memory_space=pltpu.SMEM)],
        out_specs=pl.BlockSpec((1, tq, w), lambda bi, st, pi, pj: (bi, pi[st], 0)),
        scratch_shapes=[pltpu.VMEM((nq, tk, tq), F32),
                        pltpu.VMEM((1, tq), F32),
                        pltpu.VMEM((tq, w), BF16),
                        pltpu.VMEM((IDX_HEADS, tq, LANES), BF16),
                        pltpu.VMEM((IDX_HEADS, tq), F32),
                        pltpu.VMEM((IDX_HEADS, tq), F32),
                        pltpu.VMEM((ATTN_HEADS, tq), F32),
                        pltpu.VMEM((ATTN_HEADS, 8, tq), F32),
                        pltpu.VMEM((ATTN_HEADS, ATTN_HEAD_DIM, tq), F32),
                        pltpu.VMEM((ATTN_HEADS, tk, tq), BF16)])
    return pl.pallas_call(
        functools.partial(_dsa_kernel, tq=tq, tk=tk, topk=topk),
        grid_spec=grid_spec,
        out_shape=jax.ShapeDtypeStruct((b, s, w), BF16),
        compiler_params=_params("parallel", "arbitrary"),
        name="dsa",
    )(pi, pj, proj, proj, proj, ikd, kn, proj, qnorm, jnp.asarray(eq, BF16), jnp.asarray(esel, BF16), smax)


def _cast_rows_to(src_ref, dst_ref, step=256):
    def body(r, c):
        r0 = pl.multiple_of(r * step, step)
        dst_ref[pl.ds(r0, step), :] = src_ref[0, pl.ds(r0, step), :].astype(BF16)
        return c
    lax.fori_loop(0, dst_ref.shape[0] // step, body, 0)


def _out_proj_kernel(yh_ref, ya_ref, w_ref, x_ref, o_ref, wb_ref):
    @pl.when(pl.program_id(0) == 0)
    def _():
        _cast_rows_to(w_ref, wb_ref)
    kh = yh_ref.shape[1]
    o_ref[...] = x_ref[...] + _dot(yh_ref[...], wb_ref[0:kh, :]) + _dot(ya_ref[...], wb_ref[kh:2 * kh, :])


def out_proj(yh, ya, w_out, layer, x, tm):
    m, kh = yh.shape
    _, k, n = w_out.shape
    return pl.pallas_call(
        _out_proj_kernel,
        grid=(m // tm,),
        in_specs=[pl.BlockSpec((tm, kh), lambda i: (i, 0)),
                  pl.BlockSpec((tm, kh), lambda i: (i, 0)),
                  pl.BlockSpec((1, k, n), lambda i: (layer, 0, 0), pipeline_mode=pl.Buffered(1)),
                  pl.BlockSpec((tm, n), lambda i: (i, 0))],
        out_specs=pl.BlockSpec((tm, n), lambda i: (i, 0)),
        out_shape=jax.ShapeDtypeStruct((m, n), F32),
        scratch_shapes=[pltpu.VMEM((k, n), BF16)],
        compiler_params=_params("arbitrary"),
        name="out_proj",
    )(yh, ya, w_out, x)


def _mem_kv_kernel(mem_ref, mg_ref, wk_ref, wv_ref, kg_ref, k_ref, v_ref):
    memn = _rms(mem_ref[0], mg_ref[...]).astype(BF16)
    k = _dot(memn, wk_ref[0].astype(BF16))
    kg = kg_ref[...]
    for h in range(CROSS_HEADS):
        sl = slice(h * CROSS_HEAD_DIM, (h + 1) * CROSS_HEAD_DIM)
        k_ref[0, :, sl] = _rms(k[:, sl], kg).astype(BF16)
    v_ref[0] = _dot(memn, wv_ref[0].astype(BF16)).astype(BF16)


def mem_kv(mem, mem_norm, wk, wv, layer, xk_norm):
    b, nm, d = mem.shape
    cw = wk.shape[2]
    full = lambda a: pl.BlockSpec(a.shape, lambda bi: (0,) * a.ndim)
    wspec = pl.BlockSpec((1, d, cw), lambda bi: (layer, 0, 0))
    return pl.pallas_call(
        _mem_kv_kernel,
        grid=(b,),
        in_specs=[pl.BlockSpec((1, nm, d), lambda bi: (bi, 0, 0)),
                  full(mem_norm), wspec, wspec, full(xk_norm)],
        out_specs=[pl.BlockSpec((1, nm, cw), lambda bi: (bi, 0, 0))] * 2,
        out_shape=[jax.ShapeDtypeStruct((b, nm, cw), BF16)] * 2,
        compiler_params=_params("parallel"),
        name="mem_kv",
    )(mem, mem_norm, wk, wv, xk_norm)


def _cross_kernel(h_ref, ng_ref, wq_ref, qg_ref, k_ref, v_ref, wo_ref, o_ref, hn_ref, oc_ref, wqb_ref, wob_ref):
    @pl.when((pl.program_id(0) == 0) & (pl.program_id(1) == 0))
    def _():
        _cast_rows_to(wq_ref, wqb_ref)
        _cast_rows_to(wo_ref, wob_ref)
    rows = h_ref.shape[1]
    dh = CROSS_HEAD_DIM
    step = 256
    gain = ng_ref[...]
    def body(r, c):
        r0 = pl.multiple_of(r * step, step)
        hn_ref[pl.ds(r0, step), :] = _rms(h_ref[0, pl.ds(r0, step), :], gain).astype(BF16)
        return c
    lax.fori_loop(0, rows // step, body, 0)
    q = _dot(hn_ref[...], wqb_ref[...])
    qg = qg_ref[...] * (dh ** -0.5)
    for h in range(CROSS_HEADS):
        sl = slice(h * dh, (h + 1) * dh)
        qn = _rms(q[:, sl], qg).astype(BF16)
        s = _dot_nt(qn, k_ref[0, :, sl])
        p = jnp.exp(s - jnp.max(s, axis=1, keepdims=True))
        l = jnp.sum(p, axis=1, keepdims=True)
        oc_ref[:, sl] = (_dot(p.astype(BF16), v_ref[0, :, sl]) / l).astype(BF16)
    o_ref[0] = h_ref[0] + _dot(oc_ref[...], wob_ref[...])


def cross_attention(h, norm_cross, wq, xq_norm, kx, vx, wo, layer, tm):
    b, s, d = h.shape
    nm, cw = kx.shape[1:]
    full = lambda a: pl.BlockSpec(a.shape, lambda bi, i: (0,) * a.ndim)
    resident = lambda a: pl.BlockSpec((1,) + a.shape[1:], lambda bi, i: (layer, 0, 0),
                                      pipeline_mode=pl.Buffered(1))
    return pl.pallas_call(
        _cross_kernel,
        grid=(b, s // tm),
        in_specs=[pl.BlockSpec((1, tm, d), lambda bi, i: (bi, i, 0)),
                  full(norm_cross), resident(wq), full(xq_norm),
                  pl.BlockSpec((1, nm, cw), lambda bi, i: (bi, 0, 0)),
                  pl.BlockSpec((1, nm, cw), lambda bi, i: (bi, 0, 0)),
                  resident(wo)],
        out_specs=pl.BlockSpec((1, tm, d), lambda bi, i: (bi, i, 0)),
        out_shape=jax.ShapeDtypeStruct((b, s, d), F32),
        scratch_shapes=[pltpu.VMEM((tm, d), BF16), pltpu.VMEM((tm, cw), BF16),
                        pltpu.VMEM(wq.shape[1:], BF16), pltpu.VMEM(wo.shape[1:], BF16)],
        compiler_params=_params("arbitrary", "arbitrary"),
        name="cross_attn",
    )(h, norm_cross, wq, xq_norm, kx, vx, wo)


def _mlp_kernel(h_ref, g_ref, wu_ref, wd_ref, o_ref, hn_ref):
    @pl.when(pl.program_id(1) == 0)
    def _():
        _norm_rows_to(h_ref, g_ref[...], hn_ref, h_ref.shape[0])
        o_ref[...] = h_ref[...]

    u = jnp.maximum(_dot(hn_ref[...], wu_ref[0].astype(BF16)), 0.0)
    o_ref[...] += _dot((u * u).astype(BF16), wd_ref[0].astype(BF16))


def mlp(h, gain, w_up, w_down, layer, tm, tf):
    m, d = h.shape
    f = w_up.shape[2]
    return pl.pallas_call(
        _mlp_kernel,
        grid=(m // tm, f // tf),
        in_specs=[pl.BlockSpec((tm, d), lambda i, j: (i, 0)),
                  pl.BlockSpec((1, d), lambda i, j: (0, 0)),
                  pl.BlockSpec((1, d, tf), lambda i, j: (layer, 0, j)),
                  pl.BlockSpec((1, tf, d), lambda i, j: (layer, j, 0))],
        out_specs=pl.BlockSpec((tm, d), lambda i, j: (i, 0), pipeline_mode=pl.Buffered(1)),
        out_shape=jax.ShapeDtypeStruct((m, d), F32),
        scratch_shapes=[pltpu.VMEM((tm, d), BF16)],
        compiler_params=_params("parallel", "arbitrary"),
        name="mlp",
    )(h, gain, w_up, w_down)


def _tile(n, pref):
    return pref if n % pref == 0 else n


def _tiles(n, s):
    return dict(
        proj_rows=_tile(n, 1024), proj_cols=768,
        hgrn_rows=_tile(s, 1024),
        dsa_block=_tile(s, 512),
        out_rows=_tile(n, 512),
        cross_rows=_tile(s, 1024),
        mlp_rows=_tile(n, 1024), mlp_ff=512)


def kernel(x, mem, norm_mix, w_in, hgrn_lb_logits, hgrn_onorm, attn_qnorm, attn_knorm, w_out,
           norm_cross, mem_norm, wq_x, wk_x, wv_x, wo_x, xq_norm, xk_norm,
           norm_mlp, w_up, w_down):
    b, s, d = x.shape
    n = b * s
    depth = w_in.shape[0]
    assert depth == 1
    l = 0
    in_width = w_in.shape[2]
    main_w = 8 * HGRN_HEADS * HGRN_KDIM
    assert in_width == main_w + IDX_DIM + IDX_HEADS
    pad_w = main_w + TAIL

    t = _tiles(n, s)
    x2 = x.reshape(n, d)

    proj = norm_matmul(x2, norm_mix[l:l + 1], w_in[l].T, pad_w,
                       t["proj_rows"], t["proj_cols"])
    proj3 = proj.reshape(b, s, pad_w)

    y_h = hgrn_group(proj3, hgrn_lb_logits, hgrn_onorm[l:l + 1], t["hgrn_rows"])
    kn, ikd = dsa_prep(proj3, attn_knorm[l:l + 1], t["dsa_block"])
    y_a = dsa_group(proj3, kn, ikd, attn_qnorm[l:l + 1], attn_knorm[l:l + 1], t["dsa_block"])

    h1 = out_proj(y_h.reshape(n, -1), y_a.reshape(n, -1), w_out, l, x2, t["out_rows"])

    kx, vx = mem_kv(mem, mem_norm[l:l + 1], wk_x, wv_x, l, xk_norm[l:l + 1])
    h2 = cross_attention(h1.reshape(b, s, d), norm_cross[l:l + 1], wq_x, xq_norm[l:l + 1],
                         kx, vx, wo_x, l, t["cross_rows"])

    h3 = mlp(h2.reshape(n, d), norm_mlp[l:l + 1], w_up, w_down, l, t["mlp_rows"], t["mlp_ff"])
    return h3.reshape(b, s, d)
```

```python
import functools

import numpy as np
import jax
import jax.numpy as jnp
from jax import lax
from jax.experimental import pallas as pl
from jax.experimental.pallas import tpu as pltpu

F32 = jnp.float32
BF16 = jnp.bfloat16
EPS = 1e-6

LANES = 128
HGRN_HEADS = 8
HGRN_KDIM = 128
HGRN_CHUNK = 64
ATTN_HEADS = 8
ATTN_HEAD_DIM = 128
IDX_HEADS = 16
IDX_DIM = 64
DSA_TOPK = 256
CROSS_HEADS = 4
CROSS_HEAD_DIM = 128
TAIL = 256
V7X_VMEM_BYTES = 64 * 1024 * 1024
VMEM_LIMIT = V7X_VMEM_BYTES * 7 // 8
NEG_INF = float("-inf")
INT_MIN = -(2 ** 31)
LOG2E = 1.4426950408889634
SOFTMAX_FIXED_SHIFT_MAX = 40.0


def _params(*sem):
    return pltpu.CompilerParams(dimension_semantics=sem, vmem_limit_bytes=VMEM_LIMIT)


def _rms(x, gain):
    return x * lax.rsqrt(jnp.mean(x * x, axis=-1, keepdims=True) + EPS) * gain


def _sigmoid(x):
    return 1.0 / (1.0 + jnp.exp(-x))


def _dot(a, b):
    return jnp.dot(a, b, preferred_element_type=F32)


def _dot_nt(a, b):
    return lax.dot_general(a, b, (((1,), (1,)), ((), ())), preferred_element_type=F32)


def _dot_tn(a, b):
    return lax.dot_general(a, b, (((0,), (0,)), ((), ())), preferred_element_type=F32)


def _norm_rows_to(x_ref, gain, dst_ref, rows):
    step = 256
    def body(r, c):
        r0 = pl.multiple_of(r * step, step)
        x = x_ref[pl.ds(r0, step), :]
        dst_ref[pl.ds(r0, step), :] = _rms(x, gain).astype(BF16)
        return c
    lax.fori_loop(0, rows // step, body, 0)


def _norm_matmul_kernel(x_ref, g_ref, wt_ref, o_ref, xn_ref, wb_ref, *, valid_cols):
    j = pl.program_id(1)
    r = pl.program_id(2)
    tn = o_ref.shape[1]

    @pl.when(j == 0)
    def _():
        _norm_rows_to(x_ref, g_ref[...], xn_ref.at[r], x_ref.shape[0])

    @pl.when(r == 0)
    def _():
        wb_ref[...] = wt_ref[...].astype(BF16)

    res = _dot_nt(xn_ref[r], wb_ref[...])

    @pl.when((j + 1) * tn <= valid_cols)
    def _():
        o_ref[...] = res.astype(o_ref.dtype)

    @pl.when((j + 1) * tn > valid_cols)
    def _():
        col = j * tn + lax.broadcasted_iota(jnp.int32, res.shape, 1)
        o_ref[...] = jnp.where(col < valid_cols, res, 0.0).astype(o_ref.dtype)


def norm_matmul(x, gain, wt, n, tm, tn, group):
    m, k = x.shape
    x_map = lambda g, j, r: (g * group + jnp.where(j == 0, r, group - 1), 0)
    return pl.pallas_call(
        functools.partial(_norm_matmul_kernel, valid_cols=wt.shape[0]),
        grid=(m // (group * tm), n // tn, group),
        in_specs=[pl.BlockSpec((tm, k), x_map),
                  pl.BlockSpec((1, k), lambda g, j, r: (0, 0)),
                  pl.BlockSpec((tn, k), lambda g, j, r: (j, 0))],
        out_specs=pl.BlockSpec((tm, tn), lambda g, j, r: (g * group + r, j)),
        out_shape=jax.ShapeDtypeStruct((m, n), BF16),
        scratch_shapes=[pltpu.VMEM((group, tm, k), BF16), pltpu.VMEM((tn, k), BF16)],
        compiler_params=_params("arbitrary", "arbitrary", "arbitrary"),
        name="in_proj",
    )(x, gain, wt)


def _hgrn_kernel(q_ref, f_ref, i_ref, g_ref, lbl_ref, on_ref, tril_ref, o_ref,
                 state_ref, a_ref, qi_ref, ks_ref, el_ref, qt_ref, kt_ref, sc_ref, scb_ref, *, nchunks):
    c = HGRN_CHUNK
    dk = HGRN_KDIM
    pw = 2 * dk
    npairs = HGRN_HEADS // 2
    nlev = 6

    @pl.when(pl.program_id(1) == 0)
    def _():
        state_ref[...] = jnp.zeros_like(state_ref)
        kt_ref[...] = jnp.zeros_like(kt_ref)

    lbl = lbl_ref[...]
    e = jnp.exp(lbl - jnp.max(lbl, axis=0, keepdims=True))
    lb = e[0:1] / jnp.sum(e, axis=0, keepdims=True)
    onorm = on_ref[...]
    tril = tril_ref[...]

    row = lax.broadcasted_iota(jnp.int32, (c, 2 * c), 0)
    lane = lax.broadcasted_iota(jnp.int32, (c, 2 * c), 1)
    col = lane & (c - 1)
    first = lane < c
    eye = row == col
    level_masks = []
    lg = nlev - 1
    while lg >= 0:
        level_masks.append(((row >> (lg + 1)) == (col >> (lg + 1)))
                           & (((row >> lg) & 1) == 1) & (((col >> lg) & 1) == 0))
        lg -= 1
    lane_p = lax.broadcasted_iota(jnp.int32, (c, pw), 1)
    odd_row = (lax.broadcasted_iota(jnp.int32, (c, pw), 0) & 1) == 1
    sub8 = lax.broadcasted_iota(jnp.int32, (8, pw), 0)
    zeros_st = jnp.zeros((dk, dk), BF16)

    def chunk_body(ci, carry):
        r0 = pl.multiple_of(ci * c, c)

        def _decay_factors():
            hq = q_ref[0, pl.ds(r0, c), :].astype(F32)
            hf = f_ref[0, pl.ds(r0, c), :].astype(F32)
            qf = hq * _sigmoid(hq) * (dk ** -0.5)
            f = lb + (1.0 - lb) * _sigmoid(hf)
            logf = jnp.log(f) * LOG2E
            kk = 1.0 - f
            g0 = logf.astype(BF16)
            g1 = (logf - g0.astype(F32)).astype(BF16)
            a_ref[...] = _dot(tril, g0) + _dot(tril, g1)
            for p in range(npairs):
                sl = slice(p * pw, (p + 1) * pw)
                a = a_ref[:, sl]
                row = lambda r, n: jnp.broadcast_to(a_ref[r:r + 1, sl], (n, pw))
                e0 = jnp.exp2(a)
                qp = qf[:, sl]
                kp = kk[:, sl]
                qi_ref[p] = (qp * e0).astype(BF16)
                ks_ref[p] = (kp * jnp.exp2(row(c - 1, c) - a)).astype(BF16)
                el_ref[p] = e0[c - 1:c]
                for l in range(nlev):
                    h = c >> (l + 1)
                    if h >= 4:
                        ref = jnp.concatenate([row(m * 2 * h + h - 1, 2 * h) for m in range(c // (2 * h))], axis=0)
                    elif h == 2:
                        ref = jnp.concatenate([jnp.where(sub8 < 4, row(8 * m + 1, 8), row(8 * m + 5, 8))
                                               for m in range(c // 8)], axis=0)
                    else:
                        ref = jnp.where(odd_row, pltpu.roll(a, 1, 0), a)
                    el = jnp.exp2(-jnp.abs(a - ref))
                    qt_ref[p, l] = (qp * el).astype(BF16)
                    kl = (kp * el).astype(BF16)
                    kt_ref[p, l, 0:c, 0:dk] = kl[:, 0:dk]
                    kt_ref[p, l, c:2 * c, dk:pw] = kl[:, dk:pw]
                qk = qp * kp
                diag = jnp.where(first, jnp.sum(qk[:, 0:dk], axis=1, keepdims=True),
                                 jnp.sum(qk[:, dk:pw], axis=1, keepdims=True))
                sc_ref[p] = jnp.where(eye, diag, 0.0)

        def _intra_chunk_scores():
            for p in range(npairs):
                sc = sc_ref[p]
                for l, msk in enumerate(level_masks):
                    sc = sc + jnp.where(msk, _dot_nt(qt_ref[p, l], kt_ref[p, l]), 0.0)
                scb_ref[p] = sc.astype(BF16)

        def _outputs_and_state():
            for p in range(npairs):
                sl = slice(p * pw, (p + 1) * pw)
                vp = i_ref[0, pl.ds(r0, c), sl]
                st_a = state_ref[2 * p]
                st_b = state_ref[2 * p + 1]
                st_bd = jnp.concatenate(
                    [jnp.concatenate([st_a.astype(BF16), zeros_st], axis=1),
                     jnp.concatenate([zeros_st, st_b.astype(BF16)], axis=1)], axis=0)
                v_bd = jnp.concatenate([jnp.where(lane_p < dk, vp, jnp.zeros_like(vp)),
                                        jnp.where(lane_p >= dk, vp, jnp.zeros_like(vp))], axis=0)
                o = _dot_nt(qi_ref[p], st_bd) + _dot(scb_ref[p], v_bd)
                upd = _dot_tn(vp, ks_ref[p])
                el = el_ref[p]
                state_ref[2 * p] = st_a * el[:, 0:dk] + upd[0:dk, 0:dk]
                state_ref[2 * p + 1] = st_b * el[:, dk:pw] + upd[dk:pw, dk:pw]
                gate = g_ref[0, pl.ds(r0, c), sl].astype(F32)
                y = jnp.concatenate([_rms(o[:, 0:dk], onorm), _rms(o[:, dk:pw], onorm)], axis=1)
                o_ref[0, pl.ds(r0, c), sl] = (y * (gate * _sigmoid(gate))).astype(o_ref.dtype)

        _decay_factors()
        _intra_chunk_scores()
        _outputs_and_state()
        return carry

    lax.fori_loop(0, nchunks, chunk_body, 0, unroll=4)


def hgrn_group(proj, lb_logits, onorm, t_blk):
    b, s, _ = proj.shape
    w = HGRN_HEADS * HGRN_KDIM
    tril = jnp.asarray(np.tril(np.ones((HGRN_CHUNK, HGRN_CHUNK), np.float32)), BF16)
    c, pw, npairs, nlev = HGRN_CHUNK, 2 * HGRN_KDIM, HGRN_HEADS // 2, 6
    col = lambda cb: pl.BlockSpec((1, t_blk, w), lambda bi, ti, cb=cb: (bi, ti, cb))
    return pl.pallas_call(
        functools.partial(_hgrn_kernel, nchunks=t_blk // HGRN_CHUNK),
        grid=(b, s // t_blk),
        in_specs=[col(0), col(1), col(2), col(3),
                  pl.BlockSpec(lb_logits.shape, lambda bi, ti: (0, 0)),
                  pl.BlockSpec((1, HGRN_KDIM), lambda bi, ti: (0, 0)),
                  pl.BlockSpec(tril.shape, lambda bi, ti: (0, 0))],
        out_specs=pl.BlockSpec((1, t_blk, w), lambda bi, ti: (bi, ti, 0)),
        out_shape=jax.ShapeDtypeStruct((b, s, w), BF16),
        scratch_shapes=[pltpu.VMEM((HGRN_HEADS, HGRN_KDIM, HGRN_KDIM), F32),
                        pltpu.VMEM((c, w), F32),
                        pltpu.VMEM((npairs, c, pw), BF16),
                        pltpu.VMEM((npairs, c, pw), BF16),
                        pltpu.VMEM((npairs, 1, pw), F32),
                        pltpu.VMEM((npairs, nlev, c, pw), BF16),
                        pltpu.VMEM((npairs, nlev, 2 * c, pw), BF16),
                        pltpu.VMEM((npairs, c, 2 * c), F32),
                        pltpu.VMEM((npairs, c, 2 * c), BF16)],
        compiler_params=_params("parallel", "arbitrary"),
        name="hgrn2",
    )(proj, proj, proj, proj, lb_logits, onorm, tril)


def _dsa_prep_kernel(k_ref, tail_ref, kg_ref, dup_ref, kn_ref, ikd_ref):
    kg = kg_ref[...]
    for h in range(ATTN_HEADS):
        sl = slice(h * ATTN_HEAD_DIM, (h + 1) * ATTN_HEAD_DIM)
        kn_ref[0, :, sl] = _rms(k_ref[0, :, sl].astype(F32), kg).astype(BF16)
    ikd_ref[0] = _dot(tail_ref[0], dup_ref[...]).astype(BF16)


def dsa_prep(proj, knorm, tm):
    b, s, _ = proj.shape
    w = ATTN_HEADS * ATTN_HEAD_DIM
    dup = np.zeros((TAIL, LANES), np.float32)
    dup[np.arange(IDX_DIM), np.arange(IDX_DIM)] = 1.0
    dup[np.arange(IDX_DIM), np.arange(IDX_DIM) + IDX_DIM] = 1.0
    return pl.pallas_call(
        _dsa_prep_kernel,
        grid=(b, s // tm),
        in_specs=[pl.BlockSpec((1, tm, w), lambda bi, i: (bi, i, 5)),
                  pl.BlockSpec((1, tm, TAIL), lambda bi, i: (bi, i, 8 * w // TAIL)),
                  pl.BlockSpec((1, ATTN_HEAD_DIM), lambda bi, i: (0, 0)),
                  pl.BlockSpec((TAIL, LANES), lambda bi, i: (0, 0))],
        out_specs=[pl.BlockSpec((1, tm, w), lambda bi, i: (bi, i, 0)),
                   pl.BlockSpec((1, tm, LANES), lambda bi, i: (bi, i, 0))],
        out_shape=[jax.ShapeDtypeStruct((b, s, w), BF16),
                   jax.ShapeDtypeStruct((b, s, LANES), BF16)],
        compiler_params=_params("parallel", "parallel"),
        name="dsa_prep",
    )(proj, proj, knorm, jnp.asarray(dup, BF16))


def _dsa_kernel(pi_ref, pj_ref, aq_ref, iq_ref, tail_ref, ikd_ref, kn_ref, v_ref, qg_ref, eq_ref, esel_ref,
                smax_ref, o_ref, keys_ref, thr_ref, qn_ref, qw_ref, lo_ref, hi_ref, m_ref, l_ref, acc_ref, p_ref,
                *, tq, tk, topk):
    step = pl.program_id(1)
    i = pi_ref[step]
    j = pj_ref[step]
    dh = ATTN_HEAD_DIM

    def causal(jj):
        s_pos = jj * tk + lax.broadcasted_iota(jnp.int32, (tk, tq), 0)
        t_pos = i * tq + lax.broadcasted_iota(jnp.int32, (tk, tq), 1)
        return s_pos <= t_pos

    @pl.when(j == 0)
    def _index_and_select():
        qg = qg_ref[...] * (dh ** -0.5 * LOG2E)
        for h in range(ATTN_HEADS):
            sl = slice(h * dh, (h + 1) * dh)
            qn_ref[:, sl] = _rms(aq_ref[0, :, sl].astype(F32), qg).astype(BF16)
        tail = tail_ref[0]
        ww = _dot(tail, eq_ref[...])
        qw = (iq_ref[0].astype(F32) * ww * (IDX_HEADS ** -0.5 * IDX_DIM ** -0.5)).astype(BF16)
        lane = lax.broadcasted_iota(jnp.int32, (tq, LANES), 1)
        zero = jnp.zeros((tq, LANES), BF16)
        for p in range(IDX_HEADS // 2):
            pair = qw[:, p * LANES:(p + 1) * LANES]
            qw_ref[2 * p] = jnp.where(lane < IDX_DIM, pair, zero)
            qw_ref[2 * p + 1] = jnp.where(lane >= IDX_DIM, pair, zero)
        w_t = _dot_nt(esel_ref[...], tail)
        lo_ref[...] = jnp.where(w_t > 0.0, 0.0, NEG_INF)
        hi_ref[...] = jnp.where(w_t > 0.0, jnp.inf, 0.0)

        def score_tile(jj, c):
            r0 = pl.multiple_of(jj * tk, tk)
            ik = ikd_ref[0, pl.ds(r0, tk), :]
            comb = jnp.zeros((tk, tq), F32)
            for h in range(IDX_HEADS):
                x = _dot_nt(ik, qw_ref[h])
                comb = comb + jnp.minimum(jnp.maximum(x, lo_ref[h:h + 1, :]), hi_ref[h:h + 1, :])
            keys_ref[jj] = jnp.where(causal(jj), comb, NEG_INF)
            return c
        lax.fori_loop(0, i + 1, score_tile, 0)

        def as_float(t):
            return pltpu.bitcast(jnp.where(t < 0, t ^ 0x7FFFFFFF, t), F32)
        def count_ge(tf):
            def body(jj, acc):
                accs = [acc, jnp.zeros_like(acc), jnp.zeros_like(acc), jnp.zeros_like(acc)]
                for r0 in range(0, tk, 8):
                    a = accs[(r0 // 8) % 4]
                    accs[(r0 // 8) % 4] = jnp.where(keys_ref[jj, r0:r0 + 8, :] >= tf, a + 1, a)
                return (accs[0] + accs[1]) + (accs[2] + accs[3])
            acc = lax.fori_loop(0, i + 1, body, jnp.zeros((8, tq), jnp.int32))
            return jnp.sum(acc.astype(F32), axis=0, keepdims=True)
        def unresolved(carry):
            bi, _, cnt = carry
            return jnp.logical_and(bi < 32, jnp.max(jnp.abs(cnt - topk)) > 0.0)
        def bit_steps(carry):
            bi, t, cnt = carry
            nbits = jnp.where(bi == 0, 24, 4)
            def one_bit(k, tc_cnt):
                t, cnt = tc_cnt
                tc = t + (jnp.int32(1) << (31 - bi - k))
                c = count_ge(as_float(tc))
                take = c >= topk
                return jnp.where(take, tc, t), jnp.where(take, c, cnt)
            t, cnt = lax.fori_loop(0, nbits, one_bit, (t, cnt))
            return bi + nbits, t, cnt
        n_all = ((i + 1) * tk).astype(F32)
        _, t_fin, _ = lax.while_loop(unresolved, bit_steps,
                                     (jnp.int32(0), jnp.full((1, tq), INT_MIN, jnp.int32),
                                      jnp.full((1, tq), n_all, F32)))
        t_float = as_float(t_fin)
        thr_ref[...] = jnp.where(t_float != t_float, NEG_INF, t_float)

        m_ref[...] = jnp.full_like(m_ref, NEG_INF)
        l_ref[...] = jnp.zeros_like(l_ref)
        acc_ref[...] = jnp.zeros_like(acc_ref)

    sel = (keys_ref[j] >= thr_ref[...]) & causal(j)
    smax = smax_ref[0]
    fixed_shift = smax <= SOFTMAX_FIXED_SHIFT_MAX

    @pl.when(fixed_shift)
    def _attend_fixed_shift():
        bias = jnp.where(sel, -smax * LOG2E, NEG_INF)
        for h in range(ATTN_HEADS):
            sl = slice(h * dh, (h + 1) * dh)
            p = jnp.exp2(_dot_nt(kn_ref[0, :, sl], qn_ref[:, sl]) + bias)
            parts = [p[r * 8:(r + 1) * 8] for r in range(tk // 8)]
            while len(parts) > 1:
                parts = [a + b for a, b in zip(parts[0::2], parts[1::2])]
            l_ref[h] += parts[0]
            p_ref[h] = p.astype(BF16)

    @pl.when(jnp.logical_and(fixed_shift, j <= i))
    def _weighted_values():
        for h in range(ATTN_HEADS):
            sl = slice(h * dh, (h + 1) * dh)
            acc_ref[h] += _dot_tn(v_ref[0, :, sl], p_ref[h])

    @pl.when(jnp.logical_not(fixed_shift))
    def _attend_running_max():
        bias = jnp.where(sel, 0.0, NEG_INF)
        m_all = m_ref[...]
        m_rows = []
        for h in range(ATTN_HEADS):
            sl = slice(h * dh, (h + 1) * dh)
            s = _dot_nt(kn_ref[0, :, sl], qn_ref[:, sl]) + bias
            m_old = m_all[h:h + 1, :]
            m_new = jnp.maximum(m_old, jnp.max(s, axis=0, keepdims=True))
            m_safe = jnp.where(m_new == NEG_INF, 0.0, m_new)
            alpha = jnp.exp2(m_old - m_safe)
            p = jnp.exp2(s - m_safe)
            l_ref[h, 0:1, :] = alpha * l_ref[h, 0:1, :] + jnp.sum(p, axis=0, keepdims=True)
            acc_ref[h] = alpha * acc_ref[h] + _dot_tn(v_ref[0, :, sl], p.astype(BF16))
            m_rows.append(m_new)
        m_ref[...] = jnp.concatenate(m_rows, axis=0)

    @pl.when(j == i)
    def _finish():
        for h in range(ATTN_HEADS):
            sl = slice(h * dh, (h + 1) * dh)
            l = jnp.sum(l_ref[h], axis=0, keepdims=True)
            o_ref[0, :, sl] = (acc_ref[h] / l).T.astype(o_ref.dtype)


def dsa_group(proj, kn, ikd, qnorm, knorm, tq):
    b, s, _ = proj.shape
    tk = tq
    smax = (ATTN_HEAD_DIM ** 0.5 * jnp.max(jnp.abs(qnorm)) * jnp.max(jnp.abs(knorm))).reshape(1).astype(F32)
    nq = s // tq
    w = ATTN_HEADS * ATTN_HEAD_DIM
    topk = min(DSA_TOPK, s // 4)
    eq = np.zeros((TAIL, IDX_HEADS * IDX_DIM), np.float32)
    esel = np.zeros((IDX_HEADS, TAIL), np.float32)
    for h in range(IDX_HEADS):
        eq[IDX_DIM + h, h * IDX_DIM:(h + 1) * IDX_DIM] = 1.0
        esel[h, IDX_DIM + h] = 1.0
    pairs = [(i, j) for i in range(nq) for j in range(i + 1)]
    pi = jnp.asarray([p[0] for p in pairs], jnp.int32)
    pj = jnp.asarray([p[1] for p in pairs], jnp.int32)
    qblk = lambda cb: pl.BlockSpec((1, tq, w), lambda bi, st, pi, pj, cb=cb: (bi, pi[st], cb))
    const = lambda a: pl.BlockSpec(a.shape, lambda bi, st, pi, pj: (0,) * a.ndim)
    grid_spec = pltpu.PrefetchScalarGridSpec(
        num_scalar_prefetch=2,
        grid=(b, len(pairs)),
        in_specs=[qblk(4),
                  qblk(7),
                  pl.BlockSpec((1, tq, TAIL), lambda bi, st, pi, pj: (bi, pi[st], 8 * w // TAIL)),
                  pl.BlockSpec((1, s, LANES), lambda bi, st, pi, pj: (bi, 0, 0)),
                  pl.BlockSpec((1, tk, w), lambda bi, st, pi, pj: (bi, pj[st], 0)),
                  pl.BlockSpec((1, tk, w), lambda bi, st, pi, pj: (bi, pj[st], 6)),
                  const(qnorm), const(eq), const(esel),
                  pl.BlockSpec(memory_space=pltpu.SMEM)],
        out_specs=pl.BlockSpec((1, tq, w), lambda bi, st, pi, pj: (bi, pi[st], 0)),
        scratch_shapes=[pltpu.VMEM((nq, tk, tq), F32),
                        pltpu.VMEM((1, tq), F32),
                        pltpu.VMEM((tq, w), BF16),
                        pltpu.VMEM((IDX_HEADS, tq, LANES), BF16),
                        pltpu.VMEM((IDX_HEADS, tq), F32),
                        pltpu.VMEM((IDX_HEADS, tq), F32),
                        pltpu.VMEM((ATTN_HEADS, tq), F32),
                        pltpu.VMEM((ATTN_HEADS, 8, tq), F32),
                        pltpu.VMEM((ATTN_HEADS, ATTN_HEAD_DIM, tq), F32),
                        pltpu.VMEM((ATTN_HEADS, tk, tq), BF16)])
    return pl.pallas_call(
        functools.partial(_dsa_kernel, tq=tq, tk=tk, topk=topk),
        grid_spec=grid_spec,
        out_shape=jax.ShapeDtypeStruct((b, s, w), BF16),
        compiler_params=_params("parallel", "arbitrary"),
        name="dsa",
    )(pi, pj, proj, proj, proj, ikd, kn, proj, qnorm, jnp.asarray(eq, BF16), jnp.asarray(esel, BF16), smax)


def _cast_rows_to(src_ref, dst_ref, step=256):
    def body(r, c):
        r0 = pl.multiple_of(r * step, step)
        dst_ref[pl.ds(r0, step), :] = src_ref[0, pl.ds(r0, step), :].astype(BF16)
        return c
    lax.fori_loop(0, dst_ref.shape[0] // step, body, 0)


def _out_proj_kernel(yh_ref, ya_ref, w_ref, x_ref, o_ref, wb_ref):
    @pl.when(pl.program_id(0) == 0)
    def _():
        _cast_rows_to(w_ref, wb_ref)
    kh = yh_ref.shape[1]
    o_ref[...] = x_ref[...] + _dot(yh_ref[...], wb_ref[0:kh, :]) + _dot(ya_ref[...], wb_ref[kh:2 * kh, :])


def out_proj(yh, ya, w_out, layer, x, tm):
    m, kh = yh.shape
    _, k, n = w_out.shape
    return pl.pallas_call(
        _out_proj_kernel,
        grid=(m // tm,),
        in_specs=[pl.BlockSpec((tm, kh), lambda i: (i, 0)),
                  pl.BlockSpec((tm, kh), lambda i: (i, 0)),
                  pl.BlockSpec((1, k, n), lambda i: (layer, 0, 0), pipeline_mode=pl.Buffered(1)),
                  pl.BlockSpec((tm, n), lambda i: (i, 0))],
        out_specs=pl.BlockSpec((tm, n), lambda i: (i, 0)),
        out_shape=jax.ShapeDtypeStruct((m, n), F32),
        scratch_shapes=[pltpu.VMEM((k, n), BF16)],
        compiler_params=_params("arbitrary"),
        name="out_proj",
    )(yh, ya, w_out, x)


def _mem_kv_kernel(mem_ref, mg_ref, wk_ref, wv_ref, kg_ref, k_ref, v_ref):
    memn = _rms(mem_ref[0], mg_ref[...]).astype(BF16)
    k = _dot(memn, wk_ref[0].astype(BF16))
    kg = kg_ref[...]
    for h in range(CROSS_HEADS):
        sl = slice(h * CROSS_HEAD_DIM, (h + 1) * CROSS_HEAD_DIM)
        k_ref[0, :, sl] = _rms(k[:, sl], kg).astype(BF16)
    v_ref[0] = _dot(memn, wv_ref[0].astype(BF16)).astype(BF16)


def mem_kv(mem, mem_norm, wk, wv, layer, xk_norm):
    b, nm, d = mem.shape
    cw = wk.shape[2]
    full = lambda a: pl.BlockSpec(a.shape, lambda bi: (0,) * a.ndim)
    wspec = pl.BlockSpec((1, d, cw), lambda bi: (layer, 0, 0))
    return pl.pallas_call(
        _mem_kv_kernel,
        grid=(b,),
        in_specs=[pl.BlockSpec((1, nm, d), lambda bi: (bi, 0, 0)),
                  full(mem_norm), wspec, wspec, full(xk_norm)],
        out_specs=[pl.BlockSpec((1, nm, cw), lambda bi: (bi, 0, 0))] * 2,
        out_shape=[jax.ShapeDtypeStruct((b, nm, cw), BF16)] * 2,
        compiler_params=_params("parallel"),
        name="mem_kv",
    )(mem, mem_norm, wk, wv, xk_norm)


def _cross_kernel(h_ref, ng_ref, wq_ref, qg_ref, k_ref, v_ref, wo_ref, o_ref, hn_ref, oc_ref, wqb_ref, wob_ref):
    @pl.when((pl.program_id(0) == 0) & (pl.program_id(1) == 0))
    def _():
        _cast_rows_to(wq_ref, wqb_ref)
        _cast_rows_to(wo_ref, wob_ref)
    rows = h_ref.shape[1]
    dh = CROSS_HEAD_DIM
    step = 256
    gain = ng_ref[...]
    def body(r, c):
        r0 = pl.multiple_of(r * step, step)
        hn_ref[pl.ds(r0, step), :] = _rms(h_ref[0, pl.ds(r0, step), :], gain).astype(BF16)
        return c
    lax.fori_loop(0, rows // step, body, 0)
    q = _dot(hn_ref[...], wqb_ref[...])
    qg = qg_ref[...] * (dh ** -0.5)
    for h in range(CROSS_HEADS):
        sl = slice(h * dh, (h + 1) * dh)
        qn = _rms(q[:, sl], qg).astype(BF16)
        s = _dot_nt(qn, k_ref[0, :, sl])
        p = jnp.exp(s - jnp.max(s, axis=1, keepdims=True))
        l = jnp.sum(p, axis=1, keepdims=True)
        oc_ref[:, sl] = (_dot(p.astype(BF16), v_ref[0, :, sl]) / l).astype(BF16)
    o_ref[0] = h_ref[0] + _dot(oc_ref[...], wob_ref[...])


def cross_attention(h, norm_cross, wq, xq_norm, kx, vx, wo, layer, tm):
    b, s, d = h.shape
    nm, cw = kx.shape[1:]
    full = lambda a: pl.BlockSpec(a.shape, lambda bi, i: (0,) * a.ndim)
    resident = lambda a: pl.BlockSpec((1,) + a.shape[1:], lambda bi, i: (layer, 0, 0),
                                      pipeline_mode=pl.Buffered(1))
    return pl.pallas_call(
        _cross_kernel,
        grid=(b, s // tm),
        in_specs=[pl.BlockSpec((1, tm, d), lambda bi, i: (bi, i, 0)),
                  full(norm_cross), resident(wq), full(xq_norm),
                  pl.BlockSpec((1, nm, cw), lambda bi, i: (bi, 0, 0)),
                  pl.BlockSpec((1, nm, cw), lambda bi, i: (bi, 0, 0)),
                  resident(wo)],
        out_specs=pl.BlockSpec((1, tm, d), lambda bi, i: (bi, i, 0)),
        out_shape=jax.ShapeDtypeStruct((b, s, d), F32),
        scratch_shapes=[pltpu.VMEM((tm, d), BF16), pltpu.VMEM((tm, cw), BF16),
                        pltpu.VMEM(wq.shape[1:], BF16), pltpu.VMEM(wo.shape[1:], BF16)],
        compiler_params=_params("arbitrary", "arbitrary"),
        name="cross_attn",
    )(h, norm_cross, wq, xq_norm, kx, vx, wo)


def _mlp_kernel(h_ref, g_ref, wu_ref, wd_ref, o_ref, hn_ref):
    @pl.when(pl.program_id(1) == 0)
    def _():
        _norm_rows_to(h_ref, g_ref[...], hn_ref, h_ref.shape[0])
        o_ref[...] = h_ref[...]

    u = jnp.maximum(_dot(hn_ref[...], wu_ref[0].astype(BF16)), 0.0)
    o_ref[...] += _dot((u * u).astype(BF16), wd_ref[0].astype(BF16))


def mlp(h, gain, w_up, w_down, layer, tm, tf):
    m, d = h.shape
    f = w_up.shape[2]
    return pl.pallas_call(
        _mlp_kernel,
        grid=(m // tm, f // tf),
        in_specs=[pl.BlockSpec((tm, d), lambda i, j: (i, 0)),
                  pl.BlockSpec((1, d), lambda i, j: (0, 0)),
                  pl.BlockSpec((1, d, tf), lambda i, j: (layer, 0, j)),
                  pl.BlockSpec((1, tf, d), lambda i, j: (layer, j, 0))],
        out_specs=pl.BlockSpec((tm, d), lambda i, j: (i, 0), pipeline_mode=pl.Buffered(1)),
        out_shape=jax.ShapeDtypeStruct((m, d), F32),
        scratch_shapes=[pltpu.VMEM((tm, d), BF16)],
        compiler_params=_params("parallel", "arbitrary"),
        name="mlp",
    )(h, gain, w_up, w_down)


def _tile(n, pref):
    return pref if n % pref == 0 else n


def _tiles(n, s):
    return dict(
        proj_rows=_tile(n, 1024), proj_cols=768,
        proj_group=4 if n % 4096 == 0 else 1,
        hgrn_rows=_tile(s, 1024),
        dsa_block=_tile(s, 512),
        out_rows=_tile(n, 512),
        cross_rows=_tile(s, 1024),
        mlp_rows=_tile(n, 1024), mlp_ff=512)


def kernel(x, mem, norm_mix, w_in, hgrn_lb_logits, hgrn_onorm, attn_qnorm, attn_knorm, w_out,
           norm_cross, mem_norm, wq_x, wk_x, wv_x, wo_x, xq_norm, xk_norm,
           norm_mlp, w_up, w_down):
    b, s, d = x.shape
    n = b * s
    depth = w_in.shape[0]
    assert depth == 1
    l = 0
    in_width = w_in.shape[2]
    main_w = 8 * HGRN_HEADS * HGRN_KDIM
    assert in_width == main_w + IDX_DIM + IDX_HEADS
    pad_w = main_w + TAIL

    t = _tiles(n, s)
    x2 = x.reshape(n, d)

    proj = norm_matmul(x2, norm_mix[l:l + 1], w_in[l].T, pad_w,
                       t["proj_rows"], t["proj_cols"], t["proj_group"])
    proj3 = proj.reshape(b, s, pad_w)

    y_h = hgrn_group(proj3, hgrn_lb_logits, hgrn_onorm[l:l + 1], t["hgrn_rows"])
    kn, ikd = dsa_prep(proj3, attn_knorm[l:l + 1], t["dsa_block"])
    y_a = dsa_group(proj3, kn, ikd, attn_qnorm[l:l + 1], attn_knorm[l:l + 1], t["dsa_block"])

    h1 = out_proj(y_h.reshape(n, -1), y_a.reshape(n, -1), w_out, l, x2, t["out_rows"])

    kx, vx = mem_kv(mem, mem_norm[l:l + 1], wk_x, wv_x, l, xk_norm[l:l + 1])
    h2 = cross_attention(h1.reshape(b, s, d), norm_cross[l:l + 1], wq_x, xq_norm[l:l + 1],
                         kx, vx, wo_x, l, t["cross_rows"])

    h3 = mlp(h2.reshape(n, d), norm_mlp[l:l + 1], w_up, w_down, l, t["mlp_rows"], t["mlp_ff"])
    return h3.reshape(b, s, d)
```

```python
import functools

import numpy as np
import jax
import jax.numpy as jnp
from jax import lax
from jax.experimental import pallas as pl
from jax.experimental.pallas import tpu as pltpu

F32 = jnp.float32
BF16 = jnp.bfloat16
EPS = 1e-6

LANES = 128
HGRN_HEADS = 8
HGRN_KDIM = 128
HGRN_CHUNK = 64
ATTN_HEADS = 8
ATTN_HEAD_DIM = 128
IDX_HEADS = 16
IDX_DIM = 64
DSA_TOPK = 256
CROSS_HEADS = 4
CROSS_HEAD_DIM = 128
TAIL = 256
V7X_VMEM_BYTES = 64 * 1024 * 1024
VMEM_LIMIT = V7X_VMEM_BYTES * 7 // 8
NEG_INF = float("-inf")
INT_MIN = -(2 ** 31)
LOG2E = 1.4426950408889634
SOFTMAX_FIXED_SHIFT_MAX = 40.0


def _params(*sem):
    return pltpu.CompilerParams(dimension_semantics=sem, vmem_limit_bytes=VMEM_LIMIT)


def _rms(x, gain):
    return x * lax.rsqrt(jnp.mean(x * x, axis=-1, keepdims=True) + EPS) * gain


def _sigmoid(x):
    return 1.0 / (1.0 + jnp.exp(-x))


def _dot(a, b):
    return jnp.dot(a, b, preferred_element_type=F32)


def _dot_nt(a, b):
    return lax.dot_general(a, b, (((1,), (1,)), ((), ())), preferred_element_type=F32)


def _dot_tn(a, b):
    return lax.dot_general(a, b, (((0,), (0,)), ((), ())), preferred_element_type=F32)


def _norm_rows_to(x_ref, gain, dst_ref, rows):
    step = 256
    def body(r, c):
        r0 = pl.multiple_of(r * step, step)
        x = x_ref[pl.ds(r0, step), :]
        dst_ref[pl.ds(r0, step), :] = _rms(x, gain).astype(BF16)
        return c
    lax.fori_loop(0, rows // step, body, 0)


def _norm_matmul_kernel(kind_ref, g_tbl, r_tbl, j_tbl, x_ref, gain_ref, wt_ref, o_ref, xn_ref, *, valid_cols):
    s = pl.program_id(0)
    tm = x_ref.shape[0]
    tn = o_ref.shape[1]

    @pl.when(kind_ref[s] == 0)
    def _():
        r0 = pl.multiple_of(r_tbl[s] * tm, tm)
        _norm_rows_to(x_ref, gain_ref[...], xn_ref.at[pl.ds(r0, tm)], tm)

    @pl.when(kind_ref[s] == 1)
    def _():
        res = _dot_nt(xn_ref[...], wt_ref[...].astype(BF16))
        col = j_tbl[s] * tn + lax.broadcasted_iota(jnp.int32, (1, tn), 1)
        o_ref[...] = jnp.where(col < valid_cols, res, 0.0).astype(o_ref.dtype)


def norm_matmul(x, gain, wt, n, tm, tn, group):
    m, k = x.shape
    ngroups, ntiles = m // (group * tm), n // tn
    kind, g_tbl, r_tbl, j_tbl = [], [], [], []
    for g in range(ngroups):
        for r in range(group):
            kind.append(0); g_tbl.append(g); r_tbl.append(r); j_tbl.append(0)
        for j in range(ntiles):
            kind.append(1); g_tbl.append(g); r_tbl.append(group - 1); j_tbl.append(j)
    tables = [jnp.asarray(t, jnp.int32) for t in (kind, g_tbl, r_tbl, j_tbl)]
    grid_spec = pltpu.PrefetchScalarGridSpec(
        num_scalar_prefetch=4,
        grid=(len(kind),),
        in_specs=[pl.BlockSpec((tm, k), lambda s, kd, gt, rt, jt: (gt[s] * group + rt[s], 0)),
                  pl.BlockSpec((1, k), lambda s, kd, gt, rt, jt: (0, 0)),
                  pl.BlockSpec((tn, k), lambda s, kd, gt, rt, jt: (jt[s], 0))],
        out_specs=pl.BlockSpec((group * tm, tn), lambda s, kd, gt, rt, jt: (gt[s], jt[s])),
        scratch_shapes=[pltpu.VMEM((group * tm, k), BF16)])
    return pl.pallas_call(
        functools.partial(_norm_matmul_kernel, valid_cols=wt.shape[0]),
        grid_spec=grid_spec,
        out_shape=jax.ShapeDtypeStruct((m, n), BF16),
        compiler_params=_params("arbitrary"),
        name="in_proj",
    )(*tables, x, gain, wt)


def _hgrn_kernel(q_ref, f_ref, i_ref, g_ref, lbl_ref, on_ref, tril_ref, o_ref,
                 state_ref, a_ref, qi_ref, ks_ref, el_ref, qt_ref, kt_ref, sc_ref, scb_ref, *, nchunks):
    c = HGRN_CHUNK
    dk = HGRN_KDIM
    pw = 2 * dk
    npairs = HGRN_HEADS // 2
    nlev = 6

    @pl.when(pl.program_id(1) == 0)
    def _():
        state_ref[...] = jnp.zeros_like(state_ref)
        kt_ref[...] = jnp.zeros_like(kt_ref)

    lbl = lbl_ref[...]
    e = jnp.exp(lbl - jnp.max(lbl, axis=0, keepdims=True))
    lb = e[0:1] / jnp.sum(e, axis=0, keepdims=True)
    onorm = on_ref[...]
    tril = tril_ref[...]

    row = lax.broadcasted_iota(jnp.int32, (c, 2 * c), 0)
    lane = lax.broadcasted_iota(jnp.int32, (c, 2 * c), 1)
    col = lane & (c - 1)
    first = lane < c
    eye = row == col
    level_masks = []
    lg = nlev - 1
    while lg >= 0:
        level_masks.append(((row >> (lg + 1)) == (col >> (lg + 1)))
                           & (((row >> lg) & 1) == 1) & (((col >> lg) & 1) == 0))
        lg -= 1
    lane_p = lax.broadcasted_iota(jnp.int32, (c, pw), 1)
    odd_row = (lax.broadcasted_iota(jnp.int32, (c, pw), 0) & 1) == 1
    sub8 = lax.broadcasted_iota(jnp.int32, (8, pw), 0)
    zeros_st = jnp.zeros((dk, dk), BF16)

    def chunk_body(ci, carry):
        r0 = pl.multiple_of(ci * c, c)

        def _decay_factors():
            hq = q_ref[0, pl.ds(r0, c), :].astype(F32)
            hf = f_ref[0, pl.ds(r0, c), :].astype(F32)
            qf = hq * _sigmoid(hq) * (dk ** -0.5)
            f = lb + (1.0 - lb) * _sigmoid(hf)
            logf = jnp.log(f) * LOG2E
            kk = 1.0 - f
            g0 = logf.astype(BF16)
            g1 = (logf - g0.astype(F32)).astype(BF16)
            a_ref[...] = _dot(tril, g0) + _dot(tril, g1)
            for p in range(npairs):
                sl = slice(p * pw, (p + 1) * pw)
                a = a_ref[:, sl]
                row = lambda r, n: jnp.broadcast_to(a_ref[r:r + 1, sl], (n, pw))
                e0 = jnp.exp2(a)
                qp = qf[:, sl]
                kp = kk[:, sl]
                qi_ref[p] = (qp * e0).astype(BF16)
                ks_ref[p] = (kp * jnp.exp2(row(c - 1, c) - a)).astype(BF16)
                el_ref[p] = e0[c - 1:c]
                for l in range(nlev):
                    h = c >> (l + 1)
                    if h >= 4:
                        ref = jnp.concatenate([row(m * 2 * h + h - 1, 2 * h) for m in range(c // (2 * h))], axis=0)
                    elif h == 2:
                        ref = jnp.concatenate([jnp.where(sub8 < 4, row(8 * m + 1, 8), row(8 * m + 5, 8))
                                               for m in range(c // 8)], axis=0)
                    else:
                        ref = jnp.where(odd_row, pltpu.roll(a, 1, 0), a)
                    el = jnp.exp2(-jnp.abs(a - ref))
                    qt_ref[p, l] = (qp * el).astype(BF16)
                    kl = (kp * el).astype(BF16)
                    kt_ref[p, l, 0:c, 0:dk] = kl[:, 0:dk]
                    kt_ref[p, l, c:2 * c, dk:pw] = kl[:, dk:pw]
                qk = qp * kp
                diag = jnp.where(first, jnp.sum(qk[:, 0:dk], axis=1, keepdims=True),
                                 jnp.sum(qk[:, dk:pw], axis=1, keepdims=True))
                sc_ref[p] = jnp.where(eye, diag, 0.0)

        def _intra_chunk_scores():
            for p in range(npairs):
                sc = sc_ref[p]
                for l, msk in enumerate(level_masks):
                    sc = sc + jnp.where(msk, _dot_nt(qt_ref[p, l], kt_ref[p, l]), 0.0)
                scb_ref[p] = sc.astype(BF16)

        def _outputs_and_state():
            for p in range(npairs):
                sl = slice(p * pw, (p + 1) * pw)
                vp = i_ref[0, pl.ds(r0, c), sl]
                st_a = state_ref[2 * p]
                st_b = state_ref[2 * p + 1]
                st_bd = jnp.concatenate(
                    [jnp.concatenate([st_a.astype(BF16), zeros_st], axis=1),
                     jnp.concatenate([zeros_st, st_b.astype(BF16)], axis=1)], axis=0)
                v_bd = jnp.concatenate([jnp.where(lane_p < dk, vp, jnp.zeros_like(vp)),
                                        jnp.where(lane_p >= dk, vp, jnp.zeros_like(vp))], axis=0)
                o = _dot_nt(qi_ref[p], st_bd) + _dot(scb_ref[p], v_bd)
                upd = _dot_tn(vp, ks_ref[p])
                el = el_ref[p]
                state_ref[2 * p] = st_a * el[:, 0:dk] + upd[0:dk, 0:dk]
                state_ref[2 * p + 1] = st_b * el[:, dk:pw] + upd[dk:pw, dk:pw]
                gate = g_ref[0, pl.ds(r0, c), sl].astype(F32)
                y = jnp.concatenate([_rms(o[:, 0:dk], onorm), _rms(o[:, dk:pw], onorm)], axis=1)
                o_ref[0, pl.ds(r0, c), sl] = (y * (gate * _sigmoid(gate))).astype(o_ref.dtype)

        _decay_factors()
        _intra_chunk_scores()
        _outputs_and_state()
        return carry

    lax.fori_loop(0, nchunks, chunk_body, 0, unroll=4)


def hgrn_group(proj, lb_logits, onorm, t_blk):
    b, s, _ = proj.shape
    w = HGRN_HEADS * HGRN_KDIM
    tril = jnp.asarray(np.tril(np.ones((HGRN_CHUNK, HGRN_CHUNK), np.float32)), BF16)
    c, pw, npairs, nlev = HGRN_CHUNK, 2 * HGRN_KDIM, HGRN_HEADS // 2, 6
    col = lambda cb: pl.BlockSpec((1, t_blk, w), lambda bi, ti, cb=cb: (bi, ti, cb))
    return pl.pallas_call(
        functools.partial(_hgrn_kernel, nchunks=t_blk // HGRN_CHUNK),
        grid=(b, s // t_blk),
        in_specs=[col(0), col(1), col(2), col(3),
                  pl.BlockSpec(lb_logits.shape, lambda bi, ti: (0, 0)),
                  pl.BlockSpec((1, HGRN_KDIM), lambda bi, ti: (0, 0)),
                  pl.BlockSpec(tril.shape, lambda bi, ti: (0, 0))],
        out_specs=pl.BlockSpec((1, t_blk, w), lambda bi, ti: (bi, ti, 0)),
        out_shape=jax.ShapeDtypeStruct((b, s, w), BF16),
        scratch_shapes=[pltpu.VMEM((HGRN_HEADS, HGRN_KDIM, HGRN_KDIM), F32),
                        pltpu.VMEM((c, w), F32),
                        pltpu.VMEM((npairs, c, pw), BF16),
                        pltpu.VMEM((npairs, c, pw), BF16),
                        pltpu.VMEM((npairs, 1, pw), F32),
                        pltpu.VMEM((npairs, nlev, c, pw), BF16),
                        pltpu.VMEM((npairs, nlev, 2 * c, pw), BF16),
                        pltpu.VMEM((npairs, c, 2 * c), F32),
                        pltpu.VMEM((npairs, c, 2 * c), BF16)],
        compiler_params=_params("parallel", "arbitrary"),
        name="hgrn2",
    )(proj, proj, proj, proj, lb_logits, onorm, tril)


def _dsa_prep_kernel(k_ref, tail_ref, kg_ref, dup_ref, kn_ref, ikd_ref):
    kg = kg_ref[...]
    for h in range(ATTN_HEADS):
        sl = slice(h * ATTN_HEAD_DIM, (h + 1) * ATTN_HEAD_DIM)
        kn_ref[0, :, sl] = _rms(k_ref[0, :, sl].astype(F32), kg).astype(BF16)
    ikd_ref[0] = _dot(tail_ref[0], dup_ref[...]).astype(BF16)


def dsa_prep(proj, knorm, tm):
    b, s, _ = proj.shape
    w = ATTN_HEADS * ATTN_HEAD_DIM
    dup = np.zeros((TAIL, LANES), np.float32)
    dup[np.arange(IDX_DIM), np.arange(IDX_DIM)] = 1.0
    dup[np.arange(IDX_DIM), np.arange(IDX_DIM) + IDX_DIM] = 1.0
    return pl.pallas_call(
        _dsa_prep_kernel,
        grid=(b, s // tm),
        in_specs=[pl.BlockSpec((1, tm, w), lambda bi, i: (bi, i, 5)),
                  pl.BlockSpec((1, tm, TAIL), lambda bi, i: (bi, i, 8 * w // TAIL)),
                  pl.BlockSpec((1, ATTN_HEAD_DIM), lambda bi, i: (0, 0)),
                  pl.BlockSpec((TAIL, LANES), lambda bi, i: (0, 0))],
        out_specs=[pl.BlockSpec((1, tm, w), lambda bi, i: (bi, i, 0)),
                   pl.BlockSpec((1, tm, LANES), lambda bi, i: (bi, i, 0))],
        out_shape=[jax.ShapeDtypeStruct((b, s, w), BF16),
                   jax.ShapeDtypeStruct((b, s, LANES), BF16)],
        compiler_params=_params("parallel", "parallel"),
        name="dsa_prep",
    )(proj, proj, knorm, jnp.asarray(dup, BF16))


def _dsa_kernel(pi_ref, pj_ref, aq_ref, iq_ref, tail_ref, ikd_ref, kn_ref, v_ref, qg_ref, eq_ref, esel_ref,
                smax_ref, o_ref, keys_ref, thr_ref, qn_ref, qw_ref, lo_ref, hi_ref, m_ref, l_ref, acc_ref, p_ref,
                *, tq, tk, topk):
    step = pl.program_id(1)
    i = pi_ref[step]
    j = pj_ref[step]
    dh = ATTN_HEAD_DIM

    def causal(jj):
        s_pos = jj * tk + lax.broadcasted_iota(jnp.int32, (tk, tq), 0)
        t_pos = i * tq + lax.broadcasted_iota(jnp.int32, (tk, tq), 1)
        return s_pos <= t_pos

    @pl.when(j == 0)
    def _index_and_select():
        qg = qg_ref[...] * (dh ** -0.5 * LOG2E)
        for h in range(ATTN_HEADS):
            sl = slice(h * dh, (h + 1) * dh)
            qn_ref[:, sl] = _rms(aq_ref[0, :, sl].astype(F32), qg).astype(BF16)
        tail = tail_ref[0]
        ww = _dot(tail, eq_ref[...])
        qw = (iq_ref[0].astype(F32) * ww * (IDX_HEADS ** -0.5 * IDX_DIM ** -0.5)).astype(BF16)
        lane = lax.broadcasted_iota(jnp.int32, (tq, LANES), 1)
        zero = jnp.zeros((tq, LANES), BF16)
        for p in range(IDX_HEADS // 2):
            pair = qw[:, p * LANES:(p + 1) * LANES]
            qw_ref[2 * p] = jnp.where(lane < IDX_DIM, pair, zero)
            qw_ref[2 * p + 1] = jnp.where(lane >= IDX_DIM, pair, zero)
        w_t = _dot_nt(esel_ref[...], tail)
        lo_ref[...] = jnp.where(w_t > 0.0, 0.0, NEG_INF)
        hi_ref[...] = jnp.where(w_t > 0.0, jnp.inf, 0.0)

        def score_tile(jj, c):
            r0 = pl.multiple_of(jj * tk, tk)
            ik = ikd_ref[0, pl.ds(r0, tk), :]
            comb = jnp.zeros((tk, tq), F32)
            for h in range(IDX_HEADS):
                x = _dot_nt(ik, qw_ref[h])
                comb = comb + jnp.minimum(jnp.maximum(x, lo_ref[h:h + 1, :]), hi_ref[h:h + 1, :])
            keys_ref[jj] = jnp.where(causal(jj), comb, NEG_INF)
            return c
        lax.fori_loop(0, i + 1, score_tile, 0)

        def as_float(t):
            return pltpu.bitcast(jnp.where(t < 0, t ^ 0x7FFFFFFF, t), F32)
        def count_ge(tf):
            def body(jj, acc):
                accs = [acc, jnp.zeros_like(acc), jnp.zeros_like(acc), jnp.zeros_like(acc)]
                for r0 in range(0, tk, 8):
                    a = accs[(r0 // 8) % 4]
                    accs[(r0 // 8) % 4] = jnp.where(keys_ref[jj, r0:r0 + 8, :] >= tf, a + 1, a)
                return (accs[0] + accs[1]) + (accs[2] + accs[3])
            acc = lax.fori_loop(0, i + 1, body, jnp.zeros((8, tq), jnp.int32))
            return jnp.sum(acc.astype(F32), axis=0, keepdims=True)
        def unresolved(carry):
            bi, _, cnt = carry
            return jnp.logical_and(bi < 32, jnp.max(jnp.abs(cnt - topk)) > 0.0)
        def bit_steps(carry):
            bi, t, cnt = carry
            nbits = jnp.where(bi == 0, 24, 4)
            def one_bit(k, tc_cnt):
                t, cnt = tc_cnt
                tc = t + (jnp.int32(1) << (31 - bi - k))
                c = count_ge(as_float(tc))
                take = c >= topk
                return jnp.where(take, tc, t), jnp.where(take, c, cnt)
            t, cnt = lax.fori_loop(0, nbits, one_bit, (t, cnt))
            return bi + nbits, t, cnt
        n_all = ((i + 1) * tk).astype(F32)
        _, t_fin, _ = lax.while_loop(unresolved, bit_steps,
                                     (jnp.int32(0), jnp.full((1, tq), INT_MIN, jnp.int32),
                                      jnp.full((1, tq), n_all, F32)))
        t_float = as_float(t_fin)
        thr_ref[...] = jnp.where(t_float != t_float, NEG_INF, t_float)

        m_ref[...] = jnp.full_like(m_ref, NEG_INF)
        l_ref[...] = jnp.zeros_like(l_ref)
        acc_ref[...] = jnp.zeros_like(acc_ref)

    sel = (keys_ref[j] >= thr_ref[...]) & causal(j)
    smax = smax_ref[0]
    fixed_shift = smax <= SOFTMAX_FIXED_SHIFT_MAX

    @pl.when(fixed_shift)
    def _attend_fixed_shift():
        bias = jnp.where(sel, -smax * LOG2E, NEG_INF)
        for h in range(ATTN_HEADS):
            sl = slice(h * dh, (h + 1) * dh)
            p = jnp.exp2(_dot_nt(kn_ref[0, :, sl], qn_ref[:, sl]) + bias)
            parts = [p[r * 8:(r + 1) * 8] for r in range(tk // 8)]
            while len(parts) > 1:
                parts = [a + b for a, b in zip(parts[0::2], parts[1::2])]
            l_ref[h] += parts[0]
            p_ref[h] = p.astype(BF16)

    @pl.when(jnp.logical_and(fixed_shift, j <= i))
    def _weighted_values():
        for h in range(ATTN_HEADS):
            sl = slice(h * dh, (h + 1) * dh)
            acc_ref[h] += _dot_tn(v_ref[0, :, sl], p_ref[h])

    @pl.when(jnp.logical_not(fixed_shift))
    def _attend_running_max():
        bias = jnp.where(sel, 0.0, NEG_INF)
        m_all = m_ref[...]
        m_rows = []
        for h in range(ATTN_HEADS):
            sl = slice(h * dh, (h + 1) * dh)
            s = _dot_nt(kn_ref[0, :, sl], qn_ref[:, sl]) + bias
            m_old = m_all[h:h + 1, :]
            m_new = jnp.maximum(m_old, jnp.max(s, axis=0, keepdims=True))
            m_safe = jnp.where(m_new == NEG_INF, 0.0, m_new)
            alpha = jnp.exp2(m_old - m_safe)
            p = jnp.exp2(s - m_safe)
            l_ref[h, 0:1, :] = alpha * l_ref[h, 0:1, :] + jnp.sum(p, axis=0, keepdims=True)
            acc_ref[h] = alpha * acc_ref[h] + _dot_tn(v_ref[0, :, sl], p.astype(BF16))
            m_rows.append(m_new)
        m_ref[...] = jnp.concatenate(m_rows, axis=0)

    @pl.when(j == i)
    def _finish():
        for h in range(ATTN_HEADS):
            sl = slice(h * dh, (h + 1) * dh)
            l = jnp.sum(l_ref[h], axis=0, keepdims=True)
            o_ref[0, :, sl] = (acc_ref[h] / l).T.astype(o_ref.dtype)


def dsa_group(proj, kn, ikd, qnorm, knorm, tq):
    b, s, _ = proj.shape
    tk = tq
    smax = (ATTN_HEAD_DIM ** 0.5 * jnp.max(jnp.abs(qnorm)) * jnp.max(jnp.abs(knorm))).reshape(1).astype(F32)
    nq = s // tq
    w = ATTN_HEADS * ATTN_HEAD_DIM
    topk = min(DSA_TOPK, s // 4)
    eq = np.zeros((TAIL, IDX_HEADS * IDX_DIM), np.float32)
    esel = np.zeros((IDX_HEADS, TAIL), np.float32)
    for h in range(IDX_HEADS):
        eq[IDX_DIM + h, h * IDX_DIM:(h + 1) * IDX_DIM] = 1.0
        esel[h, IDX_DIM + h] = 1.0
    pairs = [(i, j) for i in range(nq) for j in range(i + 1)]
    pi = jnp.asarray([p[0] for p in pairs], jnp.int32)
    pj = jnp.asarray([p[1] for p in pairs], jnp.int32)
    qblk = lambda cb: pl.BlockSpec((1, tq, w), lambda bi, st, pi, pj, cb=cb: (bi, pi[st], cb))
    const = lambda a: pl.BlockSpec(a.shape, lambda bi, st, pi, pj: (0,) * a.ndim)
    grid_spec = pltpu.PrefetchScalarGridSpec(
        num_scalar_prefetch=2,
        grid=(b, len(pairs)),
        in_specs=[qblk(4),
                  qblk(7),
                  pl.BlockSpec((1, tq, TAIL), lambda bi, st, pi, pj: (bi, pi[st], 8 * w // TAIL)),
                  pl.BlockSpec((1, s, LANES), lambda bi, st, pi, pj: (bi, 0, 0)),
                  pl.BlockSpec((1, tk, w), lambda bi, st, pi, pj: (bi, pj[st], 0)),
                  pl.BlockSpec((1, tk, w), lambda bi, st, pi, pj: (bi, pj[st], 6)),
                  const(qnorm), const(eq), const(esel),
                  pl.BlockSpec(memory_space=pltpu.SMEM)],
        out_specs=pl.BlockSpec((1, tq, w), lambda bi, st, pi, pj: (bi, pi[st], 0)),
        scratch_shapes=[pltpu.VMEM((nq, tk, tq), F32),
                        pltpu.VMEM((1, tq), F32),
                        pltpu.VMEM((tq, w), BF16),
                        pltpu.VMEM((IDX_HEADS, tq, LANES), BF16),
                        pltpu.VMEM((IDX_HEADS, tq), F32),
                        pltpu.VMEM((IDX_HEADS, tq), F32),
                        pltpu.VMEM((ATTN_HEADS, tq), F32),
                        pltpu.VMEM((ATTN_HEADS, 8, tq), F32),
                        pltpu.VMEM((ATTN_HEADS, ATTN_HEAD_DIM, tq), F32),
                        pltpu.VMEM((ATTN_HEADS, tk, tq), BF16)])
    return pl.pallas_call(
        functools.partial(_dsa_kernel, tq=tq, tk=tk, topk=topk),
        grid_spec=grid_spec,
        out_shape=jax.ShapeDtypeStruct((b, s, w), BF16),
        compiler_params=_params("parallel", "arbitrary"),
        name="dsa",
    )(pi, pj, proj, proj, proj, ikd, kn, proj, qnorm, jnp.asarray(eq, BF16), jnp.asarray(esel, BF16), smax)


def _cast_rows_to(src_ref, dst_ref, step=256):
    def body(r, c):
        r0 = pl.multiple_of(r * step, step)
        dst_ref[pl.ds(r0, step), :] = src_ref[0, pl.ds(r0, step), :].astype(BF16)
        return c
    lax.fori_loop(0, dst_ref.shape[0] // step, body, 0)


def _out_proj_kernel(yh_ref, ya_ref, w_ref, x_ref, o_ref, wb_ref):
    @pl.when(pl.program_id(0) == 0)
    def _():
        _cast_rows_to(w_ref, wb_ref)
    kh = yh_ref.shape[1]
    o_ref[...] = x_ref[...] + _dot(yh_ref[...], wb_ref[0:kh, :]) + _dot(ya_ref[...], wb_ref[kh:2 * kh, :])


def out_proj(yh, ya, w_out, layer, x, tm):
    m, kh = yh.shape
    _, k, n = w_out.shape
    return pl.pallas_call(
        _out_proj_kernel,
        grid=(m // tm,),
        in_specs=[pl.BlockSpec((tm, kh), lambda i: (i, 0)),
                  pl.BlockSpec((tm, kh), lambda i: (i, 0)),
                  pl.BlockSpec((1, k, n), lambda i: (layer, 0, 0), pipeline_mode=pl.Buffered(1)),
                  pl.BlockSpec((tm, n), lambda i: (i, 0))],
        out_specs=pl.BlockSpec((tm, n), lambda i: (i, 0)),
        out_shape=jax.ShapeDtypeStruct((m, n), F32),
        scratch_shapes=[pltpu.VMEM((k, n), BF16)],
        compiler_params=_params("arbitrary"),
        name="out_proj",
    )(yh, ya, w_out, x)


def _mem_kv_kernel(mem_ref, mg_ref, wk_ref, wv_ref, kg_ref, k_ref, v_ref):
    memn = _rms(mem_ref[0], mg_ref[...]).astype(BF16)
    k = _dot(memn, wk_ref[0].astype(BF16))
    kg = kg_ref[...]
    for h in range(CROSS_HEADS):
        sl = slice(h * CROSS_HEAD_DIM, (h + 1) * CROSS_HEAD_DIM)
        k_ref[0, :, sl] = _rms(k[:, sl], kg).astype(BF16)
    v_ref[0] = _dot(memn, wv_ref[0].astype(BF16)).astype(BF16)


def mem_kv(mem, mem_norm, wk, wv, layer, xk_norm):
    b, nm, d = mem.shape
    cw = wk.shape[2]
    full = lambda a: pl.BlockSpec(a.shape, lambda bi: (0,) * a.ndim)
    wspec = pl.BlockSpec((1, d, cw), lambda bi: (layer, 0, 0))
    return pl.pallas_call(
        _mem_kv_kernel,
        grid=(b,),
        in_specs=[pl.BlockSpec((1, nm, d), lambda bi: (bi, 0, 0)),
                  full(mem_norm), wspec, wspec, full(xk_norm)],
        out_specs=[pl.BlockSpec((1, nm, cw), lambda bi: (bi, 0, 0))] * 2,
        out_shape=[jax.ShapeDtypeStruct((b, nm, cw), BF16)] * 2,
        compiler_params=_params("parallel"),
        name="mem_kv",
    )(mem, mem_norm, wk, wv, xk_norm)


def _cross_kernel(h_ref, ng_ref, wq_ref, qg_ref, k_ref, v_ref, wo_ref, o_ref, hn_ref, oc_ref, wqb_ref, wob_ref):
    @pl.when((pl.program_id(0) == 0) & (pl.program_id(1) == 0))
    def _():
        _cast_rows_to(wq_ref, wqb_ref)
        _cast_rows_to(wo_ref, wob_ref)
    rows = h_ref.shape[1]
    dh = CROSS_HEAD_DIM
    step = 256
    gain = ng_ref[...]
    def body(r, c):
        r0 = pl.multiple_of(r * step, step)
        hn_ref[pl.ds(r0, step), :] = _rms(h_ref[0, pl.ds(r0, step), :], gain).astype(BF16)
        return c
    lax.fori_loop(0, rows // step, body, 0)
    q = _dot(hn_ref[...], wqb_ref[...])
    qg = qg_ref[...] * (dh ** -0.5)
    for h in range(CROSS_HEADS):
        sl = slice(h * dh, (h + 1) * dh)
        qn = _rms(q[:, sl], qg).astype(BF16)
        s = _dot_nt(qn, k_ref[0, :, sl])
        p = jnp.exp(s - jnp.max(s, axis=1, keepdims=True))
        l = jnp.sum(p, axis=1, keepdims=True)
        oc_ref[:, sl] = (_dot(p.astype(BF16), v_ref[0, :, sl]) / l).astype(BF16)
    o_ref[0] = h_ref[0] + _dot(oc_ref[...], wob_ref[...])


def cross_attention(h, norm_cross, wq, xq_norm, kx, vx, wo, layer, tm):
    b, s, d = h.shape
    nm, cw = kx.shape[1:]
    full = lambda a: pl.BlockSpec(a.shape, lambda bi, i: (0,) * a.ndim)
    resident = lambda a: pl.BlockSpec((1,) + a.shape[1:], lambda bi, i: (layer, 0, 0),
                                      pipeline_mode=pl.Buffered(1))
    return pl.pallas_call(
        _cross_kernel,
        grid=(b, s // tm),
        in_specs=[pl.BlockSpec((1, tm, d), lambda bi, i: (bi, i, 0)),
                  full(norm_cross), resident(wq), full(xq_norm),
                  pl.BlockSpec((1, nm, cw), lambda bi, i: (bi, 0, 0)),
                  pl.BlockSpec((1, nm, cw), lambda bi, i: (bi, 0, 0)),
                  resident(wo)],
        out_specs=pl.BlockSpec((1, tm, d), lambda bi, i: (bi, i, 0)),
        out_shape=jax.ShapeDtypeStruct((b, s, d), F32),
        scratch_shapes=[pltpu.VMEM((tm, d), BF16), pltpu.VMEM((tm, cw), BF16),
                        pltpu.VMEM(wq.shape[1:], BF16), pltpu.VMEM(wo.shape[1:], BF16)],
        compiler_params=_params("arbitrary", "arbitrary"),
        name="cross_attn",
    )(h, norm_cross, wq, xq_norm, kx, vx, wo)


def _mlp_kernel(h_ref, g_ref, wu_ref, wd_ref, o_ref, hn_ref):
    @pl.when(pl.program_id(1) == 0)
    def _():
        _norm_rows_to(h_ref, g_ref[...], hn_ref, h_ref.shape[0])
        o_ref[...] = h_ref[...]

    u = jnp.maximum(_dot(hn_ref[...], wu_ref[0].astype(BF16)), 0.0)
    o_ref[...] += _dot((u * u).astype(BF16), wd_ref[0].astype(BF16))


def mlp(h, gain, w_up, w_down, layer, tm, tf):
    m, d = h.shape
    f = w_up.shape[2]
    return pl.pallas_call(
        _mlp_kernel,
        grid=(m // tm, f // tf),
        in_specs=[pl.BlockSpec((tm, d), lambda i, j: (i, 0)),
                  pl.BlockSpec((1, d), lambda i, j: (0, 0)),
                  pl.BlockSpec((1, d, tf), lambda i, j: (layer, 0, j)),
                  pl.BlockSpec((1, tf, d), lambda i, j: (layer, j, 0))],
        out_specs=pl.BlockSpec((tm, d), lambda i, j: (i, 0), pipeline_mode=pl.Buffered(1)),
        out_shape=jax.ShapeDtypeStruct((m, d), F32),
        scratch_shapes=[pltpu.VMEM((tm, d), BF16)],
        compiler_params=_params("parallel", "arbitrary"),
        name="mlp",
    )(h, gain, w_up, w_down)


def _tile(n, pref):
    return pref if n % pref == 0 else n


def _tiles(n, s):
    return dict(
        proj_rows=_tile(n, 1024), proj_cols=768,
        proj_group=2 if n % 2048 == 0 else 1,
        hgrn_rows=_tile(s, 1024),
        dsa_block=_tile(s, 512),
        out_rows=_tile(n, 512),
        cross_rows=_tile(s, 1024),
        mlp_rows=_tile(n, 1024), mlp_ff=512)


def kernel(x, mem, norm_mix, w_in, hgrn_lb_logits, hgrn_onorm, attn_qnorm, attn_knorm, w_out,
           norm_cross, mem_norm, wq_x, wk_x, wv_x, wo_x, xq_norm, xk_norm,
           norm_mlp, w_up, w_down):
    b, s, d = x.shape
    n = b * s
    depth = w_in.shape[0]
    assert depth == 1
    l = 0
    in_width = w_in.shape[2]
    main_w = 8 * HGRN_HEADS * HGRN_KDIM
    assert in_width == main_w + IDX_DIM + IDX_HEADS
    pad_w = main_w + TAIL

    t = _tiles(n, s)
    x2 = x.reshape(n, d)

    proj = norm_matmul(x2, norm_mix[l:l + 1], w_in[l].T, pad_w,
                       t["proj_rows"], t["proj_cols"], t["proj_group"])
    proj3 = proj.reshape(b, s, pad_w)

    y_h = hgrn_group(proj3, hgrn_lb_logits, hgrn_onorm[l:l + 1], t["hgrn_rows"])
    kn, ikd = dsa_prep(proj3, attn_knorm[l:l + 1], t["dsa_block"])
    y_a = dsa_group(proj3, kn, ikd, attn_qnorm[l:l + 1], attn_knorm[l:l + 1], t["dsa_block"])

    h1 = out_proj(y_h.reshape(n, -1), y_a.reshape(n, -1), w_out, l, x2, t["out_rows"])

    kx, vx = mem_kv(mem, mem_norm[l:l + 1], wk_x, wv_x, l, xk_norm[l:l + 1])
    h2 = cross_attention(h1.reshape(b, s, d), norm_cross[l:l + 1], wq_x, xq_norm[l:l + 1],
                         kx, vx, wo_x, l, t["cross_rows"])

    h3 = mlp(h2.reshape(n, d), norm_mlp[l:l + 1], w_up, w_down, l, t["mlp_rows"], t["mlp_ff"])
    return h3.reshape(b, s, d)
```

```python
import functools

import numpy as np
import jax
import jax.numpy as jnp
from jax import lax
from jax.experimental import pallas as pl
from jax.experimental.pallas import tpu as pltpu

F32 = jnp.float32
BF16 = jnp.bfloat16
EPS = 1e-6

LANES = 128
HGRN_HEADS = 8
HGRN_KDIM = 128
HGRN_CHUNK = 64
ATTN_HEADS = 8
ATTN_HEAD_DIM = 128
IDX_HEADS = 16
IDX_DIM = 64
DSA_TOPK = 256
CROSS_HEADS = 4
CROSS_HEAD_DIM = 128
TAIL = 256
V7X_VMEM_BYTES = 64 * 1024 * 1024
VMEM_LIMIT = V7X_VMEM_BYTES * 7 // 8
NEG_INF = float("-inf")
INT_MIN = -(2 ** 31)
LOG2E = 1.4426950408889634
SOFTMAX_FIXED_SHIFT_MAX = 40.0


def _params(*sem):
    return pltpu.CompilerParams(dimension_semantics=sem, vmem_limit_bytes=VMEM_LIMIT)


def _rms(x, gain):
    return x * lax.rsqrt(jnp.mean(x * x, axis=-1, keepdims=True) + EPS) * gain


def _sigmoid(x):
    return 1.0 / (1.0 + jnp.exp(-x))


def _dot(a, b):
    return jnp.dot(a, b, preferred_element_type=F32)


def _dot_nt(a, b):
    return lax.dot_general(a, b, (((1,), (1,)), ((), ())), preferred_element_type=F32)


def _dot_tn(a, b):
    return lax.dot_general(a, b, (((0,), (0,)), ((), ())), preferred_element_type=F32)


def _norm_rows_to(x_ref, gain, dst_ref, rows):
    step = 256
    def body(r, c):
        r0 = pl.multiple_of(r * step, step)
        x = x_ref[pl.ds(r0, step), :]
        dst_ref[pl.ds(r0, step), :] = _rms(x, gain).astype(BF16)
        return c
    lax.fori_loop(0, rows // step, body, 0)


def _norm_matmul_kernel(kind_ref, g_tbl, r_tbl, j_tbl, x_ref, gain_ref, wt_ref, o_ref, xn_ref, *, valid_cols):
    s = pl.program_id(0)
    tm = x_ref.shape[0]
    tn = o_ref.shape[1]

    @pl.when(kind_ref[s] == 0)
    def _():
        r0 = pl.multiple_of(r_tbl[s] * tm, tm)
        _norm_rows_to(x_ref, gain_ref[...], xn_ref.at[pl.ds(r0, tm)], tm)

    @pl.when(kind_ref[s] == 1)
    def _():
        res = _dot_nt(xn_ref[...], wt_ref[...].astype(BF16))
        col = j_tbl[s] * tn + lax.broadcasted_iota(jnp.int32, (1, tn), 1)
        o_ref[...] = jnp.where(col < valid_cols, res, 0.0).astype(o_ref.dtype)


def norm_matmul(x, gain, wt, n, tm, tn, group):
    m, k = x.shape
    ngroups, ntiles = m // (group * tm), n // tn
    kind, g_tbl, r_tbl, j_tbl = [], [], [], []
    for g in range(ngroups):
        for r in range(group):
            kind.append(0); g_tbl.append(g); r_tbl.append(r); j_tbl.append(0)
        for j in range(ntiles):
            kind.append(1); g_tbl.append(g); r_tbl.append(group - 1); j_tbl.append(j)
    tables = [jnp.asarray(t, jnp.int32) for t in (kind, g_tbl, r_tbl, j_tbl)]
    grid_spec = pltpu.PrefetchScalarGridSpec(
        num_scalar_prefetch=4,
        grid=(len(kind),),
        in_specs=[pl.BlockSpec((tm, k), lambda s, kd, gt, rt, jt: (gt[s] * group + rt[s], 0)),
                  pl.BlockSpec((1, k), lambda s, kd, gt, rt, jt: (0, 0)),
                  pl.BlockSpec((tn, k), lambda s, kd, gt, rt, jt: (jt[s], 0))],
        out_specs=pl.BlockSpec((group * tm, tn), lambda s, kd, gt, rt, jt: (gt[s], jt[s])),
        scratch_shapes=[pltpu.VMEM((group * tm, k), BF16)])
    return pl.pallas_call(
        functools.partial(_norm_matmul_kernel, valid_cols=wt.shape[0]),
        grid_spec=grid_spec,
        out_shape=jax.ShapeDtypeStruct((m, n), BF16),
        compiler_params=_params("arbitrary"),
        name="in_proj",
    )(*tables, x, gain, wt)


def _hgrn_kernel(q_ref, f_ref, i_ref, g_ref, lbl_ref, on_ref, tril_ref, o_ref,
                 state_ref, a_ref, qi_ref, ks_ref, el_ref, qt_ref, kt_ref, sc_ref, scb_ref, *, nchunks):
    c = HGRN_CHUNK
    dk = HGRN_KDIM
    pw = 2 * dk
    npairs = HGRN_HEADS // 2
    nlev = 6

    @pl.when(pl.program_id(1) == 0)
    def _():
        state_ref[...] = jnp.zeros_like(state_ref)
        kt_ref[...] = jnp.zeros_like(kt_ref)

    lbl = lbl_ref[...]
    e = jnp.exp(lbl - jnp.max(lbl, axis=0, keepdims=True))
    lb = e[0:1] / jnp.sum(e, axis=0, keepdims=True)
    onorm = on_ref[...]
    tril = tril_ref[...]

    row = lax.broadcasted_iota(jnp.int32, (c, 2 * c), 0)
    lane = lax.broadcasted_iota(jnp.int32, (c, 2 * c), 1)
    col = lane & (c - 1)
    first = lane < c
    eye = row == col
    level_masks = []
    lg = nlev - 1
    while lg >= 0:
        level_masks.append(((row >> (lg + 1)) == (col >> (lg + 1)))
                           & (((row >> lg) & 1) == 1) & (((col >> lg) & 1) == 0))
        lg -= 1
    lane_p = lax.broadcasted_iota(jnp.int32, (c, pw), 1)
    odd_row = (lax.broadcasted_iota(jnp.int32, (c, pw), 0) & 1) == 1
    sub8 = lax.broadcasted_iota(jnp.int32, (8, pw), 0)
    zeros_st = jnp.zeros((dk, dk), BF16)

    def chunk_body(ci, carry):
        r0 = pl.multiple_of(ci * c, c)

        def _decay_factors():
            hq = q_ref[0, pl.ds(r0, c), :].astype(F32)
            hf = f_ref[0, pl.ds(r0, c), :].astype(F32)
            qf = hq * _sigmoid(hq) * (dk ** -0.5)
            f = lb + (1.0 - lb) * _sigmoid(hf)
            logf = jnp.log(f) * LOG2E
            kk = 1.0 - f
            g0 = logf.astype(BF16)
            g1 = (logf - g0.astype(F32)).astype(BF16)
            a_ref[...] = _dot(tril, g0) + _dot(tril, g1)
            for p in range(npairs):
                sl = slice(p * pw, (p + 1) * pw)
                a = a_ref[:, sl]
                row = lambda r, n: jnp.broadcast_to(a_ref[r:r + 1, sl], (n, pw))
                e0 = jnp.exp2(a)
                qp = qf[:, sl]
                kp = kk[:, sl]
                qi_ref[p] = (qp * e0).astype(BF16)
                ks_ref[p] = (kp * jnp.exp2(row(c - 1, c) - a)).astype(BF16)
                el_ref[p] = e0[c - 1:c]
                for l in range(nlev):
                    h = c >> (l + 1)
                    if h >= 4:
                        ref = jnp.concatenate([row(m * 2 * h + h - 1, 2 * h) for m in range(c // (2 * h))], axis=0)
                    elif h == 2:
                        ref = jnp.concatenate([jnp.where(sub8 < 4, row(8 * m + 1, 8), row(8 * m + 5, 8))
                                               for m in range(c // 8)], axis=0)
                    else:
                        ref = jnp.where(odd_row, pltpu.roll(a, 1, 0), a)
                    el = jnp.exp2(-jnp.abs(a - ref))
                    qt_ref[p, l] = (qp * el).astype(BF16)
                    kl = (kp * el).astype(BF16)
                    kt_ref[p, l, 0:c, 0:dk] = kl[:, 0:dk]
                    kt_ref[p, l, c:2 * c, dk:pw] = kl[:, dk:pw]
                qk = qp * kp
                diag = jnp.where(first, jnp.sum(qk[:, 0:dk], axis=1, keepdims=True),
                                 jnp.sum(qk[:, dk:pw], axis=1, keepdims=True))
                sc_ref[p] = jnp.where(eye, diag, 0.0)

        def _intra_chunk_scores():
            for p in range(npairs):
                sc = sc_ref[p]
                for l, msk in enumerate(level_masks):
                    sc = sc + jnp.where(msk, _dot_nt(qt_ref[p, l], kt_ref[p, l]), 0.0)
                scb_ref[p] = sc.astype(BF16)

        def _outputs_and_state():
            for p in range(npairs):
                sl = slice(p * pw, (p + 1) * pw)
                vp = i_ref[0, pl.ds(r0, c), sl]
                st_a = state_ref[2 * p]
                st_b = state_ref[2 * p + 1]
                st_bd = jnp.concatenate(
                    [jnp.concatenate([st_a.astype(BF16), zeros_st], axis=1),
                     jnp.concatenate([zeros_st, st_b.astype(BF16)], axis=1)], axis=0)
                v_bd = jnp.concatenate([jnp.where(lane_p < dk, vp, jnp.zeros_like(vp)),
                                        jnp.where(lane_p >= dk, vp, jnp.zeros_like(vp))], axis=0)
                o = _dot_nt(qi_ref[p], st_bd) + _dot(scb_ref[p], v_bd)
                upd = _dot_tn(vp, ks_ref[p])
                el = el_ref[p]
                state_ref[2 * p] = st_a * el[:, 0:dk] + upd[0:dk, 0:dk]
                state_ref[2 * p + 1] = st_b * el[:, dk:pw] + upd[dk:pw, dk:pw]
                gate = g_ref[0, pl.ds(r0, c), sl].astype(F32)
                y = jnp.concatenate([_rms(o[:, 0:dk], onorm), _rms(o[:, dk:pw], onorm)], axis=1)
                o_ref[0, pl.ds(r0, c), sl] = (y * (gate * _sigmoid(gate))).astype(o_ref.dtype)

        _decay_factors()
        _intra_chunk_scores()
        _outputs_and_state()
        return carry

    lax.fori_loop(0, nchunks, chunk_body, 0, unroll=4)


def hgrn_group(proj, lb_logits, onorm, t_blk):
    b, s, _ = proj.shape
    w = HGRN_HEADS * HGRN_KDIM
    tril = jnp.asarray(np.tril(np.ones((HGRN_CHUNK, HGRN_CHUNK), np.float32)), BF16)
    c, pw, npairs, nlev = HGRN_CHUNK, 2 * HGRN_KDIM, HGRN_HEADS // 2, 6
    col = lambda cb: pl.BlockSpec((1, t_blk, w), lambda bi, ti, cb=cb: (bi, ti, cb))
    return pl.pallas_call(
        functools.partial(_hgrn_kernel, nchunks=t_blk // HGRN_CHUNK),
        grid=(b, s // t_blk),
        in_specs=[col(0), col(1), col(2), col(3),
                  pl.BlockSpec(lb_logits.shape, lambda bi, ti: (0, 0)),
                  pl.BlockSpec((1, HGRN_KDIM), lambda bi, ti: (0, 0)),
                  pl.BlockSpec(tril.shape, lambda bi, ti: (0, 0))],
        out_specs=pl.BlockSpec((1, t_blk, w), lambda bi, ti: (bi, ti, 0)),
        out_shape=jax.ShapeDtypeStruct((b, s, w), BF16),
        scratch_shapes=[pltpu.VMEM((HGRN_HEADS, HGRN_KDIM, HGRN_KDIM), F32),
                        pltpu.VMEM((c, w), F32),
                        pltpu.VMEM((npairs, c, pw), BF16),
                        pltpu.VMEM((npairs, c, pw), BF16),
                        pltpu.VMEM((npairs, 1, pw), F32),
                        pltpu.VMEM((npairs, nlev, c, pw), BF16),
                        pltpu.VMEM((npairs, nlev, 2 * c, pw), BF16),
                        pltpu.VMEM((npairs, c, 2 * c), F32),
                        pltpu.VMEM((npairs, c, 2 * c), BF16)],
        compiler_params=_params("parallel", "arbitrary"),
        name="hgrn2",
    )(proj, proj, proj, proj, lb_logits, onorm, tril)


def _dsa_prep_kernel(k_ref, tail_ref, kg_ref, dup_ref, kn_ref, ikd_ref):
    kg = kg_ref[...]
    for h in range(ATTN_HEADS):
        sl = slice(h * ATTN_HEAD_DIM, (h + 1) * ATTN_HEAD_DIM)
        kn_ref[0, :, sl] = _rms(k_ref[0, :, sl].astype(F32), kg).astype(BF16)
    ikd_ref[0] = _dot(tail_ref[0], dup_ref[...]).astype(BF16)


def dsa_prep(proj, knorm, tm):
    b, s, _ = proj.shape
    w = ATTN_HEADS * ATTN_HEAD_DIM
    dup = np.zeros((TAIL, LANES), np.float32)
    dup[np.arange(IDX_DIM), np.arange(IDX_DIM)] = 1.0
    dup[np.arange(IDX_DIM), np.arange(IDX_DIM) + IDX_DIM] = 1.0
    return pl.pallas_call(
        _dsa_prep_kernel,
        grid=(b, s // tm),
        in_specs=[pl.BlockSpec((1, tm, w), lambda bi, i: (bi, i, 5)),
                  pl.BlockSpec((1, tm, TAIL), lambda bi, i: (bi, i, 8 * w // TAIL)),
                  pl.BlockSpec((1, ATTN_HEAD_DIM), lambda bi, i: (0, 0)),
                  pl.BlockSpec((TAIL, LANES), lambda bi, i: (0, 0))],
        out_specs=[pl.BlockSpec((1, tm, w), lambda bi, i: (bi, i, 0)),
                   pl.BlockSpec((1, tm, LANES), lambda bi, i: (bi, i, 0))],
        out_shape=[jax.ShapeDtypeStruct((b, s, w), BF16),
                   jax.ShapeDtypeStruct((b, s, LANES), BF16)],
        compiler_params=_params("parallel", "parallel"),
        name="dsa_prep",
    )(proj, proj, knorm, jnp.asarray(dup, BF16))


def _dsa_kernel(pi_ref, pj_ref, aq_ref, iq_ref, tail_ref, ikd_ref, kn_ref, v_ref, qg_ref, eq_ref, esel_ref,
                smax_ref, o_ref, keys_ref, thr_ref, qn_ref, qw_ref, lo_ref, hi_ref, m_ref, l_ref, acc_ref, p_ref,
                *, tq, tk, topk):
    step = pl.program_id(1)
    i = pi_ref[step]
    j = pj_ref[step]
    dh = ATTN_HEAD_DIM

    def causal(jj):
        s_pos = jj * tk + lax.broadcasted_iota(jnp.int32, (tk, tq), 0)
        t_pos = i * tq + lax.broadcasted_iota(jnp.int32, (tk, tq), 1)
        return s_pos <= t_pos

    @pl.when(j == 0)
    def _index_and_select():
        qg = qg_ref[...] * (dh ** -0.5 * LOG2E)
        for h in range(ATTN_HEADS):
            sl = slice(h * dh, (h + 1) * dh)
            qn_ref[:, sl] = _rms(aq_ref[0, :, sl].astype(F32), qg).astype(BF16)
        tail = tail_ref[0]
        ww = _dot(tail, eq_ref[...])
        qw = (iq_ref[0].astype(F32) * ww * (IDX_HEADS ** -0.5 * IDX_DIM ** -0.5)).astype(BF16)
        lane = lax.broadcasted_iota(jnp.int32, (tq, LANES), 1)
        zero = jnp.zeros((tq, LANES), BF16)
        for p in range(IDX_HEADS // 2):
            pair = qw[:, p * LANES:(p + 1) * LANES]
            qw_ref[2 * p] = jnp.where(lane < IDX_DIM, pair, zero)
            qw_ref[2 * p + 1] = jnp.where(lane >= IDX_DIM, pair, zero)
        w_t = _dot_nt(esel_ref[...], tail)
        lo_ref[...] = jnp.where(w_t > 0.0, 0.0, NEG_INF)
        hi_ref[...] = jnp.where(w_t > 0.0, jnp.inf, 0.0)

        def score_tile(jj, c):
            r0 = pl.multiple_of(jj * tk, tk)
            ik = ikd_ref[0, pl.ds(r0, tk), :]
            comb = jnp.zeros((tk, tq), F32)
            for h in range(IDX_HEADS):
                x = _dot_nt(ik, qw_ref[h])
                comb = comb + jnp.minimum(jnp.maximum(x, lo_ref[h:h + 1, :]), hi_ref[h:h + 1, :])
            keys_ref[jj] = jnp.where(causal(jj), comb, NEG_INF)
            return c
        lax.fori_loop(0, i + 1, score_tile, 0)

        def as_float(t):
            return pltpu.bitcast(jnp.where(t < 0, t ^ 0x7FFFFFFF, t), F32)
        def count_ge(tf):
            def body(jj, acc):
                accs = [acc, jnp.zeros_like(acc), jnp.zeros_like(acc), jnp.zeros_like(acc)]
                for r0 in range(0, tk, 8):
                    a = accs[(r0 // 8) % 4]
                    accs[(r0 // 8) % 4] = jnp.where(keys_ref[jj, r0:r0 + 8, :] >= tf, a + 1, a)
                return (accs[0] + accs[1]) + (accs[2] + accs[3])
            acc = lax.fori_loop(0, i + 1, body, jnp.zeros((8, tq), jnp.int32))
            return jnp.sum(acc.astype(F32), axis=0, keepdims=True)
        def unresolved(carry):
            bi, _, cnt = carry
            return jnp.logical_and(bi < 32, jnp.max(jnp.abs(cnt - topk)) > 0.0)
        def bit_steps(carry):
            bi, t, cnt = carry
            nbits = jnp.where(bi == 0, 24, 4)
            def one_bit(k, tc_cnt):
                t, cnt = tc_cnt
                tc = t + (jnp.int32(1) << (31 - bi - k))
                c = count_ge(as_float(tc))
                take = c >= topk
                return jnp.where(take, tc, t), jnp.where(take, c, cnt)
            t, cnt = lax.fori_loop(0, nbits, one_bit, (t, cnt))
            return bi + nbits, t, cnt
        n_all = ((i + 1) * tk).astype(F32)
        _, t_fin, _ = lax.while_loop(unresolved, bit_steps,
                                     (jnp.int32(0), jnp.full((1, tq), INT_MIN, jnp.int32),
                                      jnp.full((1, tq), n_all, F32)))
        t_float = as_float(t_fin)
        thr_ref[...] = jnp.where(t_float != t_float, NEG_INF, t_float)

        m_ref[...] = jnp.full_like(m_ref, NEG_INF)
        l_ref[...] = jnp.zeros_like(l_ref)
        acc_ref[...] = jnp.zeros_like(acc_ref)

    sel = (keys_ref[j] >= thr_ref[...]) & causal(j)
    smax = smax_ref[0]
    fixed_shift = smax <= SOFTMAX_FIXED_SHIFT_MAX

    @pl.when(fixed_shift)
    def _attend_fixed_shift():
        bias = jnp.where(sel, -smax * LOG2E, NEG_INF)
        for h in range(ATTN_HEADS):
            sl = slice(h * dh, (h + 1) * dh)
            p = jnp.exp2(_dot_nt(kn_ref[0, :, sl], qn_ref[:, sl]) + bias)
            parts = [p[r * 8:(r + 1) * 8] for r in range(tk // 8)]
            while len(parts) > 1:
                parts = [a + b for a, b in zip(parts[0::2], parts[1::2])]
            l_ref[h] += parts[0]
            p_ref[h] = p.astype(BF16)

    @pl.when(jnp.logical_and(fixed_shift, j <= i))
    def _weighted_values():
        for h in range(ATTN_HEADS):
            sl = slice(h * dh, (h + 1) * dh)
            acc_ref[h] += _dot_tn(v_ref[0, :, sl], p_ref[h])

    @pl.when(jnp.logical_not(fixed_shift))
    def _attend_running_max():
        bias = jnp.where(sel, 0.0, NEG_INF)
        m_all = m_ref[...]
        m_rows = []
        for h in range(ATTN_HEADS):
            sl = slice(h * dh, (h + 1) * dh)
            s = _dot_nt(kn_ref[0, :, sl], qn_ref[:, sl]) + bias
            m_old = m_all[h:h + 1, :]
            m_new = jnp.maximum(m_old, jnp.max(s, axis=0, keepdims=True))
            m_safe = jnp.where(m_new == NEG_INF, 0.0, m_new)
            alpha = jnp.exp2(m_old - m_safe)
            p = jnp.exp2(s - m_safe)
            l_ref[h, 0:1, :] = alpha * l_ref[h, 0:1, :] + jnp.sum(p, axis=0, keepdims=True)
            acc_ref[h] = alpha * acc_ref[h] + _dot_tn(v_ref[0, :, sl], p.astype(BF16))
            m_rows.append(m_new)
        m_ref[...] = jnp.concatenate(m_rows, axis=0)

    @pl.when(j == i)
    def _finish():
        for h in range(ATTN_HEADS):
            sl = slice(h * dh, (h + 1) * dh)
            l = jnp.sum(l_ref[h], axis=0, keepdims=True)
            o_ref[0, :, sl] = (acc_ref[h] / l).T.astype(o_ref.dtype)


def dsa_group(proj, kn, ikd, qnorm, knorm, tq):
    b, s, _ = proj.shape
    tk = tq
    smax = (ATTN_HEAD_DIM ** 0.5 * jnp.max(jnp.abs(qnorm)) * jnp.max(jnp.abs(knorm))).reshape(1).astype(F32)
    nq = s // tq
    w = ATTN_HEADS * ATTN_HEAD_DIM
    topk = min(DSA_TOPK, s // 4)
    eq = np.zeros((TAIL, IDX_HEADS * IDX_DIM), np.float32)
    esel = np.zeros((IDX_HEADS, TAIL), np.float32)
    for h in range(IDX_HEADS):
        eq[IDX_DIM + h, h * IDX_DIM:(h + 1) * IDX_DIM] = 1.0
        esel[h, IDX_DIM + h] = 1.0
    pairs = [(i, j) for i in range(nq) for j in range(i + 1)]
    pi = jnp.asarray([p[0] for p in pairs], jnp.int32)
    pj = jnp.asarray([p[1] for p in pairs], jnp.int32)
    qblk = lambda cb: pl.BlockSpec((1, tq, w), lambda bi, st, pi, pj, cb=cb: (bi, pi[st], cb))
    const = lambda a: pl.BlockSpec(a.shape, lambda bi, st, pi, pj: (0,) * a.ndim)
    grid_spec = pltpu.PrefetchScalarGridSpec(
        num_scalar_prefetch=2,
        grid=(b, len(pairs)),
        in_specs=[qblk(4),
                  qblk(7),
                  pl.BlockSpec((1, tq, TAIL), lambda bi, st, pi, pj: (bi, pi[st], 8 * w // TAIL)),
                  pl.BlockSpec((1, s, LANES), lambda bi, st, pi, pj: (bi, 0, 0)),
                  pl.BlockSpec((1, tk, w), lambda bi, st, pi, pj: (bi, pj[st], 0)),
                  pl.BlockSpec((1, tk, w), lambda bi, st, pi, pj: (bi, pj[st], 6)),
                  const(qnorm), const(eq), const(esel),
                  pl.BlockSpec(memory_space=pltpu.SMEM)],
        out_specs=pl.BlockSpec((1, tq, w), lambda bi, st, pi, pj: (bi, pi[st], 0)),
        scratch_shapes=[pltpu.VMEM((nq, tk, tq), F32),
                        pltpu.VMEM((1, tq), F32),
                        pltpu.VMEM((tq, w), BF16),
                        pltpu.VMEM((IDX_HEADS, tq, LANES), BF16),
                        pltpu.VMEM((IDX_HEADS, tq), F32),
                        pltpu.VMEM((IDX_HEADS, tq), F32),
                        pltpu.VMEM((ATTN_HEADS, tq), F32),
                        pltpu.VMEM((ATTN_HEADS, 8, tq), F32),
                        pltpu.VMEM((ATTN_HEADS, ATTN_HEAD_DIM, tq), F32),
                        pltpu.VMEM((ATTN_HEADS, tk, tq), BF16)])
    return pl.pallas_call(
        functools.partial(_dsa_kernel, tq=tq, tk=tk, topk=topk),
        grid_spec=grid_spec,
        out_shape=jax.ShapeDtypeStruct((b, s, w), BF16),
        compiler_params=_params("parallel", "arbitrary"),
        name="dsa",
    )(pi, pj, proj, proj, proj, ikd, kn, proj, qnorm, jnp.asarray(eq, BF16), jnp.asarray(esel, BF16), smax)


def _cast_rows_to(src_ref, dst_ref, step=256):
    def body(r, c):
        r0 = pl.multiple_of(r * step, step)
        dst_ref[pl.ds(r0, step), :] = src_ref[0, pl.ds(r0, step), :].astype(BF16)
        return c
    lax.fori_loop(0, dst_ref.shape[0] // step, body, 0)


def _out_proj_kernel(yh_ref, ya_ref, w_ref, x_ref, o_ref, wb_ref):
    @pl.when(pl.program_id(0) == 0)
    def _():
        _cast_rows_to(w_ref, wb_ref)
    kh = yh_ref.shape[1]
    o_ref[...] = x_ref[...] + _dot(yh_ref[...], wb_ref[0:kh, :]) + _dot(ya_ref[...], wb_ref[kh:2 * kh, :])


def out_proj(yh, ya, w_out, layer, x, tm):
    m, kh = yh.shape
    _, k, n = w_out.shape
    return pl.pallas_call(
        _out_proj_kernel,
        grid=(m // tm,),
        in_specs=[pl.BlockSpec((tm, kh), lambda i: (i, 0)),
                  pl.BlockSpec((tm, kh), lambda i: (i, 0)),
                  pl.BlockSpec((1, k, n), lambda i: (layer, 0, 0), pipeline_mode=pl.Buffered(1)),
                  pl.BlockSpec((tm, n), lambda i: (i, 0))],
        out_specs=pl.BlockSpec((tm, n), lambda i: (i, 0)),
        out_shape=jax.ShapeDtypeStruct((m, n), F32),
        scratch_shapes=[pltpu.VMEM((k, n), BF16)],
        compiler_params=_params("arbitrary"),
        name="out_proj",
    )(yh, ya, w_out, x)


def _mem_kv_kernel(mem_ref, mg_ref, wk_ref, wv_ref, kg_ref, k_ref, v_ref):
    memn = _rms(mem_ref[0], mg_ref[...]).astype(BF16)
    k = _dot(memn, wk_ref[0].astype(BF16))
    kg = kg_ref[...]
    for h in range(CROSS_HEADS):
        sl = slice(h * CROSS_HEAD_DIM, (h + 1) * CROSS_HEAD_DIM)
        k_ref[0, :, sl] = _rms(k[:, sl], kg).astype(BF16)
    v_ref[0] = _dot(memn, wv_ref[0].astype(BF16)).astype(BF16)


def mem_kv(mem, mem_norm, wk, wv, layer, xk_norm):
    b, nm, d = mem.shape
    cw = wk.shape[2]
    full = lambda a: pl.BlockSpec(a.shape, lambda bi: (0,) * a.ndim)
    wspec = pl.BlockSpec((1, d, cw), lambda bi: (layer, 0, 0))
    return pl.pallas_call(
        _mem_kv_kernel,
        grid=(b,),
        in_specs=[pl.BlockSpec((1, nm, d), lambda bi: (bi, 0, 0)),
                  full(mem_norm), wspec, wspec, full(xk_norm)],
        out_specs=[pl.BlockSpec((1, nm, cw), lambda bi: (bi, 0, 0))] * 2,
        out_shape=[jax.ShapeDtypeStruct((b, nm, cw), BF16)] * 2,
        compiler_params=_params("parallel"),
        name="mem_kv",
    )(mem, mem_norm, wk, wv, xk_norm)


def _cross_kernel(h_ref, ng_ref, wq_ref, qg_ref, k_ref, v_ref, wo_ref, o_ref, hn_ref, oc_ref, wqb_ref, wob_ref):
    @pl.when((pl.program_id(0) == 0) & (pl.program_id(1) == 0))
    def _():
        _cast_rows_to(wq_ref, wqb_ref)
        _cast_rows_to(wo_ref, wob_ref)
    rows = h_ref.shape[1]
    dh = CROSS_HEAD_DIM
    step = 256
    gain = ng_ref[...]
    def body(r, c):
        r0 = pl.multiple_of(r * step, step)
        hn_ref[pl.ds(r0, step), :] = _rms(h_ref[0, pl.ds(r0, step), :], gain).astype(BF16)
        return c
    lax.fori_loop(0, rows // step, body, 0)
    q = _dot(hn_ref[...], wqb_ref[...])
    qg = qg_ref[...] * (dh ** -0.5)
    for h in range(CROSS_HEADS):
        sl = slice(h * dh, (h + 1) * dh)
        qn = _rms(q[:, sl], qg).astype(BF16)
        s = _dot_nt(qn, k_ref[0, :, sl])
        p = jnp.exp(s - jnp.max(s, axis=1, keepdims=True))
        l = jnp.sum(p, axis=1, keepdims=True)
        oc_ref[:, sl] = (_dot(p.astype(BF16), v_ref[0, :, sl]) / l).astype(BF16)
    o_ref[0] = h_ref[0] + _dot(oc_ref[...], wob_ref[...])


def cross_attention(h, norm_cross, wq, xq_norm, kx, vx, wo, layer, tm):
    b, s, d = h.shape
    nm, cw = kx.shape[1:]
    full = lambda a: pl.BlockSpec(a.shape, lambda bi, i: (0,) * a.ndim)
    resident = lambda a: pl.BlockSpec((1,) + a.shape[1:], lambda bi, i: (layer, 0, 0),
                                      pipeline_mode=pl.Buffered(1))
    return pl.pallas_call(
        _cross_kernel,
        grid=(b, s // tm),
        in_specs=[pl.BlockSpec((1, tm, d), lambda bi, i: (bi, i, 0)),
                  full(norm_cross), resident(wq), full(xq_norm),
                  pl.BlockSpec((1, nm, cw), lambda bi, i: (bi, 0, 0)),
                  pl.BlockSpec((1, nm, cw), lambda bi, i: (bi, 0, 0)),
                  resident(wo)],
        out_specs=pl.BlockSpec((1, tm, d), lambda bi, i: (bi, i, 0)),
        out_shape=jax.ShapeDtypeStruct((b, s, d), F32),
        scratch_shapes=[pltpu.VMEM((tm, d), BF16), pltpu.VMEM((tm, cw), BF16),
                        pltpu.VMEM(wq.shape[1:], BF16), pltpu.VMEM(wo.shape[1:], BF16)],
        compiler_params=_params("arbitrary", "arbitrary"),
        name="cross_attn",
    )(h, norm_cross, wq, xq_norm, kx, vx, wo)


def _mlp_kernel(kind_ref, g_tbl, r_tbl, j_tbl, h_ref, gain_ref, wu_ref, wd_ref, o_ref, hn_ref):
    s = pl.program_id(0)
    hb = h_ref.shape[0]

    @pl.when(kind_ref[s] == 0)
    def _():
        r0 = pl.multiple_of(r_tbl[s] * hb, hb)
        x = h_ref[...]
        hn_ref[pl.ds(r0, hb), :] = _rms(x, gain_ref[...]).astype(BF16)
        o_ref[pl.ds(r0, hb), :] = x

    @pl.when(kind_ref[s] == 1)
    def _():
        u = jnp.maximum(_dot(hn_ref[...], wu_ref[0].astype(BF16)), 0.0)
        o_ref[...] += _dot((u * u).astype(BF16), wd_ref[0].astype(BF16))


def mlp(h, gain, w_up, w_down, layer, gm, hb, tf):
    m, d = h.shape
    f = w_up.shape[2]
    kind, g_tbl, r_tbl, j_tbl = [], [], [], []
    for g in range(m // gm):
        for r in range(gm // hb):
            kind.append(0); g_tbl.append(g); r_tbl.append(r); j_tbl.append(0)
        for j in range(f // tf):
            kind.append(1); g_tbl.append(g); r_tbl.append(gm // hb - 1); j_tbl.append(j)
    tables = [jnp.asarray(t, jnp.int32) for t in (kind, g_tbl, r_tbl, j_tbl)]
    grid_spec = pltpu.PrefetchScalarGridSpec(
        num_scalar_prefetch=4,
        grid=(len(kind),),
        in_specs=[pl.BlockSpec((hb, d), lambda s, kd, gt, rt, jt: (gt[s] * (gm // hb) + rt[s], 0)),
                  pl.BlockSpec((1, d), lambda s, kd, gt, rt, jt: (0, 0)),
                  pl.BlockSpec((1, d, tf), lambda s, kd, gt, rt, jt: (layer, 0, jt[s])),
                  pl.BlockSpec((1, tf, d), lambda s, kd, gt, rt, jt: (layer, jt[s], 0))],
        out_specs=pl.BlockSpec((gm, d), lambda s, kd, gt, rt, jt: (gt[s], 0), pipeline_mode=pl.Buffered(1)),
        scratch_shapes=[pltpu.VMEM((gm, d), BF16)])
    return pl.pallas_call(
        _mlp_kernel,
        grid_spec=grid_spec,
        out_shape=jax.ShapeDtypeStruct((m, d), F32),
        compiler_params=_params("arbitrary"),
        name="mlp",
    )(*tables, h, gain, w_up, w_down)


def _tile(n, pref):
    return pref if n % pref == 0 else n


def _tiles(n, s):
    return dict(
        proj_rows=_tile(n, 1024), proj_cols=768,
        proj_group=2 if n % 2048 == 0 else 1,
        hgrn_rows=_tile(s, 1024),
        dsa_block=_tile(s, 512),
        out_rows=_tile(n, 512),
        cross_rows=_tile(s, 1024),
        mlp_group=_tile(n, 2048), mlp_norm_rows=256, mlp_ff=512)


def kernel(x, mem, norm_mix, w_in, hgrn_lb_logits, hgrn_onorm, attn_qnorm, attn_knorm, w_out,
           norm_cross, mem_norm, wq_x, wk_x, wv_x, wo_x, xq_norm, xk_norm,
           norm_mlp, w_up, w_down):
    b, s, d = x.shape
    n = b * s
    depth = w_in.shape[0]
    assert depth == 1
    l = 0
    in_width = w_in.shape[2]
    main_w = 8 * HGRN_HEADS * HGRN_KDIM
    assert in_width == main_w + IDX_DIM + IDX_HEADS
    pad_w = main_w + TAIL

    t = _tiles(n, s)
    x2 = x.reshape(n, d)

    proj = norm_matmul(x2, norm_mix[l:l + 1], w_in[l].T, pad_w,
                       t["proj_rows"], t["proj_cols"], t["proj_group"])
    proj3 = proj.reshape(b, s, pad_w)

    y_h = hgrn_group(proj3, hgrn_lb_logits, hgrn_onorm[l:l + 1], t["hgrn_rows"])
    kn, ikd = dsa_prep(proj3, attn_knorm[l:l + 1], t["dsa_block"])
    y_a = dsa_group(proj3, kn, ikd, attn_qnorm[l:l + 1], attn_knorm[l:l + 1], t["dsa_block"])

    h1 = out_proj(y_h.reshape(n, -1), y_a.reshape(n, -1), w_out, l, x2, t["out_rows"])

    kx, vx = mem_kv(mem, mem_norm[l:l + 1], wk_x, wv_x, l, xk_norm[l:l + 1])
    h2 = cross_attention(h1.reshape(b, s, d), norm_cross[l:l + 1], wq_x, xq_norm[l:l + 1],
                         kx, vx, wo_x, l, t["cross_rows"])

    h3 = mlp(h2.reshape(n, d), norm_mlp[l:l + 1], w_up, w_down, l,
             t["mlp_group"], t["mlp_norm_rows"], t["mlp_ff"])
    return h3.reshape(b, s, d)
```

```python
import functools

import numpy as np
import jax
import jax.numpy as jnp
from jax import lax
from jax.experimental import pallas as pl
from jax.experimental.pallas import tpu as pltpu

F32 = jnp.float32
BF16 = jnp.bfloat16
EPS = 1e-6

LANES = 128
HGRN_HEADS = 8
HGRN_KDIM = 128
HGRN_CHUNK = 64
ATTN_HEADS = 8
ATTN_HEAD_DIM = 128
IDX_HEADS = 16
IDX_DIM = 64
DSA_TOPK = 256
CROSS_HEADS = 4
CROSS_HEAD_DIM = 128
TAIL = 256
V7X_VMEM_BYTES = 64 * 1024 * 1024
VMEM_LIMIT = V7X_VMEM_BYTES * 7 // 8
NEG_INF = float("-inf")
INT_MIN = -(2 ** 31)
LOG2E = 1.4426950408889634
SOFTMAX_FIXED_SHIFT_MAX = 40.0


def _params(*sem):
    return pltpu.CompilerParams(dimension_semantics=sem, vmem_limit_bytes=VMEM_LIMIT)


def _rms(x, gain):
    return x * lax.rsqrt(jnp.mean(x * x, axis=-1, keepdims=True) + EPS) * gain


def _sigmoid(x):
    return 1.0 / (1.0 + jnp.exp(-x))


def _dot(a, b):
    return jnp.dot(a, b, preferred_element_type=F32)


def _dot_nt(a, b):
    return lax.dot_general(a, b, (((1,), (1,)), ((), ())), preferred_element_type=F32)


def _dot_tn(a, b):
    return lax.dot_general(a, b, (((0,), (0,)), ((), ())), preferred_element_type=F32)


def _norm_rows_to(x_ref, gain, dst_ref, rows):
    step = 256
    def body(r, c):
        r0 = pl.multiple_of(r * step, step)
        x = x_ref[pl.ds(r0, step), :]
        dst_ref[pl.ds(r0, step), :] = _rms(x, gain).astype(BF16)
        return c
    lax.fori_loop(0, rows // step, body, 0)


def _norm_matmul_kernel(kind_ref, g_tbl, r_tbl, j_tbl, x_ref, gain_ref, wt_ref, o_ref, xn_ref, *, valid_cols):
    s = pl.program_id(0)
    tm = x_ref.shape[0]
    tn = o_ref.shape[1]

    @pl.when(kind_ref[s] == 0)
    def _():
        r0 = pl.multiple_of(r_tbl[s] * tm, tm)
        _norm_rows_to(x_ref, gain_ref[...], xn_ref.at[pl.ds(r0, tm)], tm)

    @pl.when(kind_ref[s] == 1)
    def _():
        res = _dot_nt(xn_ref[...], wt_ref[...].astype(BF16))
        col = j_tbl[s] * tn + lax.broadcasted_iota(jnp.int32, (1, tn), 1)
        o_ref[...] = jnp.where(col < valid_cols, res, 0.0).astype(o_ref.dtype)


def norm_matmul(x, gain, wt, n, tm, tn, group):
    m, k = x.shape
    ngroups, ntiles = m // (group * tm), n // tn
    kind, g_tbl, r_tbl, j_tbl = [], [], [], []
    for g in range(ngroups):
        for r in range(group):
            kind.append(0); g_tbl.append(g); r_tbl.append(r); j_tbl.append(0)
        for j in range(ntiles):
            kind.append(1); g_tbl.append(g); r_tbl.append(group - 1); j_tbl.append(j)
    tables = [jnp.asarray(t, jnp.int32) for t in (kind, g_tbl, r_tbl, j_tbl)]
    grid_spec = pltpu.PrefetchScalarGridSpec(
        num_scalar_prefetch=4,
        grid=(len(kind),),
        in_specs=[pl.BlockSpec((tm, k), lambda s, kd, gt, rt, jt: (gt[s] * group + rt[s], 0)),
                  pl.BlockSpec((1, k), lambda s, kd, gt, rt, jt: (0, 0)),
                  pl.BlockSpec((tn, k), lambda s, kd, gt, rt, jt: (jt[s], 0))],
        out_specs=pl.BlockSpec((group * tm, tn), lambda s, kd, gt, rt, jt: (gt[s], jt[s])),
        scratch_shapes=[pltpu.VMEM((group * tm, k), BF16)])
    return pl.pallas_call(
        functools.partial(_norm_matmul_kernel, valid_cols=wt.shape[0]),
        grid_spec=grid_spec,
        out_shape=jax.ShapeDtypeStruct((m, n), BF16),
        compiler_params=_params("arbitrary"),
        name="in_proj",
    )(*tables, x, gain, wt)


def _hgrn_kernel(q_ref, f_ref, i_ref, g_ref, lbl_ref, on_ref, tril_ref, o_ref,
                 state_ref, a_ref, qi_ref, ks_ref, el_ref, qt_ref, kt_ref, sc_ref, scb_ref, *, nchunks):
    c = HGRN_CHUNK
    dk = HGRN_KDIM
    pw = 2 * dk
    npairs = HGRN_HEADS // 2
    nlev = 6

    @pl.when(pl.program_id(1) == 0)
    def _():
        state_ref[...] = jnp.zeros_like(state_ref)
        kt_ref[...] = jnp.zeros_like(kt_ref)

    lbl = lbl_ref[...]
    e = jnp.exp(lbl - jnp.max(lbl, axis=0, keepdims=True))
    lb = e[0:1] / jnp.sum(e, axis=0, keepdims=True)
    onorm = on_ref[...]
    tril = tril_ref[...]

    row = lax.broadcasted_iota(jnp.int32, (c, 2 * c), 0)
    lane = lax.broadcasted_iota(jnp.int32, (c, 2 * c), 1)
    col = lane & (c - 1)
    first = lane < c
    eye = row == col
    level_masks = []
    lg = nlev - 1
    while lg >= 0:
        level_masks.append(((row >> (lg + 1)) == (col >> (lg + 1)))
                           & (((row >> lg) & 1) == 1) & (((col >> lg) & 1) == 0))
        lg -= 1
    lane_p = lax.broadcasted_iota(jnp.int32, (c, pw), 1)
    odd_row = (lax.broadcasted_iota(jnp.int32, (c, pw), 0) & 1) == 1
    sub8 = lax.broadcasted_iota(jnp.int32, (8, pw), 0)
    zeros_st = jnp.zeros((dk, dk), BF16)

    def chunk_body(ci, carry):
        r0 = pl.multiple_of(ci * c, c)

        def _decay_factors():
            hq = q_ref[0, pl.ds(r0, c), :].astype(F32)
            hf = f_ref[0, pl.ds(r0, c), :].astype(F32)
            qf = hq * _sigmoid(hq) * (dk ** -0.5)
            f = lb + (1.0 - lb) * _sigmoid(hf)
            logf = jnp.log(f) * LOG2E
            kk = 1.0 - f
            g0 = logf.astype(BF16)
            g1 = (logf - g0.astype(F32)).astype(BF16)
            a_ref[...] = _dot(tril, g0) + _dot(tril, g1)
            for p in range(npairs):
                sl = slice(p * pw, (p + 1) * pw)
                a = a_ref[:, sl]
                row = lambda r, n: jnp.broadcast_to(a_ref[r:r + 1, sl], (n, pw))
                e0 = jnp.exp2(a)
                qp = qf[:, sl]
                kp = kk[:, sl]
                qi_ref[p] = (qp * e0).astype(BF16)
                ks_ref[p] = (kp * jnp.exp2(row(c - 1, c) - a)).astype(BF16)
                el_ref[p] = e0[c - 1:c]
                for l in range(nlev):
                    h = c >> (l + 1)
                    if h >= 4:
                        ref = jnp.concatenate([row(m * 2 * h + h - 1, 2 * h) for m in range(c // (2 * h))], axis=0)
                    elif h == 2:
                        ref = jnp.concatenate([jnp.where(sub8 < 4, row(8 * m + 1, 8), row(8 * m + 5, 8))
                                               for m in range(c // 8)], axis=0)
                    else:
                        ref = jnp.where(odd_row, pltpu.roll(a, 1, 0), a)
                    el = jnp.exp2(-jnp.abs(a - ref))
                    qt_ref[p, l] = (qp * el).astype(BF16)
                    kl = (kp * el).astype(BF16)
                    kt_ref[p, l, 0:c, 0:dk] = kl[:, 0:dk]
                    kt_ref[p, l, c:2 * c, dk:pw] = kl[:, dk:pw]
                qk = qp * kp
                diag = jnp.where(first, jnp.sum(qk[:, 0:dk], axis=1, keepdims=True),
                                 jnp.sum(qk[:, dk:pw], axis=1, keepdims=True))
                sc_ref[p] = jnp.where(eye, diag, 0.0)

        def _intra_chunk_scores():
            for p in range(npairs):
                sc = sc_ref[p]
                for l, msk in enumerate(level_masks):
                    sc = sc + jnp.where(msk, _dot_nt(qt_ref[p, l], kt_ref[p, l]), 0.0)
                scb_ref[p] = sc.astype(BF16)

        def _outputs_and_state():
            for p in range(npairs):
                sl = slice(p * pw, (p + 1) * pw)
                vp = i_ref[0, pl.ds(r0, c), sl]
                st_a = state_ref[2 * p]
                st_b = state_ref[2 * p + 1]
                st_bd = jnp.concatenate(
                    [jnp.concatenate([st_a.astype(BF16), zeros_st], axis=1),
                     jnp.concatenate([zeros_st, st_b.astype(BF16)], axis=1)], axis=0)
                v_bd = jnp.concatenate([jnp.where(lane_p < dk, vp, jnp.zeros_like(vp)),
                                        jnp.where(lane_p >= dk, vp, jnp.zeros_like(vp))], axis=0)
                o = _dot_nt(qi_ref[p], st_bd) + _dot(scb_ref[p], v_bd)
                upd = _dot_tn(vp, ks_ref[p])
                el = el_ref[p]
                state_ref[2 * p] = st_a * el[:, 0:dk] + upd[0:dk, 0:dk]
                state_ref[2 * p + 1] = st_b * el[:, dk:pw] + upd[dk:pw, dk:pw]
                gate = g_ref[0, pl.ds(r0, c), sl].astype(F32)
                y = jnp.concatenate([_rms(o[:, 0:dk], onorm), _rms(o[:, dk:pw], onorm)], axis=1)
                o_ref[0, pl.ds(r0, c), sl] = (y * (gate * _sigmoid(gate))).astype(o_ref.dtype)

        _decay_factors()
        _intra_chunk_scores()
        _outputs_and_state()
        return carry

    lax.fori_loop(0, nchunks, chunk_body, 0, unroll=4)


def hgrn_group(proj, lb_logits, onorm, t_blk):
    b, s, _ = proj.shape
    w = HGRN_HEADS * HGRN_KDIM
    tril = jnp.asarray(np.tril(np.ones((HGRN_CHUNK, HGRN_CHUNK), np.float32)), BF16)
    c, pw, npairs, nlev = HGRN_CHUNK, 2 * HGRN_KDIM, HGRN_HEADS // 2, 6
    col = lambda cb: pl.BlockSpec((1, t_blk, w), lambda bi, ti, cb=cb: (bi, ti, cb))
    return pl.pallas_call(
        functools.partial(_hgrn_kernel, nchunks=t_blk // HGRN_CHUNK),
        grid=(b, s // t_blk),
        in_specs=[col(0), col(1), col(2), col(3),
                  pl.BlockSpec(lb_logits.shape, lambda bi, ti: (0, 0)),
                  pl.BlockSpec((1, HGRN_KDIM), lambda bi, ti: (0, 0)),
                  pl.BlockSpec(tril.shape, lambda bi, ti: (0, 0))],
        out_specs=pl.BlockSpec((1, t_blk, w), lambda bi, ti: (bi, ti, 0)),
        out_shape=jax.ShapeDtypeStruct((b, s, w), BF16),
        scratch_shapes=[pltpu.VMEM((HGRN_HEADS, HGRN_KDIM, HGRN_KDIM), F32),
                        pltpu.VMEM((c, w), F32),
                        pltpu.VMEM((npairs, c, pw), BF16),
                        pltpu.VMEM((npairs, c, pw), BF16),
                        pltpu.VMEM((npairs, 1, pw), F32),
                        pltpu.VMEM((npairs, nlev, c, pw), BF16),
                        pltpu.VMEM((npairs, nlev, 2 * c, pw), BF16),
                        pltpu.VMEM((npairs, c, 2 * c), F32),
                        pltpu.VMEM((npairs, c, 2 * c), BF16)],
        compiler_params=_params("parallel", "arbitrary"),
        name="hgrn2",
    )(proj, proj, proj, proj, lb_logits, onorm, tril)


def _dsa_prep_kernel(k_ref, tail_ref, kg_ref, dup_ref, kn_ref, ikd_ref):
    kg = kg_ref[...]
    for h in range(ATTN_HEADS):
        sl = slice(h * ATTN_HEAD_DIM, (h + 1) * ATTN_HEAD_DIM)
        kn_ref[0, :, sl] = _rms(k_ref[0, :, sl].astype(F32), kg).astype(BF16)
    ikd_ref[0] = _dot(tail_ref[0], dup_ref[...]).astype(BF16)


def dsa_prep(proj, knorm, tm):
    b, s, _ = proj.shape
    w = ATTN_HEADS * ATTN_HEAD_DIM
    dup = np.zeros((TAIL, LANES), np.float32)
    dup[np.arange(IDX_DIM), np.arange(IDX_DIM)] = 1.0
    dup[np.arange(IDX_DIM), np.arange(IDX_DIM) + IDX_DIM] = 1.0
    return pl.pallas_call(
        _dsa_prep_kernel,
        grid=(b, s // tm),
        in_specs=[pl.BlockSpec((1, tm, w), lambda bi, i: (bi, i, 5)),
                  pl.BlockSpec((1, tm, TAIL), lambda bi, i: (bi, i, 8 * w // TAIL)),
                  pl.BlockSpec((1, ATTN_HEAD_DIM), lambda bi, i: (0, 0)),
                  pl.BlockSpec((TAIL, LANES), lambda bi, i: (0, 0))],
        out_specs=[pl.BlockSpec((1, tm, w), lambda bi, i: (bi, i, 0)),
                   pl.BlockSpec((1, tm, LANES), lambda bi, i: (bi, i, 0))],
        out_shape=[jax.ShapeDtypeStruct((b, s, w), BF16),
                   jax.ShapeDtypeStruct((b, s, LANES), BF16)],
        compiler_params=_params("parallel", "parallel"),
        name="dsa_prep",
    )(proj, proj, knorm, jnp.asarray(dup, BF16))


def _dsa_kernel(pi_ref, pj_ref, aq_ref, iq_ref, tail_ref, ikd_ref, kn_ref, v_ref, qg_ref, eq_ref, esel_ref,
                smax_ref, o_ref, keys_ref, thr_ref, qn_ref, qw_ref, lo_ref, hi_ref, m_ref, l_ref, acc_ref, p_ref,
                *, tq, tk, topk):
    step = pl.program_id(1)
    i = pi_ref[step]
    j = pj_ref[step]
    dh = ATTN_HEAD_DIM

    def causal(jj):
        s_pos = jj * tk + lax.broadcasted_iota(jnp.int32, (tk, tq), 0)
        t_pos = i * tq + lax.broadcasted_iota(jnp.int32, (tk, tq), 1)
        return s_pos <= t_pos

    @pl.when(j == 0)
    def _index_and_select():
        qg = qg_ref[...] * (dh ** -0.5 * LOG2E)
        for h in range(ATTN_HEADS):
            sl = slice(h * dh, (h + 1) * dh)
            qn_ref[:, sl] = _rms(aq_ref[0, :, sl].astype(F32), qg).astype(BF16)
        tail = tail_ref[0]
        ww = _dot(tail, eq_ref[...])
        qw = (iq_ref[0].astype(F32) * ww * (IDX_HEADS ** -0.5 * IDX_DIM ** -0.5)).astype(BF16)
        lane = lax.broadcasted_iota(jnp.int32, (tq, LANES), 1)
        zero = jnp.zeros((tq, LANES), BF16)
        for p in range(IDX_HEADS // 2):
            pair = qw[:, p * LANES:(p + 1) * LANES]
            qw_ref[2 * p] = jnp.where(lane < IDX_DIM, pair, zero)
            qw_ref[2 * p + 1] = jnp.where(lane >= IDX_DIM, pair, zero)
        w_t = _dot_nt(esel_ref[...], tail)
        lo_ref[...] = jnp.where(w_t > 0.0, 0.0, NEG_INF)
        hi_ref[...] = jnp.where(w_t > 0.0, jnp.inf, 0.0)

        def score_tile(jj, c):
            r0 = pl.multiple_of(jj * tk, tk)
            ik = ikd_ref[0, pl.ds(r0, tk), :]
            comb = jnp.zeros((tk, tq), F32)
            for h in range(IDX_HEADS):
                x = _dot_nt(ik, qw_ref[h])
                comb = comb + jnp.minimum(jnp.maximum(x, lo_ref[h:h + 1, :]), hi_ref[h:h + 1, :])
            keys_ref[jj] = jnp.where(causal(jj), comb, NEG_INF)
            return c
        lax.fori_loop(0, i + 1, score_tile, 0)

        def as_float(t):
            return pltpu.bitcast(jnp.where(t < 0, t ^ 0x7FFFFFFF, t), F32)
        def count_ge(tf):
            def body(jj, acc):
                accs = [acc, jnp.zeros_like(acc), jnp.zeros_like(acc), jnp.zeros_like(acc)]
                for r0 in range(0, tk, 8):
                    a = accs[(r0 // 8) % 4]
                    accs[(r0 // 8) % 4] = jnp.where(keys_ref[jj, r0:r0 + 8, :] >= tf, a + 1, a)
                return (accs[0] + accs[1]) + (accs[2] + accs[3])
            acc = lax.fori_loop(0, i + 1, body, jnp.zeros((8, tq), jnp.int32))
            return jnp.sum(acc.astype(F32), axis=0, keepdims=True)
        def unresolved(carry):
            bi, _, cnt = carry
            return jnp.logical_and(bi < 32, jnp.max(jnp.abs(cnt - topk)) > 0.0)
        def bit_steps(carry):
            bi, t, cnt = carry
            nbits = jnp.where(bi == 0, 24, 4)
            def one_bit(k, tc_cnt):
                t, cnt = tc_cnt
                tc = t + (jnp.int32(1) << (31 - bi - k))
                c = count_ge(as_float(tc))
                take = c >= topk
                return jnp.where(take, tc, t), jnp.where(take, c, cnt)
            t, cnt = lax.fori_loop(0, nbits, one_bit, (t, cnt))
            return bi + nbits, t, cnt
        n_all = ((i + 1) * tk).astype(F32)
        _, t_fin, _ = lax.while_loop(unresolved, bit_steps,
                                     (jnp.int32(0), jnp.full((1, tq), INT_MIN, jnp.int32),
                                      jnp.full((1, tq), n_all, F32)))
        t_float = as_float(t_fin)
        thr_ref[...] = jnp.where(t_float != t_float, NEG_INF, t_float)

        m_ref[...] = jnp.full_like(m_ref, NEG_INF)
        l_ref[...] = jnp.zeros_like(l_ref)
        acc_ref[...] = jnp.zeros_like(acc_ref)

    sel = (keys_ref[j] >= thr_ref[...]) & causal(j)
    smax = smax_ref[0]
    fixed_shift = smax <= SOFTMAX_FIXED_SHIFT_MAX

    @pl.when(fixed_shift)
    def _attend_fixed_shift():
        bias = jnp.where(sel, -smax * LOG2E, NEG_INF)
        for h in range(ATTN_HEADS):
            sl = slice(h * dh, (h + 1) * dh)
            p = jnp.exp2(_dot_nt(kn_ref[0, :, sl], qn_ref[:, sl]) + bias)
            parts = [p[r * 8:(r + 1) * 8] for r in range(tk // 8)]
            while len(parts) > 1:
                parts = [a + b for a, b in zip(parts[0::2], parts[1::2])]
            l_ref[h] += parts[0]
            p_ref[h] = p.astype(BF16)

    @pl.when(jnp.logical_and(fixed_shift, j <= i))
    def _weighted_values():
        for h in range(ATTN_HEADS):
            sl = slice(h * dh, (h + 1) * dh)
            acc_ref[h] += _dot_tn(v_ref[0, :, sl], p_ref[h])

    @pl.when(jnp.logical_not(fixed_shift))
    def _attend_running_max():
        bias = jnp.where(sel, 0.0, NEG_INF)
        m_all = m_ref[...]
        m_rows = []
        for h in range(ATTN_HEADS):
            sl = slice(h * dh, (h + 1) * dh)
            s = _dot_nt(kn_ref[0, :, sl], qn_ref[:, sl]) + bias
            m_old = m_all[h:h + 1, :]
            m_new = jnp.maximum(m_old, jnp.max(s, axis=0, keepdims=True))
            m_safe = jnp.where(m_new == NEG_INF, 0.0, m_new)
            alpha = jnp.exp2(m_old - m_safe)
            p = jnp.exp2(s - m_safe)
            l_ref[h, 0:1, :] = alpha * l_ref[h, 0:1, :] + jnp.sum(p, axis=0, keepdims=True)
            acc_ref[h] = alpha * acc_ref[h] + _dot_tn(v_ref[0, :, sl], p.astype(BF16))
            m_rows.append(m_new)
        m_ref[...] = jnp.concatenate(m_rows, axis=0)

    @pl.when(j == i)
    def _finish():
        for h in range(ATTN_HEADS):
            sl = slice(h * dh, (h + 1) * dh)
            l = jnp.sum(l_ref[h], axis=0, keepdims=True)
            o_ref[0, :, sl] = (acc_ref[h] / l).T.astype(o_ref.dtype)


def dsa_group(proj, kn, ikd, qnorm, knorm, tq):
    b, s, _ = proj.shape
    tk = tq
    smax = (ATTN_HEAD_DIM ** 0.5 * jnp.max(jnp.abs(qnorm)) * jnp.max(jnp.abs(knorm))).reshape(1).astype(F32)
    nq = s // tq
    w = ATTN_HEADS * ATTN_HEAD_DIM
    topk = min(DSA_TOPK, s // 4)
    eq = np.zeros((TAIL, IDX_HEADS * IDX_DIM), np.float32)
    esel = np.zeros((IDX_HEADS, TAIL), np.float32)
    for h in range(IDX_HEADS):
        eq[IDX_DIM + h, h * IDX_DIM:(h + 1) * IDX_DIM] = 1.0
        esel[h, IDX_DIM + h] = 1.0
    pairs = [(i, j) for i in range(nq) for j in range(i + 1)]
    pi = jnp.asarray([p[0] for p in pairs], jnp.int32)
    pj = jnp.asarray([p[1] for p in pairs], jnp.int32)
    qblk = lambda cb: pl.BlockSpec((1, tq, w), lambda bi, st, pi, pj, cb=cb: (bi, pi[st], cb))
    const = lambda a: pl.BlockSpec(a.shape, lambda bi, st, pi, pj: (0,) * a.ndim)
    grid_spec = pltpu.PrefetchScalarGridSpec(
        num_scalar_prefetch=2,
        grid=(b, len(pairs)),
        in_specs=[qblk(4),
                  qblk(7),
                  pl.BlockSpec((1, tq, TAIL), lambda bi, st, pi, pj: (bi, pi[st], 8 * w // TAIL)),
                  pl.BlockSpec((1, s, LANES), lambda bi, st, pi, pj: (bi, 0, 0)),
                  pl.BlockSpec((1, tk, w), lambda bi, st, pi, pj: (bi, pj[st], 0)),
                  pl.BlockSpec((1, tk, w), lambda bi, st, pi, pj: (bi, pj[st], 6)),
                  const(qnorm), const(eq), const(esel),
                  pl.BlockSpec(memory_space=pltpu.SMEM)],
        out_specs=pl.BlockSpec((1, tq, w), lambda bi, st, pi, pj: (bi, pi[st], 0)),
        scratch_shapes=[pltpu.VMEM((nq, tk, tq), F32),
                        pltpu.VMEM((1, tq), F32),
                        pltpu.VMEM((tq, w), BF16),
                        pltpu.VMEM((IDX_HEADS, tq, LANES), BF16),
                        pltpu.VMEM((IDX_HEADS, tq), F32),
                        pltpu.VMEM((IDX_HEADS, tq), F32),
                        pltpu.VMEM((ATTN_HEADS, tq), F32),
                        pltpu.VMEM((ATTN_HEADS, 8, tq), F32),
                        pltpu.VMEM((ATTN_HEADS, ATTN_HEAD_DIM, tq), F32),
                        pltpu.VMEM((ATTN_HEADS, tk, tq), BF16)])
    return pl.pallas_call(
        functools.partial(_dsa_kernel, tq=tq, tk=tk, topk=topk),
        grid_spec=grid_spec,
        out_shape=jax.ShapeDtypeStruct((b, s, w), BF16),
        compiler_params=_params("parallel", "arbitrary"),
        name="dsa",
    )(pi, pj, proj, proj, proj, ikd, kn, proj, qnorm, jnp.asarray(eq, BF16), jnp.asarray(esel, BF16), smax)


def _cast_rows_to(src_ref, dst_ref, step=256):
    def body(r, c):
        r0 = pl.multiple_of(r * step, step)
        dst_ref[pl.ds(r0, step), :] = src_ref[0, pl.ds(r0, step), :].astype(BF16)
        return c
    lax.fori_loop(0, dst_ref.shape[0] // step, body, 0)


def _out_proj_kernel(yh_ref, ya_ref, w_ref, x_ref, o_ref, wb_ref):
    @pl.when(pl.program_id(0) == 0)
    def _():
        _cast_rows_to(w_ref, wb_ref)
    kh = yh_ref.shape[1]
    o_ref[...] = x_ref[...] + _dot(yh_ref[...], wb_ref[0:kh, :]) + _dot(ya_ref[...], wb_ref[kh:2 * kh, :])


def out_proj(yh, ya, w_out, layer, x, tm):
    m, kh = yh.shape
    _, k, n = w_out.shape
    return pl.pallas_call(
        _out_proj_kernel,
        grid=(m // tm,),
        in_specs=[pl.BlockSpec((tm, kh), lambda i: (i, 0)),
                  pl.BlockSpec((tm, kh), lambda i: (i, 0)),
                  pl.BlockSpec((1, k, n), lambda i: (layer, 0, 0), pipeline_mode=pl.Buffered(1)),
                  pl.BlockSpec((tm, n), lambda i: (i, 0))],
        out_specs=pl.BlockSpec((tm, n), lambda i: (i, 0)),
        out_shape=jax.ShapeDtypeStruct((m, n), F32),
        scratch_shapes=[pltpu.VMEM((k, n), BF16)],
        compiler_params=_params("arbitrary"),
        name="out_proj",
    )(yh, ya, w_out, x)


def _mem_kv_kernel(mem_ref, mg_ref, wk_ref, wv_ref, kg_ref, k_ref, v_ref):
    memn = _rms(mem_ref[0], mg_ref[...]).astype(BF16)
    k = _dot(memn, wk_ref[0].astype(BF16))
    kg = kg_ref[...]
    for h in range(CROSS_HEADS):
        sl = slice(h * CROSS_HEAD_DIM, (h + 1) * CROSS_HEAD_DIM)
        k_ref[0, :, sl] = _rms(k[:, sl], kg).astype(BF16)
    v_ref[0] = _dot(memn, wv_ref[0].astype(BF16)).astype(BF16)


def mem_kv(mem, mem_norm, wk, wv, layer, xk_norm):
    b, nm, d = mem.shape
    cw = wk.shape[2]
    full = lambda a: pl.BlockSpec(a.shape, lambda bi: (0,) * a.ndim)
    wspec = pl.BlockSpec((1, d, cw), lambda bi: (layer, 0, 0))
    return pl.pallas_call(
        _mem_kv_kernel,
        grid=(b,),
        in_specs=[pl.BlockSpec((1, nm, d), lambda bi: (bi, 0, 0)),
                  full(mem_norm), wspec, wspec, full(xk_norm)],
        out_specs=[pl.BlockSpec((1, nm, cw), lambda bi: (bi, 0, 0))] * 2,
        out_shape=[jax.ShapeDtypeStruct((b, nm, cw), BF16)] * 2,
        compiler_params=_params("parallel"),
        name="mem_kv",
    )(mem, mem_norm, wk, wv, xk_norm)


def _cross_kernel(h_ref, ng_ref, wq_ref, qg_ref, k_ref, v_ref, wo_ref, o_ref, hn_ref, oc_ref, wqb_ref, wob_ref):
    @pl.when((pl.program_id(0) == 0) & (pl.program_id(1) == 0))
    def _():
        _cast_rows_to(wq_ref, wqb_ref)
        _cast_rows_to(wo_ref, wob_ref)
    rows = h_ref.shape[1]
    dh = CROSS_HEAD_DIM
    step = 256
    gain = ng_ref[...]
    def body(r, c):
        r0 = pl.multiple_of(r * step, step)
        hn_ref[pl.ds(r0, step), :] = _rms(h_ref[0, pl.ds(r0, step), :], gain).astype(BF16)
        return c
    lax.fori_loop(0, rows // step, body, 0)
    q = _dot(hn_ref[...], wqb_ref[...])
    qg = qg_ref[...] * (dh ** -0.5)
    for h in range(CROSS_HEADS):
        sl = slice(h * dh, (h + 1) * dh)
        qn = _rms(q[:, sl], qg).astype(BF16)
        s = _dot_nt(qn, k_ref[0, :, sl])
        p = jnp.exp(s - jnp.max(s, axis=1, keepdims=True))
        l = jnp.sum(p, axis=1, keepdims=True)
        oc_ref[:, sl] = (_dot(p.astype(BF16), v_ref[0, :, sl]) / l).astype(BF16)
    o_ref[0] = h_ref[0] + _dot(oc_ref[...], wob_ref[...])


def cross_attention(h, norm_cross, wq, xq_norm, kx, vx, wo, layer, tm):
    b, s, d = h.shape
    nm, cw = kx.shape[1:]
    full = lambda a: pl.BlockSpec(a.shape, lambda bi, i: (0,) * a.ndim)
    resident = lambda a: pl.BlockSpec((1,) + a.shape[1:], lambda bi, i: (layer, 0, 0),
                                      pipeline_mode=pl.Buffered(1))
    return pl.pallas_call(
        _cross_kernel,
        grid=(b, s // tm),
        in_specs=[pl.BlockSpec((1, tm, d), lambda bi, i: (bi, i, 0)),
                  full(norm_cross), resident(wq), full(xq_norm),
                  pl.BlockSpec((1, nm, cw), lambda bi, i: (bi, 0, 0)),
                  pl.BlockSpec((1, nm, cw), lambda bi, i: (bi, 0, 0)),
                  resident(wo)],
        out_specs=pl.BlockSpec((1, tm, d), lambda bi, i: (bi, i, 0)),
        out_shape=jax.ShapeDtypeStruct((b, s, d), F32),
        scratch_shapes=[pltpu.VMEM((tm, d), BF16), pltpu.VMEM((tm, cw), BF16),
                        pltpu.VMEM(wq.shape[1:], BF16), pltpu.VMEM(wo.shape[1:], BF16)],
        compiler_params=_params("arbitrary", "arbitrary"),
        name="cross_attn",
    )(h, norm_cross, wq, xq_norm, kx, vx, wo)


MLP_WEIGHT_SLOTS = 3


def _mlp_kernel(h_ref, g_ref, wu_hbm, wd_hbm, o_ref, hn_ref, wu_buf, wd_buf, sem, *, layer, tf):
    nff = pl.num_programs(1)
    t = pl.program_id(0) * nff + pl.program_id(1)
    last = pl.num_programs(0) * nff - 1

    def tile_copies(step):
        slot = step % MLP_WEIGHT_SLOTS
        c0 = pl.multiple_of((step % nff) * tf, tf)
        return (pltpu.make_async_copy(wu_hbm.at[layer, :, pl.ds(c0, tf)], wu_buf.at[slot], sem.at[0, slot]),
                pltpu.make_async_copy(wd_hbm.at[layer, pl.ds(c0, tf), :], wd_buf.at[slot], sem.at[1, slot]))

    def start(step):
        for c in tile_copies(step):
            c.start()

    @pl.when(t == 0)
    def _():
        start(t)
        start(t + 1)

    @pl.when(t + 2 <= last)
    def _():
        start(t + 2)

    @pl.when(pl.program_id(1) == 0)
    def _():
        _norm_rows_to(h_ref, g_ref[...], hn_ref, h_ref.shape[0])
        o_ref[...] = h_ref[...]

    for c in tile_copies(t):
        c.wait()
    slot = t % MLP_WEIGHT_SLOTS
    u = jnp.maximum(_dot(hn_ref[...], wu_buf[slot].astype(BF16)), 0.0)
    o_ref[...] += _dot((u * u).astype(BF16), wd_buf[slot].astype(BF16))


def mlp(h, gain, w_up, w_down, layer, tm, tf):
    m, d = h.shape
    f = w_up.shape[2]
    assert (m // tm) * (f // tf) >= 2
    return pl.pallas_call(
        functools.partial(_mlp_kernel, layer=layer, tf=tf),
        grid=(m // tm, f // tf),
        in_specs=[pl.BlockSpec((tm, d), lambda i, j: (i, 0)),
                  pl.BlockSpec((1, d), lambda i, j: (0, 0)),
                  pl.BlockSpec(memory_space=pl.ANY),
                  pl.BlockSpec(memory_space=pl.ANY)],
        out_specs=pl.BlockSpec((tm, d), lambda i, j: (i, 0), pipeline_mode=pl.Buffered(1)),
        out_shape=jax.ShapeDtypeStruct((m, d), F32),
        scratch_shapes=[pltpu.VMEM((tm, d), BF16),
                        pltpu.VMEM((MLP_WEIGHT_SLOTS, d, tf), F32),
                        pltpu.VMEM((MLP_WEIGHT_SLOTS, tf, d), F32),
                        pltpu.SemaphoreType.DMA((2, MLP_WEIGHT_SLOTS))],
        compiler_params=_params("arbitrary", "arbitrary"),
        name="mlp",
    )(h, gain, w_up, w_down)


def _tile(n, pref):
    return pref if n % pref == 0 else n


def _tiles(n, s):
    return dict(
        proj_rows=_tile(n, 1024), proj_cols=768,
        proj_group=2 if n % 2048 == 0 else 1,
        hgrn_rows=_tile(s, 1024),
        dsa_block=_tile(s, 512),
        out_rows=_tile(n, 512),
        cross_rows=_tile(s, 1024),
        mlp_rows=_tile(n, 1024), mlp_ff=512)


def kernel(x, mem, norm_mix, w_in, hgrn_lb_logits, hgrn_onorm, attn_qnorm, attn_knorm, w_out,
           norm_cross, mem_norm, wq_x, wk_x, wv_x, wo_x, xq_norm, xk_norm,
           norm_mlp, w_up, w_down):
    b, s, d = x.shape
    n = b * s
    depth = w_in.shape[0]
    assert depth == 1
    l = 0
    in_width = w_in.shape[2]
    main_w = 8 * HGRN_HEADS * HGRN_KDIM
    assert in_width == main_w + IDX_DIM + IDX_HEADS
    pad_w = main_w + TAIL

    t = _tiles(n, s)
    x2 = x.reshape(n, d)

    proj = norm_matmul(x2, norm_mix[l:l + 1], w_in[l].T, pad_w,
                       t["proj_rows"], t["proj_cols"], t["proj_group"])
    proj3 = proj.reshape(b, s, pad_w)

    y_h = hgrn_group(proj3, hgrn_lb_logits, hgrn_onorm[l:l + 1], t["hgrn_rows"])
    kn, ikd = dsa_prep(proj3, attn_knorm[l:l + 1], t["dsa_block"])
    y_a = dsa_group(proj3, kn, ikd, attn_qnorm[l:l + 1], attn_knorm[l:l + 1], t["dsa_block"])

    h1 = out_proj(y_h.reshape(n, -1), y_a.reshape(n, -1), w_out, l, x2, t["out_rows"])

    kx, vx = mem_kv(mem, mem_norm[l:l + 1], wk_x, wv_x, l, xk_norm[l:l + 1])
    h2 = cross_attention(h1.reshape(b, s, d), norm_cross[l:l + 1], wq_x, xq_norm[l:l + 1],
                         kx, vx, wo_x, l, t["cross_rows"])

    h3 = mlp(h2.reshape(n, d), norm_mlp[l:l + 1], w_up, w_down, l, t["mlp_rows"], t["mlp_ff"])
    return h3.reshape(b, s, d)
```

```python
import functools

import numpy as np
import jax
import jax.numpy as jnp
from jax import lax
from jax.experimental import pallas as pl
from jax.experimental.pallas import tpu as pltpu

F32 = jnp.float32
BF16 = jnp.bfloat16
EPS = 1e-6

LANES = 128
HGRN_HEADS = 8
HGRN_KDIM = 128
HGRN_CHUNK = 64
ATTN_HEADS = 8
ATTN_HEAD_DIM = 128
IDX_HEADS = 16
IDX_DIM = 64
DSA_TOPK = 256
CROSS_HEADS = 4
CROSS_HEAD_DIM = 128
TAIL = 256
V7X_VMEM_BYTES = 64 * 1024 * 1024
VMEM_LIMIT = V7X_VMEM_BYTES * 7 // 8
NEG_INF = float("-inf")
INT_MIN = -(2 ** 31)
LOG2E = 1.4426950408889634
SOFTMAX_FIXED_SHIFT_MAX = 40.0


def _params(*sem):
    return pltpu.CompilerParams(dimension_semantics=sem, vmem_limit_bytes=VMEM_LIMIT)


def _rms(x, gain):
    return x * lax.rsqrt(jnp.mean(x * x, axis=-1, keepdims=True) + EPS) * gain


def _sigmoid(x):
    return 1.0 / (1.0 + jnp.exp(-x))


def _dot(a, b):
    return jnp.dot(a, b, preferred_element_type=F32)


def _dot_nt(a, b):
    return lax.dot_general(a, b, (((1,), (1,)), ((), ())), preferred_element_type=F32)


def _dot_tn(a, b):
    return lax.dot_general(a, b, (((0,), (0,)), ((), ())), preferred_element_type=F32)


def _norm_rows_to(x_ref, gain, dst_ref, rows):
    step = 256
    def body(r, c):
        r0 = pl.multiple_of(r * step, step)
        x = x_ref[pl.ds(r0, step), :]
        dst_ref[pl.ds(r0, step), :] = _rms(x, gain).astype(BF16)
        return c
    lax.fori_loop(0, rows // step, body, 0)


def _norm_matmul_kernel(kind_ref, g_tbl, r_tbl, j_tbl, x_ref, gain_ref, wt_ref, o_ref, xn_ref, *, valid_cols):
    s = pl.program_id(0)
    tm = x_ref.shape[0]
    tn = o_ref.shape[1]

    @pl.when(kind_ref[s] == 0)
    def _():
        r0 = pl.multiple_of(r_tbl[s] * tm, tm)
        _norm_rows_to(x_ref, gain_ref[...], xn_ref.at[pl.ds(r0, tm)], tm)

    @pl.when(kind_ref[s] == 1)
    def _():
        res = _dot_nt(xn_ref[...], wt_ref[...].astype(BF16))
        col = j_tbl[s] * tn + lax.broadcasted_iota(jnp.int32, (1, tn), 1)
        o_ref[...] = jnp.where(col < valid_cols, res, 0.0).astype(o_ref.dtype)


def norm_matmul(x, gain, wt, n, tm, tn, group):
    m, k = x.shape
    ngroups, ntiles = m // (group * tm), n // tn
    kind, g_tbl, r_tbl, j_tbl = [], [], [], []
    for g in range(ngroups):
        for r in range(group):
            kind.append(0); g_tbl.append(g); r_tbl.append(r); j_tbl.append(0)
        for j in range(ntiles):
            kind.append(1); g_tbl.append(g); r_tbl.append(group - 1); j_tbl.append(j)
    tables = [jnp.asarray(t, jnp.int32) for t in (kind, g_tbl, r_tbl, j_tbl)]
    grid_spec = pltpu.PrefetchScalarGridSpec(
        num_scalar_prefetch=4,
        grid=(len(kind),),
        in_specs=[pl.BlockSpec((tm, k), lambda s, kd, gt, rt, jt: (gt[s] * group + rt[s], 0)),
                  pl.BlockSpec((1, k), lambda s, kd, gt, rt, jt: (0, 0)),
                  pl.BlockSpec((tn, k), lambda s, kd, gt, rt, jt: (jt[s], 0))],
        out_specs=pl.BlockSpec((group * tm, tn), lambda s, kd, gt, rt, jt: (gt[s], jt[s])),
        scratch_shapes=[pltpu.VMEM((group * tm, k), BF16)])
    return pl.pallas_call(
        functools.partial(_norm_matmul_kernel, valid_cols=wt.shape[0]),
        grid_spec=grid_spec,
        out_shape=jax.ShapeDtypeStruct((m, n), BF16),
        compiler_params=_params("arbitrary"),
        name="in_proj",
    )(*tables, x, gain, wt)


def _hgrn_kernel(q_ref, f_ref, i_ref, g_ref, lbl_ref, on_ref, tril_ref, o_ref,
                 state_ref, a_ref, qi_ref, ks_ref, el_ref, qt_ref, kt_ref, sc_ref, scb_ref, *, nchunks):
    c = HGRN_CHUNK
    dk = HGRN_KDIM
    pw = 2 * dk
    npairs = HGRN_HEADS // 2
    nlev = 6

    @pl.when(pl.program_id(1) == 0)
    def _():
        state_ref[...] = jnp.zeros_like(state_ref)
        kt_ref[...] = jnp.zeros_like(kt_ref)

    lbl = lbl_ref[...]
    e = jnp.exp(lbl - jnp.max(lbl, axis=0, keepdims=True))
    lb = e[0:1] / jnp.sum(e, axis=0, keepdims=True)
    onorm = on_ref[...]
    tril = tril_ref[...]

    row = lax.broadcasted_iota(jnp.int32, (c, 2 * c), 0)
    lane = lax.broadcasted_iota(jnp.int32, (c, 2 * c), 1)
    col = lane & (c - 1)
    first = lane < c
    eye = row == col
    level_masks = []
    lg = nlev - 1
    while lg >= 0:
        level_masks.append(((row >> (lg + 1)) == (col >> (lg + 1)))
                           & (((row >> lg) & 1) == 1) & (((col >> lg) & 1) == 0))
        lg -= 1
    lane_p = lax.broadcasted_iota(jnp.int32, (c, pw), 1)
    odd_row = (lax.broadcasted_iota(jnp.int32, (c, pw), 0) & 1) == 1
    sub8 = lax.broadcasted_iota(jnp.int32, (8, pw), 0)
    zeros_st = jnp.zeros((dk, dk), BF16)

    def chunk_body(ci, carry):
        r0 = pl.multiple_of(ci * c, c)

        def _decay_factors():
            hq = q_ref[0, pl.ds(r0, c), :].astype(F32)
            hf = f_ref[0, pl.ds(r0, c), :].astype(F32)
            qf = hq * _sigmoid(hq) * (dk ** -0.5)
            f = lb + (1.0 - lb) * _sigmoid(hf)
            logf = jnp.log(f) * LOG2E
            kk = 1.0 - f
            g0 = logf.astype(BF16)
            g1 = (logf - g0.astype(F32)).astype(BF16)
            a_ref[...] = _dot(tril, g0) + _dot(tril, g1)
            for p in range(npairs):
                sl = slice(p * pw, (p + 1) * pw)
                a = a_ref[:, sl]
                row = lambda r, n: jnp.broadcast_to(a_ref[r:r + 1, sl], (n, pw))
                e0 = jnp.exp2(a)
                qp = qf[:, sl]
                kp = kk[:, sl]
                qi_ref[p] = (qp * e0).astype(BF16)
                ks_ref[p] = (kp * jnp.exp2(row(c - 1, c) - a)).astype(BF16)
                el_ref[p] = e0[c - 1:c]
                for l in range(nlev):
                    h = c >> (l + 1)
                    if h >= 4:
                        ref = jnp.concatenate([row(m * 2 * h + h - 1, 2 * h) for m in range(c // (2 * h))], axis=0)
                    elif h == 2:
                        ref = jnp.concatenate([jnp.where(sub8 < 4, row(8 * m + 1, 8), row(8 * m + 5, 8))
                                               for m in range(c // 8)], axis=0)
                    else:
                        ref = jnp.where(odd_row, pltpu.roll(a, 1, 0), a)
                    el = jnp.exp2(-jnp.abs(a - ref))
                    qt_ref[p, l] = (qp * el).astype(BF16)
                    kl = (kp * el).astype(BF16)
                    kt_ref[p, l, 0:c, 0:dk] = kl[:, 0:dk]
                    kt_ref[p, l, c:2 * c, dk:pw] = kl[:, dk:pw]
                qk = qp * kp
                diag = jnp.where(first, jnp.sum(qk[:, 0:dk], axis=1, keepdims=True),
                                 jnp.sum(qk[:, dk:pw], axis=1, keepdims=True))
                sc_ref[p] = jnp.where(eye, diag, 0.0)

        def _intra_chunk_scores():
            for p in range(npairs):
                sc = sc_ref[p]
                for l, msk in enumerate(level_masks):
                    sc = sc + jnp.where(msk, _dot_nt(qt_ref[p, l], kt_ref[p, l]), 0.0)
                scb_ref[p] = sc.astype(BF16)

        def _outputs_and_state():
            for p in range(npairs):
                sl = slice(p * pw, (p + 1) * pw)
                vp = i_ref[0, pl.ds(r0, c), sl]
                st_a = state_ref[2 * p]
                st_b = state_ref[2 * p + 1]
                st_bd = jnp.concatenate(
                    [jnp.concatenate([st_a.astype(BF16), zeros_st], axis=1),
                     jnp.concatenate([zeros_st, st_b.astype(BF16)], axis=1)], axis=0)
                v_bd = jnp.concatenate([jnp.where(lane_p < dk, vp, jnp.zeros_like(vp)),
                                        jnp.where(lane_p >= dk, vp, jnp.zeros_like(vp))], axis=0)
                o = _dot_nt(qi_ref[p], st_bd) + _dot(scb_ref[p], v_bd)
                upd = _dot_tn(vp, ks_ref[p])
                el = el_ref[p]
                state_ref[2 * p] = st_a * el[:, 0:dk] + upd[0:dk, 0:dk]
                state_ref[2 * p + 1] = st_b * el[:, dk:pw] + upd[dk:pw, dk:pw]
                gate = g_ref[0, pl.ds(r0, c), sl].astype(F32)
                y = jnp.concatenate([_rms(o[:, 0:dk], onorm), _rms(o[:, dk:pw], onorm)], axis=1)
                o_ref[0, pl.ds(r0, c), sl] = (y * (gate * _sigmoid(gate))).astype(o_ref.dtype)

        _decay_factors()
        _intra_chunk_scores()
        _outputs_and_state()
        return carry

    lax.fori_loop(0, nchunks, chunk_body, 0, unroll=4)


def hgrn_group(proj, lb_logits, onorm, t_blk):
    b, s, _ = proj.shape
    w = HGRN_HEADS * HGRN_KDIM
    tril = jnp.asarray(np.tril(np.ones((HGRN_CHUNK, HGRN_CHUNK), np.float32)), BF16)
    c, pw, npairs, nlev = HGRN_CHUNK, 2 * HGRN_KDIM, HGRN_HEADS // 2, 6
    col = lambda cb: pl.BlockSpec((1, t_blk, w), lambda bi, ti, cb=cb: (bi, ti, cb))
    return pl.pallas_call(
        functools.partial(_hgrn_kernel, nchunks=t_blk // HGRN_CHUNK),
        grid=(b, s // t_blk),
        in_specs=[col(0), col(1), col(2), col(3),
                  pl.BlockSpec(lb_logits.shape, lambda bi, ti: (0, 0)),
                  pl.BlockSpec((1, HGRN_KDIM), lambda bi, ti: (0, 0)),
                  pl.BlockSpec(tril.shape, lambda bi, ti: (0, 0))],
        out_specs=pl.BlockSpec((1, t_blk, w), lambda bi, ti: (bi, ti, 0)),
        out_shape=jax.ShapeDtypeStruct((b, s, w), BF16),
        scratch_shapes=[pltpu.VMEM((HGRN_HEADS, HGRN_KDIM, HGRN_KDIM), F32),
                        pltpu.VMEM((c, w), F32),
                        pltpu.VMEM((npairs, c, pw), BF16),
                        pltpu.VMEM((npairs, c, pw), BF16),
                        pltpu.VMEM((npairs, 1, pw), F32),
                        pltpu.VMEM((npairs, nlev, c, pw), BF16),
                        pltpu.VMEM((npairs, nlev, 2 * c, pw), BF16),
                        pltpu.VMEM((npairs, c, 2 * c), F32),
                        pltpu.VMEM((npairs, c, 2 * c), BF16)],
        compiler_params=_params("parallel", "arbitrary"),
        name="hgrn2",
    )(proj, proj, proj, proj, lb_logits, onorm, tril)


def _dsa_prep_kernel(k_ref, tail_ref, kg_ref, dup_ref, kn_ref, ikd_ref):
    kg = kg_ref[...]
    for h in range(ATTN_HEADS):
        sl = slice(h * ATTN_HEAD_DIM, (h + 1) * ATTN_HEAD_DIM)
        kn_ref[0, :, sl] = _rms(k_ref[0, :, sl].astype(F32), kg).astype(BF16)
    ikd_ref[0] = _dot(tail_ref[0], dup_ref[...]).astype(BF16)


def dsa_prep(proj, knorm, tm):
    b, s, _ = proj.shape
    w = ATTN_HEADS * ATTN_HEAD_DIM
    dup = np.zeros((TAIL, LANES), np.float32)
    dup[np.arange(IDX_DIM), np.arange(IDX_DIM)] = 1.0
    dup[np.arange(IDX_DIM), np.arange(IDX_DIM) + IDX_DIM] = 1.0
    return pl.pallas_call(
        _dsa_prep_kernel,
        grid=(b, s // tm),
        in_specs=[pl.BlockSpec((1, tm, w), lambda bi, i: (bi, i, 5)),
                  pl.BlockSpec((1, tm, TAIL), lambda bi, i: (bi, i, 8 * w // TAIL)),
                  pl.BlockSpec((1, ATTN_HEAD_DIM), lambda bi, i: (0, 0)),
                  pl.BlockSpec((TAIL, LANES), lambda bi, i: (0, 0))],
        out_specs=[pl.BlockSpec((1, tm, w), lambda bi, i: (bi, i, 0)),
                   pl.BlockSpec((1, tm, LANES), lambda bi, i: (bi, i, 0))],
        out_shape=[jax.ShapeDtypeStruct((b, s, w), BF16),
                   jax.ShapeDtypeStruct((b, s, LANES), BF16)],
        compiler_params=_params("parallel", "parallel"),
        name="dsa_prep",
    )(proj, proj, knorm, jnp.asarray(dup, BF16))


def _dsa_kernel(pi_ref, pj_ref, aq_ref, iq_ref, tail_ref, ikd_ref, kn_ref, v_ref, qg_ref, eq_ref, esel_ref,
                smax_ref, o_ref, keys_ref, thr_ref, qn_ref, qw_ref, lo_ref, hi_ref, m_ref, l_ref, acc_ref, p_ref,
                *, tq, tk, topk):
    step = pl.program_id(1)
    i = pi_ref[step]
    j = pj_ref[step]
    dh = ATTN_HEAD_DIM

    def causal(jj):
        s_pos = jj * tk + lax.broadcasted_iota(jnp.int32, (tk, tq), 0)
        t_pos = i * tq + lax.broadcasted_iota(jnp.int32, (tk, tq), 1)
        return s_pos <= t_pos

    @pl.when(j == 0)
    def _index_and_select():
        qg = qg_ref[...] * (dh ** -0.5 * LOG2E)
        for h in range(ATTN_HEADS):
            sl = slice(h * dh, (h + 1) * dh)
            qn_ref[:, sl] = _rms(aq_ref[0, :, sl].astype(F32), qg).astype(BF16)
        tail = tail_ref[0]
        ww = _dot(tail, eq_ref[...])
        qw = (iq_ref[0].astype(F32) * ww * (IDX_HEADS ** -0.5 * IDX_DIM ** -0.5)).astype(BF16)
        lane = lax.broadcasted_iota(jnp.int32, (tq, LANES), 1)
        zero = jnp.zeros((tq, LANES), BF16)
        for p in range(IDX_HEADS // 2):
            pair = qw[:, p * LANES:(p + 1) * LANES]
            qw_ref[2 * p] = jnp.where(lane < IDX_DIM, pair, zero)
            qw_ref[2 * p + 1] = jnp.where(lane >= IDX_DIM, pair, zero)
        w_t = _dot_nt(esel_ref[...], tail)
        lo_ref[...] = jnp.where(w_t > 0.0, 0.0, NEG_INF)
        hi_ref[...] = jnp.where(w_t > 0.0, jnp.inf, 0.0)

        def score_tile(jj, c):
            r0 = pl.multiple_of(jj * tk, tk)
            ik = ikd_ref[0, pl.ds(r0, tk), :]
            vis = causal(jj)
            for q0 in range(0, tq, 256):
                qs = slice(q0, q0 + 256)
                comb = jnp.zeros((tk, 256), F32)
                for h in range(IDX_HEADS):
                    x = _dot_nt(ik, qw_ref[h, qs, :])
                    comb = comb + jnp.minimum(jnp.maximum(x, lo_ref[h:h + 1, qs]), hi_ref[h:h + 1, qs])
                keys_ref[jj, :, qs] = jnp.where(vis[:, qs], comb, NEG_INF)
            return c
        lax.fori_loop(0, i + 1, score_tile, 0)

        def as_float(t):
            return pltpu.bitcast(jnp.where(t < 0, t ^ 0x7FFFFFFF, t), F32)
        def count_ge(tf):
            def body(jj, acc):
                accs = [acc, jnp.zeros_like(acc), jnp.zeros_like(acc), jnp.zeros_like(acc)]
                for r0 in range(0, tk, 8):
                    a = accs[(r0 // 8) % 4]
                    accs[(r0 // 8) % 4] = jnp.where(keys_ref[jj, r0:r0 + 8, :] >= tf, a + 1, a)
                return (accs[0] + accs[1]) + (accs[2] + accs[3])
            acc = lax.fori_loop(0, i + 1, body, jnp.zeros((8, tq), jnp.int32))
            return jnp.sum(acc.astype(F32), axis=0, keepdims=True)
        def unresolved(carry):
            bi, _, cnt = carry
            return jnp.logical_and(bi < 32, jnp.max(jnp.abs(cnt - topk)) > 0.0)
        def bit_steps(carry):
            bi, t, cnt = carry
            nbits = jnp.where(bi == 0, 24, 4)
            def one_bit(k, tc_cnt):
                t, cnt = tc_cnt
                tc = t + (jnp.int32(1) << (31 - bi - k))
                c = count_ge(as_float(tc))
                take = c >= topk
                return jnp.where(take, tc, t), jnp.where(take, c, cnt)
            t, cnt = lax.fori_loop(0, nbits, one_bit, (t, cnt))
            return bi + nbits, t, cnt
        n_all = ((i + 1) * tk).astype(F32)
        _, t_fin, _ = lax.while_loop(unresolved, bit_steps,
                                     (jnp.int32(0), jnp.full((1, tq), INT_MIN, jnp.int32),
                                      jnp.full((1, tq), n_all, F32)))
        t_float = as_float(t_fin)
        thr_ref[...] = jnp.where(t_float != t_float, NEG_INF, t_float)

        m_ref[...] = jnp.full_like(m_ref, NEG_INF)
        l_ref[...] = jnp.zeros_like(l_ref)
        acc_ref[...] = jnp.zeros_like(acc_ref)

    sel = (keys_ref[j] >= thr_ref[...]) & causal(j)
    smax = smax_ref[0]
    fixed_shift = smax <= SOFTMAX_FIXED_SHIFT_MAX

    @pl.when(fixed_shift)
    def _attend_fixed_shift():
        bias = jnp.where(sel, -smax * LOG2E, NEG_INF)
        for h in range(ATTN_HEADS):
            sl = slice(h * dh, (h + 1) * dh)
            p = jnp.exp2(_dot_nt(kn_ref[0, :, sl], qn_ref[:, sl]) + bias)
            parts = [p[r * 8:(r + 1) * 8] for r in range(tk // 8)]
            while len(parts) > 1:
                parts = [a + b for a, b in zip(parts[0::2], parts[1::2])]
            l_ref[h] += parts[0]
            p_ref[h] = p.astype(BF16)

    @pl.when(jnp.logical_and(fixed_shift, j <= i))
    def _weighted_values():
        for h in range(ATTN_HEADS):
            sl = slice(h * dh, (h + 1) * dh)
            acc_ref[h] += _dot_tn(v_ref[0, :, sl], p_ref[h])

    @pl.when(jnp.logical_not(fixed_shift))
    def _attend_running_max():
        bias = jnp.where(sel, 0.0, NEG_INF)
        m_all = m_ref[...]
        m_rows = []
        for h in range(ATTN_HEADS):
            sl = slice(h * dh, (h + 1) * dh)
            s = _dot_nt(kn_ref[0, :, sl], qn_ref[:, sl]) + bias
            m_old = m_all[h:h + 1, :]
            m_new = jnp.maximum(m_old, jnp.max(s, axis=0, keepdims=True))
            m_safe = jnp.where(m_new == NEG_INF, 0.0, m_new)
            alpha = jnp.exp2(m_old - m_safe)
            p = jnp.exp2(s - m_safe)
            l_ref[h, 0:1, :] = alpha * l_ref[h, 0:1, :] + jnp.sum(p, axis=0, keepdims=True)
            acc_ref[h] = alpha * acc_ref[h] + _dot_tn(v_ref[0, :, sl], p.astype(BF16))
            m_rows.append(m_new)
        m_ref[...] = jnp.concatenate(m_rows, axis=0)

    @pl.when(j == i)
    def _finish():
        for h in range(ATTN_HEADS):
            sl = slice(h * dh, (h + 1) * dh)
            l = jnp.sum(l_ref[h], axis=0, keepdims=True)
            o_ref[0, :, sl] = (acc_ref[h] / l).T.astype(o_ref.dtype)


def dsa_group(proj, kn, ikd, qnorm, knorm, tq):
    b, s, _ = proj.shape
    tk = tq
    smax = (ATTN_HEAD_DIM ** 0.5 * jnp.max(jnp.abs(qnorm)) * jnp.max(jnp.abs(knorm))).reshape(1).astype(F32)
    nq = s // tq
    w = ATTN_HEADS * ATTN_HEAD_DIM
    topk = min(DSA_TOPK, s // 4)
    eq = np.zeros((TAIL, IDX_HEADS * IDX_DIM), np.float32)
    esel = np.zeros((IDX_HEADS, TAIL), np.float32)
    for h in range(IDX_HEADS):
        eq[IDX_DIM + h, h * IDX_DIM:(h + 1) * IDX_DIM] = 1.0
        esel[h, IDX_DIM + h] = 1.0
    pairs = [(i, j) for i in range(nq) for j in range(i + 1)]
    pi = jnp.asarray([p[0] for p in pairs], jnp.int32)
    pj = jnp.asarray([p[1] for p in pairs], jnp.int32)
    qblk = lambda cb: pl.BlockSpec((1, tq, w), lambda bi, st, pi, pj, cb=cb: (bi, pi[st], cb))
    const = lambda a: pl.BlockSpec(a.shape, lambda bi, st, pi, pj: (0,) * a.ndim)
    grid_spec = pltpu.PrefetchScalarGridSpec(
        num_scalar_prefetch=2,
        grid=(b, len(pairs)),
        in_specs=[qblk(4),
                  qblk(7),
                  pl.BlockSpec((1, tq, TAIL), lambda bi, st, pi, pj: (bi, pi[st], 8 * w // TAIL)),
                  pl.BlockSpec((1, s, LANES), lambda bi, st, pi, pj: (bi, 0, 0)),
                  pl.BlockSpec((1, tk, w), lambda bi, st, pi, pj: (bi, pj[st], 0)),
                  pl.BlockSpec((1, tk, w), lambda bi, st, pi, pj: (bi, pj[st], 6)),
                  const(qnorm), const(eq), const(esel),
                  pl.BlockSpec(memory_space=pltpu.SMEM)],
        out_specs=pl.BlockSpec((1, tq, w), lambda bi, st, pi, pj: (bi, pi[st], 0)),
        scratch_shapes=[pltpu.VMEM((nq, tk, tq), F32),
                        pltpu.VMEM((1, tq), F32),
                        pltpu.VMEM((tq, w), BF16),
                        pltpu.VMEM((IDX_HEADS, tq, LANES), BF16),
                        pltpu.VMEM((IDX_HEADS, tq), F32),
                        pltpu.VMEM((IDX_HEADS, tq), F32),
                        pltpu.VMEM((ATTN_HEADS, tq), F32),
                        pltpu.VMEM((ATTN_HEADS, 8, tq), F32),
                        pltpu.VMEM((ATTN_HEADS, ATTN_HEAD_DIM, tq), F32),
                        pltpu.VMEM((ATTN_HEADS, tk, tq), BF16)])
    return pl.pallas_call(
        functools.partial(_dsa_kernel, tq=tq, tk=tk, topk=topk),
        grid_spec=grid_spec,
        out_shape=jax.ShapeDtypeStruct((b, s, w), BF16),
        compiler_params=_params("parallel", "arbitrary"),
        name="dsa",
    )(pi, pj, proj, proj, proj, ikd, kn, proj, qnorm, jnp.asarray(eq, BF16), jnp.asarray(esel, BF16), smax)


def _cast_rows_to(src_ref, dst_ref, step=256):
    def body(r, c):
        r0 = pl.multiple_of(r * step, step)
        dst_ref[pl.ds(r0, step), :] = src_ref[0, pl.ds(r0, step), :].astype(BF16)
        return c
    lax.fori_loop(0, dst_ref.shape[0] // step, body, 0)


def _out_proj_kernel(yh_ref, ya_ref, w_ref, x_ref, o_ref, wb_ref):
    @pl.when(pl.program_id(0) == 0)
    def _():
        _cast_rows_to(w_ref, wb_ref)
    kh = yh_ref.shape[1]
    o_ref[...] = x_ref[...] + _dot(yh_ref[...], wb_ref[0:kh, :]) + _dot(ya_ref[...], wb_ref[kh:2 * kh, :])


def out_proj(yh, ya, w_out, layer, x, tm):
    m, kh = yh.shape
    _, k, n = w_out.shape
    return pl.pallas_call(
        _out_proj_kernel,
        grid=(m // tm,),
        in_specs=[pl.BlockSpec((tm, kh), lambda i: (i, 0)),
                  pl.BlockSpec((tm, kh), lambda i: (i, 0)),
                  pl.BlockSpec((1, k, n), lambda i: (layer, 0, 0), pipeline_mode=pl.Buffered(1)),
                  pl.BlockSpec((tm, n), lambda i: (i, 0))],
        out_specs=pl.BlockSpec((tm, n), lambda i: (i, 0)),
        out_shape=jax.ShapeDtypeStruct((m, n), F32),
        scratch_shapes=[pltpu.VMEM((k, n), BF16)],
        compiler_params=_params("arbitrary"),
        name="out_proj",
    )(yh, ya, w_out, x)


def _mem_kv_kernel(mem_ref, mg_ref, wk_ref, wv_ref, kg_ref, k_ref, v_ref):
    memn = _rms(mem_ref[0], mg_ref[...]).astype(BF16)
    k = _dot(memn, wk_ref[0].astype(BF16))
    kg = kg_ref[...]
    for h in range(CROSS_HEADS):
        sl = slice(h * CROSS_HEAD_DIM, (h + 1) * CROSS_HEAD_DIM)
        k_ref[0, :, sl] = _rms(k[:, sl], kg).astype(BF16)
    v_ref[0] = _dot(memn, wv_ref[0].astype(BF16)).astype(BF16)


def mem_kv(mem, mem_norm, wk, wv, layer, xk_norm):
    b, nm, d = mem.shape
    cw = wk.shape[2]
    full = lambda a: pl.BlockSpec(a.shape, lambda bi: (0,) * a.ndim)
    wspec = pl.BlockSpec((1, d, cw), lambda bi: (layer, 0, 0))
    return pl.pallas_call(
        _mem_kv_kernel,
        grid=(b,),
        in_specs=[pl.BlockSpec((1, nm, d), lambda bi: (bi, 0, 0)),
                  full(mem_norm), wspec, wspec, full(xk_norm)],
        out_specs=[pl.BlockSpec((1, nm, cw), lambda bi: (bi, 0, 0))] * 2,
        out_shape=[jax.ShapeDtypeStruct((b, nm, cw), BF16)] * 2,
        compiler_params=_params("parallel"),
        name="mem_kv",
    )(mem, mem_norm, wk, wv, xk_norm)


def _cross_kernel(h_ref, ng_ref, wq_ref, qg_ref, k_ref, v_ref, wo_ref, o_ref, hn_ref, oc_ref, wqb_ref, wob_ref):
    @pl.when((pl.program_id(0) == 0) & (pl.program_id(1) == 0))
    def _():
        _cast_rows_to(wq_ref, wqb_ref)
        _cast_rows_to(wo_ref, wob_ref)
    rows = h_ref.shape[1]
    dh = CROSS_HEAD_DIM
    step = 256
    gain = ng_ref[...]
    def body(r, c):
        r0 = pl.multiple_of(r * step, step)
        hn_ref[pl.ds(r0, step), :] = _rms(h_ref[0, pl.ds(r0, step), :], gain).astype(BF16)
        return c
    lax.fori_loop(0, rows // step, body, 0)
    q = _dot(hn_ref[...], wqb_ref[...])
    qg = qg_ref[...] * (dh ** -0.5)
    for h in range(CROSS_HEADS):
        sl = slice(h * dh, (h + 1) * dh)
        qn = _rms(q[:, sl], qg).astype(BF16)
        s = _dot_nt(qn, k_ref[0, :, sl])
        p = jnp.exp(s - jnp.max(s, axis=1, keepdims=True))
        l = jnp.sum(p, axis=1, keepdims=True)
        oc_ref[:, sl] = (_dot(p.astype(BF16), v_ref[0, :, sl]) / l).astype(BF16)
    o_ref[0] = h_ref[0] + _dot(oc_ref[...], wob_ref[...])


def cross_attention(h, norm_cross, wq, xq_norm, kx, vx, wo, layer, tm):
    b, s, d = h.shape
    nm, cw = kx.shape[1:]
    full = lambda a: pl.BlockSpec(a.shape, lambda bi, i: (0,) * a.ndim)
    resident = lambda a: pl.BlockSpec((1,) + a.shape[1:], lambda bi, i: (layer, 0, 0),
                                      pipeline_mode=pl.Buffered(1))
    return pl.pallas_call(
        _cross_kernel,
        grid=(b, s // tm),
        in_specs=[pl.BlockSpec((1, tm, d), lambda bi, i: (bi, i, 0)),
                  full(norm_cross), resident(wq), full(xq_norm),
                  pl.BlockSpec((1, nm, cw), lambda bi, i: (bi, 0, 0)),
                  pl.BlockSpec((1, nm, cw), lambda bi, i: (bi, 0, 0)),
                  resident(wo)],
        out_specs=pl.BlockSpec((1, tm, d), lambda bi, i: (bi, i, 0)),
        out_shape=jax.ShapeDtypeStruct((b, s, d), F32),
        scratch_shapes=[pltpu.VMEM((tm, d), BF16), pltpu.VMEM((tm, cw), BF16),
                        pltpu.VMEM(wq.shape[1:], BF16), pltpu.VMEM(wo.shape[1:], BF16)],
        compiler_params=_params("arbitrary", "arbitrary"),
        name="cross_attn",
    )(h, norm_cross, wq, xq_norm, kx, vx, wo)


def _mlp_kernel(h_ref, g_ref, wu_ref, wd_ref, o_ref, hn_ref):
    @pl.when(pl.program_id(1) == 0)
    def _():
        _norm_rows_to(h_ref, g_ref[...], hn_ref, h_ref.shape[0])
        o_ref[...] = h_ref[...]

    u = jnp.maximum(_dot(hn_ref[...], wu_ref[0].astype(BF16)), 0.0)
    o_ref[...] += _dot((u * u).astype(BF16), wd_ref[0].astype(BF16))


def mlp(h, gain, w_up, w_down, layer, tm, tf):
    m, d = h.shape
    f = w_up.shape[2]
    return pl.pallas_call(
        _mlp_kernel,
        grid=(m // tm, f // tf),
        in_specs=[pl.BlockSpec((tm, d), lambda i, j: (i, 0)),
                  pl.BlockSpec((1, d), lambda i, j: (0, 0)),
                  pl.BlockSpec((1, d, tf), lambda i, j: (layer, 0, j)),
                  pl.BlockSpec((1, tf, d), lambda i, j: (layer, j, 0))],
        out_specs=pl.BlockSpec((tm, d), lambda i, j: (i, 0), pipeline_mode=pl.Buffered(1)),
        out_shape=jax.ShapeDtypeStruct((m, d), F32),
        scratch_shapes=[pltpu.VMEM((tm, d), BF16)],
        compiler_params=_params("parallel", "arbitrary"),
        name="mlp",
    )(h, gain, w_up, w_down)


def _tile(n, pref):
    return pref if n % pref == 0 else n


def _tiles(n, s):
    return dict(
        proj_rows=_tile(n, 1024), proj_cols=768,
        proj_group=2 if n % 2048 == 0 else 1,
        hgrn_rows=_tile(s, 1024),
        dsa_block=_tile(s, 512),
        out_rows=_tile(n, 512),
        cross_rows=_tile(s, 1024),
        mlp_rows=_tile(n, 1024), mlp_ff=512)


def kernel(x, mem, norm_mix, w_in, hgrn_lb_logits, hgrn_onorm, attn_qnorm, attn_knorm, w_out,
           norm_cross, mem_norm, wq_x, wk_x, wv_x, wo_x, xq_norm, xk_norm,
           norm_mlp, w_up, w_down):
    b, s, d = x.shape
    n = b * s
    depth = w_in.shape[0]
    assert depth == 1
    l = 0
    in_width = w_in.shape[2]
    main_w = 8 * HGRN_HEADS * HGRN_KDIM
    assert in_width == main_w + IDX_DIM + IDX_HEADS
    pad_w = main_w + TAIL

    t = _tiles(n, s)
    x2 = x.reshape(n, d)

    proj = norm_matmul(x2, norm_mix[l:l + 1], w_in[l].T, pad_w,
                       t["proj_rows"], t["proj_cols"], t["proj_group"])
    proj3 = proj.reshape(b, s, pad_w)

    y_h = hgrn_group(proj3, hgrn_lb_logits, hgrn_onorm[l:l + 1], t["hgrn_rows"])
    kn, ikd = dsa_prep(proj3, attn_knorm[l:l + 1], t["dsa_block"])
    y_a = dsa_group(proj3, kn, ikd, attn_qnorm[l:l + 1], attn_knorm[l:l + 1], t["dsa_block"])

    h1 = out_proj(y_h.reshape(n, -1), y_a.reshape(n, -1), w_out, l, x2, t["out_rows"])

    kx, vx = mem_kv(mem, mem_norm[l:l + 1], wk_x, wv_x, l, xk_norm[l:l + 1])
    h2 = cross_attention(h1.reshape(b, s, d), norm_cross[l:l + 1], wq_x, xq_norm[l:l + 1],
                         kx, vx, wo_x, l, t["cross_rows"])

    h3 = mlp(h2.reshape(n, d), norm_mlp[l:l + 1], w_up, w_down, l, t["mlp_rows"], t["mlp_ff"])
    return h3.reshape(b, s, d)
```

```python
import functools

import numpy as np
import jax
import jax.numpy as jnp
from jax import lax
from jax.experimental import pallas as pl
from jax.experimental.pallas import tpu as pltpu

F32 = jnp.float32
BF16 = jnp.bfloat16
EPS = 1e-6

LANES = 128
HGRN_HEADS = 8
HGRN_KDIM = 128
HGRN_CHUNK = 64
ATTN_HEADS = 8
ATTN_HEAD_DIM = 128
IDX_HEADS = 16
IDX_DIM = 64
DSA_TOPK = 256
CROSS_HEADS = 4
CROSS_HEAD_DIM = 128
TAIL = 256
V7X_VMEM_BYTES = 64 * 1024 * 1024
VMEM_LIMIT = V7X_VMEM_BYTES * 7 // 8
NEG_INF = float("-inf")
INT_MIN = -(2 ** 31)
LOG2E = 1.4426950408889634
SOFTMAX_FIXED_SHIFT_MAX = 40.0


def _params(*sem):
    return pltpu.CompilerParams(dimension_semantics=sem, vmem_limit_bytes=VMEM_LIMIT)


def _rms(x, gain):
    return x * lax.rsqrt(jnp.mean(x * x, axis=-1, keepdims=True) + EPS) * gain


def _sigmoid(x):
    return 1.0 / (1.0 + jnp.exp(-x))


def _dot(a, b):
    return jnp.dot(a, b, preferred_element_type=F32)


def _dot_nt(a, b):
    return lax.dot_general(a, b, (((1,), (1,)), ((), ())), preferred_element_type=F32)


def _dot_tn(a, b):
    return lax.dot_general(a, b, (((0,), (0,)), ((), ())), preferred_element_type=F32)


def _norm_rows_to(x_ref, gain, dst_ref, rows):
    step = 256
    def body(r, c):
        r0 = pl.multiple_of(r * step, step)
        x = x_ref[pl.ds(r0, step), :]
        dst_ref[pl.ds(r0, step), :] = _rms(x, gain).astype(BF16)
        return c
    lax.fori_loop(0, rows // step, body, 0)


def _norm_matmul_kernel(kind_ref, g_tbl, r_tbl, j_tbl, x_ref, gain_ref, wt_ref, o_ref, xn_ref, *, valid_cols):
    s = pl.program_id(0)
    tm = x_ref.shape[0]
    tn = o_ref.shape[1]

    @pl.when(kind_ref[s] == 0)
    def _():
        r0 = pl.multiple_of(r_tbl[s] * tm, tm)
        _norm_rows_to(x_ref, gain_ref[...], xn_ref.at[pl.ds(r0, tm)], tm)

    @pl.when(kind_ref[s] == 1)
    def _():
        res = _dot_nt(xn_ref[...], wt_ref[...].astype(BF16))
        col = j_tbl[s] * tn + lax.broadcasted_iota(jnp.int32, (1, tn), 1)
        o_ref[...] = jnp.where(col < valid_cols, res, 0.0).astype(o_ref.dtype)


def norm_matmul(x, gain, wt, n, tm, tn, group):
    m, k = x.shape
    ngroups, ntiles = m // (group * tm), n // tn
    kind, g_tbl, r_tbl, j_tbl = [], [], [], []
    for g in range(ngroups):
        for r in range(group):
            kind.append(0); g_tbl.append(g); r_tbl.append(r); j_tbl.append(0)
        for j in range(ntiles):
            kind.append(1); g_tbl.append(g); r_tbl.append(group - 1); j_tbl.append(j)
    tables = [jnp.asarray(t, jnp.int32) for t in (kind, g_tbl, r_tbl, j_tbl)]
    grid_spec = pltpu.PrefetchScalarGridSpec(
        num_scalar_prefetch=4,
        grid=(len(kind),),
        in_specs=[pl.BlockSpec((tm, k), lambda s, kd, gt, rt, jt: (gt[s] * group + rt[s], 0)),
                  pl.BlockSpec((1, k), lambda s, kd, gt, rt, jt: (0, 0)),
                  pl.BlockSpec((tn, k), lambda s, kd, gt, rt, jt: (jt[s], 0))],
        out_specs=pl.BlockSpec((group * tm, tn), lambda s, kd, gt, rt, jt: (gt[s], jt[s])),
        scratch_shapes=[pltpu.VMEM((group * tm, k), BF16)])
    return pl.pallas_call(
        functools.partial(_norm_matmul_kernel, valid_cols=wt.shape[0]),
        grid_spec=grid_spec,
        out_shape=jax.ShapeDtypeStruct((m, n), BF16),
        compiler_params=_params("arbitrary"),
        name="in_proj",
    )(*tables, x, gain, wt)


def _hgrn_kernel(q_ref, f_ref, i_ref, g_ref, lbl_ref, on_ref, tril_ref, o_ref,
                 state_ref, a_ref, qi_ref, ks_ref, el_ref, qt_ref, kt_ref, sc_ref, scb_ref, *, nchunks):
    c = HGRN_CHUNK
    dk = HGRN_KDIM
    pw = 2 * dk
    npairs = HGRN_HEADS // 2
    nlev = 6

    @pl.when(pl.program_id(1) == 0)
    def _():
        state_ref[...] = jnp.zeros_like(state_ref)
        kt_ref[...] = jnp.zeros_like(kt_ref)

    lbl = lbl_ref[...]
    e = jnp.exp(lbl - jnp.max(lbl, axis=0, keepdims=True))
    lb = e[0:1] / jnp.sum(e, axis=0, keepdims=True)
    onorm = on_ref[...]
    tril = tril_ref[...]

    row = lax.broadcasted_iota(jnp.int32, (c, 2 * c), 0)
    lane = lax.broadcasted_iota(jnp.int32, (c, 2 * c), 1)
    col = lane & (c - 1)
    first = lane < c
    eye = row == col
    level_masks = []
    lg = nlev - 1
    while lg >= 0:
        level_masks.append(((row >> (lg + 1)) == (col >> (lg + 1)))
                           & (((row >> lg) & 1) == 1) & (((col >> lg) & 1) == 0))
        lg -= 1
    lane_p = lax.broadcasted_iota(jnp.int32, (c, pw), 1)
    odd_row = (lax.broadcasted_iota(jnp.int32, (c, pw), 0) & 1) == 1
    sub8 = lax.broadcasted_iota(jnp.int32, (8, pw), 0)
    zeros_st = jnp.zeros((dk, dk), BF16)

    def chunk_body(ci, carry):
        r0 = pl.multiple_of(ci * c, c)

        def _decay_factors():
            hq = q_ref[0, pl.ds(r0, c), :].astype(F32)
            hf = f_ref[0, pl.ds(r0, c), :].astype(F32)
            qf = hq * _sigmoid(hq) * (dk ** -0.5)
            f = lb + (1.0 - lb) * _sigmoid(hf)
            logf = jnp.log(f) * LOG2E
            kk = 1.0 - f
            g0 = logf.astype(BF16)
            g1 = (logf - g0.astype(F32)).astype(BF16)
            a_ref[...] = _dot(tril, g0) + _dot(tril, g1)
            for p in range(npairs):
                sl = slice(p * pw, (p + 1) * pw)
                a = a_ref[:, sl]
                row = lambda r, n: jnp.broadcast_to(a_ref[r:r + 1, sl], (n, pw))
                e0 = jnp.exp2(a)
                qp = qf[:, sl]
                kp = kk[:, sl]
                qi_ref[p] = (qp * e0).astype(BF16)
                ks_ref[p] = (kp * jnp.exp2(row(c - 1, c) - a)).astype(BF16)
                el_ref[p] = e0[c - 1:c]
                for l in range(nlev):
                    h = c >> (l + 1)
                    if h >= 4:
                        ref = jnp.concatenate([row(m * 2 * h + h - 1, 2 * h) for m in range(c // (2 * h))], axis=0)
                    elif h == 2:
                        ref = jnp.concatenate([jnp.where(sub8 < 4, row(8 * m + 1, 8), row(8 * m + 5, 8))
                                               for m in range(c // 8)], axis=0)
                    else:
                        ref = jnp.where(odd_row, pltpu.roll(a, 1, 0), a)
                    el = jnp.exp2(-jnp.abs(a - ref))
                    qt_ref[p, l] = (qp * el).astype(BF16)
                    kl = (kp * el).astype(BF16)
                    kt_ref[p, l, 0:c, 0:dk] = kl[:, 0:dk]
                    kt_ref[p, l, c:2 * c, dk:pw] = kl[:, dk:pw]
                qk = qp * kp
                diag = jnp.where(first, jnp.sum(qk[:, 0:dk], axis=1, keepdims=True),
                                 jnp.sum(qk[:, dk:pw], axis=1, keepdims=True))
                sc_ref[p] = jnp.where(eye, diag, 0.0)

        def _intra_chunk_scores():
            for p in range(npairs):
                sc = sc_ref[p]
                for l, msk in enumerate(level_masks):
                    sc = sc + jnp.where(msk, _dot_nt(qt_ref[p, l], kt_ref[p, l]), 0.0)
                scb_ref[p] = sc.astype(BF16)

        def _outputs_and_state():
            for p in range(npairs):
                sl = slice(p * pw, (p + 1) * pw)
                vp = i_ref[0, pl.ds(r0, c), sl]
                st_a = state_ref[2 * p]
                st_b = state_ref[2 * p + 1]
                st_bd = jnp.concatenate(
                    [jnp.concatenate([st_a.astype(BF16), zeros_st], axis=1),
                     jnp.concatenate([zeros_st, st_b.astype(BF16)], axis=1)], axis=0)
                v_bd = jnp.concatenate([jnp.where(lane_p < dk, vp, jnp.zeros_like(vp)),
                                        jnp.where(lane_p >= dk, vp, jnp.zeros_like(vp))], axis=0)
                o = _dot_nt(qi_ref[p], st_bd) + _dot(scb_ref[p], v_bd)
                upd = _dot_tn(vp, ks_ref[p])
                el = el_ref[p]
                state_ref[2 * p] = st_a * el[:, 0:dk] + upd[0:dk, 0:dk]
                state_ref[2 * p + 1] = st_b * el[:, dk:pw] + upd[dk:pw, dk:pw]
                gate = g_ref[0, pl.ds(r0, c), sl].astype(F32)
                y = jnp.concatenate([_rms(o[:, 0:dk], onorm), _rms(o[:, dk:pw], onorm)], axis=1)
                o_ref[0, pl.ds(r0, c), sl] = (y * (gate * _sigmoid(gate))).astype(o_ref.dtype)

        _decay_factors()
        _intra_chunk_scores()
        _outputs_and_state()
        return carry

    lax.fori_loop(0, nchunks, chunk_body, 0, unroll=8)


def hgrn_group(proj, lb_logits, onorm, t_blk):
    b, s, _ = proj.shape
    w = HGRN_HEADS * HGRN_KDIM
    tril = jnp.asarray(np.tril(np.ones((HGRN_CHUNK, HGRN_CHUNK), np.float32)), BF16)
    c, pw, npairs, nlev = HGRN_CHUNK, 2 * HGRN_KDIM, HGRN_HEADS // 2, 6
    col = lambda cb: pl.BlockSpec((1, t_blk, w), lambda bi, ti, cb=cb: (bi, ti, cb))
    return pl.pallas_call(
        functools.partial(_hgrn_kernel, nchunks=t_blk // HGRN_CHUNK),
        grid=(b, s // t_blk),
        in_specs=[col(0), col(1), col(2), col(3),
                  pl.BlockSpec(lb_logits.shape, lambda bi, ti: (0, 0)),
                  pl.BlockSpec((1, HGRN_KDIM), lambda bi, ti: (0, 0)),
                  pl.BlockSpec(tril.shape, lambda bi, ti: (0, 0))],
        out_specs=pl.BlockSpec((1, t_blk, w), lambda bi, ti: (bi, ti, 0)),
        out_shape=jax.ShapeDtypeStruct((b, s, w), BF16),
        scratch_shapes=[pltpu.VMEM((HGRN_HEADS, HGRN_KDIM, HGRN_KDIM), F32),
                        pltpu.VMEM((c, w), F32),
                        pltpu.VMEM((npairs, c, pw), BF16),
                        pltpu.VMEM((npairs, c, pw), BF16),
                        pltpu.VMEM((npairs, 1, pw), F32),
                        pltpu.VMEM((npairs, nlev, c, pw), BF16),
                        pltpu.VMEM((npairs, nlev, 2 * c, pw), BF16),
                        pltpu.VMEM((npairs, c, 2 * c), F32),
                        pltpu.VMEM((npairs, c, 2 * c), BF16)],
        compiler_params=_params("parallel", "arbitrary"),
        name="hgrn2",
    )(proj, proj, proj, proj, lb_logits, onorm, tril)


def _dsa_prep_kernel(k_ref, tail_ref, kg_ref, dup_ref, kn_ref, ikd_ref):
    kg = kg_ref[...]
    for h in range(ATTN_HEADS):
        sl = slice(h * ATTN_HEAD_DIM, (h + 1) * ATTN_HEAD_DIM)
        kn_ref[0, :, sl] = _rms(k_ref[0, :, sl].astype(F32), kg).astype(BF16)
    ikd_ref[0] = _dot(tail_ref[0], dup_ref[...]).astype(BF16)


def dsa_prep(proj, knorm, tm):
    b, s, _ = proj.shape
    w = ATTN_HEADS * ATTN_HEAD_DIM
    dup = np.zeros((TAIL, LANES), np.float32)
    dup[np.arange(IDX_DIM), np.arange(IDX_DIM)] = 1.0
    dup[np.arange(IDX_DIM), np.arange(IDX_DIM) + IDX_DIM] = 1.0
    return pl.pallas_call(
        _dsa_prep_kernel,
        grid=(b, s // tm),
        in_specs=[pl.BlockSpec((1, tm, w), lambda bi, i: (bi, i, 5)),
                  pl.BlockSpec((1, tm, TAIL), lambda bi, i: (bi, i, 8 * w // TAIL)),
                  pl.BlockSpec((1, ATTN_HEAD_DIM), lambda bi, i: (0, 0)),
                  pl.BlockSpec((TAIL, LANES), lambda bi, i: (0, 0))],
        out_specs=[pl.BlockSpec((1, tm, w), lambda bi, i: (bi, i, 0)),
                   pl.BlockSpec((1, tm, LANES), lambda bi, i: (bi, i, 0))],
        out_shape=[jax.ShapeDtypeStruct((b, s, w), BF16),
                   jax.ShapeDtypeStruct((b, s, LANES), BF16)],
        compiler_params=_params("parallel", "parallel"),
        name="dsa_prep",
    )(proj, proj, knorm, jnp.asarray(dup, BF16))


def _dsa_kernel(pi_ref, pj_ref, aq_ref, iq_ref, tail_ref, ikd_ref, kn_ref, v_ref, qg_ref, eq_ref, esel_ref,
                smax_ref, o_ref, keys_ref, thr_ref, qn_ref, qw_ref, lo_ref, hi_ref, m_ref, l_ref, acc_ref, p_ref,
                *, tq, tk, topk):
    step = pl.program_id(1)
    i = pi_ref[step]
    j = pj_ref[step]
    dh = ATTN_HEAD_DIM

    def causal(jj):
        s_pos = jj * tk + lax.broadcasted_iota(jnp.int32, (tk, tq), 0)
        t_pos = i * tq + lax.broadcasted_iota(jnp.int32, (tk, tq), 1)
        return s_pos <= t_pos

    @pl.when(j == 0)
    def _index_and_select():
        qg = qg_ref[...] * (dh ** -0.5 * LOG2E)
        for h in range(ATTN_HEADS):
            sl = slice(h * dh, (h + 1) * dh)
            qn_ref[:, sl] = _rms(aq_ref[0, :, sl].astype(F32), qg).astype(BF16)
        tail = tail_ref[0]
        ww = _dot(tail, eq_ref[...])
        qw = (iq_ref[0].astype(F32) * ww * (IDX_HEADS ** -0.5 * IDX_DIM ** -0.5)).astype(BF16)
        lane = lax.broadcasted_iota(jnp.int32, (tq, LANES), 1)
        zero = jnp.zeros((tq, LANES), BF16)
        for p in range(IDX_HEADS // 2):
            pair = qw[:, p * LANES:(p + 1) * LANES]
            qw_ref[2 * p] = jnp.where(lane < IDX_DIM, pair, zero)
            qw_ref[2 * p + 1] = jnp.where(lane >= IDX_DIM, pair, zero)
        w_t = _dot_nt(esel_ref[...], tail)
        lo_ref[...] = jnp.where(w_t > 0.0, 0.0, NEG_INF)
        hi_ref[...] = jnp.where(w_t > 0.0, jnp.inf, 0.0)

        def score_tile(jj, c):
            r0 = pl.multiple_of(jj * tk, tk)
            ik = ikd_ref[0, pl.ds(r0, tk), :]
            vis = causal(jj)
            for q0 in range(0, tq, 256):
                qs = slice(q0, q0 + 256)
                comb = jnp.zeros((tk, 256), F32)
                for h in range(IDX_HEADS):
                    x = _dot_nt(ik, qw_ref[h, qs, :])
                    comb = comb + jnp.minimum(jnp.maximum(x, lo_ref[h:h + 1, qs]), hi_ref[h:h + 1, qs])
                keys_ref[jj, :, qs] = jnp.where(vis[:, qs], comb, NEG_INF)
            return c
        lax.fori_loop(0, i + 1, score_tile, 0)

        def as_float(t):
            return pltpu.bitcast(jnp.where(t < 0, t ^ 0x7FFFFFFF, t), F32)
        def count_ge(tf):
            def body(jj, acc):
                accs = [acc, jnp.zeros_like(acc), jnp.zeros_like(acc), jnp.zeros_like(acc)]
                for r0 in range(0, tk, 8):
                    a = accs[(r0 // 8) % 4]
                    accs[(r0 // 8) % 4] = jnp.where(keys_ref[jj, r0:r0 + 8, :] >= tf, a + 1, a)
                return (accs[0] + accs[1]) + (accs[2] + accs[3])
            acc = lax.fori_loop(0, i + 1, body, jnp.zeros((8, tq), jnp.int32))
            return jnp.sum(acc.astype(F32), axis=0, keepdims=True)
        def unresolved(carry):
            bi, _, cnt = carry
            return jnp.logical_and(bi < 32, jnp.max(jnp.abs(cnt - topk)) > 0.0)
        def bit_steps(carry):
            bi, t, cnt = carry
            nbits = jnp.where(bi == 0, 24, 4)
            def one_bit(k, tc_cnt):
                t, cnt = tc_cnt
                tc = t + (jnp.int32(1) << (31 - bi - k))
                c = count_ge(as_float(tc))
                take = c >= topk
                return jnp.where(take, tc, t), jnp.where(take, c, cnt)
            t, cnt = lax.fori_loop(0, nbits, one_bit, (t, cnt))
            return bi + nbits, t, cnt
        n_all = ((i + 1) * tk).astype(F32)
        _, t_fin, _ = lax.while_loop(unresolved, bit_steps,
                                     (jnp.int32(0), jnp.full((1, tq), INT_MIN, jnp.int32),
                                      jnp.full((1, tq), n_all, F32)))
        t_float = as_float(t_fin)
        thr_ref[...] = jnp.where(t_float != t_float, NEG_INF, t_float)

        m_ref[...] = jnp.full_like(m_ref, NEG_INF)
        l_ref[...] = jnp.zeros_like(l_ref)
        acc_ref[...] = jnp.zeros_like(acc_ref)

    sel = (keys_ref[j] >= thr_ref[...]) & causal(j)
    smax = smax_ref[0]
    fixed_shift = smax <= SOFTMAX_FIXED_SHIFT_MAX

    @pl.when(fixed_shift)
    def _attend_fixed_shift():
        bias = jnp.where(sel, -smax * LOG2E, NEG_INF)
        for h in range(ATTN_HEADS):
            sl = slice(h * dh, (h + 1) * dh)
            p = jnp.exp2(_dot_nt(kn_ref[0, :, sl], qn_ref[:, sl]) + bias)
            parts = [p[r * 8:(r + 1) * 8] for r in range(tk // 8)]
            while len(parts) > 1:
                parts = [a + b for a, b in zip(parts[0::2], parts[1::2])]
            l_ref[h] += parts[0]
            p_ref[h] = p.astype(BF16)

    @pl.when(jnp.logical_and(fixed_shift, j <= i))
    def _weighted_values():
        for h in range(ATTN_HEADS):
            sl = slice(h * dh, (h + 1) * dh)
            acc_ref[h] += _dot_tn(v_ref[0, :, sl], p_ref[h])

    @pl.when(jnp.logical_not(fixed_shift))
    def _attend_running_max():
        bias = jnp.where(sel, 0.0, NEG_INF)
        m_all = m_ref[...]
        m_rows = []
        for h in range(ATTN_HEADS):
            sl = slice(h * dh, (h + 1) * dh)
            s = _dot_nt(kn_ref[0, :, sl], qn_ref[:, sl]) + bias
            m_old = m_all[h:h + 1, :]
            m_new = jnp.maximum(m_old, jnp.max(s, axis=0, keepdims=True))
            m_safe = jnp.where(m_new == NEG_INF, 0.0, m_new)
            alpha = jnp.exp2(m_old - m_safe)
            p = jnp.exp2(s - m_safe)
            l_ref[h, 0:1, :] = alpha * l_ref[h, 0:1, :] + jnp.sum(p, axis=0, keepdims=True)
            acc_ref[h] = alpha * acc_ref[h] + _dot_tn(v_ref[0, :, sl], p.astype(BF16))
            m_rows.append(m_new)
        m_ref[...] = jnp.concatenate(m_rows, axis=0)

    @pl.when(j == i)
    def _finish():
        for h in range(ATTN_HEADS):
            sl = slice(h * dh, (h + 1) * dh)
            l = jnp.sum(l_ref[h], axis=0, keepdims=True)
            o_ref[0, :, sl] = (acc_ref[h] / l).T.astype(o_ref.dtype)


def dsa_group(proj, kn, ikd, qnorm, knorm, tq):
    b, s, _ = proj.shape
    tk = tq
    smax = (ATTN_HEAD_DIM ** 0.5 * jnp.max(jnp.abs(qnorm)) * jnp.max(jnp.abs(knorm))).reshape(1).astype(F32)
    nq = s // tq
    w = ATTN_HEADS * ATTN_HEAD_DIM
    topk = min(DSA_TOPK, s // 4)
    eq = np.zeros((TAIL, IDX_HEADS * IDX_DIM), np.float32)
    esel = np.zeros((IDX_HEADS, TAIL), np.float32)
    for h in range(IDX_HEADS):
        eq[IDX_DIM + h, h * IDX_DIM:(h + 1) * IDX_DIM] = 1.0
        esel[h, IDX_DIM + h] = 1.0
    pairs = [(i, j) for i in range(nq) for j in range(i + 1)]
    pi = jnp.asarray([p[0] for p in pairs], jnp.int32)
    pj = jnp.asarray([p[1] for p in pairs], jnp.int32)
    qblk = lambda cb: pl.BlockSpec((1, tq, w), lambda bi, st, pi, pj, cb=cb: (bi, pi[st], cb))
    const = lambda a: pl.BlockSpec(a.shape, lambda bi, st, pi, pj: (0,) * a.ndim)
    grid_spec = pltpu.PrefetchScalarGridSpec(
        num_scalar_prefetch=2,
        grid=(b, len(pairs)),
        in_specs=[qblk(4),
                  qblk(7),
                  pl.BlockSpec((1, tq, TAIL), lambda bi, st, pi, pj: (bi, pi[st], 8 * w // TAIL)),
                  pl.BlockSpec((1, s, LANES), lambda bi, st, pi, pj: (bi, 0, 0)),
                  pl.BlockSpec((1, tk, w), lambda bi, st, pi, pj: (bi, pj[st], 0)),
                  pl.BlockSpec((1, tk, w), lambda bi, st, pi, pj: (bi, pj[st], 6)),
                  const(qnorm), const(eq), const(esel),
                  pl.BlockSpec(memory_space=pltpu.SMEM)],
        out_specs=pl.BlockSpec((1, tq, w), lambda bi, st, pi, pj: (bi, pi[st], 0)),
        scratch_shapes=[pltpu.VMEM((nq, tk, tq), F32),
                        pltpu.VMEM((1, tq), F32),
                        pltpu.VMEM((tq, w), BF16),
                        pltpu.VMEM((IDX_HEADS, tq, LANES), BF16),
                        pltpu.VMEM((IDX_HEADS, tq), F32),
                        pltpu.VMEM((IDX_HEADS, tq), F32),
                        pltpu.VMEM((ATTN_HEADS, tq), F32),
                        pltpu.VMEM((ATTN_HEADS, 8, tq), F32),
                        pltpu.VMEM((ATTN_HEADS, ATTN_HEAD_DIM, tq), F32),
                        pltpu.VMEM((ATTN_HEADS, tk, tq), BF16)])
    return pl.pallas_call(
        functools.partial(_dsa_kernel, tq=tq, tk=tk, topk=topk),
        grid_spec=grid_spec,
        out_shape=jax.ShapeDtypeStruct((b, s, w), BF16),
        compiler_params=_params("parallel", "arbitrary"),
        name="dsa",
    )(pi, pj, proj, proj, proj, ikd, kn, proj, qnorm, jnp.asarray(eq, BF16), jnp.asarray(esel, BF16), smax)


def _cast_rows_to(src_ref, dst_ref, step=256):
    def body(r, c):
        r0 = pl.multiple_of(r * step, step)
        dst_ref[pl.ds(r0, step), :] = src_ref[0, pl.ds(r0, step), :].astype(BF16)
        return c
    lax.fori_loop(0, dst_ref.shape[0] // step, body, 0)


def _out_proj_kernel(yh_ref, ya_ref, w_ref, x_ref, o_ref, wb_ref):
    @pl.when(pl.program_id(0) == 0)
    def _():
        _cast_rows_to(w_ref, wb_ref)
    kh = yh_ref.shape[1]
    o_ref[...] = x_ref[...] + _dot(yh_ref[...], wb_ref[0:kh, :]) + _dot(ya_ref[...], wb_ref[kh:2 * kh, :])


def out_proj(yh, ya, w_out, layer, x, tm):
    m, kh = yh.shape
    _, k, n = w_out.shape
    return pl.pallas_call(
        _out_proj_kernel,
        grid=(m // tm,),
        in_specs=[pl.BlockSpec((tm, kh), lambda i: (i, 0)),
                  pl.BlockSpec((tm, kh), lambda i: (i, 0)),
                  pl.BlockSpec((1, k, n), lambda i: (layer, 0, 0), pipeline_mode=pl.Buffered(1)),
                  pl.BlockSpec((tm, n), lambda i: (i, 0))],
        out_specs=pl.BlockSpec((tm, n), lambda i: (i, 0)),
        out_shape=jax.ShapeDtypeStruct((m, n), F32),
        scratch_shapes=[pltpu.VMEM((k, n), BF16)],
        compiler_params=_params("arbitrary"),
        name="out_proj",
    )(yh, ya, w_out, x)


def _mem_kv_kernel(mem_ref, mg_ref, wk_ref, wv_ref, kg_ref, k_ref, v_ref):
    memn = _rms(mem_ref[0], mg_ref[...]).astype(BF16)
    k = _dot(memn, wk_ref[0].astype(BF16))
    kg = kg_ref[...]
    for h in range(CROSS_HEADS):
        sl = slice(h * CROSS_HEAD_DIM, (h + 1) * CROSS_HEAD_DIM)
        k_ref[0, :, sl] = _rms(k[:, sl], kg).astype(BF16)
    v_ref[0] = _dot(memn, wv_ref[0].astype(BF16)).astype(BF16)


def mem_kv(mem, mem_norm, wk, wv, layer, xk_norm):
    b, nm, d = mem.shape
    cw = wk.shape[2]
    full = lambda a: pl.BlockSpec(a.shape, lambda bi: (0,) * a.ndim)
    wspec = pl.BlockSpec((1, d, cw), lambda bi: (layer, 0, 0))
    return pl.pallas_call(
        _mem_kv_kernel,
        grid=(b,),
        in_specs=[pl.BlockSpec((1, nm, d), lambda bi: (bi, 0, 0)),
                  full(mem_norm), wspec, wspec, full(xk_norm)],
        out_specs=[pl.BlockSpec((1, nm, cw), lambda bi: (bi, 0, 0))] * 2,
        out_shape=[jax.ShapeDtypeStruct((b, nm, cw), BF16)] * 2,
        compiler_params=_params("parallel"),
        name="mem_kv",
    )(mem, mem_norm, wk, wv, xk_norm)


def _cross_kernel(h_ref, ng_ref, wq_ref, qg_ref, k_ref, v_ref, wo_ref, o_ref, hn_ref, oc_ref, wqb_ref, wob_ref):
    @pl.when((pl.program_id(0) == 0) & (pl.program_id(1) == 0))
    def _():
        _cast_rows_to(wq_ref, wqb_ref)
        _cast_rows_to(wo_ref, wob_ref)
    rows = h_ref.shape[1]
    dh = CROSS_HEAD_DIM
    step = 256
    gain = ng_ref[...]
    def body(r, c):
        r0 = pl.multiple_of(r * step, step)
        hn_ref[pl.ds(r0, step), :] = _rms(h_ref[0, pl.ds(r0, step), :], gain).astype(BF16)
        return c
    lax.fori_loop(0, rows // step, body, 0)
    q = _dot(hn_ref[...], wqb_ref[...])
    qg = qg_ref[...] * (dh ** -0.5)
    for h in range(CROSS_HEADS):
        sl = slice(h * dh, (h + 1) * dh)
        qn = _rms(q[:, sl], qg).astype(BF16)
        s = _dot_nt(qn, k_ref[0, :, sl])
        p = jnp.exp(s - jnp.max(s, axis=1, keepdims=True))
        l = jnp.sum(p, axis=1, keepdims=True)
        oc_ref[:, sl] = (_dot(p.astype(BF16), v_ref[0, :, sl]) / l).astype(BF16)
    o_ref[0] = h_ref[0] + _dot(oc_ref[...], wob_ref[...])


def cross_attention(h, norm_cross, wq, xq_norm, kx, vx, wo, layer, tm):
    b, s, d = h.shape
    nm, cw = kx.shape[1:]
    full = lambda a: pl.BlockSpec(a.shape, lambda bi, i: (0,) * a.ndim)
    resident = lambda a: pl.BlockSpec((1,) + a.shape[1:], lambda bi, i: (layer, 0, 0),
                                      pipeline_mode=pl.Buffered(1))
    return pl.pallas_call(
        _cross_kernel,
        grid=(b, s // tm),
        in_specs=[pl.BlockSpec((1, tm, d), lambda bi, i: (bi, i, 0)),
                  full(norm_cross), resident(wq), full(xq_norm),
                  pl.BlockSpec((1, nm, cw), lambda bi, i: (bi, 0, 0)),
                  pl.BlockSpec((1, nm, cw), lambda bi, i: (bi, 0, 0)),
                  resident(wo)],
        out_specs=pl.BlockSpec((1, tm, d), lambda bi, i: (bi, i, 0)),
        out_shape=jax.ShapeDtypeStruct((b, s, d), F32),
        scratch_shapes=[pltpu.VMEM((tm, d), BF16), pltpu.VMEM((tm, cw), BF16),
                        pltpu.VMEM(wq.shape[1:], BF16), pltpu.VMEM(wo.shape[1:], BF16)],
        compiler_params=_params("arbitrary", "arbitrary"),
        name="cross_attn",
    )(h, norm_cross, wq, xq_norm, kx, vx, wo)


def _mlp_kernel(h_ref, g_ref, wu_ref, wd_ref, o_ref, hn_ref):
    @pl.when(pl.program_id(1) == 0)
    def _():
        _norm_rows_to(h_ref, g_ref[...], hn_ref, h_ref.shape[0])
        o_ref[...] = h_ref[...]

    u = jnp.maximum(_dot(hn_ref[...], wu_ref[0].astype(BF16)), 0.0)
    o_ref[...] += _dot((u * u).astype(BF16), wd_ref[0].astype(BF16))


def mlp(h, gain, w_up, w_down, layer, tm, tf):
    m, d = h.shape
    f = w_up.shape[2]
    return pl.pallas_call(
        _mlp_kernel,
        grid=(m // tm, f // tf),
        in_specs=[pl.BlockSpec((tm, d), lambda i, j: (i, 0)),
                  pl.BlockSpec((1, d), lambda i, j: (0, 0)),
                  pl.BlockSpec((1, d, tf), lambda i, j: (layer, 0, j)),
                  pl.BlockSpec((1, tf, d), lambda i, j: (layer, j, 0))],
        out_specs=pl.BlockSpec((tm, d), lambda i, j: (i, 0), pipeline_mode=pl.Buffered(1)),
        out_shape=jax.ShapeDtypeStruct((m, d), F32),
        scratch_shapes=[pltpu.VMEM((tm, d), BF16)],
        compiler_params=_params("parallel", "arbitrary"),
        name="mlp",
    )(h, gain, w_up, w_down)


def _tile(n, pref):
    return pref if n % pref == 0 else n


def _tiles(n, s):
    return dict(
        proj_rows=_tile(n, 1024), proj_cols=768,
        proj_group=2 if n % 2048 == 0 else 1,
        hgrn_rows=_tile(s, 1024),
        dsa_block=_tile(s, 512),
        out_rows=_tile(n, 512),
        cross_rows=_tile(s, 1024),
        mlp_rows=_tile(n, 1024), mlp_ff=512)


def kernel(x, mem, norm_mix, w_in, hgrn_lb_logits, hgrn_onorm, attn_qnorm, attn_knorm, w_out,
           norm_cross, mem_norm, wq_x, wk_x, wv_x, wo_x, xq_norm, xk_norm,
           norm_mlp, w_up, w_down):
    b, s, d = x.shape
    n = b * s
    depth = w_in.shape[0]
    assert depth == 1
    l = 0
    in_width = w_in.shape[2]
    main_w = 8 * HGRN_HEADS * HGRN_KDIM
    assert in_width == main_w + IDX_DIM + IDX_HEADS
    pad_w = main_w + TAIL

    t = _tiles(n, s)
    x2 = x.reshape(n, d)

    proj = norm_matmul(x2, norm_mix[l:l + 1], w_in[l].T, pad_w,
                       t["proj_rows"], t["proj_cols"], t["proj_group"])
    proj3 = proj.reshape(b, s, pad_w)

    y_h = hgrn_group(proj3, hgrn_lb_logits, hgrn_onorm[l:l + 1], t["hgrn_rows"])
    kn, ikd = dsa_prep(proj3, attn_knorm[l:l + 1], t["dsa_block"])
    y_a = dsa_group(proj3, kn, ikd, attn_qnorm[l:l + 1], attn_knorm[l:l + 1], t["dsa_block"])

    h1 = out_proj(y_h.reshape(n, -1), y_a.reshape(n, -1), w_out, l, x2, t["out_rows"])

    kx, vx = mem_kv(mem, mem_norm[l:l + 1], wk_x, wv_x, l, xk_norm[l:l + 1])
    h2 = cross_attention(h1.reshape(b, s, d), norm_cross[l:l + 1], wq_x, xq_norm[l:l + 1],
                         kx, vx, wo_x, l, t["cross_rows"])

    h3 = mlp(h2.reshape(n, d), norm_mlp[l:l + 1], w_up, w_down, l, t["mlp_rows"], t["mlp_ff"])
    return h3.reshape(b, s, d)
```
